```python
import math
import jax, jax.numpy as jnp
from jax import lax
import numpy as np

D_MODEL = 2048
BATCH = 4
SEQ = 4096
DEPTH = 1

D_MIX = D_MODEL
ATTN_W = D_MIX // 2
SSM_W = D_MIX - ATTN_W
HEAD_DIM = 64
N_HEADS = ATTN_W // HEAD_DIM
N_KV_HEADS = 4
KV_REP = N_HEADS // N_KV_HEADS
KV_W = N_KV_HEADS * HEAD_DIM
WINDOW = 128
BLOCK = 128
ROPE_THETA = 10000.0
SSM_H = 16
SSM_G = SSM_W // SSM_H
SSM_P = 64
SSM_CHUNK = 128
DT_MIN = 1e-3
DT_MAX = 1e-1
NORM_EPS = 1e-6
IN_W = ATTN_W + 2 * KV_W + ATTN_W + SSM_W + SSM_W

kernel_name = "hymba_swa_sink_s5_hybrid"


def rms_norm(x, w):
    xf = x.astype(jnp.float32)
    y = xf * lax.rsqrt(jnp.mean(xf * xf, axis=-1, keepdims=True) + NORM_EPS)
    return (y * w.astype(jnp.float32)).astype(x.dtype)


def rope_tables(positions):
    inv_freq = ROPE_THETA ** (-jnp.arange(0, HEAD_DIM, 2, dtype=jnp.float32) / HEAD_DIM)
    ang = positions.astype(jnp.float32)[..., None] * inv_freq
    return jnp.cos(ang)[:, :, None, :], jnp.sin(ang)[:, :, None, :]


def apply_rope(t, cos, sin):
    tf = t.astype(jnp.float32)
    t1, t2 = tf[..., : HEAD_DIM // 2], tf[..., HEAD_DIM // 2 :]
    out = jnp.concatenate([t1 * cos - t2 * sin, t2 * cos + t1 * sin], axis=-1)
    return out.astype(t.dtype)


def swa_sink_attention(q, k, v, positions, q_norm_w, k_norm_w, sinks):
    B, L = q.shape[0], q.shape[1]
    nb = L // BLOCK
    q = q.reshape(B, L, N_HEADS, HEAD_DIM)
    k = k.reshape(B, L, N_KV_HEADS, HEAD_DIM)
    v = v.reshape(B, L, N_KV_HEADS, HEAD_DIM)
    cos, sin = rope_tables(positions)
    q = apply_rope(rms_norm(q, q_norm_w), cos, sin)
    k = apply_rope(rms_norm(k, k_norm_w), cos, sin)

    qb = q.reshape(B, nb, BLOCK, N_KV_HEADS, KV_REP, HEAD_DIM)

    def with_prev(t):
        t = t.reshape(B, nb, BLOCK, N_KV_HEADS, HEAD_DIM)
        prev = jnp.pad(t[:, :-1], ((0, 0), (1, 0), (0, 0), (0, 0), (0, 0)))
        return jnp.concatenate([prev, t], axis=2)

    kb, vb = with_prev(k), with_prev(v)
    scale = 1.0 / math.sqrt(HEAD_DIM)
    s = jnp.einsum('bnqgrd,bnkgd->bngrqk', qb, kb).astype(jnp.float32) * scale

    qi = jnp.arange(BLOCK)[:, None] + BLOCK
    ki = jnp.arange(2 * BLOCK)[None, :]
    rel = qi - ki
    band = (rel >= 0) & (rel < WINDOW)
    has_prev = (jnp.arange(nb)[:, None, None] > 0) | (ki >= BLOCK)[None]
    mask = band[None] & has_prev
    s = jnp.where(mask[None, :, None, None], s, jnp.float32(-1e30))

    sink = jnp.broadcast_to(
        sinks.astype(jnp.float32).reshape(1, 1, N_KV_HEADS, KV_REP, 1, 1),
        s.shape[:-1] + (1,))
    p = jax.nn.softmax(jnp.concatenate([s, sink], axis=-1), axis=-1)[..., :-1]
    o = jnp.einsum('bngrqk,bnkgd->bnqgrd', p.astype(vb.dtype), vb)
    return o.reshape(B, L, ATTN_W)


def s5_ssm(u, a_re, a_im, log_step, b_re, b_im, c_re, c_im, d_skip):
    B, L = u.shape[0], u.shape[1]
    nc = L // SSM_CHUNK
    f32 = jnp.float32
    lam = lax.complex(a_re.astype(f32), a_im.astype(f32))
    delta = jnp.exp(log_step.astype(f32))[:, None]
    lam_bar = jnp.exp(lam * delta)
    b = lax.complex(b_re.astype(f32), b_im.astype(f32))
    b_bar = ((lam_bar - 1.0) / lam)[..., None] * b
    c = lax.complex(c_re.astype(f32), c_im.astype(f32))

    uf = u.astype(f32)
    ug = uf.reshape(B, nc, SSM_CHUNK, SSM_G, SSM_H).transpose(1, 0, 2, 3, 4)

    def combine(left, right):
        a_l, x_l = left
        a_r, x_r = right
        return a_r * a_l, a_r * x_l + x_r

    def chunk_step(h0, u_c):
        bu = jnp.einsum('gph,bigh->bigp', b_bar, u_c.astype(jnp.complex64))
        a = jnp.broadcast_to(lam_bar, bu.shape)
        a_cum, h_loc = lax.associative_scan(combine, (a, bu), axis=1)
        h = h_loc + a_cum * h0[:, None]
        y = jnp.einsum('ghp,bigp->bigh', c, h).real
        return h[:, -1], y

    h0 = jnp.zeros((B, SSM_G, SSM_P), jnp.complex64)
    _, ys = lax.scan(chunk_step, h0, ug)
    y = ys.transpose(1, 0, 2, 3, 4).reshape(B, L, SSM_W)
    return y + d_skip.astype(f32) * uf


def setup_inputs(seed: int = 0) -> dict:
    key = jax.random.key(seed)
    ks = jax.random.split(key, 20)
    f32 = jnp.float32
    x = jax.random.normal(ks[0], (BATCH, SEQ, D_MODEL), f32)
    offs = jax.random.randint(ks[1], (BATCH, 1), 0, 1024, dtype=jnp.int32)
    positions = (jnp.arange(SEQ, dtype=jnp.int32)[None, :] + offs).astype(jnp.int32)
    norm_w = 1.0 + 0.02 * jax.random.normal(ks[2], (D_MODEL,), f32)
    w_in = jax.random.normal(ks[3], (D_MODEL, IN_W), f32) * D_MODEL ** -0.5
    q_norm_w = 1.0 + 0.02 * jax.random.normal(ks[4], (HEAD_DIM,), f32)
    k_norm_w = 1.0 + 0.02 * jax.random.normal(ks[5], (HEAD_DIM,), f32)
    sinks = jax.random.normal(ks[6], (N_HEADS,), f32)
    n = jnp.arange(SSM_P, dtype=f32)[None, :]
    a_re = -0.5 + 0.01 * jax.random.normal(ks[7], (SSM_G, SSM_P), f32)
    a_im = math.pi * n + 0.01 * jax.random.normal(ks[8], (SSM_G, SSM_P), f32)
    log_step = jax.random.uniform(ks[9], (SSM_G,), f32, math.log(DT_MIN), math.log(DT_MAX))
    b_scale = (2.0 * SSM_H) ** -0.5
    b_re = jax.random.normal(ks[10], (SSM_G, SSM_P, SSM_H), f32) * b_scale
    b_im = jax.random.normal(ks[11], (SSM_G, SSM_P, SSM_H), f32) * b_scale
    c_scale = (2.0 * SSM_P) ** -0.5
    c_re = jax.random.normal(ks[12], (SSM_G, SSM_H, SSM_P), f32) * c_scale
    c_im = jax.random.normal(ks[13], (SSM_G, SSM_H, SSM_P), f32) * c_scale
    d_skip = jax.random.normal(ks[14], (SSM_W,), f32)
    w_glu = jax.random.normal(ks[15], (SSM_W, SSM_W), f32) * SSM_W ** -0.5
    b_glu = 0.02 * jax.random.normal(ks[16], (SSM_W,), f32)
    attn_out_norm_w = 1.0 + 0.02 * jax.random.normal(ks[17], (ATTN_W,), f32)
    ssm_out_norm_w = 1.0 + 0.02 * jax.random.normal(ks[18], (SSM_W,), f32)
    w_out = jax.random.normal(ks[19], (D_MIX, D_MODEL), f32) * D_MIX ** -0.5
    return {"x": x, "positions": positions, "norm_w": norm_w, "w_in": w_in,
            "q_norm_w": q_norm_w, "k_norm_w": k_norm_w, "sinks": sinks,
            "a_re": a_re, "a_im": a_im, "log_step": log_step,
            "b_re": b_re, "b_im": b_im, "c_re": c_re, "c_im": c_im,
            "d_skip": d_skip, "w_glu": w_glu, "b_glu": b_glu,
            "attn_out_norm_w": attn_out_norm_w, "ssm_out_norm_w": ssm_out_norm_w,
            "w_out": w_out}


def reference(x, positions, norm_w, w_in, q_norm_w, k_norm_w, sinks,
              a_re, a_im, log_step, b_re, b_im, c_re, c_im, d_skip, w_glu, b_glu,
              attn_out_norm_w, ssm_out_norm_w, w_out):
    for _ in range(DEPTH):
        h = rms_norm(x, norm_w)
        proj = jnp.einsum('bld,de->ble', h, w_in)
        splits = np.cumsum([ATTN_W, KV_W, KV_W, ATTN_W, SSM_W])
        q, k, v, z_attn, u, z_ssm = jnp.split(proj, splits, axis=-1)

        o_attn = swa_sink_attention(q, k, v, positions, q_norm_w, k_norm_w, sinks)
        o_attn = o_attn * jax.nn.silu(z_attn)

        y = s5_ssm(u, a_re, a_im, log_step, b_re, b_im, c_re, c_im, d_skip)
        y = jax.nn.gelu(y)
        y = y * jax.nn.sigmoid(y @ w_glu.astype(jnp.float32) + b_glu.astype(jnp.float32))
        o_ssm = y.astype(x.dtype) * jax.nn.silu(z_ssm)

        merged = jnp.concatenate([rms_norm(o_attn, attn_out_norm_w),
                                  rms_norm(o_ssm, ssm_out_norm_w)], axis=-1)
        x = x + jnp.einsum('ble,ed->bld', merged, w_out).astype(x.dtype)
    return x
```

```python
import functools
import math

import jax
import jax.numpy as jnp
import numpy as np
from jax import lax
from jax.experimental import pallas as pl
from jax.experimental.pallas import tpu as pltpu

D_MODEL = 2048
ATTN_W = 1024
KV_W = 256
HEAD_DIM = 64
N_HEADS = 16
N_KV_HEADS = 4
KV_REP = N_HEADS // N_KV_HEADS
SSM_W = 1024
SSM_H = 16
SSM_G = 64
SSM_P = 64
BLOCK = 128
ROPE_THETA = 10000.0
NORM_EPS = 1e-6
IN_W = 2 * ATTN_W + 2 * KV_W + 2 * SSM_W

LANES = 128
MXU_DIM = 256
SSM_Q = 8
SLAB_GROUPS = LANES // SSM_H
N_SLABS = SSM_W // LANES
SLAB_IN = SSM_Q * LANES
SLAB_STATE = 2 * SLAB_GROUPS * SSM_P
VMEM_LIMIT = 56 * 1024 * 1024

_BF16 = jnp.bfloat16
_F32 = jnp.float32


def _q_head_order():
    order = []
    for slab in range(N_HEADS // 2):
        j, r = divmod(slab, KV_REP)
        order += [2 * j * KV_REP + r, (2 * j + 1) * KV_REP + r]
    return order


def _resident(shape):
    nd = len(shape)
    return pl.BlockSpec(shape, lambda *_: (0,) * nd, pipeline_mode=pl.Buffered(1))


def _rope(t, cos, sin_signed):
    lane = lax.broadcasted_iota(jnp.int32, t.shape, 1)
    first_half = (lane % HEAD_DIM) < (HEAD_DIM // 2)
    swapped = jnp.where(first_half,
                        pltpu.roll(t, LANES - HEAD_DIM // 2, axis=1),
                        pltpu.roll(t, HEAD_DIM // 2, axis=1))
    return t * cos + swapped * sin_signed


def _in_proj_kernel(x_ref, pos_ref, w_ref, qw_ref, kw_ref, invf_ref, sgn_ref, ones_ref,
                    q_ref, k_ref, v_ref, za_ref, ucat_ref, zs_ref, hn_ref, u_ref):
    tm = x_ref.shape[0]
    x = x_ref[...]
    rs = lax.rsqrt(jnp.mean(x * x, axis=-1, keepdims=True) + NORM_EPS)
    hn_ref[...] = (x * rs).astype(_BF16)

    ang = pos_ref[...].astype(_F32) * invf_ref[...]
    cos = jnp.cos(ang)
    sin_signed = jnp.sin(ang) * sgn_ref[...]

    def proj(c0, width):
        return jnp.dot(hn_ref[...], w_ref[:, c0:c0 + width], preferred_element_type=_F32)

    def normed_rope(c0, nw_ref, n0, out_ref):
        acc = proj(c0, MXU_DIM)
        ssq = jnp.dot((acc * acc).astype(_BF16), ones_ref[...], preferred_element_type=_F32)
        t = acc * lax.rsqrt(ssq * (1.0 / HEAD_DIM) + NORM_EPS) * nw_ref[:, n0:n0 + MXU_DIM]
        for s in range(MXU_DIM // LANES):
            sl = slice(s * LANES, (s + 1) * LANES)
            out_ref[:, n0 + s * LANES:n0 + (s + 1) * LANES] = _rope(t[:, sl], cos, sin_signed).astype(_BF16)

    for c in range(ATTN_W // MXU_DIM):
        normed_rope(c * MXU_DIM, qw_ref, c * MXU_DIM, q_ref)
    normed_rope(ATTN_W, kw_ref, 0, k_ref)
    v_ref[...] = proj(ATTN_W + KV_W, KV_W).astype(_BF16)

    half = ATTN_W // 2
    base = ATTN_W + 2 * KV_W
    for c in range(2):
        za_ref[:, c * half:(c + 1) * half] = proj(base + c * half, half).astype(_BF16)
    base += ATTN_W
    for c in range(2):
        acc = proj(base + c * half, half)
        for s in range(half // LANES):
            u_ref[c * (half // LANES) + s] = acc[:, s * LANES:(s + 1) * LANES]
    base += SSM_W
    for c in range(2):
        zs_ref[:, c * half:(c + 1) * half] = proj(base + c * half, half).astype(_BF16)

    nc = tm // SSM_Q
    for s in range(N_SLABS):
        for i in range(SSM_Q):
            c0 = s * SLAB_IN + i * LANES
            ucat_ref[:, c0:c0 + LANES] = u_ref[s, pl.ds(i, nc, stride=SSM_Q), :].astype(_BF16)


def _in_proj(x2, pos2, w_in, qw, kw, invf, sgn, ones, tm):
    t = x2.shape[0]
    row = lambda w: pl.BlockSpec((tm, w), lambda i: (i, 0))
    return pl.pallas_call(
        _in_proj_kernel,
        grid=(t // tm,),
        in_specs=[row(D_MODEL), row(1), _resident(w_in.shape), _resident(qw.shape),
                  _resident(kw.shape), _resident(invf.shape), _resident(sgn.shape),
                  _resident(ones.shape)],
        out_specs=[row(ATTN_W), row(KV_W), row(KV_W), row(ATTN_W),
                   pl.BlockSpec((tm // SSM_Q, N_SLABS * SLAB_IN), lambda i: (i, 0)), row(SSM_W)],
        out_shape=[jax.ShapeDtypeStruct((t, ATTN_W), _BF16),
                   jax.ShapeDtypeStruct((t, KV_W), _BF16),
                   jax.ShapeDtypeStruct((t, KV_W), _BF16),
                   jax.ShapeDtypeStruct((t, ATTN_W), _BF16),
                   jax.ShapeDtypeStruct((t // SSM_Q, N_SLABS * SLAB_IN), _BF16),
                   jax.ShapeDtypeStruct((t, SSM_W), _BF16)],
        scratch_shapes=[pltpu.VMEM((tm, D_MODEL), _BF16), pltpu.VMEM((N_SLABS, tm, LANES), _F32)],
        compiler_params=pltpu.CompilerParams(dimension_semantics=("arbitrary",),
                                             vmem_limit_bytes=VMEM_LIMIT),
        name="in_proj",
    )(x2, pos2, w_in, qw, kw, invf, sgn, ones)


def _attn_kernel(sinks_ref, q_ref, kp_ref, kc_ref, vp_ref, vc_ref, za_ref, nw_ref, o_ref, acc_ref):
    n = pl.program_id(1)
    iq = lax.broadcasted_iota(jnp.int32, (BLOCK, 2 * BLOCK), 0)
    ik = lax.broadcasted_iota(jnp.int32, (BLOCK, 2 * BLOCK), 1)
    rel = ik - iq
    allowed = (rel >= 1) & (rel <= BLOCK) & ((ik >= BLOCK) | (n > 0))
    lane_kv = lax.broadcasted_iota(jnp.int32, (2 * BLOCK, LANES), 1)
    lane_o = lax.broadcasted_iota(jnp.int32, (BLOCK, LANES), 1)

    for j in range(N_KV_HEADS // 2):
        sl = slice(j * LANES, (j + 1) * LANES)
        k2 = jnp.concatenate([kp_ref[:, sl], kc_ref[:, sl]], axis=0)
        v2 = jnp.concatenate([vp_ref[:, sl], vc_ref[:, sl]], axis=0)
        zero = jnp.zeros_like(k2)
        qs = jnp.concatenate(
            [q_ref[:, (KV_REP * j + r) * LANES:(KV_REP * j + r + 1) * LANES] for r in range(KV_REP)],
            axis=0)
        probs, rdens = [], []
        for half in range(2):
            keep = (lane_kv < HEAD_DIM) if half == 0 else (lane_kv >= HEAD_DIM)
            s = lax.dot_general(qs, jnp.where(keep, k2, zero), (((1,), (1,)), ((), ())),
                                preferred_element_type=_F32)
            p_rows, rd_rows = [], []
            for r in range(KV_REP):
                sink = sinks_ref[2 * KV_REP * j + KV_REP * half + r]
                sr = jnp.where(allowed, s[r * BLOCK:(r + 1) * BLOCK], -1e30)
                m = jnp.maximum(jnp.max(sr, axis=-1, keepdims=True), sink)
                p = jnp.exp(sr - m)
                den = jnp.sum(p, axis=-1, keepdims=True) + jnp.exp(sink - m)
                p_rows.append(p.astype(_BF16))
                rd_rows.append(1.0 / den)
            probs.append(jnp.concatenate(p_rows, axis=0))
            rdens.append(rd_rows)
        pcat = jnp.concatenate(probs, axis=1)
        vcat = jnp.concatenate([jnp.where(lane_kv < HEAD_DIM, v2, zero),
                                jnp.where(lane_kv >= HEAD_DIM, v2, zero)], axis=0)
        o = jnp.dot(pcat, vcat, preferred_element_type=_F32)
        for r in range(KV_REP):
            scale = jnp.where(lane_o < HEAD_DIM, rdens[0][r], rdens[1][r])
            c0 = (KV_REP * j + r) * LANES
            acc_ref[:, c0:c0 + LANES] = o[r * BLOCK:(r + 1) * BLOCK] * scale

    z = za_ref[...].astype(_F32)
    g = acc_ref[...] * (z * jax.nn.sigmoid(z))
    rs = lax.rsqrt(jnp.mean(g * g, axis=-1, keepdims=True) + NORM_EPS)
    o_ref[...] = (g * rs * nw_ref[...]).astype(_BF16)


def _attention(sinks_perm, q, k, v, za, nw, batch, seq):
    nb = seq // BLOCK
    cur = lambda w: pl.BlockSpec((BLOCK, w), lambda b, n: (b * nb + n, 0))
    prev = lambda w: pl.BlockSpec((BLOCK, w), lambda b, n: (b * nb + jnp.maximum(n - 1, 0), 0))
    return pl.pallas_call(
        _attn_kernel,
        grid=(batch, nb),
        in_specs=[pl.BlockSpec(memory_space=pltpu.SMEM), cur(ATTN_W), prev(KV_W), cur(KV_W),
                  prev(KV_W), cur(KV_W), cur(ATTN_W), _resident(nw.shape)],
        out_specs=cur(ATTN_W),
        out_shape=jax.ShapeDtypeStruct((batch * seq, ATTN_W), _BF16),
        scratch_shapes=[pltpu.VMEM((BLOCK, ATTN_W), _F32)],
        compiler_params=pltpu.CompilerParams(dimension_semantics=("arbitrary", "arbitrary")),
        name="swa_attention",
    )(sinks_perm, q, k, k, v, v, za, nw)


def _ssm_weights(a_re, a_im, log_step, b_re, b_im, c_re, c_im, d_skip, n_scan):
    f32 = _F32
    a_re, a_im = a_re.astype(f32), a_im.astype(f32)
    delta = jnp.exp(log_step.astype(f32))[:, None]
    mag = jnp.exp(a_re * delta)
    lr, li = mag * jnp.cos(a_im * delta), mag * jnp.sin(a_im * delta)
    den = a_re * a_re + a_im * a_im
    fr = ((lr - 1.0) * a_re + li * a_im) / den
    fi = (li * a_re - (lr - 1.0) * a_im) / den
    b_re, b_im = b_re.astype(f32), b_im.astype(f32)
    bbr = fr[..., None] * b_re - fi[..., None] * b_im
    bbi = fr[..., None] * b_im + fi[..., None] * b_re
    c_re, c_im = c_re.astype(f32), c_im.astype(f32)

    pw_r, pw_i = [jnp.ones_like(lr)], [jnp.zeros_like(li)]
    for _ in range(SSM_Q):
        pr, pi = pw_r[-1], pw_i[-1]
        pw_r.append(pr * lr - pi * li)
        pw_i.append(pr * li + pi * lr)
    rev_r = jnp.stack(pw_r[SSM_Q - 1::-1])
    rev_i = jnp.stack(pw_i[SSM_Q - 1::-1])
    pw_r, pw_i = jnp.stack(pw_r), jnp.stack(pw_i)

    cl_r = c_re[None] * pw_r[:, :, None, :] - c_im[None] * pw_i[:, :, None, :]
    cl_i = c_re[None] * pw_i[:, :, None, :] + c_im[None] * pw_r[:, :, None, :]
    kern = (jnp.einsum('dghp,gpk->dghk', cl_r[:SSM_Q], bbr, precision=lax.Precision.HIGHEST)
            - jnp.einsum('dghp,gpk->dghk', cl_i[:SSM_Q], bbi, precision=lax.Precision.HIGHEST))
    eye_h = jnp.eye(SSM_H, dtype=f32)
    kern = kern.at[0].add(d_skip.astype(f32).reshape(SSM_G, SSM_H)[:, :, None] * eye_h[None])

    eye_g = jnp.eye(SLAB_GROUPS, dtype=f32)
    slab = lambda a: a.reshape(a.shape[0], N_SLABS, SLAB_GROUPS, *a.shape[2:])

    i_in = np.arange(SSM_Q)[:, None]
    i_out = np.arange(SSM_Q)[None, :]
    lag = np.clip(i_out - i_in, 0, SSM_Q - 1)
    causal = jnp.asarray((i_out >= i_in).astype(np.float32))
    kt = slab(kern)[lag] * causal[:, :, None, None, None, None]
    w_toep = jnp.einsum('ijsgkh,gm->sighjmk', kt, eye_g).reshape(N_SLABS, SLAB_IN, SLAB_IN)

    sin_r = rev_r[..., None] * bbr[None] - rev_i[..., None] * bbi[None]
    sin_i = rev_r[..., None] * bbi[None] + rev_i[..., None] * bbr[None]
    sin = jnp.stack([slab(sin_r), slab(sin_i)])
    w_sin = jnp.einsum('cisgph,gm->sighcmp', sin, eye_g).reshape(N_SLABS, SLAB_IN, SLAB_STATE)
    w1 = jnp.concatenate([w_toep, w_sin], axis=-1)

    sout = jnp.stack([slab(cl_r[1:]), -slab(cl_i[1:])])
    w2 = jnp.einsum('cjsgkp,gm->scgpjmk', sout, eye_g).reshape(N_SLABS, SLAB_STATE, SLAB_IN)

    ar, ai = pw_r[SSM_Q], pw_i[SSM_Q]
    scan = []
    for _ in range(n_scan):
        scan.append(jnp.concatenate([ar.reshape(N_SLABS, -1), ai.reshape(N_SLABS, -1)], axis=-1))
        ar, ai = ar * ar - ai * ai, 2.0 * ar * ai
    decay = jnp.stack(scan, axis=1)
    return w1.astype(_BF16), w2.astype(_BF16), decay


def _ssm_kernel(x_ref, w1_ref, w2_ref, dec_ref, y_ref, h_ref, y_scr):
    nc = x_ref.shape[0]
    half = SLAB_STATE // 2

    @pl.when(pl.program_id(2) == 0)
    def _():
        h_ref[...] = jnp.zeros_like(h_ref)

    r = jnp.dot(x_ref[...], w1_ref[0], preferred_element_type=_F32)
    y_intra = r[:, :SLAB_IN]
    sr, si = r[:, SLAB_IN:SLAB_IN + half], r[:, SLAB_IN + half:]
    hin_r, hin_i = h_ref[0:1, :half], h_ref[0:1, half:]
    row = lax.broadcasted_iota(jnp.int32, (nc, half), 0)

    ar, ai = dec_ref[0, 0:1, :half], dec_ref[0, 0:1, half:]
    sr = sr + jnp.where(row == 0, ar * hin_r - ai * hin_i, 0.0)
    si = si + jnp.where(row == 0, ar * hin_i + ai * hin_r, 0.0)
    for k in range(dec_ref.shape[1]):
        shift = 1 << k
        pr, pi = dec_ref[0, k:k + 1, :half], dec_ref[0, k:k + 1, half:]
        tr = jnp.where(row >= shift, pltpu.roll(sr, shift, axis=0), 0.0)
        ti = jnp.where(row >= shift, pltpu.roll(si, shift, axis=0), 0.0)
        sr, si = sr + pr * tr - pi * ti, si + pr * ti + pi * tr

    h_ref[0:1, :half] = sr[nc - 1:nc]
    h_ref[0:1, half:] = si[nc - 1:nc]
    prev_r = jnp.where(row == 0, hin_r, pltpu.roll(sr, 1, axis=0))
    prev_i = jnp.where(row == 0, hin_i, pltpu.roll(si, 1, axis=0))
    hprev = jnp.concatenate([prev_r, prev_i], axis=1).astype(_BF16)
    y = y_intra + jnp.dot(hprev, w2_ref[0], preferred_element_type=_F32)

    for i in range(SSM_Q):
        y_scr[pl.ds(i, nc, stride=SSM_Q), :] = y[:, i * LANES:(i + 1) * LANES]
    y_ref[...] = y_scr[...].astype(y_ref.dtype)


def _ssm(ucat, w1, w2, decay, batch, seq, nc):
    tiles = seq // (nc * SSM_Q)
    return pl.pallas_call(
        _ssm_kernel,
        grid=(N_SLABS, batch, tiles),
        in_specs=[pl.BlockSpec((nc, SLAB_IN), lambda s, b, t: (b * tiles + t, s)),
                  pl.BlockSpec((1,) + w1.shape[1:], lambda s, b, t: (s, 0, 0)),
                  pl.BlockSpec((1,) + w2.shape[1:], lambda s, b, t: (s, 0, 0)),
                  pl.BlockSpec((1,) + decay.shape[1:], lambda s, b, t: (s, 0, 0))],
        out_specs=pl.BlockSpec((nc * SSM_Q, LANES), lambda s, b, t: (b * tiles + t, s)),
        out_shape=jax.ShapeDtypeStruct((batch * seq, SSM_W), _BF16),
        scratch_shapes=[pltpu.VMEM((8, SLAB_STATE), _F32), pltpu.VMEM((nc * SSM_Q, LANES), _F32)],
        compiler_params=pltpu.CompilerParams(
            dimension_semantics=("arbitrary", "arbitrary", "arbitrary"),
            vmem_limit_bytes=VMEM_LIMIT),
        name="s5_mixer",
    )(ucat, w1, w2, decay)


def _out_kernel(x_ref, oa_ref, y_ref, zs_ref, wg_ref, bg_ref, nw_ref, woa_ref, wos_ref, out_ref):
    y = y_ref[...].astype(_F32)
    y = jax.nn.gelu(y)
    gate = jnp.dot(y.astype(_BF16), wg_ref[...], preferred_element_type=_F32) + bg_ref[...]
    z = zs_ref[...].astype(_F32)
    o = y * jax.nn.sigmoid(gate) * (z * jax.nn.sigmoid(z))
    rs = lax.rsqrt(jnp.mean(o * o, axis=-1, keepdims=True) + NORM_EPS)
    o = (o * rs * nw_ref[...]).astype(_BF16)
    delta = (jnp.dot(oa_ref[...], woa_ref[...], preferred_element_type=_F32)
             + jnp.dot(o, wos_ref[...], preferred_element_type=_F32))
    out_ref[...] = x_ref[...] + delta


def _out_proj(x2, oa, y, zs, wg, bg, nw, woa, wos, tm):
    t = x2.shape[0]
    row = lambda w: pl.BlockSpec((tm, w), lambda i: (i, 0))
    return pl.pallas_call(
        _out_kernel,
        grid=(t // tm,),
        in_specs=[row(D_MODEL), row(ATTN_W), row(SSM_W), row(SSM_W), _resident(wg.shape),
                  _resident(bg.shape), _resident(nw.shape), _resident(woa.shape),
                  _resident(wos.shape)],
        out_specs=row(D_MODEL),
        out_shape=jax.ShapeDtypeStruct((t, D_MODEL), x2.dtype),
        compiler_params=pltpu.CompilerParams(dimension_semantics=("arbitrary",),
                                             vmem_limit_bytes=VMEM_LIMIT),
        name="out_proj",
    )(x2, oa, y, zs, wg, bg, nw, woa, wos)


def _pick_tile(n, target, multiple):
    best = multiple
    for cand in range(multiple, min(n, target) + 1, multiple):
        if n % cand == 0:
            best = cand
    return best


def kernel(x, positions, norm_w, w_in, q_norm_w, k_norm_w, sinks, a_re, a_im, log_step,
           b_re, b_im, c_re, c_im, d_skip, w_glu, b_glu, attn_out_norm_w, ssm_out_norm_w, w_out):
    batch, seq, _ = x.shape
    assert seq % BLOCK == 0 and x.shape[2] == D_MODEL and w_in.shape == (D_MODEL, IN_W)
    t = batch * seq
    f32 = _F32

    head_cols = np.concatenate([np.arange(h * HEAD_DIM, (h + 1) * HEAD_DIM) for h in _q_head_order()])
    in_cols = np.concatenate([head_cols, np.arange(ATTN_W, ATTN_W + 2 * KV_W),
                              ATTN_W + 2 * KV_W + head_cols,
                              np.arange(2 * ATTN_W + 2 * KV_W, IN_W)])
    w_in_k = (norm_w.astype(f32)[:, None] * w_in.astype(f32))[:, in_cols].astype(_BF16)
    qw = (jnp.tile(q_norm_w.astype(f32), N_HEADS) * (1.0 / math.sqrt(HEAD_DIM)))[None, :]
    kw = jnp.tile(k_norm_w.astype(f32), N_KV_HEADS)[None, :]
    inv_freq = ROPE_THETA ** (-jnp.arange(0, HEAD_DIM, 2, dtype=f32) / HEAD_DIM)
    invf = jnp.tile(inv_freq, LANES // (HEAD_DIM // 2))[None, :]
    sgn = jnp.tile(jnp.concatenate([-jnp.ones(HEAD_DIM // 2, f32), jnp.ones(HEAD_DIM // 2, f32)]),
                   LANES // HEAD_DIM)[None, :]
    head_of_lane = np.arange(MXU_DIM) // HEAD_DIM
    ones = jnp.asarray(head_of_lane[:, None] == head_of_lane[None, :], _BF16)
    sinks_k = sinks.astype(f32)
    attn_nw = attn_out_norm_w.astype(f32)[head_cols][None, :]
    wo_attn = w_out[:ATTN_W][head_cols].astype(_BF16)
    wo_ssm = w_out[ATTN_W:].astype(_BF16)

    x2 = x.reshape(t, D_MODEL)
    pos2 = positions.reshape(t, 1).astype(jnp.int32)

    tm = _pick_tile(seq, 512, BLOCK)
    q, k, v, za, ucat, zs = _in_proj(x2, pos2, w_in_k, qw, kw, invf, sgn, ones, tm)

    o_attn = _attention(sinks_k, q, k, v, za, attn_nw, batch, seq)

    nc = _pick_tile(seq // SSM_Q, 128, 16)
    assert nc & (nc - 1) == 0, "chunk rows per tile must be a power of two for the scan"
    w1, w2, decay = _ssm_weights(a_re, a_im, log_step, b_re, b_im, c_re, c_im, d_skip,
                                 n_scan=int(math.log2(nc)))
    y = _ssm(ucat, w1, w2, decay, batch, seq, nc)

    out = _out_proj(x2, o_attn, y, zs, w_glu.astype(_BF16), b_glu.astype(f32)[None, :],
                    ssm_out_norm_w.astype(f32)[None, :], wo_attn, wo_ssm, tm)
    return out.reshape(x.shape)
```

```python
import functools
import math

import jax
import jax.numpy as jnp
import numpy as np
from jax import lax
from jax.experimental import pallas as pl
from jax.experimental.pallas import tpu as pltpu

D_MODEL = 2048
ATTN_W = 1024
KV_W = 256
HEAD_DIM = 64
N_HEADS = 16
N_KV_HEADS = 4
KV_REP = N_HEADS // N_KV_HEADS
SSM_W = 1024
SSM_H = 16
SSM_G = 64
SSM_P = 64
BLOCK = 128
ROPE_THETA = 10000.0
NORM_EPS = 1e-6
IN_W = 2 * ATTN_W + 2 * KV_W + 2 * SSM_W

LANES = 128
MXU_DIM = 256
SSM_Q = 8
SLAB_GROUPS = LANES // SSM_H
N_SLABS = SSM_W // LANES
SLAB_IN = SSM_Q * LANES
SLAB_STATE = 2 * SLAB_GROUPS * SSM_P
VMEM_LIMIT = 56 * 1024 * 1024

_BF16 = jnp.bfloat16
_F32 = jnp.float32


def _pair_heads(a, axis):
    shape = a.shape
    a = a.reshape(shape[:axis] + (N_KV_HEADS // 2, 2, KV_REP, HEAD_DIM) + shape[axis + 1:])
    a = jnp.swapaxes(a, axis + 1, axis + 2)
    return a.reshape(shape)


def _resident(shape):
    nd = len(shape)
    return pl.BlockSpec(shape, lambda *_: (0,) * nd, pipeline_mode=pl.Buffered(1))


def _rope(t, cos, sin_signed):
    lane = lax.broadcasted_iota(jnp.int32, t.shape, 1)
    first_half = (lane % HEAD_DIM) < (HEAD_DIM // 2)
    swapped = jnp.where(first_half,
                        pltpu.roll(t, LANES - HEAD_DIM // 2, axis=1),
                        pltpu.roll(t, HEAD_DIM // 2, axis=1))
    return t * cos + swapped * sin_signed


def _in_proj_kernel(x_ref, pos_ref, w_ref, qw_ref, kw_ref, invf_ref, sgn_ref, ones_ref,
                    q_ref, k_ref, v_ref, za_ref, ucat_ref, zs_ref, hn_ref, u_ref):
    tm = x_ref.shape[0]
    x = x_ref[...]
    rs = lax.rsqrt(jnp.mean(x * x, axis=-1, keepdims=True) + NORM_EPS)
    hn_ref[...] = (x * rs).astype(_BF16)

    nsub = tm // LANES
    ang = jnp.concatenate(
        [jnp.broadcast_to(pos_ref[pl.ds(pl.program_id(0) * nsub + j, 1), :].astype(_F32),
                          (LANES, LANES)).T for j in range(nsub)], axis=0) * invf_ref[...]
    cos = jnp.cos(ang)
    sin_signed = jnp.sin(ang) * sgn_ref[...]

    def proj(c0, width):
        return jnp.dot(hn_ref[...], w_ref[:, c0:c0 + width], preferred_element_type=_F32)

    def normed_rope(c0, nw_ref, n0, out_ref):
        acc = proj(c0, MXU_DIM)
        ssq = jnp.dot((acc * acc).astype(_BF16), ones_ref[...], preferred_element_type=_F32)
        t = acc * lax.rsqrt(ssq * (1.0 / HEAD_DIM) + NORM_EPS) * nw_ref[:, n0:n0 + MXU_DIM]
        for s in range(MXU_DIM // LANES):
            sl = slice(s * LANES, (s + 1) * LANES)
            out_ref[:, n0 + s * LANES:n0 + (s + 1) * LANES] = _rope(t[:, sl], cos, sin_signed).astype(_BF16)

    for c in range(ATTN_W // MXU_DIM):
        normed_rope(c * MXU_DIM, qw_ref, c * MXU_DIM, q_ref)
    normed_rope(ATTN_W, kw_ref, 0, k_ref)
    v_ref[...] = proj(ATTN_W + KV_W, KV_W).astype(_BF16)

    half = ATTN_W // 2
    base = ATTN_W + 2 * KV_W
    for c in range(2):
        za_ref[:, c * half:(c + 1) * half] = proj(base + c * half, half).astype(_BF16)
    base += ATTN_W
    for c in range(2):
        acc = proj(base + c * half, half)
        for s in range(half // LANES):
            u_ref[c * (half // LANES) + s] = acc[:, s * LANES:(s + 1) * LANES]
    base += SSM_W
    for c in range(2):
        zs_ref[:, c * half:(c + 1) * half] = proj(base + c * half, half).astype(_BF16)

    nc = tm // SSM_Q
    for s in range(N_SLABS):
        for i in range(SSM_Q):
            c0 = s * SLAB_IN + i * LANES
            ucat_ref[:, c0:c0 + LANES] = u_ref[s, pl.ds(i, nc, stride=SSM_Q), :].astype(_BF16)


def _in_proj(x2, pos2, w_in, qw, kw, invf, sgn, ones, tm):
    t = x2.shape[0]
    row = lambda w: pl.BlockSpec((tm, w), lambda i: (i, 0))
    return pl.pallas_call(
        _in_proj_kernel,
        grid=(t // tm,),
        in_specs=[row(D_MODEL), _resident(pos2.shape), _resident(w_in.shape), _resident(qw.shape),
                  _resident(kw.shape), _resident(invf.shape), _resident(sgn.shape),
                  _resident(ones.shape)],
        out_specs=[row(ATTN_W), row(KV_W), row(KV_W), row(ATTN_W),
                   pl.BlockSpec((tm // SSM_Q, N_SLABS * SLAB_IN), lambda i: (i, 0)), row(SSM_W)],
        out_shape=[jax.ShapeDtypeStruct((t, ATTN_W), _BF16),
                   jax.ShapeDtypeStruct((t, KV_W), _BF16),
                   jax.ShapeDtypeStruct((t, KV_W), _BF16),
                   jax.ShapeDtypeStruct((t, ATTN_W), _BF16),
                   jax.ShapeDtypeStruct((t // SSM_Q, N_SLABS * SLAB_IN), _BF16),
                   jax.ShapeDtypeStruct((t, SSM_W), _BF16)],
        scratch_shapes=[pltpu.VMEM((tm, D_MODEL), _BF16), pltpu.VMEM((N_SLABS, tm, LANES), _F32)],
        compiler_params=pltpu.CompilerParams(dimension_semantics=("arbitrary",),
                                             vmem_limit_bytes=VMEM_LIMIT),
        name="in_proj",
    )(x2, pos2, w_in, qw, kw, invf, sgn, ones)


def _attn_kernel(sinks_ref, q_ref, kp_ref, kc_ref, vp_ref, vc_ref, za_ref, nw_ref, o_ref, acc_ref):
    n = pl.program_id(1)
    iq = lax.broadcasted_iota(jnp.int32, (BLOCK, 2 * BLOCK), 0)
    ik = lax.broadcasted_iota(jnp.int32, (BLOCK, 2 * BLOCK), 1)
    rel = ik - iq
    allowed = (rel >= 1) & (rel <= BLOCK) & ((ik >= BLOCK) | (n > 0))
    lane_kv = lax.broadcasted_iota(jnp.int32, (2 * BLOCK, LANES), 1)
    lane_o = lax.broadcasted_iota(jnp.int32, (BLOCK, LANES), 1)

    for j in range(N_KV_HEADS // 2):
        sl = slice(j * LANES, (j + 1) * LANES)
        k2 = jnp.concatenate([kp_ref[:, sl], kc_ref[:, sl]], axis=0)
        v2 = jnp.concatenate([vp_ref[:, sl], vc_ref[:, sl]], axis=0)
        zero = jnp.zeros_like(k2)
        qs = jnp.concatenate(
            [q_ref[:, (KV_REP * j + r) * LANES:(KV_REP * j + r + 1) * LANES] for r in range(KV_REP)],
            axis=0)
        probs, rdens = [], []
        for half in range(2):
            keep = (lane_kv < HEAD_DIM) if half == 0 else (lane_kv >= HEAD_DIM)
            s = lax.dot_general(qs, jnp.where(keep, k2, zero), (((1,), (1,)), ((), ())),
                                preferred_element_type=_F32)
            p_rows, rd_rows = [], []
            for r in range(KV_REP):
                sink = sinks_ref[2 * KV_REP * j + KV_REP * half + r]
                sr = jnp.where(allowed, s[r * BLOCK:(r + 1) * BLOCK], -1e30)
                m = jnp.maximum(jnp.max(sr, axis=-1, keepdims=True), sink)
                p = jnp.exp(sr - m)
                den = jnp.sum(p, axis=-1, keepdims=True) + jnp.exp(sink - m)
                p_rows.append(p.astype(_BF16))
                rd_rows.append(1.0 / den)
            probs.append(jnp.concatenate(p_rows, axis=0))
            rdens.append(rd_rows)
        pcat = jnp.concatenate(probs, axis=1)
        vcat = jnp.concatenate([jnp.where(lane_kv < HEAD_DIM, v2, zero),
                                jnp.where(lane_kv >= HEAD_DIM, v2, zero)], axis=0)
        o = jnp.dot(pcat, vcat, preferred_element_type=_F32)
        for r in range(KV_REP):
            scale = jnp.where(lane_o < HEAD_DIM, rdens[0][r], rdens[1][r])
            c0 = (KV_REP * j + r) * LANES
            acc_ref[:, c0:c0 + LANES] = o[r * BLOCK:(r + 1) * BLOCK] * scale

    z = za_ref[...].astype(_F32)
    g = acc_ref[...] * (z * jax.nn.sigmoid(z))
    rs = lax.rsqrt(jnp.mean(g * g, axis=-1, keepdims=True) + NORM_EPS)
    o_ref[...] = (g * rs * nw_ref[...]).astype(_BF16)


def _attention(sinks_perm, q, k, v, za, nw, batch, seq):
    nb = seq // BLOCK
    cur = lambda w: pl.BlockSpec((BLOCK, w), lambda b, n: (b * nb + n, 0))
    prev = lambda w: pl.BlockSpec((BLOCK, w), lambda b, n: (b * nb + jnp.maximum(n - 1, 0), 0))
    return pl.pallas_call(
        _attn_kernel,
        grid=(batch, nb),
        in_specs=[pl.BlockSpec(memory_space=pltpu.SMEM), cur(ATTN_W), prev(KV_W), cur(KV_W),
                  prev(KV_W), cur(KV_W), cur(ATTN_W), _resident(nw.shape)],
        out_specs=cur(ATTN_W),
        out_shape=jax.ShapeDtypeStruct((batch * seq, ATTN_W), _BF16),
        scratch_shapes=[pltpu.VMEM((BLOCK, ATTN_W), _F32)],
        compiler_params=pltpu.CompilerParams(dimension_semantics=("arbitrary", "arbitrary")),
        name="swa_attention",
    )(sinks_perm, q, k, k, v, v, za, nw)


def _ssm_weights(a_re, a_im, log_step, b_re, b_im, c_re, c_im, d_skip, n_scan):
    f32 = _F32
    a_re, a_im = a_re.astype(f32), a_im.astype(f32)
    delta = jnp.exp(log_step.astype(f32))[:, None]
    mag = jnp.exp(a_re * delta)
    lr, li = mag * jnp.cos(a_im * delta), mag * jnp.sin(a_im * delta)
    den = a_re * a_re + a_im * a_im
    fr = ((lr - 1.0) * a_re + li * a_im) / den
    fi = (li * a_re - (lr - 1.0) * a_im) / den
    b_re, b_im = b_re.astype(f32), b_im.astype(f32)
    bbr = fr[..., None] * b_re - fi[..., None] * b_im
    bbi = fr[..., None] * b_im + fi[..., None] * b_re
    c_re, c_im = c_re.astype(f32), c_im.astype(f32)

    pw_r, pw_i = [jnp.ones_like(lr)], [jnp.zeros_like(li)]
    for _ in range(SSM_Q):
        pr, pi = pw_r[-1], pw_i[-1]
        pw_r.append(pr * lr - pi * li)
        pw_i.append(pr * li + pi * lr)
    rev_r = jnp.stack(pw_r[SSM_Q - 1::-1])
    rev_i = jnp.stack(pw_i[SSM_Q - 1::-1])
    pw_r, pw_i = jnp.stack(pw_r), jnp.stack(pw_i)

    cl_r = c_re[None] * pw_r[:, :, None, :] - c_im[None] * pw_i[:, :, None, :]
    cl_i = c_re[None] * pw_i[:, :, None, :] + c_im[None] * pw_r[:, :, None, :]
    kern = (jnp.einsum('dghp,gpk->dghk', cl_r[:SSM_Q], bbr, precision=lax.Precision.HIGHEST)
            - jnp.einsum('dghp,gpk->dghk', cl_i[:SSM_Q], bbi, precision=lax.Precision.HIGHEST))
    eye_h = jnp.eye(SSM_H, dtype=f32)
    kern = kern.at[0].add(d_skip.astype(f32).reshape(SSM_G, SSM_H)[:, :, None] * eye_h[None])

    def per_slab(a):
        a = a.reshape(SSM_Q, N_SLABS, SLAB_GROUPS * SSM_H, a.shape[-1])
        return a.transpose(1, 0, 2, 3)

    kt = per_slab(kern.transpose(0, 1, 3, 2))
    kt = jnp.pad(kt, ((0, 0), (0, 0), (0, 0), (0, LANES - SSM_H)))

    sin_r = rev_r[..., None] * bbr[None] - rev_i[..., None] * bbi[None]
    sin_i = rev_r[..., None] * bbi[None] + rev_i[..., None] * bbr[None]
    cs = per_slab(jnp.concatenate([sin_r, sin_i], axis=2).transpose(0, 1, 3, 2))

    co = per_slab(jnp.concatenate([cl_r[1:], -cl_i[1:]], axis=3))

    ar, ai = pw_r[SSM_Q], pw_i[SSM_Q]
    scan = []
    for _ in range(n_scan):
        scan.append(jnp.concatenate([ar.reshape(N_SLABS, -1), ai.reshape(N_SLABS, -1)], axis=-1))
        ar, ai = ar * ar - ai * ai, 2.0 * ar * ai
    decay = jnp.stack(scan, axis=1)
    return kt.astype(_BF16), cs.astype(_BF16), co.astype(_BF16), decay


def _expand_kernel(kt_ref, cs_ref, co_ref, w1_ref, w2_ref):
    half = SLAB_STATE // 2
    r16 = lax.broadcasted_iota(jnp.int32, (LANES, LANES), 0)
    c16 = lax.broadcasted_iota(jnp.int32, (LANES, LANES), 1)
    rep_h = (r16 == c16 % SSM_H).astype(_BF16)
    diag_h = (r16 // SSM_H) == (c16 // SSM_H)
    rs = lax.broadcasted_iota(jnp.int32, (LANES, SLAB_STATE), 0)
    cst = lax.broadcasted_iota(jnp.int32, (LANES, SLAB_STATE), 1)
    rep_s = ((rs // SSM_P == cst // half) & (rs % SSM_P == cst % SSM_P)).astype(_BF16)
    diag_s = (rs // SSM_H) == ((cst % half) // SSM_P)

    zero = jnp.zeros((LANES, LANES), _BF16)
    blocks = [jnp.where(diag_h, jnp.dot(kt_ref[0, d], rep_h, preferred_element_type=_F32), 0.0).astype(_BF16)
              for d in range(SSM_Q)]
    for i in range(SSM_Q):
        for j in range(SSM_Q):
            w1_ref[0, i * LANES:(i + 1) * LANES, j * LANES:(j + 1) * LANES] = blocks[j - i] if j >= i else zero
        w_in = jnp.where(diag_s, jnp.dot(cs_ref[0, i], rep_s, preferred_element_type=_F32), 0.0)
        w1_ref[0, i * LANES:(i + 1) * LANES, SLAB_IN:] = w_in.astype(_BF16)
        w_out_t = jnp.where(diag_s, jnp.dot(co_ref[0, i], rep_s, preferred_element_type=_F32), 0.0)
        w2_ref[0, :, i * LANES:(i + 1) * LANES] = w_out_t.T.astype(_BF16)


def _expand(kt, cs, co):
    blk = pl.BlockSpec((1, SSM_Q, LANES, LANES), lambda s: (s, 0, 0, 0))
    return pl.pallas_call(
        _expand_kernel,
        grid=(N_SLABS,),
        in_specs=[blk, blk, blk],
        out_specs=[pl.BlockSpec((1, SLAB_IN, SLAB_IN + SLAB_STATE), lambda s: (s, 0, 0)),
                   pl.BlockSpec((1, SLAB_STATE, SLAB_IN), lambda s: (s, 0, 0))],
        out_shape=[jax.ShapeDtypeStruct((N_SLABS, SLAB_IN, SLAB_IN + SLAB_STATE), _BF16),
                   jax.ShapeDtypeStruct((N_SLABS, SLAB_STATE, SLAB_IN), _BF16)],
        compiler_params=pltpu.CompilerParams(dimension_semantics=("arbitrary",)),
        name="s5_expand",
    )(kt, cs, co)


def _ssm_kernel(x_ref, w1_ref, w2_ref, dec_ref, y_ref, h_ref, y_scr):
    nc = x_ref.shape[0]
    half = SLAB_STATE // 2

    @pl.when(pl.program_id(2) == 0)
    def _():
        h_ref[...] = jnp.zeros_like(h_ref)

    r = jnp.dot(x_ref[...], w1_ref[0], preferred_element_type=_F32)
    y_intra = r[:, :SLAB_IN]
    sr, si = r[:, SLAB_IN:SLAB_IN + half], r[:, SLAB_IN + half:]
    hin_r, hin_i = h_ref[0:1, :half], h_ref[0:1, half:]
    row = lax.broadcasted_iota(jnp.int32, (nc, half), 0)

    ar, ai = dec_ref[0, 0:1, :half], dec_ref[0, 0:1, half:]
    sr = sr + jnp.where(row == 0, ar * hin_r - ai * hin_i, 0.0)
    si = si + jnp.where(row == 0, ar * hin_i + ai * hin_r, 0.0)
    for k in range(dec_ref.shape[1]):
        shift = 1 << k
        pr, pi = dec_ref[0, k:k + 1, :half], dec_ref[0, k:k + 1, half:]
        tr = jnp.where(row >= shift, pltpu.roll(sr, shift, axis=0), 0.0)
        ti = jnp.where(row >= shift, pltpu.roll(si, shift, axis=0), 0.0)
        sr, si = sr + pr * tr - pi * ti, si + pr * ti + pi * tr

    h_ref[0:1, :half] = sr[nc - 1:nc]
    h_ref[0:1, half:] = si[nc - 1:nc]
    prev_r = jnp.where(row == 0, hin_r, pltpu.roll(sr, 1, axis=0))
    prev_i = jnp.where(row == 0, hin_i, pltpu.roll(si, 1, axis=0))
    hprev = jnp.concatenate([prev_r, prev_i], axis=1).astype(_BF16)
    y = y_intra + jnp.dot(hprev, w2_ref[0], preferred_element_type=_F32)

    for i in range(SSM_Q):
        y_scr[pl.ds(i, nc, stride=SSM_Q), :] = y[:, i * LANES:(i + 1) * LANES]
    y_ref[...] = y_scr[...].astype(y_ref.dtype)


def _ssm(ucat, w1, w2, decay, batch, seq, nc):
    tiles = seq // (nc * SSM_Q)
    return pl.pallas_call(
        _ssm_kernel,
        grid=(N_SLABS, batch, tiles),
        in_specs=[pl.BlockSpec((nc, SLAB_IN), lambda s, b, t: (b * tiles + t, s)),
                  pl.BlockSpec((1,) + w1.shape[1:], lambda s, b, t: (s, 0, 0)),
                  pl.BlockSpec((1,) + w2.shape[1:], lambda s, b, t: (s, 0, 0)),
                  pl.BlockSpec((1,) + decay.shape[1:], lambda s, b, t: (s, 0, 0))],
        out_specs=pl.BlockSpec((nc * SSM_Q, LANES), lambda s, b, t: (b * tiles + t, s)),
        out_shape=jax.ShapeDtypeStruct((batch * seq, SSM_W), _BF16),
        scratch_shapes=[pltpu.VMEM((8, SLAB_STATE), _F32), pltpu.VMEM((nc * SSM_Q, LANES), _F32)],
        compiler_params=pltpu.CompilerParams(
            dimension_semantics=("arbitrary", "arbitrary", "arbitrary"),
            vmem_limit_bytes=VMEM_LIMIT),
        name="s5_mixer",
    )(ucat, w1, w2, decay)


def _out_kernel(x_ref, oa_ref, y_ref, zs_ref, wg_ref, bg_ref, nw_ref, woa_ref, wos_ref, out_ref):
    y = y_ref[...].astype(_F32)
    y = jax.nn.gelu(y)
    gate = jnp.dot(y.astype(_BF16), wg_ref[...], preferred_element_type=_F32) + bg_ref[...]
    z = zs_ref[...].astype(_F32)
    o = y * jax.nn.sigmoid(gate) * (z * jax.nn.sigmoid(z))
    rs = lax.rsqrt(jnp.mean(o * o, axis=-1, keepdims=True) + NORM_EPS)
    o = (o * rs * nw_ref[...]).astype(_BF16)
    delta = (jnp.dot(oa_ref[...], woa_ref[...], preferred_element_type=_F32)
             + jnp.dot(o, wos_ref[...], preferred_element_type=_F32))
    out_ref[...] = x_ref[...] + delta


def _out_proj(x2, oa, y, zs, wg, bg, nw, woa, wos, tm):
    t = x2.shape[0]
    row = lambda w: pl.BlockSpec((tm, w), lambda i: (i, 0))
    return pl.pallas_call(
        _out_kernel,
        grid=(t // tm,),
        in_specs=[row(D_MODEL), row(ATTN_W), row(SSM_W), row(SSM_W), _resident(wg.shape),
                  _resident(bg.shape), _resident(nw.shape), _resident(woa.shape),
                  _resident(wos.shape)],
        out_specs=row(D_MODEL),
        out_shape=jax.ShapeDtypeStruct((t, D_MODEL), x2.dtype),
        compiler_params=pltpu.CompilerParams(dimension_semantics=("arbitrary",),
                                             vmem_limit_bytes=VMEM_LIMIT),
        name="out_proj",
    )(x2, oa, y, zs, wg, bg, nw, woa, wos)


def _pick_tile(n, target, multiple):
    best = multiple
    for cand in range(multiple, min(n, target) + 1, multiple):
        if n % cand == 0:
            best = cand
    return best


def kernel(x, positions, norm_w, w_in, q_norm_w, k_norm_w, sinks, a_re, a_im, log_step,
           b_re, b_im, c_re, c_im, d_skip, w_glu, b_glu, attn_out_norm_w, ssm_out_norm_w, w_out):
    batch, seq, _ = x.shape
    assert seq % BLOCK == 0 and x.shape[2] == D_MODEL and w_in.shape == (D_MODEL, IN_W)
    t = batch * seq
    f32 = _F32

    w_in_f = norm_w.astype(f32)[:, None] * w_in.astype(f32)
    w_in_k = jnp.concatenate(
        [_pair_heads(w_in_f[:, :ATTN_W], 1), w_in_f[:, ATTN_W:ATTN_W + 2 * KV_W],
         _pair_heads(w_in_f[:, ATTN_W + 2 * KV_W:2 * ATTN_W + 2 * KV_W], 1),
         w_in_f[:, 2 * ATTN_W + 2 * KV_W:]], axis=1).astype(_BF16)
    qw =(jnp.tile(q_norm_w.astype(f32), N_HEADS) * (1.0 / math.sqrt(HEAD_DIM)))[None, :]
    kw = jnp.tile(k_norm_w.astype(f32), N_KV_HEADS)[None, :]
    inv_freq = ROPE_THETA ** (-jnp.arange(0, HEAD_DIM, 2, dtype=f32) / HEAD_DIM)
    invf = jnp.tile(inv_freq, LANES // (HEAD_DIM // 2))[None, :]
    sgn = jnp.tile(jnp.concatenate([-jnp.ones(HEAD_DIM // 2, f32), jnp.ones(HEAD_DIM // 2, f32)]),
                   LANES // HEAD_DIM)[None, :]
    head_of_lane = np.arange(MXU_DIM) // HEAD_DIM
    ones = jnp.asarray(head_of_lane[:, None] == head_of_lane[None, :], _BF16)
    sinks_k = sinks.astype(f32)
    attn_nw = _pair_heads(attn_out_norm_w.astype(f32), 0)[None, :]
    wo_attn = _pair_heads(w_out[:ATTN_W], 0).astype(_BF16)
    wo_ssm = w_out[ATTN_W:].astype(_BF16)

    x2 = x.reshape(t, D_MODEL)
    pos2 = positions.reshape(t // LANES, LANES).astype(jnp.int32)

    tm = _pick_tile(seq, 512, BLOCK)
    q, k, v, za, ucat, zs = _in_proj(x2, pos2, w_in_k, qw, kw, invf, sgn, ones, tm)

    o_attn = _attention(sinks_k, q, k, v, za, attn_nw, batch, seq)

    nc = _pick_tile(seq // SSM_Q, 128, 16)
    assert nc & (nc - 1) == 0, "chunk rows per tile must be a power of two for the scan"
    kt, cs, co, decay = _ssm_weights(a_re, a_im, log_step, b_re, b_im, c_re, c_im, d_skip,
                                     n_scan=int(math.log2(nc)))
    w1, w2 = _expand(kt, cs, co)
    y = _ssm(ucat, w1, w2, decay, batch, seq, nc)

    out = _out_proj(x2, o_attn, y, zs, w_glu.astype(_BF16), b_glu.astype(f32)[None, :],
                    ssm_out_norm_w.astype(f32)[None, :], wo_attn, wo_ssm, tm)
    return out.reshape(x.shape)
```

```python
import functools
import math

import jax
import jax.numpy as jnp
import numpy as np
from jax import lax
from jax.experimental import pallas as pl
from jax.experimental.pallas import tpu as pltpu

D_MODEL = 2048
ATTN_W = 1024
KV_W = 256
HEAD_DIM = 64
N_HEADS = 16
N_KV_HEADS = 4
KV_REP = N_HEADS // N_KV_HEADS
SSM_W = 1024
SSM_H = 16
SSM_G = 64
SSM_P = 64
BLOCK = 128
ROPE_THETA = 10000.0
NORM_EPS = 1e-6
IN_W = 2 * ATTN_W + 2 * KV_W + 2 * SSM_W

LANES = 128
MXU_DIM = 256
SSM_Q = 8
SLAB_GROUPS = LANES // SSM_H
N_SLABS = SSM_W // LANES
SLAB_IN = SSM_Q * LANES
SLAB_STATE = 2 * SLAB_GROUPS * SSM_P
VMEM_LIMIT = 56 * 1024 * 1024

_BF16 = jnp.bfloat16
_F32 = jnp.float32


def _pair_heads(a, axis):
    shape = a.shape
    a = a.reshape(shape[:axis] + (N_KV_HEADS // 2, 2, KV_REP, HEAD_DIM) + shape[axis + 1:])
    a = jnp.swapaxes(a, axis + 1, axis + 2)
    return a.reshape(shape)


def _resident(shape):
    nd = len(shape)
    return pl.BlockSpec(shape, lambda *_: (0,) * nd, pipeline_mode=pl.Buffered(1))


def _rope(t, cos, sin_signed):
    lane = lax.broadcasted_iota(jnp.int32, t.shape, 1)
    first_half = (lane % HEAD_DIM) < (HEAD_DIM // 2)
    swapped = jnp.where(first_half,
                        pltpu.roll(t, LANES - HEAD_DIM // 2, axis=1),
                        pltpu.roll(t, HEAD_DIM // 2, axis=1))
    return t * cos + swapped * sin_signed


def _in_proj_kernel(x_ref, pos_ref, w_ref, qw_ref, kw_ref, invf_ref, sgn_ref, ones_ref,
                    q_ref, k_ref, v_ref, za_ref, ucat_ref, zs_ref, hn_ref, u_ref):
    tm = x_ref.shape[0]
    x = x_ref[...]
    hn_ref[...] = x.astype(_BF16)
    rs = jnp.broadcast_to(lax.rsqrt(jnp.mean(x * x, axis=-1, keepdims=True) + NORM_EPS), (tm, LANES))
    wide = 2 * MXU_DIM
    rs_wide = jnp.concatenate([rs] * (wide // LANES), axis=1)

    nsub = tm // LANES
    ang = jnp.concatenate(
        [jnp.broadcast_to(pos_ref[pl.ds(pl.program_id(0) * nsub + j, 1), :].astype(_F32),
                          (LANES, LANES)).T for j in range(nsub)], axis=0) * invf_ref[...]
    cos = jnp.cos(ang)
    sin_signed = jnp.sin(ang) * sgn_ref[...]

    def proj(c0):
        return jnp.dot(hn_ref[...], w_ref[:, c0:c0 + wide], preferred_element_type=_F32) * rs_wide

    def normed_rope(acc, nw, out_ref, n0):
        ssq = jnp.dot((acc * acc).astype(_BF16), ones_ref[...], preferred_element_type=_F32)
        t = acc * lax.rsqrt(ssq * (1.0 / HEAD_DIM) + NORM_EPS) * nw
        for s in range(MXU_DIM // LANES):
            sl = slice(s * LANES, (s + 1) * LANES)
            out_ref[:, n0 + s * LANES:n0 + (s + 1) * LANES] = _rope(t[:, sl], cos, sin_signed).astype(_BF16)

    base = 2 * ATTN_W + 2 * KV_W
    nc = tm // SSM_Q
    for c in range(SSM_W // wide):
        acc = proj(base + c * wide)
        for s in range(wide // LANES):
            u_ref[c * (wide // LANES) + s] = acc[:, s * LANES:(s + 1) * LANES]
    for s in range(N_SLABS):
        for i in range(SSM_Q):
            c0 = s * SLAB_IN + i * LANES
            ucat_ref[:, c0:c0 + LANES] = u_ref[s, pl.ds(i, nc, stride=SSM_Q), :].astype(_BF16)

    for c in range(ATTN_W // wide):
        acc = proj(c * wide)
        for h in range(wide // MXU_DIM):
            n0 = c * wide + h * MXU_DIM
            normed_rope(acc[:, h * MXU_DIM:(h + 1) * MXU_DIM], qw_ref[:, n0:n0 + MXU_DIM], q_ref, n0)
    acc = proj(ATTN_W)
    normed_rope(acc[:, :KV_W], kw_ref[...], k_ref, 0)
    v_ref[...] = acc[:, KV_W:].astype(_BF16)
    for out_ref, base in ((za_ref, ATTN_W + 2 * KV_W), (zs_ref, 2 * ATTN_W + 2 * KV_W + SSM_W)):
        for c in range(ATTN_W // wide):
            z = proj(base + c * wide)
            out_ref[:, c * wide:(c + 1) * wide] = (z * jax.nn.sigmoid(z)).astype(_BF16)


def _in_proj(x2, pos2, w_in, qw, kw, invf, sgn, ones, tm):
    t = x2.shape[0]
    row = lambda w: pl.BlockSpec((tm, w), lambda i: (i, 0))
    return pl.pallas_call(
        _in_proj_kernel,
        grid=(t // tm,),
        in_specs=[row(D_MODEL), _resident(pos2.shape), _resident(w_in.shape), _resident(qw.shape),
                  _resident(kw.shape), _resident(invf.shape), _resident(sgn.shape),
                  _resident(ones.shape)],
        out_specs=[row(ATTN_W), row(KV_W), row(KV_W), row(ATTN_W),
                   pl.BlockSpec((tm // SSM_Q, N_SLABS * SLAB_IN), lambda i: (i, 0)), row(SSM_W)],
        out_shape=[jax.ShapeDtypeStruct((t, ATTN_W), _BF16),
                   jax.ShapeDtypeStruct((t, KV_W), _BF16),
                   jax.ShapeDtypeStruct((t, KV_W), _BF16),
                   jax.ShapeDtypeStruct((t, ATTN_W), _BF16),
                   jax.ShapeDtypeStruct((t // SSM_Q, N_SLABS * SLAB_IN), _BF16),
                   jax.ShapeDtypeStruct((t, SSM_W), _BF16)],
        scratch_shapes=[pltpu.VMEM((tm, D_MODEL), _BF16), pltpu.VMEM((N_SLABS, tm, LANES), _F32)],
        compiler_params=pltpu.CompilerParams(dimension_semantics=("arbitrary",),
                                             vmem_limit_bytes=VMEM_LIMIT),
        name="in_proj",
    )(x2, pos2, w_in, qw, kw, invf, sgn, ones)


def _attn_kernel(sinks_ref, q_ref, kp_ref, kc_ref, vp_ref, vc_ref, za_ref, nw_ref, o_ref, acc_ref):
    n = pl.program_id(1)
    iq = lax.broadcasted_iota(jnp.int32, (BLOCK, 2 * BLOCK), 0)
    ik = lax.broadcasted_iota(jnp.int32, (BLOCK, 2 * BLOCK), 1)
    rel = ik - iq
    allowed = (rel >= 1) & (rel <= BLOCK) & ((ik >= BLOCK) | (n > 0))
    lane_kv = lax.broadcasted_iota(jnp.int32, (2 * BLOCK, LANES), 1)
    lane_o = lax.broadcasted_iota(jnp.int32, (BLOCK, LANES), 1)

    for j in range(N_KV_HEADS // 2):
        sl = slice(j * LANES, (j + 1) * LANES)
        k2 = jnp.concatenate([kp_ref[:, sl], kc_ref[:, sl]], axis=0)
        v2 = jnp.concatenate([vp_ref[:, sl], vc_ref[:, sl]], axis=0)
        zero = jnp.zeros_like(k2)
        qs = jnp.concatenate(
            [q_ref[:, (KV_REP * j + r) * LANES:(KV_REP * j + r + 1) * LANES] for r in range(KV_REP)],
            axis=0)
        probs, rdens = [], []
        for half in range(2):
            keep = (lane_kv < HEAD_DIM) if half == 0 else (lane_kv >= HEAD_DIM)
            s = lax.dot_general(qs, jnp.where(keep, k2, zero), (((1,), (1,)), ((), ())),
                                preferred_element_type=_F32)
            p_rows, rd_rows = [], []
            for r in range(KV_REP):
                sink = sinks_ref[2 * KV_REP * j + KV_REP * half + r]
                sr = jnp.where(allowed, s[r * BLOCK:(r + 1) * BLOCK], -1e30)
                m = jnp.maximum(jnp.max(sr, axis=-1, keepdims=True), sink)
                p = jnp.exp(sr - m)
                den = jnp.sum(p, axis=-1, keepdims=True) + jnp.exp(sink - m)
                p_rows.append(p.astype(_BF16))
                rd_rows.append(1.0 / den)
            probs.append(jnp.concatenate(p_rows, axis=0))
            rdens.append(rd_rows)
        pcat = jnp.concatenate(probs, axis=1)
        vcat = jnp.concatenate([jnp.where(lane_kv < HEAD_DIM, v2, zero),
                                jnp.where(lane_kv >= HEAD_DIM, v2, zero)], axis=0)
        o = jnp.dot(pcat, vcat, preferred_element_type=_F32)
        for r in range(KV_REP):
            scale = jnp.where(lane_o < HEAD_DIM, rdens[0][r], rdens[1][r])
            c0 = (KV_REP * j + r) * LANES
            acc_ref[:, c0:c0 + LANES] = o[r * BLOCK:(r + 1) * BLOCK] * scale

    g = acc_ref[...] * za_ref[...].astype(_F32)
    rs = lax.rsqrt(jnp.mean(g * g, axis=-1, keepdims=True) + NORM_EPS)
    o_ref[...] = (g * rs * nw_ref[...]).astype(_BF16)


def _attention(sinks_perm, q, k, v, za, nw, batch, seq):
    nb = seq // BLOCK
    cur = lambda w: pl.BlockSpec((BLOCK, w), lambda b, n: (b * nb + n, 0))
    prev = lambda w: pl.BlockSpec((BLOCK, w), lambda b, n: (b * nb + jnp.maximum(n - 1, 0), 0))
    return pl.pallas_call(
        _attn_kernel,
        grid=(batch, nb),
        in_specs=[pl.BlockSpec(memory_space=pltpu.SMEM), cur(ATTN_W), prev(KV_W), cur(KV_W),
                  prev(KV_W), cur(KV_W), cur(ATTN_W), _resident(nw.shape)],
        out_specs=cur(ATTN_W),
        out_shape=jax.ShapeDtypeStruct((batch * seq, ATTN_W), _BF16),
        scratch_shapes=[pltpu.VMEM((BLOCK, ATTN_W), _F32)],
        compiler_params=pltpu.CompilerParams(dimension_semantics=("arbitrary", "arbitrary")),
        name="swa_attention",
    )(sinks_perm, q, k, k, v, v, za, nw)


def _ssm_weights(a_re, a_im, log_step, b_re, b_im, c_re, c_im, d_skip, n_scan):
    f32 = _F32
    a_re, a_im = a_re.astype(f32), a_im.astype(f32)
    delta = jnp.exp(log_step.astype(f32))[:, None]
    mag = jnp.exp(a_re * delta)
    lr, li = mag * jnp.cos(a_im * delta), mag * jnp.sin(a_im * delta)
    den = a_re * a_re + a_im * a_im
    fr = ((lr - 1.0) * a_re + li * a_im) / den
    fi = (li * a_re - (lr - 1.0) * a_im) / den
    b_re, b_im = b_re.astype(f32), b_im.astype(f32)
    bbr = fr[..., None] * b_re - fi[..., None] * b_im
    bbi = fr[..., None] * b_im + fi[..., None] * b_re
    c_re, c_im = c_re.astype(f32), c_im.astype(f32)

    pw_r, pw_i = [jnp.ones_like(lr)], [jnp.zeros_like(li)]
    for _ in range(SSM_Q):
        pr, pi = pw_r[-1], pw_i[-1]
        pw_r.append(pr * lr - pi * li)
        pw_i.append(pr * li + pi * lr)
    rev_r = jnp.stack(pw_r[SSM_Q - 1::-1])
    rev_i = jnp.stack(pw_i[SSM_Q - 1::-1])
    pw_r, pw_i = jnp.stack(pw_r), jnp.stack(pw_i)

    cl_r = c_re[None] * pw_r[:, :, None, :] - c_im[None] * pw_i[:, :, None, :]
    cl_i = c_re[None] * pw_i[:, :, None, :] + c_im[None] * pw_r[:, :, None, :]
    kern = (jnp.einsum('dghp,gpk->dghk', cl_r[:SSM_Q], bbr, precision=lax.Precision.HIGHEST)
            - jnp.einsum('dghp,gpk->dghk', cl_i[:SSM_Q], bbi, precision=lax.Precision.HIGHEST))
    eye_h = jnp.eye(SSM_H, dtype=f32)
    kern = kern.at[0].add(d_skip.astype(f32).reshape(SSM_G, SSM_H)[:, :, None] * eye_h[None])

    def per_slab(a):
        a = a.reshape(SSM_Q, N_SLABS, SLAB_GROUPS * SSM_H, a.shape[-1])
        return a.transpose(1, 0, 2, 3)

    kt = per_slab(kern.transpose(0, 1, 3, 2))
    kt = jnp.pad(kt, ((0, 0), (0, 0), (0, 0), (0, LANES - SSM_H)))

    sin_r = rev_r[..., None] * bbr[None] - rev_i[..., None] * bbi[None]
    sin_i = rev_r[..., None] * bbi[None] + rev_i[..., None] * bbr[None]
    cs = per_slab(jnp.concatenate([sin_r, sin_i], axis=2).transpose(0, 1, 3, 2))

    co = per_slab(jnp.concatenate([cl_r[1:], -cl_i[1:]], axis=3))

    ar, ai = pw_r[SSM_Q], pw_i[SSM_Q]
    scan = []
    for _ in range(n_scan):
        scan.append(jnp.concatenate([ar.reshape(N_SLABS, -1), ai.reshape(N_SLABS, -1)], axis=-1))
        ar, ai = ar * ar - ai * ai, 2.0 * ar * ai
    decay = jnp.stack(scan, axis=1)
    return kt.astype(_BF16), cs.astype(_BF16), co.astype(_BF16), decay


def _expand_kernel(kt_ref, cs_ref, co_ref, w1_ref, w2_ref):
    half = SLAB_STATE // 2
    r16 = lax.broadcasted_iota(jnp.int32, (LANES, LANES), 0)
    c16 = lax.broadcasted_iota(jnp.int32, (LANES, LANES), 1)
    rep_h = (r16 == c16 % SSM_H).astype(_BF16)
    diag_h = (r16 // SSM_H) == (c16 // SSM_H)
    rs = lax.broadcasted_iota(jnp.int32, (LANES, SLAB_STATE), 0)
    cst = lax.broadcasted_iota(jnp.int32, (LANES, SLAB_STATE), 1)
    rep_s = ((rs // SSM_P == cst // half) & (rs % SSM_P == cst % SSM_P)).astype(_BF16)
    diag_s = (rs // SSM_H) == ((cst % half) // SSM_P)

    zero = jnp.zeros((LANES, LANES), _BF16)
    blocks = [jnp.where(diag_h, jnp.dot(kt_ref[0, d], rep_h, preferred_element_type=_F32), 0.0).astype(_BF16)
              for d in range(SSM_Q)]
    for i in range(SSM_Q):
        for j in range(SSM_Q):
            w1_ref[0, i * LANES:(i + 1) * LANES, j * LANES:(j + 1) * LANES] = blocks[j - i] if j >= i else zero
        w_in = jnp.where(diag_s, jnp.dot(cs_ref[0, i], rep_s, preferred_element_type=_F32), 0.0)
        w1_ref[0, i * LANES:(i + 1) * LANES, SLAB_IN:] = w_in.astype(_BF16)
        w_out_t = jnp.where(diag_s, jnp.dot(co_ref[0, i], rep_s, preferred_element_type=_F32), 0.0)
        w2_ref[0, :, i * LANES:(i + 1) * LANES] = w_out_t.T.astype(_BF16)


def _expand(kt, cs, co):
    blk = pl.BlockSpec((1, SSM_Q, LANES, LANES), lambda s: (s, 0, 0, 0))
    return pl.pallas_call(
        _expand_kernel,
        grid=(N_SLABS,),
        in_specs=[blk, blk, blk],
        out_specs=[pl.BlockSpec((1, SLAB_IN, SLAB_IN + SLAB_STATE), lambda s: (s, 0, 0)),
                   pl.BlockSpec((1, SLAB_STATE, SLAB_IN), lambda s: (s, 0, 0))],
        out_shape=[jax.ShapeDtypeStruct((N_SLABS, SLAB_IN, SLAB_IN + SLAB_STATE), _BF16),
                   jax.ShapeDtypeStruct((N_SLABS, SLAB_STATE, SLAB_IN), _BF16)],
        compiler_params=pltpu.CompilerParams(dimension_semantics=("arbitrary",)),
        name="s5_expand",
    )(kt, cs, co)


def _ssm_kernel(x_ref, w1_ref, w2_ref, dec_ref, y_ref, h_ref, y_scr):
    nc = x_ref.shape[0]
    half = SLAB_STATE // 2

    @pl.when(pl.program_id(2) == 0)
    def _():
        h_ref[...] = jnp.zeros_like(h_ref)

    r = jnp.dot(x_ref[...], w1_ref[0], preferred_element_type=_F32)
    y_intra = r[:, :SLAB_IN]
    sr, si = r[:, SLAB_IN:SLAB_IN + half], r[:, SLAB_IN + half:]
    hin_r, hin_i = h_ref[0:1, :half], h_ref[0:1, half:]
    row = lax.broadcasted_iota(jnp.int32, (nc, half), 0)

    ar, ai = dec_ref[0, 0:1, :half], dec_ref[0, 0:1, half:]
    sr = sr + jnp.where(row == 0, ar * hin_r - ai * hin_i, 0.0)
    si = si + jnp.where(row == 0, ar * hin_i + ai * hin_r, 0.0)
    for k in range(dec_ref.shape[1]):
        shift = 1 << k
        pr, pi = dec_ref[0, k:k + 1, :half], dec_ref[0, k:k + 1, half:]
        tr = jnp.where(row >= shift, pltpu.roll(sr, shift, axis=0), 0.0)
        ti = jnp.where(row >= shift, pltpu.roll(si, shift, axis=0), 0.0)
        sr, si = sr + pr * tr - pi * ti, si + pr * ti + pi * tr

    h_ref[0:1, :half] = sr[nc - 1:nc]
    h_ref[0:1, half:] = si[nc - 1:nc]
    prev_r = jnp.where(row == 0, hin_r, pltpu.roll(sr, 1, axis=0))
    prev_i = jnp.where(row == 0, hin_i, pltpu.roll(si, 1, axis=0))
    hprev = jnp.concatenate([prev_r, prev_i], axis=1).astype(_BF16)
    y = y_intra + jnp.dot(hprev, w2_ref[0], preferred_element_type=_F32)

    for i in range(SSM_Q):
        y_scr[pl.ds(i, nc, stride=SSM_Q), :] = y[:, i * LANES:(i + 1) * LANES]
    y_ref[...] = y_scr[...].astype(y_ref.dtype)


def _ssm(ucat, w1, w2, decay, batch, seq, nc):
    tiles = seq // (nc * SSM_Q)
    return pl.pallas_call(
        _ssm_kernel,
        grid=(N_SLABS, batch, tiles),
        in_specs=[pl.BlockSpec((nc, SLAB_IN), lambda s, b, t: (b * tiles + t, s)),
                  pl.BlockSpec((1,) + w1.shape[1:], lambda s, b, t: (s, 0, 0)),
                  pl.BlockSpec((1,) + w2.shape[1:], lambda s, b, t: (s, 0, 0)),
                  pl.BlockSpec((1,) + decay.shape[1:], lambda s, b, t: (s, 0, 0))],
        out_specs=pl.BlockSpec((nc * SSM_Q, LANES), lambda s, b, t: (b * tiles + t, s)),
        out_shape=jax.ShapeDtypeStruct((batch * seq, SSM_W), _BF16),
        scratch_shapes=[pltpu.VMEM((8, SLAB_STATE), _F32), pltpu.VMEM((nc * SSM_Q, LANES), _F32)],
        compiler_params=pltpu.CompilerParams(
            dimension_semantics=("arbitrary", "arbitrary", "arbitrary"),
            vmem_limit_bytes=VMEM_LIMIT),
        name="s5_mixer",
    )(ucat, w1, w2, decay)


def _gelu_tanh(y):
    c = math.sqrt(2.0 / math.pi)
    return (0.5 * y) * (1.0 + jnp.tanh(y * (c + (c * 0.044715) * (y * y))))


def _out_kernel(x_ref, oa_ref, y_ref, zs_ref, wg_ref, bg_ref, woa_ref, wos_ref, out_ref):
    y = _gelu_tanh(y_ref[...].astype(_F32))
    gate = jnp.dot(y.astype(_BF16), wg_ref[...], preferred_element_type=_F32) + bg_ref[...]
    o = y * jax.nn.sigmoid(gate) * zs_ref[...].astype(_F32)
    rs = lax.rsqrt(jnp.mean(o * o, axis=-1, keepdims=True) + NORM_EPS)
    out_ref[...] = (x_ref[...] + jnp.dot(oa_ref[...], woa_ref[...], preferred_element_type=_F32)
                    + rs * jnp.dot(o.astype(_BF16), wos_ref[...], preferred_element_type=_F32))


def _out_proj(x2, oa, y, zs, wg, bg, woa, wos, tm):
    t = x2.shape[0]
    row = lambda w: pl.BlockSpec((tm, w), lambda i: (i, 0))
    return pl.pallas_call(
        _out_kernel,
        grid=(t // tm,),
        in_specs=[row(D_MODEL), row(ATTN_W), row(SSM_W), row(SSM_W), _resident(wg.shape),
                  _resident(bg.shape), _resident(woa.shape), _resident(wos.shape)],
        out_specs=row(D_MODEL),
        out_shape=jax.ShapeDtypeStruct((t, D_MODEL), x2.dtype),
        compiler_params=pltpu.CompilerParams(dimension_semantics=("arbitrary",),
                                             vmem_limit_bytes=VMEM_LIMIT),
        name="out_proj",
    )(x2, oa, y, zs, wg, bg, woa, wos)


def _pick_tile(n, target, multiple):
    best = multiple
    for cand in range(multiple, min(n, target) + 1, multiple):
        if n % cand == 0:
            best = cand
    return best


def kernel(x, positions, norm_w, w_in, q_norm_w, k_norm_w, sinks, a_re, a_im, log_step,
           b_re, b_im, c_re, c_im, d_skip, w_glu, b_glu, attn_out_norm_w, ssm_out_norm_w, w_out):
    batch, seq, _ = x.shape
    assert seq % BLOCK == 0 and x.shape[2] == D_MODEL and w_in.shape == (D_MODEL, IN_W)
    t = batch * seq
    f32 = _F32

    w_in_f = norm_w.astype(f32)[:, None] * w_in.astype(f32)
    w_in_k = jnp.concatenate(
        [_pair_heads(w_in_f[:, :ATTN_W], 1), w_in_f[:, ATTN_W:ATTN_W + 2 * KV_W],
         _pair_heads(w_in_f[:, ATTN_W + 2 * KV_W:2 * ATTN_W + 2 * KV_W], 1),
         w_in_f[:, 2 * ATTN_W + 2 * KV_W:]], axis=1).astype(_BF16)
    qw =(jnp.tile(q_norm_w.astype(f32), N_HEADS) * (1.0 / math.sqrt(HEAD_DIM)))[None, :]
    kw = jnp.tile(k_norm_w.astype(f32), N_KV_HEADS)[None, :]
    inv_freq = ROPE_THETA ** (-jnp.arange(0, HEAD_DIM, 2, dtype=f32) / HEAD_DIM)
    invf = jnp.tile(inv_freq, LANES // (HEAD_DIM // 2))[None, :]
    sgn = jnp.tile(jnp.concatenate([-jnp.ones(HEAD_DIM // 2, f32), jnp.ones(HEAD_DIM // 2, f32)]),
                   LANES // HEAD_DIM)[None, :]
    head_of_lane = np.arange(MXU_DIM) // HEAD_DIM
    ones = jnp.asarray(head_of_lane[:, None] == head_of_lane[None, :], _BF16)
    sinks_k = sinks.astype(f32)
    attn_nw = _pair_heads(attn_out_norm_w.astype(f32), 0)[None, :]
    wo_attn = _pair_heads(w_out[:ATTN_W], 0).astype(_BF16)
    wo_ssm = (ssm_out_norm_w.astype(f32)[:, None] * w_out[ATTN_W:].astype(f32)).astype(_BF16)

    x2 = x.reshape(t, D_MODEL)
    pos2 = positions.reshape(t // LANES, LANES).astype(jnp.int32)

    tm = _pick_tile(seq, 512, BLOCK)
    q, k, v, za, ucat, zs = _in_proj(x2, pos2, w_in_k, qw, kw, invf, sgn, ones, tm)

    o_attn = _attention(sinks_k, q, k, v, za, attn_nw, batch, seq)

    nc = _pick_tile(seq // SSM_Q, 128, 16)
    assert nc & (nc - 1) == 0, "chunk rows per tile must be a power of two for the scan"
    kt, cs, co, decay = _ssm_weights(a_re, a_im, log_step, b_re, b_im, c_re, c_im, d_skip,
                                     n_scan=int(math.log2(nc)))
    w1, w2 = _expand(kt, cs, co)
    y = _ssm(ucat, w1, w2, decay, batch, seq, nc)

    out = _out_proj(x2, o_attn, y, zs, w_glu.astype(_BF16), b_glu.astype(f32)[None, :],
                    wo_attn, wo_ssm, tm)
    return out.reshape(x.shape)
```

```python
import functools
import math

import jax
import jax.numpy as jnp
import numpy as np
from jax import lax
from jax.experimental import pallas as pl
from jax.experimental.pallas import tpu as pltpu

D_MODEL = 2048
ATTN_W = 1024
KV_W = 256
HEAD_DIM = 64
N_HEADS = 16
N_KV_HEADS = 4
KV_REP = N_HEADS // N_KV_HEADS
SSM_W = 1024
SSM_H = 16
SSM_G = 64
SSM_P = 64
BLOCK = 128
ROPE_THETA = 10000.0
NORM_EPS = 1e-6
IN_W = 2 * ATTN_W + 2 * KV_W + 2 * SSM_W

LANES = 128
SUBLANES = 8
MXU_DIM = 256
SSM_Q = 8
SLAB_GROUPS = LANES // SSM_H
N_SLABS = SSM_W // LANES
SLAB_IN = SSM_Q * LANES
SLAB_STATE = 2 * SLAB_GROUPS * SSM_P
VMEM_LIMIT = 56 * 1024 * 1024

_BF16 = jnp.bfloat16
_F32 = jnp.float32


def _pair_heads(a, axis):
    shape = a.shape
    a = a.reshape(shape[:axis] + (N_KV_HEADS // 2, 2, KV_REP, HEAD_DIM) + shape[axis + 1:])
    a = jnp.swapaxes(a, axis + 1, axis + 2)
    return a.reshape(shape)


def _resident(shape):
    nd = len(shape)
    return pl.BlockSpec(shape, lambda *_: (0,) * nd, pipeline_mode=pl.Buffered(1))


def _rope(t, cos, sin_signed):
    lane = lax.broadcasted_iota(jnp.int32, t.shape, 1)
    first_half = (lane % HEAD_DIM) < (HEAD_DIM // 2)
    swapped = jnp.where(first_half,
                        pltpu.roll(t, LANES - HEAD_DIM // 2, axis=1),
                        pltpu.roll(t, HEAD_DIM // 2, axis=1))
    return t * cos + swapped * sin_signed


def _in_proj_kernel(x_ref, pos_ref, w_ref, qw_ref, kw_ref, invf_ref, sgn_ref, ones_ref,
                    q_ref, k_ref, v_ref, za_ref, ucat_ref, zs_ref, hn_ref, u_ref):
    tm = x_ref.shape[0]
    x = x_ref[...]
    hn_ref[...] = x.astype(_BF16)
    rs = jnp.broadcast_to(lax.rsqrt(jnp.mean(x * x, axis=-1, keepdims=True) + NORM_EPS), (tm, LANES))
    wide = 2 * MXU_DIM
    rs_wide = jnp.concatenate([rs] * (wide // LANES), axis=1)

    nsub = tm // LANES
    ang = jnp.concatenate(
        [jnp.broadcast_to(pos_ref[pl.ds(pl.program_id(0) * nsub + j, 1), :].astype(_F32),
                          (LANES, LANES)).T for j in range(nsub)], axis=0) * invf_ref[...]
    cos = jnp.cos(ang)
    sin_signed = jnp.sin(ang) * sgn_ref[...]

    def proj(c0):
        return jnp.dot(hn_ref[...], w_ref[:, c0:c0 + wide], preferred_element_type=_F32) * rs_wide

    def normed_rope(acc, nw, out_ref, n0):
        ssq = jnp.dot((acc * acc).astype(_BF16), ones_ref[...], preferred_element_type=_F32)
        t = acc * lax.rsqrt(ssq * (1.0 / HEAD_DIM) + NORM_EPS) * nw
        for s in range(MXU_DIM // LANES):
            sl = slice(s * LANES, (s + 1) * LANES)
            out_ref[:, n0 + s * LANES:n0 + (s + 1) * LANES] = _rope(t[:, sl], cos, sin_signed).astype(_BF16)

    base = 2 * ATTN_W + 2 * KV_W
    nc = tm // SSM_Q
    for c in range(SSM_W // wide):
        acc = proj(base + c * wide)
        for s in range(wide // LANES):
            u_ref[c * (wide // LANES) + s] = acc[:, s * LANES:(s + 1) * LANES]
    for s in range(N_SLABS):
        for i in range(SSM_Q):
            c0 = s * SLAB_IN + i * LANES
            ucat_ref[:, c0:c0 + LANES] = u_ref[s, pl.ds(i, nc, stride=SSM_Q), :].astype(_BF16)

    for c in range(ATTN_W // wide):
        acc = proj(c * wide)
        for h in range(wide // MXU_DIM):
            n0 = c * wide + h * MXU_DIM
            normed_rope(acc[:, h * MXU_DIM:(h + 1) * MXU_DIM], qw_ref[:, n0:n0 + MXU_DIM], q_ref, n0)
    acc = proj(ATTN_W)
    normed_rope(acc[:, :KV_W], kw_ref[...], k_ref, 0)
    v_ref[...] = acc[:, KV_W:].astype(_BF16)
    for out_ref, base in ((za_ref, ATTN_W + 2 * KV_W), (zs_ref, 2 * ATTN_W + 2 * KV_W + SSM_W)):
        for c in range(ATTN_W // wide):
            z = proj(base + c * wide)
            out_ref[:, c * wide:(c + 1) * wide] = (z * jax.nn.sigmoid(z)).astype(_BF16)


def _in_proj(x2, pos2, w_in, qw, kw, invf, sgn, ones, tm):
    t = x2.shape[0]
    row = lambda w: pl.BlockSpec((tm, w), lambda i: (i, 0))
    return pl.pallas_call(
        _in_proj_kernel,
        grid=(t // tm,),
        in_specs=[row(D_MODEL), _resident(pos2.shape), _resident(w_in.shape), _resident(qw.shape),
                  _resident(kw.shape), _resident(invf.shape), _resident(sgn.shape),
                  _resident(ones.shape)],
        out_specs=[row(ATTN_W), row(KV_W), row(KV_W), row(ATTN_W),
                   pl.BlockSpec((tm // SSM_Q, N_SLABS * SLAB_IN), lambda i: (i, 0)), row(SSM_W)],
        out_shape=[jax.ShapeDtypeStruct((t, ATTN_W), _BF16),
                   jax.ShapeDtypeStruct((t, KV_W), _BF16),
                   jax.ShapeDtypeStruct((t, KV_W), _BF16),
                   jax.ShapeDtypeStruct((t, ATTN_W), _BF16),
                   jax.ShapeDtypeStruct((t // SSM_Q, N_SLABS * SLAB_IN), _BF16),
                   jax.ShapeDtypeStruct((t, SSM_W), _BF16)],
        scratch_shapes=[pltpu.VMEM((tm, D_MODEL), _BF16), pltpu.VMEM((N_SLABS, tm, LANES), _F32)],
        compiler_params=pltpu.CompilerParams(dimension_semantics=("arbitrary",),
                                             vmem_limit_bytes=VMEM_LIMIT),
        name="in_proj",
    )(x2, pos2, w_in, qw, kw, invf, sgn, ones)


def _attn_kernel(sinks_ref, q_ref, kp_ref, kc_ref, vp_ref, vc_ref, za_ref, nw_ref, o_ref, acc_ref):
    n = pl.program_id(1)
    iq = lax.broadcasted_iota(jnp.int32, (BLOCK, 2 * BLOCK), 0)
    ik = lax.broadcasted_iota(jnp.int32, (BLOCK, 2 * BLOCK), 1)
    rel = ik - iq
    allowed = (rel >= 1) & (rel <= BLOCK) & ((ik >= BLOCK) | (n > 0))
    lane_kv = lax.broadcasted_iota(jnp.int32, (2 * BLOCK, LANES), 1)
    lane_o = lax.broadcasted_iota(jnp.int32, (BLOCK, LANES), 1)

    for j in range(N_KV_HEADS // 2):
        sl = slice(j * LANES, (j + 1) * LANES)
        k2 = jnp.concatenate([kp_ref[:, sl], kc_ref[:, sl]], axis=0)
        v2 = jnp.concatenate([vp_ref[:, sl], vc_ref[:, sl]], axis=0)
        zero = jnp.zeros_like(k2)
        qs = jnp.concatenate(
            [q_ref[:, (KV_REP * j + r) * LANES:(KV_REP * j + r + 1) * LANES] for r in range(KV_REP)],
            axis=0)
        probs, rdens = [], []
        for half in range(2):
            keep = (lane_kv < HEAD_DIM) if half == 0 else (lane_kv >= HEAD_DIM)
            s = lax.dot_general(qs, jnp.where(keep, k2, zero), (((1,), (1,)), ((), ())),
                                preferred_element_type=_F32)
            p_rows, rd_rows = [], []
            for r in range(KV_REP):
                sink = sinks_ref[2 * KV_REP * j + KV_REP * half + r]
                sr = jnp.where(allowed, s[r * BLOCK:(r + 1) * BLOCK], -1e30)
                m = jnp.maximum(jnp.max(sr, axis=-1, keepdims=True), sink)
                p = jnp.exp(sr - m)
                den = jnp.sum(p, axis=-1, keepdims=True) + jnp.exp(sink - m)
                p_rows.append(p.astype(_BF16))
                rd_rows.append(1.0 / den)
            probs.append(jnp.concatenate(p_rows, axis=0))
            rdens.append(rd_rows)
        pcat = jnp.concatenate(probs, axis=1)
        vcat = jnp.concatenate([jnp.where(lane_kv < HEAD_DIM, v2, zero),
                                jnp.where(lane_kv >= HEAD_DIM, v2, zero)], axis=0)
        o = jnp.dot(pcat, vcat, preferred_element_type=_F32)
        for r in range(KV_REP):
            scale = jnp.where(lane_o < HEAD_DIM, rdens[0][r], rdens[1][r])
            c0 = (KV_REP * j + r) * LANES
            acc_ref[:, c0:c0 + LANES] = o[r * BLOCK:(r + 1) * BLOCK] * scale

    g = acc_ref[...] * za_ref[...].astype(_F32)
    rs = lax.rsqrt(jnp.mean(g * g, axis=-1, keepdims=True) + NORM_EPS)
    o_ref[...] = (g * rs * nw_ref[...]).astype(_BF16)


def _attention(sinks_perm, q, k, v, za, nw, batch, seq):
    nb = seq // BLOCK
    cur = lambda w: pl.BlockSpec((BLOCK, w), lambda b, n: (b * nb + n, 0))
    prev = lambda w: pl.BlockSpec((BLOCK, w), lambda b, n: (b * nb + jnp.maximum(n - 1, 0), 0))
    return pl.pallas_call(
        _attn_kernel,
        grid=(batch, nb),
        in_specs=[pl.BlockSpec(memory_space=pltpu.SMEM), cur(ATTN_W), prev(KV_W), cur(KV_W),
                  prev(KV_W), cur(KV_W), cur(ATTN_W), _resident(nw.shape)],
        out_specs=cur(ATTN_W),
        out_shape=jax.ShapeDtypeStruct((batch * seq, ATTN_W), _BF16),
        scratch_shapes=[pltpu.VMEM((BLOCK, ATTN_W), _F32)],
        compiler_params=pltpu.CompilerParams(dimension_semantics=("arbitrary", "arbitrary")),
        name="swa_attention",
    )(sinks_perm, q, k, k, v, v, za, nw)


def _ssm_weights(a_re, a_im, log_step, b_re, b_im, c_re, c_im, d_skip, n_scan):
    f32 = _F32
    a_re, a_im = a_re.astype(f32), a_im.astype(f32)
    delta = jnp.exp(log_step.astype(f32))[:, None]
    mag = jnp.exp(a_re * delta)
    lr, li = mag * jnp.cos(a_im * delta), mag * jnp.sin(a_im * delta)
    den = a_re * a_re + a_im * a_im
    fr = ((lr - 1.0) * a_re + li * a_im) / den
    fi = (li * a_re - (lr - 1.0) * a_im) / den
    b_re, b_im = b_re.astype(f32), b_im.astype(f32)
    bbr = fr[..., None] * b_re - fi[..., None] * b_im
    bbi = fr[..., None] * b_im + fi[..., None] * b_re
    c_re, c_im = c_re.astype(f32), c_im.astype(f32)

    pw_r, pw_i = [jnp.ones_like(lr)], [jnp.zeros_like(li)]
    for _ in range(SSM_Q):
        pr, pi = pw_r[-1], pw_i[-1]
        pw_r.append(pr * lr - pi * li)
        pw_i.append(pr * li + pi * lr)
    rev_r = jnp.stack(pw_r[SSM_Q - 1::-1])
    rev_i = jnp.stack(pw_i[SSM_Q - 1::-1])
    pw_r, pw_i = jnp.stack(pw_r), jnp.stack(pw_i)

    cl_r = c_re[None] * pw_r[:, :, None, :] - c_im[None] * pw_i[:, :, None, :]
    cl_i = c_re[None] * pw_i[:, :, None, :] + c_im[None] * pw_r[:, :, None, :]
    kern = (jnp.einsum('dghp,gpk->dghk', cl_r[:SSM_Q], bbr, precision=lax.Precision.HIGHEST)
            - jnp.einsum('dghp,gpk->dghk', cl_i[:SSM_Q], bbi, precision=lax.Precision.HIGHEST))
    eye_h = jnp.eye(SSM_H, dtype=f32)
    kern = kern.at[0].add(d_skip.astype(f32).reshape(SSM_G, SSM_H)[:, :, None] * eye_h[None])

    def per_slab(a):
        a = a.reshape(SSM_Q, N_SLABS, SLAB_GROUPS * SSM_H, a.shape[-1])
        return a.transpose(1, 0, 2, 3)

    kt = per_slab(kern.transpose(0, 1, 3, 2))
    kt = jnp.pad(kt, ((0, 0), (0, 0), (0, 0), (0, LANES - SSM_H)))

    sin_r = rev_r[..., None] * bbr[None] - rev_i[..., None] * bbi[None]
    sin_i = rev_r[..., None] * bbi[None] + rev_i[..., None] * bbr[None]
    cs = per_slab(jnp.concatenate([sin_r, sin_i], axis=2).transpose(0, 1, 3, 2))

    co = per_slab(jnp.concatenate([cl_r[1:], -cl_i[1:]], axis=3))

    lanes = lambda r, i: jnp.stack([r.reshape(N_SLABS, -1, 2 * SSM_P), i.reshape(N_SLABS, -1, 2 * SSM_P)],
                                   axis=2).reshape(N_SLABS, SLAB_STATE)
    qr, qi = pw_r[SSM_Q], pw_i[SSM_Q]
    ar, ai = qr, qi
    rows = []
    for k in range(n_scan):
        if k < 3:
            rows += [jnp.zeros_like(lanes(ar, ai))] * (1 << k) + [lanes(ar, ai)] * (SUBLANES - (1 << k))
        else:
            rows.append(lanes(ar, ai))
        ar, ai = ar * ar - ai * ai, 2.0 * ar * ai
    fr, fi = jnp.ones_like(qr), jnp.zeros_like(qi)
    powers = []
    for _ in range(SUBLANES):
        powers.append(lanes(fr, fi))
        fr, fi = fr * qr - fi * qi, fr * qi + fi * qr
    decay = jnp.stack(rows[:3 * SUBLANES] + powers + rows[3 * SUBLANES:], axis=1)
    return kt.astype(_BF16), cs.astype(_BF16), co.astype(_BF16), decay


def _expand_kernel(kt_ref, cs_ref, co_ref, w1_ref, w2_ref):
    r16 = lax.broadcasted_iota(jnp.int32, (LANES, LANES), 0)
    c16 = lax.broadcasted_iota(jnp.int32, (LANES, LANES), 1)
    rep_h = (r16 == c16 % SSM_H).astype(_BF16)
    diag_h = (r16 // SSM_H) == (c16 // SSM_H)
    rs = lax.broadcasted_iota(jnp.int32, (LANES, SLAB_STATE), 0)
    cst = lax.broadcasted_iota(jnp.int32, (LANES, SLAB_STATE), 1)
    rep_s = ((rs // SSM_P == (cst // LANES) % 2) & (rs % SSM_P == cst % SSM_P)).astype(_BF16)
    diag_s = (rs // SSM_H) == (2 * (cst // (2 * LANES)) + (cst % LANES) // SSM_P)

    zero = jnp.zeros((LANES, LANES), _BF16)
    blocks = [jnp.where(diag_h, jnp.dot(kt_ref[0, d], rep_h, preferred_element_type=_F32), 0.0).astype(_BF16)
              for d in range(SSM_Q)]
    for i in range(SSM_Q):
        for j in range(SSM_Q):
            w1_ref[0, i * LANES:(i + 1) * LANES, j * LANES:(j + 1) * LANES] = blocks[j - i] if j >= i else zero
        w_in = jnp.where(diag_s, jnp.dot(cs_ref[0, i], rep_s, preferred_element_type=_F32), 0.0)
        w1_ref[0, i * LANES:(i + 1) * LANES, SLAB_IN:] = w_in.astype(_BF16)
        w_out_t = jnp.where(diag_s, jnp.dot(co_ref[0, i], rep_s, preferred_element_type=_F32), 0.0)
        w2_ref[0, :, i * LANES:(i + 1) * LANES] = w_out_t.T.astype(_BF16)


def _expand(kt, cs, co):
    blk = pl.BlockSpec((1, SSM_Q, LANES, LANES), lambda s: (s, 0, 0, 0))
    return pl.pallas_call(
        _expand_kernel,
        grid=(N_SLABS,),
        in_specs=[blk, blk, blk],
        out_specs=[pl.BlockSpec((1, SLAB_IN, SLAB_IN + SLAB_STATE), lambda s: (s, 0, 0)),
                   pl.BlockSpec((1, SLAB_STATE, SLAB_IN), lambda s: (s, 0, 0))],
        out_shape=[jax.ShapeDtypeStruct((N_SLABS, SLAB_IN, SLAB_IN + SLAB_STATE), _BF16),
                   jax.ShapeDtypeStruct((N_SLABS, SLAB_STATE, SLAB_IN), _BF16)],
        compiler_params=pltpu.CompilerParams(dimension_semantics=("arbitrary",)),
        name="s5_expand",
    )(kt, cs, co)


def _ssm_kernel(x_ref, w1_ref, w2_ref, dec_ref, y_ref, h_ref, t_scr, y_scr):
    nc = x_ref.shape[0]
    nb = nc // SUBLANES
    piece = 2 * LANES
    powers = 3 * SUBLANES
    lvl2 = 4 * SUBLANES

    def cmul_add(acc_r, acc_i, pr, pi, tr, ti):
        return acc_r + pr * tr - pi * ti, acc_i + pr * ti + pi * tr

    @pl.when(pl.program_id(2) == 0)
    def _():
        h_ref[...] = jnp.zeros_like(h_ref)

    x = x_ref[...]
    s_all = jnp.dot(x, w1_ref[0, :, SLAB_IN:], preferred_element_type=_F32)

    def roll_in_block(a, shift):
        return pltpu.roll(a.reshape(nb, SUBLANES, LANES), shift, axis=1).reshape(nc, LANES)

    sub = lax.broadcasted_iota(jnp.int32, (nc, LANES), 0) % SUBLANES
    brow = lax.broadcasted_iota(jnp.int32, (nb, LANES), 0)
    y = None
    for p in range(SLAB_STATE // piece):
        re, im = slice(p * piece, p * piece + LANES), slice(p * piece + LANES, (p + 1) * piece)
        table = lambda row0, n: (dec_ref[0, row0:row0 + n, re], dec_ref[0, row0:row0 + n, im])
        tall = lambda a: jnp.concatenate([a] * nb, axis=0)

        sr, si = s_all[:, re], s_all[:, im]
        hin_r, hin_i = h_ref[0:1, re], h_ref[0:1, im]

        for k in range(3):
            mr, mi = table(k * SUBLANES, SUBLANES)
            sr, si = cmul_add(sr, si, tall(mr), tall(mi),
                              roll_in_block(sr, 1 << k), roll_in_block(si, 1 << k))

        t_scr[2 * p] = sr
        t_scr[2 * p + 1] = si
        br = t_scr[2 * p, pl.ds(SUBLANES - 1, nb, stride=SUBLANES), :]
        bi = t_scr[2 * p + 1, pl.ds(SUBLANES - 1, nb, stride=SUBLANES), :]
        pr, pi = table(lvl2, 1)
        br = br + jnp.where(brow == 0, pr * hin_r - pi * hin_i, 0.0)
        bi = bi + jnp.where(brow == 0, pr * hin_i + pi * hin_r, 0.0)
        for k in range(dec_ref.shape[1] - lvl2):
            shift = 1 << k
            tr = jnp.where(brow >= shift, pltpu.roll(br, shift, axis=0), 0.0)
            ti = jnp.where(brow >= shift, pltpu.roll(bi, shift, axis=0), 0.0)
            br, bi = cmul_add(br, bi, *table(lvl2 + k, 1), tr, ti)
        h_ref[0:1, re] = br[nb - 1:nb]
        h_ref[0:1, im] = bi[nb - 1:nb]
        bpr = jnp.where(brow == 0, hin_r, pltpu.roll(br, 1, axis=0))
        bpi = jnp.where(brow == 0, hin_i, pltpu.roll(bi, 1, axis=0))

        rep = lambda a: jnp.concatenate(
            [jnp.broadcast_to(a[k:k + 1], (SUBLANES, LANES)) for k in range(nb)], axis=0)
        fr, fi = table(powers, SUBLANES)
        tr = jnp.where(sub >= 1, roll_in_block(sr, 1), 0.0)
        ti = jnp.where(sub >= 1, roll_in_block(si, 1), 0.0)
        prev_r, prev_i = cmul_add(tr, ti, tall(fr), tall(fi), rep(bpr), rep(bpi))
        hprev = jnp.concatenate([prev_r, prev_i], axis=1).astype(_BF16)
        part = jnp.dot(hprev, w2_ref[0, p * piece:(p + 1) * piece, :], preferred_element_type=_F32)
        if y is None:
            y = jnp.dot(x, w1_ref[0, :, :SLAB_IN], preferred_element_type=_F32) + part
        else:
            y = y + part

    for i in range(SSM_Q):
        y_scr[pl.ds(i, nc, stride=SSM_Q), :] = y[:, i * LANES:(i + 1) * LANES]
    y_ref[...] = y_scr[...].astype(y_ref.dtype)


def _ssm(ucat, w1, w2, decay, batch, seq, nc):
    tiles = seq // (nc * SSM_Q)
    return pl.pallas_call(
        _ssm_kernel,
        grid=(N_SLABS, batch, tiles),
        in_specs=[pl.BlockSpec((nc, SLAB_IN), lambda s, b, t: (b * tiles + t, s)),
                  pl.BlockSpec((1,) + w1.shape[1:], lambda s, b, t: (s, 0, 0)),
                  pl.BlockSpec((1,) + w2.shape[1:], lambda s, b, t: (s, 0, 0)),
                  pl.BlockSpec((1,) + decay.shape[1:], lambda s, b, t: (s, 0, 0))],
        out_specs=pl.BlockSpec((nc * SSM_Q, LANES), lambda s, b, t: (b * tiles + t, s)),
        out_shape=jax.ShapeDtypeStruct((batch * seq, SSM_W), _BF16),
        scratch_shapes=[pltpu.VMEM((SUBLANES, SLAB_STATE), _F32),
                        pltpu.VMEM((SLAB_STATE // LANES, nc, LANES), _F32),
                        pltpu.VMEM((nc * SSM_Q, LANES), _F32)],
        compiler_params=pltpu.CompilerParams(
            dimension_semantics=("arbitrary", "arbitrary", "arbitrary"),
            vmem_limit_bytes=VMEM_LIMIT),
        name="s5_mixer",
    )(ucat, w1, w2, decay)


def _gelu_tanh(y):
    c = math.sqrt(2.0 / math.pi)
    return (0.5 * y) * (1.0 + jnp.tanh(y * (c + (c * 0.044715) * (y * y))))


def _out_kernel(x_ref, oa_ref, y_ref, zs_ref, wg_ref, bg_ref, woa_ref, wos_ref, out_ref):
    y = _gelu_tanh(y_ref[...].astype(_F32))
    gate = jnp.dot(y.astype(_BF16), wg_ref[...], preferred_element_type=_F32) + bg_ref[...]
    o = y * jax.nn.sigmoid(gate) * zs_ref[...].astype(_F32)
    rs = lax.rsqrt(jnp.mean(o * o, axis=-1, keepdims=True) + NORM_EPS)
    out_ref[...] = (x_ref[...] + jnp.dot(oa_ref[...], woa_ref[...], preferred_element_type=_F32)
                    + rs * jnp.dot(o.astype(_BF16), wos_ref[...], preferred_element_type=_F32))


def _out_proj(x2, oa, y, zs, wg, bg, woa, wos, tm):
    t = x2.shape[0]
    row = lambda w: pl.BlockSpec((tm, w), lambda i: (i, 0))
    return pl.pallas_call(
        _out_kernel,
        grid=(t // tm,),
        in_specs=[row(D_MODEL), row(ATTN_W), row(SSM_W), row(SSM_W), _resident(wg.shape),
                  _resident(bg.shape), _resident(woa.shape), _resident(wos.shape)],
        out_specs=row(D_MODEL),
        out_shape=jax.ShapeDtypeStruct((t, D_MODEL), x2.dtype),
        compiler_params=pltpu.CompilerParams(dimension_semantics=("arbitrary",),
                                             vmem_limit_bytes=VMEM_LIMIT),
        name="out_proj",
    )(x2, oa, y, zs, wg, bg, woa, wos)


def _pick_tile(n, target, multiple):
    best = multiple
    for cand in range(multiple, min(n, target) + 1, multiple):
        if n % cand == 0:
            best = cand
    return best


def kernel(x, positions, norm_w, w_in, q_norm_w, k_norm_w, sinks, a_re, a_im, log_step,
           b_re, b_im, c_re, c_im, d_skip, w_glu, b_glu, attn_out_norm_w, ssm_out_norm_w, w_out):
    batch, seq, _ = x.shape
    assert seq % BLOCK == 0 and x.shape[2] == D_MODEL and w_in.shape == (D_MODEL, IN_W)
    t = batch * seq
    f32 = _F32

    w_in_f = norm_w.astype(f32)[:, None] * w_in.astype(f32)
    w_in_k = jnp.concatenate(
        [_pair_heads(w_in_f[:, :ATTN_W], 1), w_in_f[:, ATTN_W:ATTN_W + 2 * KV_W],
         _pair_heads(w_in_f[:, ATTN_W + 2 * KV_W:2 * ATTN_W + 2 * KV_W], 1),
         w_in_f[:, 2 * ATTN_W + 2 * KV_W:]], axis=1).astype(_BF16)
    qw =(jnp.tile(q_norm_w.astype(f32), N_HEADS) * (1.0 / math.sqrt(HEAD_DIM)))[None, :]
    kw = jnp.tile(k_norm_w.astype(f32), N_KV_HEADS)[None, :]
    inv_freq = ROPE_THETA ** (-jnp.arange(0, HEAD_DIM, 2, dtype=f32) / HEAD_DIM)
    invf = jnp.tile(inv_freq, LANES // (HEAD_DIM // 2))[None, :]
    sgn = jnp.tile(jnp.concatenate([-jnp.ones(HEAD_DIM // 2, f32), jnp.ones(HEAD_DIM // 2, f32)]),
                   LANES // HEAD_DIM)[None, :]
    head_of_lane = np.arange(MXU_DIM) // HEAD_DIM
    ones = jnp.asarray(head_of_lane[:, None] == head_of_lane[None, :], _BF16)
    sinks_k = sinks.astype(f32)
    attn_nw = _pair_heads(attn_out_norm_w.astype(f32), 0)[None, :]
    wo_attn = _pair_heads(w_out[:ATTN_W], 0).astype(_BF16)
    wo_ssm = (ssm_out_norm_w.astype(f32)[:, None] * w_out[ATTN_W:].astype(f32)).astype(_BF16)

    x2 = x.reshape(t, D_MODEL)
    pos2 = positions.reshape(t // LANES, LANES).astype(jnp.int32)

    tm = _pick_tile(seq, 512, BLOCK)
    q, k, v, za, ucat, zs = _in_proj(x2, pos2, w_in_k, qw, kw, invf, sgn, ones, tm)

    o_attn = _attention(sinks_k, q, k, v, za, attn_nw, batch, seq)

    nc = _pick_tile(seq // SSM_Q, 512, 16)
    assert nc & (nc - 1) == 0, "chunk rows per tile must be a power of two for the scan"
    kt, cs, co, decay = _ssm_weights(a_re, a_im, log_step, b_re, b_im, c_re, c_im, d_skip,
                                     n_scan=int(math.log2(nc)))
    w1, w2 = _expand(kt, cs, co)
    y = _ssm(ucat, w1, w2, decay, batch, seq, nc)

    out = _out_proj(x2, o_attn, y, zs, w_glu.astype(_BF16), b_glu.astype(f32)[None, :],
                    wo_attn, wo_ssm, tm)
    return out.reshape(x.shape)
```

```python
import math

import jax
import jax.numpy as jnp
import numpy as np
from jax import lax
from jax.experimental import pallas as pl
from jax.experimental.pallas import tpu as pltpu

D_MODEL = 2048
ATTN_W = 1024
KV_W = 256
HEAD_DIM = 64
N_HEADS = 16
N_KV_HEADS = 4
KV_REP = N_HEADS // N_KV_HEADS
SSM_W = 1024
SSM_H = 16
SSM_G = 64
SSM_P = 64
BLOCK = 128
ROPE_THETA = 10000.0
NORM_EPS = 1e-6
IN_W = 2 * ATTN_W + 2 * KV_W + 2 * SSM_W

LANES = 128
SUBLANES = 8
MXU_DIM = 256
SSM_Q = 8
SLAB_GROUPS = LANES // SSM_H
N_SLABS = SSM_W // LANES
SLAB_IN = SSM_Q * LANES
SLAB_STATE = 2 * SLAB_GROUPS * SSM_P
VMEM_LIMIT = 56 * 1024 * 1024

_BF16 = jnp.bfloat16
_F32 = jnp.float32


def _resident(shape):
    nd = len(shape)
    return pl.BlockSpec(shape, lambda *_: (0,) * nd, pipeline_mode=pl.Buffered(1))


def _pair_heads(a):
    shape = a.shape
    a = a.reshape((N_KV_HEADS // 2, 2, KV_REP, HEAD_DIM) + shape[1:])
    return jnp.swapaxes(a, 1, 2).reshape(shape)


def _swap_halves(a):
    return pltpu.roll(a, HEAD_DIM, axis=1)


def _pair_slabs(n0, n1, n2, n3):
    low = lax.broadcasted_iota(jnp.int32, n0.shape, 1) < HEAD_DIM
    return (jnp.where(low, n0, _swap_halves(n2)), jnp.where(low, _swap_halves(n0), n2),
            jnp.where(low, n1, _swap_halves(n3)), jnp.where(low, _swap_halves(n1), n3))


def _rope(t, cos, sin_signed):
    lane = lax.broadcasted_iota(jnp.int32, t.shape, 1)
    first_half = (lane % HEAD_DIM) < (HEAD_DIM // 2)
    swapped = jnp.where(first_half,
                        pltpu.roll(t, LANES - HEAD_DIM // 2, axis=1),
                        pltpu.roll(t, HEAD_DIM // 2, axis=1))
    return t * cos + swapped * sin_signed


def _in_proj_kernel(x_ref, pos_ref, w_ref, qw_ref, kw_ref, invf_ref, sgn_ref, ones_ref,
                    q_ref, k_ref, v_ref, za_ref, ucat_ref, zs_ref, hn_ref, u_ref):
    tm = x_ref.shape[0]
    x = x_ref[...]
    hn_ref[...] = x.astype(_BF16)
    rs = jnp.broadcast_to(lax.rsqrt(jnp.mean(x * x, axis=-1, keepdims=True) + NORM_EPS), (tm, LANES))
    wide = 2 * MXU_DIM
    rs_wide = jnp.concatenate([rs] * (wide // LANES), axis=1)

    nsub = tm // LANES
    ang = jnp.concatenate(
        [jnp.broadcast_to(pos_ref[pl.ds(pl.program_id(0) * nsub + j, 1), :].astype(_F32),
                          (LANES, LANES)).T for j in range(nsub)], axis=0) * invf_ref[...]
    cos = jnp.cos(ang)
    sin_signed = jnp.sin(ang) * sgn_ref[...]

    def proj(c0):
        return jnp.dot(hn_ref[...], w_ref[:, c0:c0 + wide], preferred_element_type=_F32) * rs_wide

    def normed_rope(acc, nw):
        ssq = jnp.dot((acc * acc).astype(_BF16), ones_ref[...], preferred_element_type=_F32)
        t = acc * lax.rsqrt(ssq * (1.0 / HEAD_DIM) + NORM_EPS) * nw
        return [_rope(t[:, s * LANES:(s + 1) * LANES], cos, sin_signed) for s in range(MXU_DIM // LANES)]

    base = 2 * ATTN_W + 2 * KV_W
    nc = tm // SSM_Q
    for c in range(SSM_W // wide):
        acc = proj(base + c * wide)
        for s in range(wide // LANES):
            u_ref[c * (wide // LANES) + s] = acc[:, s * LANES:(s + 1) * LANES]
    for s in range(N_SLABS):
        for i in range(SSM_Q):
            c0 = s * SLAB_IN + i * LANES
            ucat_ref[:, c0:c0 + LANES] = u_ref[s, pl.ds(i, nc, stride=SSM_Q), :].astype(_BF16)

    for c in range(ATTN_W // wide):
        acc = proj(c * wide)
        slabs = []
        for h in range(wide // MXU_DIM):
            n0 = c * wide + h * MXU_DIM
            slabs += normed_rope(acc[:, h * MXU_DIM:(h + 1) * MXU_DIM], qw_ref[:, n0:n0 + MXU_DIM])
        for r, slab in enumerate(_pair_slabs(*slabs)):
            q_ref[:, c * wide + r * LANES:c * wide + (r + 1) * LANES] = slab.astype(_BF16)
    acc = proj(ATTN_W)
    for s, slab in enumerate(normed_rope(acc[:, :KV_W], kw_ref[...])):
        k_ref[:, s * LANES:(s + 1) * LANES] = slab.astype(_BF16)
    v_ref[...] = acc[:, KV_W:].astype(_BF16)
    for c in range(ATTN_W // wide):
        z = proj(ATTN_W + 2 * KV_W + c * wide)
        z = z * jax.nn.sigmoid(z)
        for r, slab in enumerate(_pair_slabs(*[z[:, s * LANES:(s + 1) * LANES] for s in range(wide // LANES)])):
            za_ref[:, c * wide + r * LANES:c * wide + (r + 1) * LANES] = slab.astype(_BF16)
    for c in range(SSM_W // wide):
        z = proj(2 * ATTN_W + 2 * KV_W + SSM_W + c * wide)
        zs_ref[:, c * wide:(c + 1) * wide] = (z * jax.nn.sigmoid(z)).astype(_BF16)


def _in_proj(x2, pos2, w_in, qw, kw, invf, sgn, ones, tm):
    t = x2.shape[0]
    row = lambda w: pl.BlockSpec((tm, w), lambda i: (i, 0))
    return pl.pallas_call(
        _in_proj_kernel,
        grid=(t // tm,),
        in_specs=[row(D_MODEL), _resident(pos2.shape), _resident(w_in.shape), _resident(qw.shape),
                  _resident(kw.shape), _resident(invf.shape), _resident(sgn.shape),
                  _resident(ones.shape)],
        out_specs=[row(ATTN_W), row(KV_W), row(KV_W), row(ATTN_W),
                   pl.BlockSpec((tm // SSM_Q, N_SLABS * SLAB_IN), lambda i: (i, 0)), row(SSM_W)],
        out_shape=[jax.ShapeDtypeStruct((t, ATTN_W), _BF16),
                   jax.ShapeDtypeStruct((t, KV_W), _BF16),
                   jax.ShapeDtypeStruct((t, KV_W), _BF16),
                   jax.ShapeDtypeStruct((t, ATTN_W), _BF16),
                   jax.ShapeDtypeStruct((t // SSM_Q, N_SLABS * SLAB_IN), _BF16),
                   jax.ShapeDtypeStruct((t, SSM_W), _BF16)],
        scratch_shapes=[pltpu.VMEM((tm, D_MODEL), _BF16), pltpu.VMEM((N_SLABS, tm, LANES), _F32)],
        compiler_params=pltpu.CompilerParams(dimension_semantics=("arbitrary",),
                                             vmem_limit_bytes=VMEM_LIMIT),
        name="in_proj",
    )(x2, pos2, w_in, qw, kw, invf, sgn, ones)


def _attn_kernel(sinks_ref, q_ref, kp_ref, kc_ref, vp_ref, vc_ref, za_ref, nw_ref, o_ref, acc_ref):
    n = pl.program_id(1)
    iq = lax.broadcasted_iota(jnp.int32, (BLOCK, 2 * BLOCK), 0)
    ik = lax.broadcasted_iota(jnp.int32, (BLOCK, 2 * BLOCK), 1)
    rel = ik - iq
    allowed = (rel >= 1) & (rel <= BLOCK) & ((ik >= BLOCK) | (n > 0))
    lane_kv = lax.broadcasted_iota(jnp.int32, (2 * BLOCK, LANES), 1)
    lane_o = lax.broadcasted_iota(jnp.int32, (BLOCK, LANES), 1)

    for j in range(N_KV_HEADS // 2):
        sl = slice(j * LANES, (j + 1) * LANES)
        k2 = jnp.concatenate([kp_ref[:, sl], kc_ref[:, sl]], axis=0)
        v2 = jnp.concatenate([vp_ref[:, sl], vc_ref[:, sl]], axis=0)
        zero = jnp.zeros_like(k2)
        qs = jnp.concatenate(
            [q_ref[:, (KV_REP * j + r) * LANES:(KV_REP * j + r + 1) * LANES] for r in range(KV_REP)],
            axis=0)
        probs, rdens = [], []
        for half in range(2):
            keep = (lane_kv < HEAD_DIM) if half == 0 else (lane_kv >= HEAD_DIM)
            s = lax.dot_general(qs, jnp.where(keep, k2, zero), (((1,), (1,)), ((), ())),
                                preferred_element_type=_F32)
            p_rows, rd_rows = [], []
            for r in range(KV_REP):
                sink = sinks_ref[2 * KV_REP * j + KV_REP * half + r]
                sr = jnp.where(allowed, s[r * BLOCK:(r + 1) * BLOCK], -1e30)
                m = jnp.maximum(jnp.max(sr, axis=-1, keepdims=True), sink)
                p = jnp.exp(sr - m)
                den = jnp.sum(p, axis=-1, keepdims=True) + jnp.exp(sink - m)
                p_rows.append(p.astype(_BF16))
                rd_rows.append(1.0 / den)
            probs.append(jnp.concatenate(p_rows, axis=0))
            rdens.append(rd_rows)
        pcat = jnp.concatenate(probs, axis=1)
        vcat = jnp.concatenate([jnp.where(lane_kv < HEAD_DIM, v2, zero),
                                jnp.where(lane_kv >= HEAD_DIM, v2, zero)], axis=0)
        o = jnp.dot(pcat, vcat, preferred_element_type=_F32)
        for r in range(KV_REP):
            scale = jnp.where(lane_o < HEAD_DIM, rdens[0][r], rdens[1][r])
            c0 = (KV_REP * j + r) * LANES
            acc_ref[:, c0:c0 + LANES] = o[r * BLOCK:(r + 1) * BLOCK] * scale

    g = acc_ref[...] * za_ref[...].astype(_F32)
    rs = lax.rsqrt(jnp.mean(g * g, axis=-1, keepdims=True) + NORM_EPS)
    o_ref[...] = (g * rs * nw_ref[...]).astype(_BF16)


def _attention(sinks, q, k, v, za, nw, batch, seq):
    nb = seq // BLOCK
    cur = lambda w: pl.BlockSpec((BLOCK, w), lambda b, n: (b * nb + n, 0))
    prev = lambda w: pl.BlockSpec((BLOCK, w), lambda b, n: (b * nb + jnp.maximum(n - 1, 0), 0))
    return pl.pallas_call(
        _attn_kernel,
        grid=(batch, nb),
        in_specs=[pl.BlockSpec(memory_space=pltpu.SMEM), cur(ATTN_W), prev(KV_W), cur(KV_W),
                  prev(KV_W), cur(KV_W), cur(ATTN_W), _resident(nw.shape)],
        out_specs=cur(ATTN_W),
        out_shape=jax.ShapeDtypeStruct((batch * seq, ATTN_W), _BF16),
        scratch_shapes=[pltpu.VMEM((BLOCK, ATTN_W), _F32)],
        compiler_params=pltpu.CompilerParams(dimension_semantics=("arbitrary", "arbitrary")),
        name="swa_attention",
    )(sinks, q, k, k, v, v, za, nw)


def _ssm_weights(a_re, a_im, log_step, b_re, b_im, c_re, c_im, d_skip, n_scan):
    f32 = _F32
    a_re, a_im = a_re.astype(f32), a_im.astype(f32)
    delta = jnp.exp(log_step.astype(f32))[:, None]
    zr, zi = a_re * delta, a_im * delta

    def lam_pow(exponents):
        m = jnp.asarray(np.asarray(exponents, np.float32))[:, None, None]
        mag = jnp.exp(m * zr)
        return mag * jnp.cos(m * zi), mag * jnp.sin(m * zi)

    pw_r, pw_i = lam_pow(range(SSM_Q + 1))
    rev_r, rev_i = lam_pow(range(SSM_Q - 1, -1, -1))
    lr, li = pw_r[1], pw_i[1]
    den = a_re * a_re + a_im * a_im
    fr = ((lr - 1.0) * a_re + li * a_im) / den
    fi = (li * a_re - (lr - 1.0) * a_im) / den
    b_re, b_im = b_re.astype(f32), b_im.astype(f32)
    bbr = fr[..., None] * b_re - fi[..., None] * b_im
    bbi = fr[..., None] * b_im + fi[..., None] * b_re
    c_re, c_im = c_re.astype(f32), c_im.astype(f32)

    cl_r = c_re[None] * pw_r[:, :, None, :] - c_im[None] * pw_i[:, :, None, :]
    cl_i = c_re[None] * pw_i[:, :, None, :] + c_im[None] * pw_r[:, :, None, :]

    def per_slab(a):
        a = a.reshape(SSM_Q, N_SLABS, SLAB_GROUPS * SSM_H, a.shape[-1])
        return a.transpose(1, 0, 2, 3)

    prod = (cl_r[:SSM_Q].transpose(0, 1, 3, 2)[:, :, None, :, :] * bbr.transpose(0, 2, 1)[None, :, :, :, None]
            - cl_i[:SSM_Q].transpose(0, 1, 3, 2)[:, :, None, :, :] * bbi.transpose(0, 2, 1)[None, :, :, :, None])
    kern = jnp.sum(prod, axis=3)
    skip = d_skip.astype(f32).reshape(SSM_G, SSM_H)[:, :, None] * jnp.eye(SSM_H, dtype=f32)[None]
    kern = jnp.concatenate([kern[:1] + skip[None], kern[1:]], axis=0)
    kt = jnp.pad(per_slab(kern), ((0, 0), (0, 0), (0, 0), (0, LANES - SSM_H)))

    sin_r = rev_r[..., None] * bbr[None] - rev_i[..., None] * bbi[None]
    sin_i = rev_r[..., None] * bbi[None] + rev_i[..., None] * bbr[None]
    cs = per_slab(jnp.concatenate([sin_r, sin_i], axis=2).transpose(0, 1, 3, 2))

    co = per_slab(jnp.concatenate([cl_r[1:], -cl_i[1:]], axis=3))

    exps, keep = [], []
    for k in range(3):
        exps += [SSM_Q << k] * SUBLANES
        keep += [0.0] * (1 << k) + [1.0] * (SUBLANES - (1 << k))
    exps += [SSM_Q * j for j in range(SUBLANES)] + [SSM_Q << k for k in range(3, n_scan)]
    keep += [1.0] * (SUBLANES + n_scan - 3)
    tr, ti = lam_pow(exps)
    keep = jnp.asarray(np.asarray(keep, np.float32))[:, None, None]
    rows = jnp.stack([(tr * keep).reshape(-1, N_SLABS, SLAB_GROUPS // 2, 2 * SSM_P),
                      (ti * keep).reshape(-1, N_SLABS, SLAB_GROUPS // 2, 2 * SSM_P)], axis=3)
    decay = rows.transpose(1, 0, 2, 3, 4).reshape(N_SLABS, len(exps), SLAB_STATE)
    return kt.astype(_BF16), cs.astype(_BF16), co.astype(_BF16), decay


def _expand_kernel(kt_ref, cs_ref, co_ref, w1_ref, w2_ref):
    r16 = lax.broadcasted_iota(jnp.int32, (LANES, LANES), 0)
    c16 = lax.broadcasted_iota(jnp.int32, (LANES, LANES), 1)
    rep_h = (r16 == c16 % SSM_H).astype(_BF16)
    diag_h = (r16 // SSM_H) == (c16 // SSM_H)
    rs = lax.broadcasted_iota(jnp.int32, (LANES, SLAB_STATE), 0)
    cst = lax.broadcasted_iota(jnp.int32, (LANES, SLAB_STATE), 1)
    rep_s = ((rs // SSM_P == (cst // LANES) % 2) & (rs % SSM_P == cst % SSM_P)).astype(_BF16)
    diag_s = (rs // SSM_H) == (2 * (cst // (2 * LANES)) + (cst % LANES) // SSM_P)

    zero = jnp.zeros((LANES, LANES), _BF16)
    blocks = [jnp.where(diag_h, jnp.dot(kt_ref[0, d], rep_h, preferred_element_type=_F32), 0.0).astype(_BF16)
              for d in range(SSM_Q)]
    for i in range(SSM_Q):
        for j in range(SSM_Q):
            w1_ref[0, i * LANES:(i + 1) * LANES, j * LANES:(j + 1) * LANES] = blocks[j - i] if j >= i else zero
        w_in = jnp.where(diag_s, jnp.dot(cs_ref[0, i], rep_s, preferred_element_type=_F32), 0.0)
        w1_ref[0, i * LANES:(i + 1) * LANES, SLAB_IN:] = w_in.astype(_BF16)
        w_out_t = jnp.where(diag_s, jnp.dot(co_ref[0, i], rep_s, preferred_element_type=_F32), 0.0)
        w2_ref[0, :, i * LANES:(i + 1) * LANES] = w_out_t.T.astype(_BF16)


def _expand(kt, cs, co):
    blk = pl.BlockSpec((1, SSM_Q, LANES, LANES), lambda s: (s, 0, 0, 0))
    return pl.pallas_call(
        _expand_kernel,
        grid=(N_SLABS,),
        in_specs=[blk, blk, blk],
        out_specs=[pl.BlockSpec((1, SLAB_IN, SLAB_IN + SLAB_STATE), lambda s: (s, 0, 0)),
                   pl.BlockSpec((1, SLAB_STATE, SLAB_IN), lambda s: (s, 0, 0))],
        out_shape=[jax.ShapeDtypeStruct((N_SLABS, SLAB_IN, SLAB_IN + SLAB_STATE), _BF16),
                   jax.ShapeDtypeStruct((N_SLABS, SLAB_STATE, SLAB_IN), _BF16)],
        compiler_params=pltpu.CompilerParams(dimension_semantics=("arbitrary",)),
        name="s5_expand",
    )(kt, cs, co)


def _ssm_kernel(x_ref, w1_ref, w2_ref, dec_ref, y_ref, h_ref, t_scr, y_scr):
    nc = x_ref.shape[0]
    nb = nc // SUBLANES
    piece = 2 * LANES
    powers = 3 * SUBLANES
    lvl2 = 4 * SUBLANES

    def cmul_add(acc_r, acc_i, pr, pi, tr, ti):
        return acc_r + pr * tr - pi * ti, acc_i + pr * ti + pi * tr

    def roll_in_block(a, shift):
        return pltpu.roll(a.reshape(nb, SUBLANES, LANES), shift, axis=1).reshape(nc, LANES)

    @pl.when(pl.program_id(2) == 0)
    def _():
        h_ref[...] = jnp.zeros_like(h_ref)

    x = x_ref[...]
    s_all = jnp.dot(x, w1_ref[0, :, SLAB_IN:], preferred_element_type=_F32)
    sub = lax.broadcasted_iota(jnp.int32, (nc, LANES), 0) % SUBLANES
    brow = lax.broadcasted_iota(jnp.int32, (nb, LANES), 0)
    y = None
    for p in range(SLAB_STATE // piece):
        re, im = slice(p * piece, p * piece + LANES), slice(p * piece + LANES, (p + 1) * piece)
        table = lambda row0, n: (dec_ref[0, row0:row0 + n, re], dec_ref[0, row0:row0 + n, im])
        tall = lambda a: jnp.concatenate([a] * nb, axis=0)

        sr, si = s_all[:, re], s_all[:, im]
        hin_r, hin_i = h_ref[0:1, re], h_ref[0:1, im]

        for k in range(3):
            mr, mi = table(k * SUBLANES, SUBLANES)
            sr, si = cmul_add(sr, si, tall(mr), tall(mi),
                              roll_in_block(sr, 1 << k), roll_in_block(si, 1 << k))

        t_scr[2 * p] = sr
        t_scr[2 * p + 1] = si
        br = t_scr[2 * p, pl.ds(SUBLANES - 1, nb, stride=SUBLANES), :]
        bi = t_scr[2 * p + 1, pl.ds(SUBLANES - 1, nb, stride=SUBLANES), :]
        pr, pi = table(lvl2, 1)
        br = br + jnp.where(brow == 0, pr * hin_r - pi * hin_i, 0.0)
        bi = bi + jnp.where(brow == 0, pr * hin_i + pi * hin_r, 0.0)
        for k in range(dec_ref.shape[1] - lvl2):
            shift = 1 << k
            tr = jnp.where(brow >= shift, pltpu.roll(br, shift, axis=0), 0.0)
            ti = jnp.where(brow >= shift, pltpu.roll(bi, shift, axis=0), 0.0)
            br, bi = cmul_add(br, bi, *table(lvl2 + k, 1), tr, ti)
        h_ref[0:1, re] = br[nb - 1:nb]
        h_ref[0:1, im] = bi[nb - 1:nb]
        bpr = jnp.where(brow == 0, hin_r, pltpu.roll(br, 1, axis=0))
        bpi = jnp.where(brow == 0, hin_i, pltpu.roll(bi, 1, axis=0))

        rep = lambda a: jnp.concatenate(
            [jnp.broadcast_to(a[k:k + 1], (SUBLANES, LANES)) for k in range(nb)], axis=0)
        fr, fi = table(powers, SUBLANES)
        tr = jnp.where(sub >= 1, roll_in_block(sr, 1), 0.0)
        ti = jnp.where(sub >= 1, roll_in_block(si, 1), 0.0)
        prev_r, prev_i = cmul_add(tr, ti, tall(fr), tall(fi), rep(bpr), rep(bpi))
        hprev = jnp.concatenate([prev_r, prev_i], axis=1).astype(_BF16)
        part = jnp.dot(hprev, w2_ref[0, p * piece:(p + 1) * piece, :], preferred_element_type=_F32)
        if y is None:
            y = jnp.dot(x, w1_ref[0, :, :SLAB_IN], preferred_element_type=_F32) + part
        else:
            y = y + part

    for i in range(SSM_Q):
        y_scr[pl.ds(i, nc, stride=SSM_Q), :] = y[:, i * LANES:(i + 1) * LANES]
    y_ref[...] = y_scr[...].astype(y_ref.dtype)


def _ssm(ucat, w1, w2, decay, batch, seq, nc):
    tiles = seq // (nc * SSM_Q)
    return pl.pallas_call(
        _ssm_kernel,
        grid=(N_SLABS, batch, tiles),
        in_specs=[pl.BlockSpec((nc, SLAB_IN), lambda s, b, t: (b * tiles + t, s)),
                  pl.BlockSpec((1,) + w1.shape[1:], lambda s, b, t: (s, 0, 0)),
                  pl.BlockSpec((1,) + w2.shape[1:], lambda s, b, t: (s, 0, 0)),
                  pl.BlockSpec((1,) + decay.shape[1:], lambda s, b, t: (s, 0, 0))],
        out_specs=pl.BlockSpec((nc * SSM_Q, LANES), lambda s, b, t: (b * tiles + t, s)),
        out_shape=jax.ShapeDtypeStruct((batch * seq, SSM_W), _BF16),
        scratch_shapes=[pltpu.VMEM((SUBLANES, SLAB_STATE), _F32),
                        pltpu.VMEM((SLAB_STATE // LANES, nc, LANES), _F32),
                        pltpu.VMEM((nc * SSM_Q, LANES), _F32)],
        compiler_params=pltpu.CompilerParams(
            dimension_semantics=("arbitrary", "arbitrary", "arbitrary"),
            vmem_limit_bytes=VMEM_LIMIT),
        name="s5_mixer",
    )(ucat, w1, w2, decay)


def _gelu_tanh(y):
    c = math.sqrt(2.0 / math.pi)
    return (0.5 * y) * (1.0 + jnp.tanh(y * (c + (c * 0.044715) * (y * y))))


def _out_kernel(x_ref, oa_ref, y_ref, zs_ref, wg_ref, bg_ref, wo_ref, out_ref):
    y = _gelu_tanh(y_ref[...].astype(_F32))
    gate = jnp.dot(y.astype(_BF16), wg_ref[...], preferred_element_type=_F32) + bg_ref[...]
    o = y * jax.nn.sigmoid(gate) * zs_ref[...].astype(_F32)
    rs = lax.rsqrt(jnp.mean(o * o, axis=-1, keepdims=True) + NORM_EPS)
    out_ref[...] = (x_ref[...]
                    + jnp.dot(oa_ref[...], wo_ref[:ATTN_W, :], preferred_element_type=_F32)
                    + rs * jnp.dot(o.astype(_BF16), wo_ref[ATTN_W:, :], preferred_element_type=_F32))


def _out_proj(x2, oa, y, zs, wg, bg, wo, tm):
    t = x2.shape[0]
    row = lambda w: pl.BlockSpec((tm, w), lambda i: (i, 0))
    return pl.pallas_call(
        _out_kernel,
        grid=(t // tm,),
        in_specs=[row(D_MODEL), row(ATTN_W), row(SSM_W), row(SSM_W), _resident(wg.shape),
                  _resident(bg.shape), _resident(wo.shape)],
        out_specs=row(D_MODEL),
        out_shape=jax.ShapeDtypeStruct((t, D_MODEL), x2.dtype),
        compiler_params=pltpu.CompilerParams(dimension_semantics=("arbitrary",),
                                             vmem_limit_bytes=VMEM_LIMIT),
        name="out_proj",
    )(x2, oa, y, zs, wg, bg, wo)


def _pick_tile(n, target, multiple):
    best = multiple
    for cand in range(multiple, min(n, target) + 1, multiple):
        if n % cand == 0:
            best = cand
    return best


def kernel(x, positions, norm_w, w_in, q_norm_w, k_norm_w, sinks, a_re, a_im, log_step,
           b_re, b_im, c_re, c_im, d_skip, w_glu, b_glu, attn_out_norm_w, ssm_out_norm_w, w_out):
    batch, seq, _ = x.shape
    assert seq % BLOCK == 0 and x.shape[2] == D_MODEL and w_in.shape == (D_MODEL, IN_W)
    t = batch * seq
    f32 = _F32

    w_in_k = (norm_w.astype(f32)[:, None] * w_in.astype(f32)).astype(_BF16)
    w_out_k = jnp.concatenate(
        [_pair_heads(w_out[:ATTN_W].astype(f32)),
         ssm_out_norm_w.astype(f32)[:, None] * w_out[ATTN_W:].astype(f32)], axis=0).astype(_BF16)
    qw = (jnp.tile(q_norm_w.astype(f32), N_HEADS) * (1.0 / math.sqrt(HEAD_DIM)))[None, :]
    kw = jnp.tile(k_norm_w.astype(f32), N_KV_HEADS)[None, :]
    inv_freq = ROPE_THETA ** (-jnp.arange(0, HEAD_DIM, 2, dtype=f32) / HEAD_DIM)
    invf = jnp.tile(inv_freq, LANES // (HEAD_DIM // 2))[None, :]
    sgn = jnp.tile(jnp.concatenate([-jnp.ones(HEAD_DIM // 2, f32), jnp.ones(HEAD_DIM // 2, f32)]),
                   LANES // HEAD_DIM)[None, :]
    head_of_lane = np.arange(MXU_DIM) // HEAD_DIM
    ones = jnp.asarray(head_of_lane[:, None] == head_of_lane[None, :], _BF16)

    x2 = x.reshape(t, D_MODEL)
    pos2 = positions.reshape(t // LANES, LANES).astype(jnp.int32)

    tm = _pick_tile(seq, 512, BLOCK)
    q, k, v, za, ucat, zs = _in_proj(x2, pos2, w_in_k, qw, kw, invf, sgn, ones, tm)

    attn_nw = _pair_heads(attn_out_norm_w.astype(f32))[None, :]
    o_attn = _attention(sinks.astype(f32), q, k, v, za, attn_nw, batch, seq)

    nc = _pick_tile(seq // SSM_Q, 512, 16)
    assert nc & (nc - 1) == 0, "chunk rows per tile must be a power of two for the scan"
    kt, cs, co, decay = _ssm_weights(a_re, a_im, log_step, b_re, b_im, c_re, c_im, d_skip,
                                     n_scan=int(math.log2(nc)))
    w1, w2 = _expand(kt, cs, co)
    y = _ssm(ucat, w1, w2, decay, batch, seq, nc)

    out = _out_proj(x2, o_attn, y, zs, w_glu.astype(_BF16), b_glu.astype(f32)[None, :], w_out_k, tm)
    return out.reshape(x.shape)
```

```python
import functools
import math

import jax
import jax.numpy as jnp
import numpy as np
from jax import lax
from jax.experimental import pallas as pl
from jax.experimental.pallas import tpu as pltpu

D_MODEL = 2048
ATTN_W = 1024
KV_W = 256
HEAD_DIM = 64
N_HEADS = 16
N_KV_HEADS = 4
KV_REP = N_HEADS // N_KV_HEADS
SSM_W = 1024
SSM_H = 16
SSM_G = 64
SSM_P = 64
BLOCK = 128
ROPE_THETA = 10000.0
NORM_EPS = 1e-6
IN_W = 2 * ATTN_W + 2 * KV_W + 2 * SSM_W

LANES = 128
SUBLANES = 8
MXU_DIM = 256
SSM_Q = 8
SLAB_GROUPS = LANES // SSM_H
N_SLABS = SSM_W // LANES
SLAB_IN = SSM_Q * LANES
SLAB_STATE = 2 * SLAB_GROUPS * SSM_P
VMEM_LIMIT = 56 * 1024 * 1024

_BF16 = jnp.bfloat16
_F32 = jnp.float32


def _resident(shape):
    nd = len(shape)
    return pl.BlockSpec(shape, lambda *_: (0,) * nd, pipeline_mode=pl.Buffered(1))


def _pair_heads(a):
    shape = a.shape
    a = a.reshape((N_KV_HEADS // 2, 2, KV_REP, HEAD_DIM) + shape[1:])
    return jnp.swapaxes(a, 1, 2).reshape(shape)


def _swap_halves(a):
    return pltpu.roll(a, HEAD_DIM, axis=1)


def _pair_slabs(n0, n1, n2, n3):
    low = lax.broadcasted_iota(jnp.int32, n0.shape, 1) < HEAD_DIM
    return (jnp.where(low, n0, _swap_halves(n2)), jnp.where(low, _swap_halves(n0), n2),
            jnp.where(low, n1, _swap_halves(n3)), jnp.where(low, _swap_halves(n1), n3))


def _rope(t, cos, sin_signed):
    lane = lax.broadcasted_iota(jnp.int32, t.shape, 1)
    first_half = (lane % HEAD_DIM) < (HEAD_DIM // 2)
    swapped = jnp.where(first_half,
                        pltpu.roll(t, LANES - HEAD_DIM // 2, axis=1),
                        pltpu.roll(t, HEAD_DIM // 2, axis=1))
    return t * cos + swapped * sin_signed


def _in_proj_kernel(x_ref, pos_ref, w_ref, qw_ref, kw_ref, invf_ref, sgn_ref, ones_ref,
                    q_ref, k_ref, v_ref, za_ref, ucat_ref, zs_ref, hn_ref, u_ref):
    tm = x_ref.shape[0]
    x = x_ref[...]
    hn_ref[...] = x.astype(_BF16)
    rs = jnp.broadcast_to(lax.rsqrt(jnp.mean(x * x, axis=-1, keepdims=True) + NORM_EPS), (tm, LANES))
    wide = 2 * MXU_DIM
    rs_wide = jnp.concatenate([rs] * (wide // LANES), axis=1)

    nsub = tm // LANES
    ang = jnp.concatenate(
        [jnp.broadcast_to(pos_ref[pl.ds(pl.program_id(0) * nsub + j, 1), :].astype(_F32),
                          (LANES, LANES)).T for j in range(nsub)], axis=0) * invf_ref[...]
    cos = jnp.cos(ang)
    sin_signed = jnp.sin(ang) * sgn_ref[...]

    def proj(c0):
        return jnp.dot(hn_ref[...], w_ref[:, c0:c0 + wide], preferred_element_type=_F32) * rs_wide

    def normed_rope(acc, nw):
        ssq = jnp.dot((acc * acc).astype(_BF16), ones_ref[...], preferred_element_type=_F32)
        t = acc * lax.rsqrt(ssq * (1.0 / HEAD_DIM) + NORM_EPS) * nw
        return [_rope(t[:, s * LANES:(s + 1) * LANES], cos, sin_signed) for s in range(MXU_DIM // LANES)]

    base = 2 * ATTN_W + 2 * KV_W
    nc = tm // SSM_Q
    for c in range(SSM_W // wide):
        acc = proj(base + c * wide)
        for s in range(wide // LANES):
            u_ref[c * (wide // LANES) + s] = acc[:, s * LANES:(s + 1) * LANES]
    for s in range(N_SLABS):
        for i in range(SSM_Q):
            c0 = s * SLAB_IN + i * LANES
            ucat_ref[:, c0:c0 + LANES] = u_ref[s, pl.ds(i, nc, stride=SSM_Q), :].astype(_BF16)

    for c in range(ATTN_W // wide):
        acc = proj(c * wide)
        slabs = []
        for h in range(wide // MXU_DIM):
            n0 = c * wide + h * MXU_DIM
            slabs += normed_rope(acc[:, h * MXU_DIM:(h + 1) * MXU_DIM], qw_ref[:, n0:n0 + MXU_DIM])
        for r, slab in enumerate(_pair_slabs(*slabs)):
            q_ref[:, c * wide + r * LANES:c * wide + (r + 1) * LANES] = slab.astype(_BF16)
    acc = proj(ATTN_W)
    for s, slab in enumerate(normed_rope(acc[:, :KV_W], kw_ref[...])):
        k_ref[:, s * LANES:(s + 1) * LANES] = slab.astype(_BF16)
    v_ref[...] = acc[:, KV_W:].astype(_BF16)
    for c in range(ATTN_W // wide):
        z = proj(ATTN_W + 2 * KV_W + c * wide)
        z = z * jax.nn.sigmoid(z)
        for r, slab in enumerate(_pair_slabs(*[z[:, s * LANES:(s + 1) * LANES] for s in range(wide // LANES)])):
            za_ref[:, c * wide + r * LANES:c * wide + (r + 1) * LANES] = slab.astype(_BF16)
    for c in range(SSM_W // wide):
        z = proj(2 * ATTN_W + 2 * KV_W + SSM_W + c * wide)
        zs_ref[:, c * wide:(c + 1) * wide] = (z * jax.nn.sigmoid(z)).astype(_BF16)


def _in_proj(x2, pos2, w_in, qw, kw, invf, sgn, ones, tm):
    t = x2.shape[0]
    row = lambda w: pl.BlockSpec((tm, w), lambda i: (i, 0))
    return pl.pallas_call(
        _in_proj_kernel,
        grid=(t // tm,),
        in_specs=[row(D_MODEL), _resident(pos2.shape), _resident(w_in.shape), _resident(qw.shape),
                  _resident(kw.shape), _resident(invf.shape), _resident(sgn.shape),
                  _resident(ones.shape)],
        out_specs=[row(ATTN_W), row(KV_W), row(KV_W), row(ATTN_W),
                   pl.BlockSpec((tm // SSM_Q, N_SLABS * SLAB_IN), lambda i: (i, 0)), row(SSM_W)],
        out_shape=[jax.ShapeDtypeStruct((t, ATTN_W), _BF16),
                   jax.ShapeDtypeStruct((t, KV_W), _BF16),
                   jax.ShapeDtypeStruct((t, KV_W), _BF16),
                   jax.ShapeDtypeStruct((t, ATTN_W), _BF16),
                   jax.ShapeDtypeStruct((t // SSM_Q, N_SLABS * SLAB_IN), _BF16),
                   jax.ShapeDtypeStruct((t, SSM_W), _BF16)],
        scratch_shapes=[pltpu.VMEM((tm, D_MODEL), _BF16), pltpu.VMEM((N_SLABS, tm, LANES), _F32)],
        compiler_params=pltpu.CompilerParams(dimension_semantics=("arbitrary",),
                                             vmem_limit_bytes=VMEM_LIMIT),
        name="in_proj",
    )(x2, pos2, w_in, qw, kw, invf, sgn, ones)


def _attn_kernel(sinks_ref, q_ref, kp_ref, kc_ref, vp_ref, vc_ref, za_ref, nw_ref, o_ref, acc_ref,
                 *, tiles_per_seq):
    blocks = q_ref.shape[0] // BLOCK
    seq_start = pl.program_id(0) % tiles_per_seq == 0
    two = 2 * BLOCK
    ik = lax.broadcasted_iota(jnp.int32, (two, two), 0)
    iq = lax.broadcasted_iota(jnp.int32, (two, two), 1) % BLOCK
    rel = ik - iq
    band = (rel >= 1) & (rel <= BLOCK)
    band_first = band & ((ik >= BLOCK) | jnp.logical_not(seq_start))
    lane_kv = lax.broadcasted_iota(jnp.int32, (two, LANES), 1)
    first_slab = lax.broadcasted_iota(jnp.int32, (1, two), 1) < BLOCK
    row_o = lax.broadcasted_iota(jnp.int32, (LANES, two), 0)
    halves = [lane_kv < HEAD_DIM, lane_kv >= HEAD_DIM]
    zero = jnp.zeros((two, LANES), _BF16)
    key_head = lax.broadcasted_iota(jnp.int32, (2 * SUBLANES, 2 * two), 1) // two
    count_rows = (lax.broadcasted_iota(jnp.int32, (2 * SUBLANES, 2 * two), 0) == key_head).astype(_F32)

    def keys(prev_ref, cur_ref, b, sl):
        if b == 0:
            return jnp.concatenate([prev_ref[:, sl], cur_ref[0:BLOCK, sl]], axis=0)
        return cur_ref[(b - 1) * BLOCK:(b + 1) * BLOCK, sl]

    scores, vts = {}, {}
    for b in range(blocks):
        rows = slice(b * BLOCK, (b + 1) * BLOCK)
        for j in range(N_KV_HEADS // 2):
            sl = slice(j * LANES, (j + 1) * LANES)
            k2 = keys(kp_ref, kc_ref, b, sl)
            v2 = keys(vp_ref, vc_ref, b, sl)
            vts[b, j] = jnp.concatenate(
                [jnp.concatenate([jnp.where(keep, v2, zero) for keep in halves], axis=0).astype(_F32).T,
                 count_rows], axis=0).astype(_BF16)
            for t in range(KV_REP // 2):
                r0 = KV_REP * j + 2 * t
                qq = q_ref[rows, r0 * LANES:(r0 + 2) * LANES]
                qq = jnp.concatenate([qq[:, :LANES], qq[:, LANES:]], axis=0)
                for half, keep in enumerate(halves):
                    scores[b, j, t, half] = lax.dot_general(
                        jnp.where(keep, k2, zero), qq, (((1,), (1,)), ((), ())),
                        preferred_element_type=_F32)
    for b in range(blocks):
        rows = slice(b * BLOCK, (b + 1) * BLOCK)
        allowed = band_first if b == 0 else band
        for j in range(N_KV_HEADS // 2):
            for t in range(KV_REP // 2):
                r0 = KV_REP * j + 2 * t
                probs, sink_terms = [], []
                for half in range(2):
                    h0 = 2 * KV_REP * j + KV_REP * half + 2 * t
                    sink = jnp.where(first_slab, sinks_ref[h0], sinks_ref[h0 + 1])
                    s = jnp.where(allowed, scores[b, j, t, half], -1e30)
                    m = jnp.maximum(jnp.max(s, axis=0, keepdims=True), sink)
                    probs.append(jnp.exp2(s - m).astype(_BF16))
                    sink_terms.append(jnp.exp2(sink - m))
                o_t = jnp.dot(vts[b, j], jnp.concatenate(probs, axis=0), preferred_element_type=_F32)
                rdens = [1.0 / (o_t[LANES + half:LANES + half + 1] + sink_terms[half]) for half in range(2)]
                o_t = o_t[:LANES] * jnp.where(row_o < HEAD_DIM, rdens[0], rdens[1])
                for c in range(2):
                    acc_ref[rows, (r0 + c) * LANES:(r0 + c + 1) * LANES] = o_t[:, c * BLOCK:(c + 1) * BLOCK].T

    g = acc_ref[...] * za_ref[...].astype(_F32)
    rs = lax.rsqrt(jnp.mean(g * g, axis=-1, keepdims=True) + NORM_EPS)
    o_ref[...] = (g * rs * nw_ref[...]).astype(_BF16)


def _attention(sinks, q, k, v, za, nw, seq, tm):
    t = q.shape[0]
    blocks = tm // BLOCK
    cur = lambda w: pl.BlockSpec((tm, w), lambda i: (i, 0))
    prev = lambda w: pl.BlockSpec((BLOCK, w), lambda i: (jnp.maximum(i * blocks - 1, 0), 0))
    return pl.pallas_call(
        functools.partial(_attn_kernel, tiles_per_seq=seq // tm),
        grid=(t // tm,),
        in_specs=[pl.BlockSpec(memory_space=pltpu.SMEM), cur(ATTN_W), prev(KV_W), cur(KV_W),
                  prev(KV_W), cur(KV_W), cur(ATTN_W), _resident(nw.shape)],
        out_specs=cur(ATTN_W),
        out_shape=jax.ShapeDtypeStruct((t, ATTN_W), _BF16),
        scratch_shapes=[pltpu.VMEM((tm, ATTN_W), _F32)],
        compiler_params=pltpu.CompilerParams(dimension_semantics=("arbitrary",)),
        name="swa_attention",
    )(sinks, q, k, k, v, v, za, nw)


def _ssm_weights(a_re, a_im, log_step, b_re, b_im, c_re, c_im, d_skip, n_scan):
    f32 = _F32
    a_re, a_im = a_re.astype(f32), a_im.astype(f32)
    delta = jnp.exp(log_step.astype(f32))[:, None]
    zr, zi = a_re * delta, a_im * delta

    def lam_pow(exponents):
        m = jnp.asarray(np.asarray(exponents, np.float32))[:, None, None]
        mag = jnp.exp(m * zr)
        return mag * jnp.cos(m * zi), mag * jnp.sin(m * zi)

    pw_r, pw_i = lam_pow(range(SSM_Q + 1))
    rev_r, rev_i = lam_pow(range(SSM_Q - 1, -1, -1))
    lr, li = pw_r[1], pw_i[1]
    den = a_re * a_re + a_im * a_im
    fr = ((lr - 1.0) * a_re + li * a_im) / den
    fi = (li * a_re - (lr - 1.0) * a_im) / den
    b_re, b_im = b_re.astype(f32), b_im.astype(f32)
    bbr = fr[..., None] * b_re - fi[..., None] * b_im
    bbi = fr[..., None] * b_im + fi[..., None] * b_re
    c_re, c_im = c_re.astype(f32), c_im.astype(f32)

    cl_r = c_re[None] * pw_r[:, :, None, :] - c_im[None] * pw_i[:, :, None, :]
    cl_i = c_re[None] * pw_i[:, :, None, :] + c_im[None] * pw_r[:, :, None, :]

    def per_slab(a):
        a = a.reshape(SSM_Q, N_SLABS, SLAB_GROUPS * SSM_H, a.shape[-1])
        return a.transpose(1, 0, 2, 3)

    prod = (cl_r[:SSM_Q].transpose(0, 1, 3, 2)[:, :, None, :, :] * bbr.transpose(0, 2, 1)[None, :, :, :, None]
            - cl_i[:SSM_Q].transpose(0, 1, 3, 2)[:, :, None, :, :] * bbi.transpose(0, 2, 1)[None, :, :, :, None])
    kern = jnp.sum(prod, axis=3)
    skip = d_skip.astype(f32).reshape(SSM_G, SSM_H)[:, :, None] * jnp.eye(SSM_H, dtype=f32)[None]
    kern = jnp.concatenate([kern[:1] + skip[None], kern[1:]], axis=0)
    kt = jnp.pad(per_slab(kern), ((0, 0), (0, 0), (0, 0), (0, LANES - SSM_H)))

    sin_r = rev_r[..., None] * bbr[None] - rev_i[..., None] * bbi[None]
    sin_i = rev_r[..., None] * bbi[None] + rev_i[..., None] * bbr[None]
    cs = per_slab(jnp.concatenate([sin_r, sin_i], axis=2).transpose(0, 1, 3, 2))

    co = per_slab(jnp.concatenate([cl_r[1:], -cl_i[1:]], axis=3))

    exps, keep = [], []
    for k in range(3):
        exps += [SSM_Q << k] * SUBLANES
        keep += [0.0] * (1 << k) + [1.0] * (SUBLANES - (1 << k))
    exps += [SSM_Q * j for j in range(SUBLANES)] + [SSM_Q << k for k in range(3, n_scan)]
    keep += [1.0] * (SUBLANES + n_scan - 3)
    tr, ti = lam_pow(exps)
    keep = jnp.asarray(np.asarray(keep, np.float32))[:, None, None]
    rows = jnp.stack([(tr * keep).reshape(-1, N_SLABS, SLAB_GROUPS // 2, 2 * SSM_P),
                      (ti * keep).reshape(-1, N_SLABS, SLAB_GROUPS // 2, 2 * SSM_P)], axis=3)
    decay = rows.transpose(1, 0, 2, 3, 4).reshape(N_SLABS, len(exps), SLAB_STATE)
    return kt.astype(_BF16), cs.astype(_BF16), co.astype(_BF16), decay


def _expand_kernel(kt_ref, cs_ref, co_ref, w1_ref, w2_ref):
    r16 = lax.broadcasted_iota(jnp.int32, (LANES, LANES), 0)
    c16 = lax.broadcasted_iota(jnp.int32, (LANES, LANES), 1)
    rep_h = (r16 == c16 % SSM_H).astype(_BF16)
    diag_h = (r16 // SSM_H) == (c16 // SSM_H)
    rs = lax.broadcasted_iota(jnp.int32, (LANES, SLAB_STATE), 0)
    cst = lax.broadcasted_iota(jnp.int32, (LANES, SLAB_STATE), 1)
    rep_s = ((rs // SSM_P == (cst // LANES) % 2) & (rs % SSM_P == cst % SSM_P)).astype(_BF16)
    diag_s = (rs // SSM_H) == (2 * (cst // (2 * LANES)) + (cst % LANES) // SSM_P)

    zero = jnp.zeros((LANES, LANES), _BF16)
    blocks = [jnp.where(diag_h, jnp.dot(kt_ref[0, d], rep_h, preferred_element_type=_F32), 0.0).astype(_BF16)
              for d in range(SSM_Q)]
    for i in range(SSM_Q):
        for j in range(SSM_Q):
            w1_ref[0, i * LANES:(i + 1) * LANES, j * LANES:(j + 1) * LANES] = blocks[j - i] if j >= i else zero
        w_in = jnp.where(diag_s, jnp.dot(cs_ref[0, i], rep_s, preferred_element_type=_F32), 0.0)
        w1_ref[0, i * LANES:(i + 1) * LANES, SLAB_IN:] = w_in.astype(_BF16)
        w_out_t = jnp.where(diag_s, jnp.dot(co_ref[0, i], rep_s, preferred_element_type=_F32), 0.0)
        w2_ref[0, :, i * LANES:(i + 1) * LANES] = w_out_t.T.astype(_BF16)


def _expand(kt, cs, co):
    blk = pl.BlockSpec((1, SSM_Q, LANES, LANES), lambda s: (s, 0, 0, 0))
    return pl.pallas_call(
        _expand_kernel,
        grid=(N_SLABS,),
        in_specs=[blk, blk, blk],
        out_specs=[pl.BlockSpec((1, SLAB_IN, SLAB_IN + SLAB_STATE), lambda s: (s, 0, 0)),
                   pl.BlockSpec((1, SLAB_STATE, SLAB_IN), lambda s: (s, 0, 0))],
        out_shape=[jax.ShapeDtypeStruct((N_SLABS, SLAB_IN, SLAB_IN + SLAB_STATE), _BF16),
                   jax.ShapeDtypeStruct((N_SLABS, SLAB_STATE, SLAB_IN), _BF16)],
        compiler_params=pltpu.CompilerParams(dimension_semantics=("arbitrary",)),
        name="s5_expand",
    )(kt, cs, co)


def _ssm_kernel(x_ref, w1_ref, w2_ref, dec_ref, y_ref, h_ref, t_scr, y_scr):
    nc = x_ref.shape[0]
    nb = nc // SUBLANES
    piece = 2 * LANES
    powers = 3 * SUBLANES
    lvl2 = 4 * SUBLANES

    def cmul_add(acc_r, acc_i, pr, pi, tr, ti):
        return acc_r + pr * tr - pi * ti, acc_i + pr * ti + pi * tr

    def roll_in_block(a, shift):
        return pltpu.roll(a.reshape(nb, SUBLANES, LANES), shift, axis=1).reshape(nc, LANES)

    @pl.when(pl.program_id(2) == 0)
    def _():
        h_ref[...] = jnp.zeros_like(h_ref)

    x = x_ref[...]
    s_all = jnp.dot(x, w1_ref[0, :, SLAB_IN:], preferred_element_type=_F32)
    sub = lax.broadcasted_iota(jnp.int32, (nc, LANES), 0) % SUBLANES
    brow = lax.broadcasted_iota(jnp.int32, (nb, LANES), 0)
    y = None
    for p in range(SLAB_STATE // piece):
        re, im = slice(p * piece, p * piece + LANES), slice(p * piece + LANES, (p + 1) * piece)
        table = lambda row0, n: (dec_ref[0, row0:row0 + n, re], dec_ref[0, row0:row0 + n, im])
        tall = lambda a: jnp.concatenate([a] * nb, axis=0)

        sr, si = s_all[:, re], s_all[:, im]
        hin_r, hin_i = h_ref[0:1, re], h_ref[0:1, im]

        for k in range(3):
            mr, mi = table(k * SUBLANES, SUBLANES)
            sr, si = cmul_add(sr, si, tall(mr), tall(mi),
                              roll_in_block(sr, 1 << k), roll_in_block(si, 1 << k))

        t_scr[2 * p] = sr
        t_scr[2 * p + 1] = si
        br = t_scr[2 * p, pl.ds(SUBLANES - 1, nb, stride=SUBLANES), :]
        bi = t_scr[2 * p + 1, pl.ds(SUBLANES - 1, nb, stride=SUBLANES), :]
        pr, pi = table(lvl2, 1)
        br = br + jnp.where(brow == 0, pr * hin_r - pi * hin_i, 0.0)
        bi = bi + jnp.where(brow == 0, pr * hin_i + pi * hin_r, 0.0)
        for k in range(dec_ref.shape[1] - lvl2):
            shift = 1 << k
            tr = jnp.where(brow >= shift, pltpu.roll(br, shift, axis=0), 0.0)
            ti = jnp.where(brow >= shift, pltpu.roll(bi, shift, axis=0), 0.0)
            br, bi = cmul_add(br, bi, *table(lvl2 + k, 1), tr, ti)
        h_ref[0:1, re] = br[nb - 1:nb]
        h_ref[0:1, im] = bi[nb - 1:nb]
        bpr = jnp.where(brow == 0, hin_r, pltpu.roll(br, 1, axis=0))
        bpi = jnp.where(brow == 0, hin_i, pltpu.roll(bi, 1, axis=0))

        rep = lambda a: jnp.concatenate(
            [jnp.broadcast_to(a[k:k + 1], (SUBLANES, LANES)) for k in range(nb)], axis=0)
        fr, fi = table(powers, SUBLANES)
        tr = jnp.where(sub >= 1, roll_in_block(sr, 1), 0.0)
        ti = jnp.where(sub >= 1, roll_in_block(si, 1), 0.0)
        prev_r, prev_i = cmul_add(tr, ti, tall(fr), tall(fi), rep(bpr), rep(bpi))
        hprev = jnp.concatenate([prev_r, prev_i], axis=1).astype(_BF16)
        part = jnp.dot(hprev, w2_ref[0, p * piece:(p + 1) * piece, :], preferred_element_type=_F32)
        if y is None:
            y = jnp.dot(x, w1_ref[0, :, :SLAB_IN], preferred_element_type=_F32) + part
        else:
            y = y + part

    for i in range(SSM_Q):
        y_scr[pl.ds(i, nc, stride=SSM_Q), :] = y[:, i * LANES:(i + 1) * LANES]
    y_ref[...] = y_scr[...].astype(y_ref.dtype)


def _ssm(ucat, w1, w2, decay, batch, seq, nc):
    tiles = seq // (nc * SSM_Q)
    return pl.pallas_call(
        _ssm_kernel,
        grid=(N_SLABS, batch, tiles),
        in_specs=[pl.BlockSpec((nc, SLAB_IN), lambda s, b, t: (b * tiles + t, s)),
                  pl.BlockSpec((1,) + w1.shape[1:], lambda s, b, t: (s, 0, 0)),
                  pl.BlockSpec((1,) + w2.shape[1:], lambda s, b, t: (s, 0, 0)),
                  pl.BlockSpec((1,) + decay.shape[1:], lambda s, b, t: (s, 0, 0))],
        out_specs=pl.BlockSpec((nc * SSM_Q, LANES), lambda s, b, t: (b * tiles + t, s)),
        out_shape=jax.ShapeDtypeStruct((batch * seq, SSM_W), _BF16),
        scratch_shapes=[pltpu.VMEM((SUBLANES, SLAB_STATE), _F32),
                        pltpu.VMEM((SLAB_STATE // LANES, nc, LANES), _F32),
                        pltpu.VMEM((nc * SSM_Q, LANES), _F32)],
        compiler_params=pltpu.CompilerParams(
            dimension_semantics=("arbitrary", "arbitrary", "arbitrary"),
            vmem_limit_bytes=VMEM_LIMIT),
        name="s5_mixer",
    )(ucat, w1, w2, decay)


def _gelu_tanh(y):
    c = math.sqrt(2.0 / math.pi)
    return (0.5 * y) * (1.0 + jnp.tanh(y * (c + (c * 0.044715) * (y * y))))


def _out_kernel(x_ref, oa_ref, y_ref, zs_ref, wg_ref, bg_ref, wo_ref, out_ref):
    y = _gelu_tanh(y_ref[...].astype(_F32))
    gate = jnp.dot(y.astype(_BF16), wg_ref[...], preferred_element_type=_F32) + bg_ref[...]
    o = y * jax.nn.sigmoid(gate) * zs_ref[...].astype(_F32)
    rs = lax.rsqrt(jnp.mean(o * o, axis=-1, keepdims=True) + NORM_EPS)
    out_ref[...] = (x_ref[...]
                    + jnp.dot(oa_ref[...], wo_ref[:ATTN_W, :], preferred_element_type=_F32)
                    + rs * jnp.dot(o.astype(_BF16), wo_ref[ATTN_W:, :], preferred_element_type=_F32))


def _out_proj(x2, oa, y, zs, wg, bg, wo, tm):
    t = x2.shape[0]
    row = lambda w: pl.BlockSpec((tm, w), lambda i: (i, 0))
    return pl.pallas_call(
        _out_kernel,
        grid=(t // tm,),
        in_specs=[row(D_MODEL), row(ATTN_W), row(SSM_W), row(SSM_W), _resident(wg.shape),
                  _resident(bg.shape), _resident(wo.shape)],
        out_specs=row(D_MODEL),
        out_shape=jax.ShapeDtypeStruct((t, D_MODEL), x2.dtype),
        compiler_params=pltpu.CompilerParams(dimension_semantics=("arbitrary",),
                                             vmem_limit_bytes=VMEM_LIMIT),
        name="out_proj",
    )(x2, oa, y, zs, wg, bg, wo)


def _pick_tile(n, target, multiple):
    best = multiple
    for cand in range(multiple, min(n, target) + 1, multiple):
        if n % cand == 0:
            best = cand
    return best


def kernel(x, positions, norm_w, w_in, q_norm_w, k_norm_w, sinks, a_re, a_im, log_step,
           b_re, b_im, c_re, c_im, d_skip, w_glu, b_glu, attn_out_norm_w, ssm_out_norm_w, w_out):
    batch, seq, _ = x.shape
    assert seq % BLOCK == 0 and x.shape[2] == D_MODEL and w_in.shape == (D_MODEL, IN_W)
    t = batch * seq
    f32 = _F32

    w_in_k = (norm_w.astype(f32)[:, None] * w_in.astype(f32)).astype(_BF16)
    w_out_k = jnp.concatenate(
        [_pair_heads(w_out[:ATTN_W].astype(f32)),
         ssm_out_norm_w.astype(f32)[:, None] * w_out[ATTN_W:].astype(f32)], axis=0).astype(_BF16)
    log2e = math.log2(math.e)
    qw = (jnp.tile(q_norm_w.astype(f32), N_HEADS) * (log2e / math.sqrt(HEAD_DIM)))[None, :]
    kw = jnp.tile(k_norm_w.astype(f32), N_KV_HEADS)[None, :]
    inv_freq = ROPE_THETA ** (-jnp.arange(0, HEAD_DIM, 2, dtype=f32) / HEAD_DIM)
    invf = jnp.tile(inv_freq, LANES // (HEAD_DIM // 2))[None, :]
    sgn = jnp.tile(jnp.concatenate([-jnp.ones(HEAD_DIM // 2, f32), jnp.ones(HEAD_DIM // 2, f32)]),
                   LANES // HEAD_DIM)[None, :]
    head_of_lane = np.arange(MXU_DIM) // HEAD_DIM
    ones = jnp.asarray(head_of_lane[:, None] == head_of_lane[None, :], _BF16)

    x2 = x.reshape(t, D_MODEL)
    pos2 = positions.reshape(t // LANES, LANES).astype(jnp.int32)

    tm = _pick_tile(seq, 512, BLOCK)
    q, k, v, za, ucat, zs = _in_proj(x2, pos2, w_in_k, qw, kw, invf, sgn, ones, tm)

    attn_nw = _pair_heads(attn_out_norm_w.astype(f32))[None, :]
    o_attn = _attention(sinks.astype(f32) * log2e, q, k, v, za, attn_nw, seq, _pick_tile(seq, 512, BLOCK))

    nc = _pick_tile(seq // SSM_Q, 512, 16)
    assert nc & (nc - 1) == 0, "chunk rows per tile must be a power of two for the scan"
    kt, cs, co, decay = _ssm_weights(a_re, a_im, log_step, b_re, b_im, c_re, c_im, d_skip,
                                     n_scan=int(math.log2(nc)))
    w1, w2 = _expand(kt, cs, co)
    y = _ssm(ucat, w1, w2, decay, batch, seq, nc)

    out = _out_proj(x2, o_attn, y, zs, w_glu.astype(_BF16), b_glu.astype(f32)[None, :], w_out_k, tm)
    return out.reshape(x.shape)
```

```python
import functools
import math

import jax
import jax.numpy as jnp
import numpy as np
from jax import lax
from jax.experimental import pallas as pl
from jax.experimental.pallas import tpu as pltpu

D_MODEL = 2048
ATTN_W = 1024
KV_W = 256
HEAD_DIM = 64
N_HEADS = 16
N_KV_HEADS = 4
KV_REP = N_HEADS // N_KV_HEADS
SSM_W = 1024
SSM_H = 16
SSM_G = 64
SSM_P = 64
BLOCK = 128
ROPE_THETA = 10000.0
NORM_EPS = 1e-6
IN_W = 2 * ATTN_W + 2 * KV_W + 2 * SSM_W

LANES = 128
SUBLANES = 8
MXU_DIM = 256
SSM_Q = 8
SLAB_GROUPS = LANES // SSM_H
N_SLABS = SSM_W // LANES
SLAB_IN = SSM_Q * LANES
SLAB_STATE = 2 * SLAB_GROUPS * SSM_P
VMEM_LIMIT = 56 * 1024 * 1024

_BF16 = jnp.bfloat16
_F32 = jnp.float32


def _resident(shape):
    nd = len(shape)
    return pl.BlockSpec(shape, lambda *_: (0,) * nd, pipeline_mode=pl.Buffered(1))


def _pair_heads(a):
    shape = a.shape
    a = a.reshape((N_KV_HEADS // 2, 2, KV_REP, HEAD_DIM) + shape[1:])
    return jnp.swapaxes(a, 1, 2).reshape(shape)


def _swap_halves(a):
    return pltpu.roll(a, HEAD_DIM, axis=1)


def _pair_slabs(n0, n1, n2, n3):
    low = lax.broadcasted_iota(jnp.int32, n0.shape, 1) < HEAD_DIM
    return (jnp.where(low, n0, _swap_halves(n2)), jnp.where(low, _swap_halves(n0), n2),
            jnp.where(low, n1, _swap_halves(n3)), jnp.where(low, _swap_halves(n1), n3))


def _rope(t, cos, sin_signed):
    lane = lax.broadcasted_iota(jnp.int32, t.shape, 1)
    first_half = (lane % HEAD_DIM) < (HEAD_DIM // 2)
    swapped = jnp.where(first_half,
                        pltpu.roll(t, LANES - HEAD_DIM // 2, axis=1),
                        pltpu.roll(t, HEAD_DIM // 2, axis=1))
    return t * cos + swapped * sin_signed


def _in_proj_kernel(x_ref, pos_ref, w_ref, qw_ref, kw_ref, invf_ref, sgn_ref, ones_ref,
                    q_ref, k_ref, v_ref, za_ref, ucat_ref, zs_ref, hn_ref, u_ref):
    tm = x_ref.shape[0]
    x = x_ref[...]
    hn_ref[...] = x.astype(_BF16)
    rs = jnp.broadcast_to(lax.rsqrt(jnp.mean(x * x, axis=-1, keepdims=True) + NORM_EPS), (tm, LANES))
    wide = 2 * MXU_DIM
    rs_wide = jnp.concatenate([rs] * (wide // LANES), axis=1)

    nsub = tm // LANES
    ang = jnp.concatenate(
        [jnp.broadcast_to(pos_ref[pl.ds(pl.program_id(0) * nsub + j, 1), :].astype(_F32),
                          (LANES, LANES)).T for j in range(nsub)], axis=0) * invf_ref[...]
    cos = jnp.cos(ang)
    sin_signed = jnp.sin(ang) * sgn_ref[...]

    def proj(c0):
        return jnp.dot(hn_ref[...], w_ref[:, c0:c0 + wide], preferred_element_type=_F32) * rs_wide

    def normed_rope(acc, nw):
        ssq = jnp.dot((acc * acc).astype(_BF16), ones_ref[...], preferred_element_type=_F32)
        t = acc * lax.rsqrt(ssq * (1.0 / HEAD_DIM) + NORM_EPS) * nw
        return [_rope(t[:, s * LANES:(s + 1) * LANES], cos, sin_signed) for s in range(MXU_DIM // LANES)]

    base = 2 * ATTN_W + 2 * KV_W
    nc = tm // SSM_Q
    for c in range(SSM_W // wide):
        acc = proj(base + c * wide)
        for s in range(wide // LANES):
            u_ref[c * (wide // LANES) + s] = acc[:, s * LANES:(s + 1) * LANES]
    for s in range(N_SLABS):
        for i in range(SSM_Q):
            c0 = s * SLAB_IN + i * LANES
            ucat_ref[:, c0:c0 + LANES] = u_ref[s, pl.ds(i, nc, stride=SSM_Q), :].astype(_BF16)

    for c in range(ATTN_W // wide):
        acc = proj(c * wide)
        slabs = []
        for h in range(wide // MXU_DIM):
            n0 = c * wide + h * MXU_DIM
            slabs += normed_rope(acc[:, h * MXU_DIM:(h + 1) * MXU_DIM], qw_ref[:, n0:n0 + MXU_DIM])
        for r, slab in enumerate(_pair_slabs(*slabs)):
            q_ref[:, c * wide + r * LANES:c * wide + (r + 1) * LANES] = slab.astype(_BF16)
    acc = proj(ATTN_W)
    for s, slab in enumerate(normed_rope(acc[:, :KV_W], kw_ref[...])):
        k_ref[:, s * LANES:(s + 1) * LANES] = slab.astype(_BF16)
    v_ref[...] = acc[:, KV_W:].astype(_BF16)
    for c in range(ATTN_W // wide):
        z = proj(ATTN_W + 2 * KV_W + c * wide)
        z = z * jax.nn.sigmoid(z)
        for r, slab in enumerate(_pair_slabs(*[z[:, s * LANES:(s + 1) * LANES] for s in range(wide // LANES)])):
            za_ref[:, c * wide + r * LANES:c * wide + (r + 1) * LANES] = slab.astype(_BF16)
    for c in range(SSM_W // wide):
        z = proj(2 * ATTN_W + 2 * KV_W + SSM_W + c * wide)
        zs_ref[:, c * wide:(c + 1) * wide] = (z * jax.nn.sigmoid(z)).astype(_BF16)


def _in_proj(x2, pos2, w_in, qw, kw, invf, sgn, ones, tm):
    t = x2.shape[0]
    row = lambda w: pl.BlockSpec((tm, w), lambda i: (i, 0))
    return pl.pallas_call(
        _in_proj_kernel,
        grid=(t // tm,),
        in_specs=[row(D_MODEL), _resident(pos2.shape), _resident(w_in.shape), _resident(qw.shape),
                  _resident(kw.shape), _resident(invf.shape), _resident(sgn.shape),
                  _resident(ones.shape)],
        out_specs=[row(ATTN_W), row(KV_W), row(KV_W), row(ATTN_W),
                   pl.BlockSpec((tm // SSM_Q, N_SLABS * SLAB_IN), lambda i: (i, 0)), row(SSM_W)],
        out_shape=[jax.ShapeDtypeStruct((t, ATTN_W), _BF16),
                   jax.ShapeDtypeStruct((t, KV_W), _BF16),
                   jax.ShapeDtypeStruct((t, KV_W), _BF16),
                   jax.ShapeDtypeStruct((t, ATTN_W), _BF16),
                   jax.ShapeDtypeStruct((t // SSM_Q, N_SLABS * SLAB_IN), _BF16),
                   jax.ShapeDtypeStruct((t, SSM_W), _BF16)],
        scratch_shapes=[pltpu.VMEM((tm, D_MODEL), _BF16), pltpu.VMEM((N_SLABS, tm, LANES), _F32)],
        compiler_params=pltpu.CompilerParams(dimension_semantics=("arbitrary",),
                                             vmem_limit_bytes=VMEM_LIMIT),
        name="in_proj",
    )(x2, pos2, w_in, qw, kw, invf, sgn, ones)


def _attn_scores(q_ref, kp_ref, kc_ref, vp_ref, vc_ref):
    blocks = q_ref.shape[0] // BLOCK
    two = 2 * BLOCK
    lane_kv = lax.broadcasted_iota(jnp.int32, (two, LANES), 1)
    halves = [lane_kv < HEAD_DIM, lane_kv >= HEAD_DIM]
    zero = jnp.zeros((two, LANES), _BF16)
    key_head = lax.broadcasted_iota(jnp.int32, (2 * SUBLANES, 2 * two), 1) // two
    count_rows = (lax.broadcasted_iota(jnp.int32, (2 * SUBLANES, 2 * two), 0) == key_head).astype(_F32)

    def keys(prev_ref, cur_ref, b, sl):
        if b == 0:
            return jnp.concatenate([prev_ref[:, sl], cur_ref[0:BLOCK, sl]], axis=0)
        return cur_ref[(b - 1) * BLOCK:(b + 1) * BLOCK, sl]

    scores, vts = {}, {}
    for b in range(blocks):
        rows = slice(b * BLOCK, (b + 1) * BLOCK)
        for j in range(N_KV_HEADS // 2):
            sl = slice(j * LANES, (j + 1) * LANES)
            k2 = keys(kp_ref, kc_ref, b, sl)
            v2 = keys(vp_ref, vc_ref, b, sl)
            vts[b, j] = jnp.concatenate(
                [jnp.concatenate([jnp.where(keep, v2, zero) for keep in halves], axis=0).astype(_F32).T,
                 count_rows], axis=0).astype(_BF16)
            for t in range(KV_REP // 2):
                r0 = KV_REP * j + 2 * t
                qq = q_ref[rows, r0 * LANES:(r0 + 2) * LANES]
                qq = jnp.concatenate([qq[:, :LANES], qq[:, LANES:]], axis=0)
                for half, keep in enumerate(halves):
                    scores[b, j, t, half] = lax.dot_general(
                        jnp.where(keep, k2, zero), qq, (((1,), (1,)), ((), ())),
                        preferred_element_type=_F32)
    return scores, vts


def _attn_softmax_pv(sinks_ref, scores, vts, acc_ref, seq_start):
    blocks = acc_ref.shape[0] // BLOCK
    two = 2 * BLOCK
    ik = lax.broadcasted_iota(jnp.int32, (two, two), 0)
    iq = lax.broadcasted_iota(jnp.int32, (two, two), 1) % BLOCK
    rel = ik - iq
    band = (rel >= 1) & (rel <= BLOCK)
    band_first = band & ((ik >= BLOCK) | jnp.logical_not(seq_start))
    first_slab = lax.broadcasted_iota(jnp.int32, (1, two), 1) < BLOCK
    row_o = lax.broadcasted_iota(jnp.int32, (LANES, two), 0)
    for b in range(blocks):
        rows = slice(b * BLOCK, (b + 1) * BLOCK)
        allowed = band_first if b == 0 else band
        for j in range(N_KV_HEADS // 2):
            for t in range(KV_REP // 2):
                r0 = KV_REP * j + 2 * t
                probs, sink_terms = [], []
                for half in range(2):
                    h0 = 2 * KV_REP * j + KV_REP * half + 2 * t
                    sink = jnp.where(first_slab, sinks_ref[h0], sinks_ref[h0 + 1])
                    s = jnp.where(allowed, scores[b, j, t, half], -1e30)
                    m = jnp.maximum(jnp.max(s, axis=0, keepdims=True), sink)
                    probs.append(jnp.exp2(s - m).astype(_BF16))
                    sink_terms.append(jnp.exp2(sink - m))
                o_t = jnp.dot(vts[b, j], jnp.concatenate(probs, axis=0), preferred_element_type=_F32)
                rdens = [1.0 / (o_t[LANES + half:LANES + half + 1] + sink_terms[half]) for half in range(2)]
                o_t = o_t[:LANES] * jnp.where(row_o < HEAD_DIM, rdens[0], rdens[1])
                for c in range(2):
                    acc_ref[rows, (r0 + c) * LANES:(r0 + c + 1) * LANES] = o_t[:, c * BLOCK:(c + 1) * BLOCK].T


def _ssm_weights(a_re, a_im, log_step, b_re, b_im, c_re, c_im, d_skip, n_scan):
    f32 = _F32
    a_re, a_im = a_re.astype(f32), a_im.astype(f32)
    delta = jnp.exp(log_step.astype(f32))[:, None]
    zr, zi = a_re * delta, a_im * delta

    def lam_pow(exponents):
        m = jnp.asarray(np.asarray(exponents, np.float32))[:, None, None]
        mag = jnp.exp(m * zr)
        return mag * jnp.cos(m * zi), mag * jnp.sin(m * zi)

    pw_r, pw_i = lam_pow(range(SSM_Q + 1))
    rev_r, rev_i = lam_pow(range(SSM_Q - 1, -1, -1))
    lr, li = pw_r[1], pw_i[1]
    den = a_re * a_re + a_im * a_im
    fr = ((lr - 1.0) * a_re + li * a_im) / den
    fi = (li * a_re - (lr - 1.0) * a_im) / den
    b_re, b_im = b_re.astype(f32), b_im.astype(f32)
    bbr = fr[..., None] * b_re - fi[..., None] * b_im
    bbi = fr[..., None] * b_im + fi[..., None] * b_re
    c_re, c_im = c_re.astype(f32), c_im.astype(f32)

    cl_r = c_re[None] * pw_r[:, :, None, :] - c_im[None] * pw_i[:, :, None, :]
    cl_i = c_re[None] * pw_i[:, :, None, :] + c_im[None] * pw_r[:, :, None, :]

    def per_slab(a):
        a = a.reshape(SSM_Q, N_SLABS, SLAB_GROUPS * SSM_H, a.shape[-1])
        return a.transpose(1, 0, 2, 3)

    prod = (cl_r[:SSM_Q].transpose(0, 1, 3, 2)[:, :, None, :, :] * bbr.transpose(0, 2, 1)[None, :, :, :, None]
            - cl_i[:SSM_Q].transpose(0, 1, 3, 2)[:, :, None, :, :] * bbi.transpose(0, 2, 1)[None, :, :, :, None])
    kern = jnp.sum(prod, axis=3)
    skip = d_skip.astype(f32).reshape(SSM_G, SSM_H)[:, :, None] * jnp.eye(SSM_H, dtype=f32)[None]
    kern = jnp.concatenate([kern[:1] + skip[None], kern[1:]], axis=0)
    kt = jnp.pad(per_slab(kern), ((0, 0), (0, 0), (0, 0), (0, LANES - SSM_H)))

    sin_r = rev_r[..., None] * bbr[None] - rev_i[..., None] * bbi[None]
    sin_i = rev_r[..., None] * bbi[None] + rev_i[..., None] * bbr[None]
    cs = per_slab(jnp.concatenate([sin_r, sin_i], axis=2).transpose(0, 1, 3, 2))

    co = per_slab(jnp.concatenate([cl_r[1:], -cl_i[1:]], axis=3))

    exps, keep = [], []
    for k in range(3):
        exps += [SSM_Q << k] * SUBLANES
        keep += [0.0] * (1 << k) + [1.0] * (SUBLANES - (1 << k))
    exps += [SSM_Q * j for j in range(SUBLANES)] + [SSM_Q << k for k in range(3, n_scan)]
    keep += [1.0] * (SUBLANES + n_scan - 3)
    tr, ti = lam_pow(exps)
    keep = jnp.asarray(np.asarray(keep, np.float32))[:, None, None]
    rows = jnp.stack([(tr * keep).reshape(-1, N_SLABS, SLAB_GROUPS // 2, 2 * SSM_P),
                      (ti * keep).reshape(-1, N_SLABS, SLAB_GROUPS // 2, 2 * SSM_P)], axis=3)
    decay = rows.transpose(1, 0, 2, 3, 4).reshape(N_SLABS, len(exps), SLAB_STATE)
    return kt.astype(_BF16), cs.astype(_BF16), co.astype(_BF16), decay


def _expand_kernel(kt_ref, cs_ref, co_ref, w1_ref, w2_ref):
    r16 = lax.broadcasted_iota(jnp.int32, (LANES, LANES), 0)
    c16 = lax.broadcasted_iota(jnp.int32, (LANES, LANES), 1)
    rep_h = (r16 == c16 % SSM_H).astype(_BF16)
    diag_h = (r16 // SSM_H) == (c16 // SSM_H)
    rs = lax.broadcasted_iota(jnp.int32, (LANES, SLAB_STATE), 0)
    cst = lax.broadcasted_iota(jnp.int32, (LANES, SLAB_STATE), 1)
    rep_s = ((rs // SSM_P == (cst // LANES) % 2) & (rs % SSM_P == cst % SSM_P)).astype(_BF16)
    diag_s = (rs // SSM_H) == (2 * (cst // (2 * LANES)) + (cst % LANES) // SSM_P)

    zero = jnp.zeros((LANES, LANES), _BF16)
    blocks = [jnp.where(diag_h, jnp.dot(kt_ref[0, d], rep_h, preferred_element_type=_F32), 0.0).astype(_BF16)
              for d in range(SSM_Q)]
    for i in range(SSM_Q):
        for j in range(SSM_Q):
            w1_ref[0, i * LANES:(i + 1) * LANES, j * LANES:(j + 1) * LANES] = blocks[j - i] if j >= i else zero
        w_in = jnp.where(diag_s, jnp.dot(cs_ref[0, i], rep_s, preferred_element_type=_F32), 0.0)
        w1_ref[0, i * LANES:(i + 1) * LANES, SLAB_IN:] = w_in.astype(_BF16)
        w_out_t = jnp.where(diag_s, jnp.dot(co_ref[0, i], rep_s, preferred_element_type=_F32), 0.0)
        w2_ref[0, :, i * LANES:(i + 1) * LANES] = w_out_t.T.astype(_BF16)


def _expand(kt, cs, co):
    blk = pl.BlockSpec((1, SSM_Q, LANES, LANES), lambda s: (s, 0, 0, 0))
    return pl.pallas_call(
        _expand_kernel,
        grid=(N_SLABS,),
        in_specs=[blk, blk, blk],
        out_specs=[pl.BlockSpec((1, SLAB_IN, SLAB_IN + SLAB_STATE), lambda s: (s, 0, 0)),
                   pl.BlockSpec((1, SLAB_STATE, SLAB_IN), lambda s: (s, 0, 0))],
        out_shape=[jax.ShapeDtypeStruct((N_SLABS, SLAB_IN, SLAB_IN + SLAB_STATE), _BF16),
                   jax.ShapeDtypeStruct((N_SLABS, SLAB_STATE, SLAB_IN), _BF16)],
        compiler_params=pltpu.CompilerParams(dimension_semantics=("arbitrary",)),
        name="s5_expand",
    )(kt, cs, co)


def _ssm_kernel(x_ref, w1_ref, w2_ref, dec_ref, y_ref, h_ref, t_scr, y_scr):
    nc = x_ref.shape[0]
    nb = nc // SUBLANES
    piece = 2 * LANES
    powers = 3 * SUBLANES
    lvl2 = 4 * SUBLANES

    def cmul_add(acc_r, acc_i, pr, pi, tr, ti):
        return acc_r + pr * tr - pi * ti, acc_i + pr * ti + pi * tr

    def roll_in_block(a, shift):
        return pltpu.roll(a.reshape(nb, SUBLANES, LANES), shift, axis=1).reshape(nc, LANES)

    @pl.when(pl.program_id(2) == 0)
    def _():
        h_ref[...] = jnp.zeros_like(h_ref)

    x = x_ref[...]
    s_all = jnp.dot(x, w1_ref[0, :, SLAB_IN:], preferred_element_type=_F32)
    sub = lax.broadcasted_iota(jnp.int32, (nc, LANES), 0) % SUBLANES
    brow = lax.broadcasted_iota(jnp.int32, (nb, LANES), 0)
    y = None
    for p in range(SLAB_STATE // piece):
        re, im = slice(p * piece, p * piece + LANES), slice(p * piece + LANES, (p + 1) * piece)
        table = lambda row0, n: (dec_ref[0, row0:row0 + n, re], dec_ref[0, row0:row0 + n, im])
        tall = lambda a: jnp.concatenate([a] * nb, axis=0)

        sr, si = s_all[:, re], s_all[:, im]
        hin_r, hin_i = h_ref[0:1, re], h_ref[0:1, im]

        for k in range(3):
            mr, mi = table(k * SUBLANES, SUBLANES)
            sr, si = cmul_add(sr, si, tall(mr), tall(mi),
                              roll_in_block(sr, 1 << k), roll_in_block(si, 1 << k))

        t_scr[2 * p] = sr
        t_scr[2 * p + 1] = si
        br = t_scr[2 * p, pl.ds(SUBLANES - 1, nb, stride=SUBLANES), :]
        bi = t_scr[2 * p + 1, pl.ds(SUBLANES - 1, nb, stride=SUBLANES), :]
        pr, pi = table(lvl2, 1)
        br = br + jnp.where(brow == 0, pr * hin_r - pi * hin_i, 0.0)
        bi = bi + jnp.where(brow == 0, pr * hin_i + pi * hin_r, 0.0)
        for k in range(dec_ref.shape[1] - lvl2):
            shift = 1 << k
            tr = jnp.where(brow >= shift, pltpu.roll(br, shift, axis=0), 0.0)
            ti = jnp.where(brow >= shift, pltpu.roll(bi, shift, axis=0), 0.0)
            br, bi = cmul_add(br, bi, *table(lvl2 + k, 1), tr, ti)
        h_ref[0:1, re] = br[nb - 1:nb]
        h_ref[0:1, im] = bi[nb - 1:nb]
        bpr = jnp.where(brow == 0, hin_r, pltpu.roll(br, 1, axis=0))
        bpi = jnp.where(brow == 0, hin_i, pltpu.roll(bi, 1, axis=0))

        rep = lambda a: jnp.concatenate(
            [jnp.broadcast_to(a[k:k + 1], (SUBLANES, LANES)) for k in range(nb)], axis=0)
        fr, fi = table(powers, SUBLANES)
        tr = jnp.where(sub >= 1, roll_in_block(sr, 1), 0.0)
        ti = jnp.where(sub >= 1, roll_in_block(si, 1), 0.0)
        prev_r, prev_i = cmul_add(tr, ti, tall(fr), tall(fi), rep(bpr), rep(bpi))
        hprev = jnp.concatenate([prev_r, prev_i], axis=1).astype(_BF16)
        part = jnp.dot(hprev, w2_ref[0, p * piece:(p + 1) * piece, :], preferred_element_type=_F32)
        if y is None:
            y = jnp.dot(x, w1_ref[0, :, :SLAB_IN], preferred_element_type=_F32) + part
        else:
            y = y + part

    for i in range(SSM_Q):
        y_scr[pl.ds(i, nc, stride=SSM_Q), :] = y[:, i * LANES:(i + 1) * LANES]
    y_ref[...] = y_scr[...].astype(y_ref.dtype)


def _ssm(ucat, w1, w2, decay, batch, seq, nc):
    tiles = seq // (nc * SSM_Q)
    return pl.pallas_call(
        _ssm_kernel,
        grid=(N_SLABS, batch, tiles),
        in_specs=[pl.BlockSpec((nc, SLAB_IN), lambda s, b, t: (b * tiles + t, s)),
                  pl.BlockSpec((1,) + w1.shape[1:], lambda s, b, t: (s, 0, 0)),
                  pl.BlockSpec((1,) + w2.shape[1:], lambda s, b, t: (s, 0, 0)),
                  pl.BlockSpec((1,) + decay.shape[1:], lambda s, b, t: (s, 0, 0))],
        out_specs=pl.BlockSpec((nc * SSM_Q, LANES), lambda s, b, t: (b * tiles + t, s)),
        out_shape=jax.ShapeDtypeStruct((batch * seq, SSM_W), _BF16),
        scratch_shapes=[pltpu.VMEM((SUBLANES, SLAB_STATE), _F32),
                        pltpu.VMEM((SLAB_STATE // LANES, nc, LANES), _F32),
                        pltpu.VMEM((nc * SSM_Q, LANES), _F32)],
        compiler_params=pltpu.CompilerParams(
            dimension_semantics=("arbitrary", "arbitrary", "arbitrary"),
            vmem_limit_bytes=VMEM_LIMIT),
        name="s5_mixer",
    )(ucat, w1, w2, decay)


def _gelu_tanh(y):
    c = math.sqrt(2.0 / math.pi)
    return (0.5 * y) * (1.0 + jnp.tanh(y * (c + (c * 0.044715) * (y * y))))


def _mix_out_kernel(sinks_ref, x_ref, q_ref, kp_ref, kc_ref, vp_ref, vc_ref, za_ref, y_ref, zs_ref,
                    nw_ref, wg_ref, bg_ref, wo_ref, out_ref, acc_ref, *, tiles_per_seq):
    scores, vts = _attn_scores(q_ref, kp_ref, kc_ref, vp_ref, vc_ref)

    y = _gelu_tanh(y_ref[...].astype(_F32))
    gate = jnp.dot(y.astype(_BF16), wg_ref[...], preferred_element_type=_F32) + bg_ref[...]
    o = y * jax.nn.sigmoid(gate) * zs_ref[...].astype(_F32)
    rs = lax.rsqrt(jnp.mean(o * o, axis=-1, keepdims=True) + NORM_EPS)
    out_ref[...] = x_ref[...] + rs * jnp.dot(o.astype(_BF16), wo_ref[ATTN_W:, :], preferred_element_type=_F32)

    seq_start = pl.program_id(0) % tiles_per_seq == 0
    _attn_softmax_pv(sinks_ref, scores, vts, acc_ref, seq_start)
    g = acc_ref[...] * za_ref[...].astype(_F32)
    rs = lax.rsqrt(jnp.mean(g * g, axis=-1, keepdims=True) + NORM_EPS)
    oa = (g * rs * nw_ref[...]).astype(_BF16)
    out_ref[...] += jnp.dot(oa, wo_ref[:ATTN_W, :], preferred_element_type=_F32)


def _mix_out(sinks, x2, q, k, v, za, y, zs, nw, wg, bg, wo, seq, tm):
    t = x2.shape[0]
    blocks = tm // BLOCK
    row = lambda w: pl.BlockSpec((tm, w), lambda i: (i, 0))
    prev = lambda w: pl.BlockSpec((BLOCK, w), lambda i: (jnp.maximum(i * blocks - 1, 0), 0))
    return pl.pallas_call(
        functools.partial(_mix_out_kernel, tiles_per_seq=seq // tm),
        grid=(t // tm,),
        in_specs=[pl.BlockSpec(memory_space=pltpu.SMEM), row(D_MODEL), row(ATTN_W), prev(KV_W), row(KV_W),
                  prev(KV_W), row(KV_W), row(ATTN_W), row(SSM_W), row(SSM_W), _resident(nw.shape),
                  _resident(wg.shape), _resident(bg.shape), _resident(wo.shape)],
        out_specs=row(D_MODEL),
        out_shape=jax.ShapeDtypeStruct((t, D_MODEL), x2.dtype),
        scratch_shapes=[pltpu.VMEM((tm, ATTN_W), _F32)],
        compiler_params=pltpu.CompilerParams(dimension_semantics=("arbitrary",),
                                             vmem_limit_bytes=VMEM_LIMIT),
        name="mix_out",
    )(sinks, x2, q, k, k, v, v, za, y, zs, nw, wg, bg, wo)


def _pick_tile(n, target, multiple):
    best = multiple
    for cand in range(multiple, min(n, target) + 1, multiple):
        if n % cand == 0:
            best = cand
    return best


def kernel(x, positions, norm_w, w_in, q_norm_w, k_norm_w, sinks, a_re, a_im, log_step,
           b_re, b_im, c_re, c_im, d_skip, w_glu, b_glu, attn_out_norm_w, ssm_out_norm_w, w_out):
    batch, seq, _ = x.shape
    assert seq % BLOCK == 0 and x.shape[2] == D_MODEL and w_in.shape == (D_MODEL, IN_W)
    t = batch * seq
    f32 = _F32

    w_in_k = (norm_w.astype(f32)[:, None] * w_in.astype(f32)).astype(_BF16)
    w_out_k = jnp.concatenate(
        [_pair_heads(w_out[:ATTN_W].astype(f32)),
         ssm_out_norm_w.astype(f32)[:, None] * w_out[ATTN_W:].astype(f32)], axis=0).astype(_BF16)
    log2e = math.log2(math.e)
    qw = (jnp.tile(q_norm_w.astype(f32), N_HEADS) * (log2e / math.sqrt(HEAD_DIM)))[None, :]
    kw = jnp.tile(k_norm_w.astype(f32), N_KV_HEADS)[None, :]
    inv_freq = ROPE_THETA ** (-jnp.arange(0, HEAD_DIM, 2, dtype=f32) / HEAD_DIM)
    invf = jnp.tile(inv_freq, LANES // (HEAD_DIM // 2))[None, :]
    sgn = jnp.tile(jnp.concatenate([-jnp.ones(HEAD_DIM // 2, f32), jnp.ones(HEAD_DIM // 2, f32)]),
                   LANES // HEAD_DIM)[None, :]
    head_of_lane = np.arange(MXU_DIM) // HEAD_DIM
    ones = jnp.asarray(head_of_lane[:, None] == head_of_lane[None, :], _BF16)

    x2 = x.reshape(t, D_MODEL)
    pos2 = positions.reshape(t // LANES, LANES).astype(jnp.int32)

    tm = _pick_tile(seq, 512, BLOCK)
    q, k, v, za, ucat, zs = _in_proj(x2, pos2, w_in_k, qw, kw, invf, sgn, ones, tm)


    nc = _pick_tile(seq // SSM_Q, 512, 16)
    assert nc & (nc - 1) == 0, "chunk rows per tile must be a power of two for the scan"
    kt, cs, co, decay = _ssm_weights(a_re, a_im, log_step, b_re, b_im, c_re, c_im, d_skip,
                                     n_scan=int(math.log2(nc)))
    w1, w2 = _expand(kt, cs, co)
    y = _ssm(ucat, w1, w2, decay, batch, seq, nc)

    attn_nw = _pair_heads(attn_out_norm_w.astype(f32))[None, :]
    out = _mix_out(sinks.astype(f32) * log2e, x2, q, k, v, za, y, zs, attn_nw, w_glu.astype(_BF16),
                   b_glu.astype(f32)[None, :], w_out_k, seq, tm)
    return out.reshape(x.shape)
```

```python
import functools
import math

import jax
import jax.numpy as jnp
import numpy as np
from jax import lax
from jax.experimental import pallas as pl
from jax.experimental.pallas import tpu as pltpu

D_MODEL = 2048
ATTN_W = 1024
KV_W = 256
HEAD_DIM = 64
N_HEADS = 16
N_KV_HEADS = 4
KV_REP = N_HEADS // N_KV_HEADS
SSM_W = 1024
SSM_H = 16
SSM_G = 64
SSM_P = 64
BLOCK = 128
ROPE_THETA = 10000.0
NORM_EPS = 1e-6
IN_W = 2 * ATTN_W + 2 * KV_W + 2 * SSM_W

LANES = 128
SUBLANES = 8
MXU_DIM = 256
SSM_Q = 8
SLAB_GROUPS = LANES // SSM_H
N_SLABS = SSM_W // LANES
SLAB_IN = SSM_Q * LANES
SLAB_STATE = 2 * SLAB_GROUPS * SSM_P
VMEM_LIMIT = 56 * 1024 * 1024

_BF16 = jnp.bfloat16
_F32 = jnp.float32


def _resident(shape):
    nd = len(shape)
    return pl.BlockSpec(shape, lambda *_: (0,) * nd, pipeline_mode=pl.Buffered(1))


def _pair_heads(a):
    shape = a.shape
    a = a.reshape((N_KV_HEADS // 2, 2, KV_REP, HEAD_DIM) + shape[1:])
    return jnp.swapaxes(a, 1, 2).reshape(shape)


def _swap_halves(a):
    return pltpu.roll(a, HEAD_DIM, axis=1)


def _pair_slabs(n0, n1, n2, n3):
    low = lax.broadcasted_iota(jnp.int32, n0.shape, 1) < HEAD_DIM
    return (jnp.where(low, n0, _swap_halves(n2)), jnp.where(low, _swap_halves(n0), n2),
            jnp.where(low, n1, _swap_halves(n3)), jnp.where(low, _swap_halves(n1), n3))


def _rope(t, cos, sin_signed):
    lane = lax.broadcasted_iota(jnp.int32, t.shape, 1)
    first_half = (lane % HEAD_DIM) < (HEAD_DIM // 2)
    swapped = jnp.where(first_half,
                        pltpu.roll(t, LANES - HEAD_DIM // 2, axis=1),
                        pltpu.roll(t, HEAD_DIM // 2, axis=1))
    return t * cos + swapped * sin_signed


def _in_proj_kernel(x_ref, pos_ref, w_ref, qw_ref, kw_ref, invf_ref, sgn_ref, ones_ref,
                    q_ref, k_ref, v_ref, za_ref, ucat_ref, zs_ref, hn_ref, u_ref):
    tm = x_ref.shape[0]
    x = x_ref[...]
    hn_ref[...] = x.astype(_BF16)
    rs = jnp.broadcast_to(lax.rsqrt(jnp.mean(x * x, axis=-1, keepdims=True) + NORM_EPS), (tm, LANES))
    wide = 2 * MXU_DIM
    rs_wide = jnp.concatenate([rs] * (wide // LANES), axis=1)

    nsub = tm // LANES
    ang = jnp.concatenate(
        [jnp.broadcast_to(pos_ref[pl.ds(pl.program_id(0) * nsub + j, 1), :].astype(_F32),
                          (LANES, LANES)).T for j in range(nsub)], axis=0) * invf_ref[...]
    cos = jnp.cos(ang)
    sin_signed = jnp.sin(ang) * sgn_ref[...]

    def proj(c0):
        return jnp.dot(hn_ref[...], w_ref[:, c0:c0 + wide], preferred_element_type=_F32) * rs_wide

    def normed_rope(acc, nw):
        ssq = jnp.dot((acc * acc).astype(_BF16), ones_ref[...], preferred_element_type=_F32)
        t = acc * lax.rsqrt(ssq * (1.0 / HEAD_DIM) + NORM_EPS) * nw
        return [_rope(t[:, s * LANES:(s + 1) * LANES], cos, sin_signed) for s in range(MXU_DIM // LANES)]

    base = 2 * ATTN_W + 2 * KV_W
    nc = tm // SSM_Q
    for c in range(SSM_W // wide):
        acc = proj(base + c * wide)
        for s in range(wide // LANES):
            u_ref[c * (wide // LANES) + s] = acc[:, s * LANES:(s + 1) * LANES]
    for s in range(N_SLABS):
        for i in range(SSM_Q):
            c0 = s * SLAB_IN + i * LANES
            ucat_ref[:, c0:c0 + LANES] = u_ref[s, pl.ds(i, nc, stride=SSM_Q), :].astype(_BF16)

    for c in range(ATTN_W // wide):
        acc = proj(c * wide)
        slabs = []
        for h in range(wide // MXU_DIM):
            n0 = c * wide + h * MXU_DIM
            slabs += normed_rope(acc[:, h * MXU_DIM:(h + 1) * MXU_DIM], qw_ref[:, n0:n0 + MXU_DIM])
        for r, slab in enumerate(_pair_slabs(*slabs)):
            q_ref[:, c * wide + r * LANES:c * wide + (r + 1) * LANES] = slab.astype(_BF16)
    acc = proj(ATTN_W)
    for s, slab in enumerate(normed_rope(acc[:, :KV_W], kw_ref[...])):
        k_ref[:, s * LANES:(s + 1) * LANES] = slab.astype(_BF16)
    v_ref[...] = acc[:, KV_W:].astype(_BF16)
    for c in range(ATTN_W // wide):
        z = proj(ATTN_W + 2 * KV_W + c * wide)
        z = z * jax.nn.sigmoid(z)
        for r, slab in enumerate(_pair_slabs(*[z[:, s * LANES:(s + 1) * LANES] for s in range(wide // LANES)])):
            za_ref[:, c * wide + r * LANES:c * wide + (r + 1) * LANES] = slab.astype(_BF16)
    for c in range(SSM_W // wide):
        z = proj(2 * ATTN_W + 2 * KV_W + SSM_W + c * wide)
        zs_ref[:, c * wide:(c + 1) * wide] = (z * jax.nn.sigmoid(z)).astype(_BF16)


def _in_proj(x2, pos2, w_in, qw, kw, invf, sgn, ones, tm):
    t = x2.shape[0]
    row = lambda w: pl.BlockSpec((tm, w), lambda i: (i, 0))
    return pl.pallas_call(
        _in_proj_kernel,
        grid=(t // tm,),
        in_specs=[row(D_MODEL), _resident(pos2.shape), _resident(w_in.shape), _resident(qw.shape),
                  _resident(kw.shape), _resident(invf.shape), _resident(sgn.shape),
                  _resident(ones.shape)],
        out_specs=[row(ATTN_W), row(KV_W), row(KV_W), row(ATTN_W),
                   pl.BlockSpec((tm // SSM_Q, N_SLABS * SLAB_IN), lambda i: (i, 0)), row(SSM_W)],
        out_shape=[jax.ShapeDtypeStruct((t, ATTN_W), _BF16),
                   jax.ShapeDtypeStruct((t, KV_W), _BF16),
                   jax.ShapeDtypeStruct((t, KV_W), _BF16),
                   jax.ShapeDtypeStruct((t, ATTN_W), _BF16),
                   jax.ShapeDtypeStruct((t // SSM_Q, N_SLABS * SLAB_IN), _BF16),
                   jax.ShapeDtypeStruct((t, SSM_W), _BF16)],
        scratch_shapes=[pltpu.VMEM((tm, D_MODEL), _BF16), pltpu.VMEM((N_SLABS, tm, LANES), _F32)],
        compiler_params=pltpu.CompilerParams(dimension_semantics=("arbitrary",),
                                             vmem_limit_bytes=VMEM_LIMIT),
        name="in_proj",
    )(x2, pos2, w_in, qw, kw, invf, sgn, ones)


def _attn_scores(q_ref, kp_ref, kc_ref, vp_ref, vc_ref):
    blocks = q_ref.shape[0] // BLOCK
    two = 2 * BLOCK
    lane_kv = lax.broadcasted_iota(jnp.int32, (two, LANES), 1)
    halves = [lane_kv < HEAD_DIM, lane_kv >= HEAD_DIM]
    zero = jnp.zeros((two, LANES), _BF16)
    key_head = lax.broadcasted_iota(jnp.int32, (2 * SUBLANES, 2 * two), 1) // two
    count_rows = (lax.broadcasted_iota(jnp.int32, (2 * SUBLANES, 2 * two), 0) == key_head).astype(_F32)

    def keys(prev_ref, cur_ref, b, sl):
        if b == 0:
            return jnp.concatenate([prev_ref[:, sl], cur_ref[0:BLOCK, sl]], axis=0)
        return cur_ref[(b - 1) * BLOCK:(b + 1) * BLOCK, sl]

    scores, vts = {}, {}
    for b in range(blocks):
        rows = slice(b * BLOCK, (b + 1) * BLOCK)
        for j in range(N_KV_HEADS // 2):
            sl = slice(j * LANES, (j + 1) * LANES)
            k2 = keys(kp_ref, kc_ref, b, sl)
            v2 = keys(vp_ref, vc_ref, b, sl)
            vts[b, j] = jnp.concatenate(
                [jnp.concatenate([jnp.where(keep, v2, zero) for keep in halves], axis=0).astype(_F32).T,
                 count_rows], axis=0).astype(_BF16)
            for t in range(KV_REP // 2):
                r0 = KV_REP * j + 2 * t
                qq = q_ref[rows, r0 * LANES:(r0 + 2) * LANES]
                qq = jnp.concatenate([qq[:, :LANES], qq[:, LANES:]], axis=0)
                for half, keep in enumerate(halves):
                    scores[b, j, t, half] = lax.dot_general(
                        jnp.where(keep, k2, zero), qq, (((1,), (1,)), ((), ())),
                        preferred_element_type=_F32)
    return scores, vts


def _attn_softmax_pv(sinks_ref, scores, vts, acc_ref, seq_start):
    blocks = acc_ref.shape[0] // BLOCK
    two = 2 * BLOCK
    ik = lax.broadcasted_iota(jnp.int32, (two, two), 0)
    iq = lax.broadcasted_iota(jnp.int32, (two, two), 1) % BLOCK
    rel = ik - iq
    band = (rel >= 1) & (rel <= BLOCK)
    band_first = band & ((ik >= BLOCK) | jnp.logical_not(seq_start))
    first_slab = lax.broadcasted_iota(jnp.int32, (1, two), 1) < BLOCK
    row_o = lax.broadcasted_iota(jnp.int32, (LANES, two), 0)
    for b in range(blocks):
        rows = slice(b * BLOCK, (b + 1) * BLOCK)
        allowed = band_first if b == 0 else band
        for j in range(N_KV_HEADS // 2):
            for t in range(KV_REP // 2):
                r0 = KV_REP * j + 2 * t
                probs, sink_terms = [], []
                for half in range(2):
                    h0 = 2 * KV_REP * j + KV_REP * half + 2 * t
                    sink = jnp.where(first_slab, sinks_ref[h0], sinks_ref[h0 + 1])
                    s = jnp.where(allowed, scores[b, j, t, half], -1e30)
                    m = jnp.maximum(jnp.max(s, axis=0, keepdims=True), sink)
                    probs.append(jnp.exp2(s - m).astype(_BF16))
                    sink_terms.append(jnp.exp2(sink - m))
                o_t = jnp.dot(vts[b, j], jnp.concatenate(probs, axis=0), preferred_element_type=_F32)
                rdens = [1.0 / (o_t[LANES + half:LANES + half + 1] + sink_terms[half]) for half in range(2)]
                o_t = o_t[:LANES] * jnp.where(row_o < HEAD_DIM, rdens[0], rdens[1])
                for c in range(2):
                    acc_ref[rows, (r0 + c) * LANES:(r0 + c + 1) * LANES] = o_t[:, c * BLOCK:(c + 1) * BLOCK].T


def _ssm_weights(a_re, a_im, log_step, b_re, b_im, c_re, c_im, d_skip, n_scan):
    f32 = _F32
    a_re, a_im = a_re.astype(f32), a_im.astype(f32)
    delta = jnp.exp(log_step.astype(f32))[:, None]
    zr, zi = a_re * delta, a_im * delta

    def lam_pow(exponents):
        m = jnp.asarray(np.asarray(exponents, np.float32))[:, None, None]
        mag = jnp.exp(m * zr)
        return mag * jnp.cos(m * zi), mag * jnp.sin(m * zi)

    pw_r, pw_i = lam_pow(range(SSM_Q + 1))
    rev_r, rev_i = lam_pow(range(SSM_Q - 1, -1, -1))
    lr, li = pw_r[1], pw_i[1]
    den = a_re * a_re + a_im * a_im
    fr = ((lr - 1.0) * a_re + li * a_im) / den
    fi = (li * a_re - (lr - 1.0) * a_im) / den
    b_re, b_im = b_re.astype(f32), b_im.astype(f32)
    bbr = fr[..., None] * b_re - fi[..., None] * b_im
    bbi = fr[..., None] * b_im + fi[..., None] * b_re
    c_re, c_im = c_re.astype(f32), c_im.astype(f32)

    cl_r = c_re[None] * pw_r[:, :, None, :] - c_im[None] * pw_i[:, :, None, :]
    cl_i = c_re[None] * pw_i[:, :, None, :] + c_im[None] * pw_r[:, :, None, :]

    def per_slab(a):
        a = a.reshape(a.shape[0], N_SLABS, SLAB_GROUPS * SSM_H, a.shape[-1])
        return a.transpose(1, 0, 2, 3)

    sin_r = rev_r[..., None] * bbr[None] - rev_i[..., None] * bbi[None]
    sin_i = rev_r[..., None] * bbi[None] + rev_i[..., None] * bbr[None]
    cs = per_slab(jnp.concatenate([sin_r, sin_i], axis=2).transpose(0, 1, 3, 2))

    cl = per_slab(jnp.concatenate([cl_r, -cl_i], axis=3))
    skip = d_skip.astype(f32).reshape(N_SLABS, 1, LANES)

    exps, keep = [], []
    for k in range(3):
        exps += [SSM_Q << k] * SUBLANES
        keep += [0.0] * (1 << k) + [1.0] * (SUBLANES - (1 << k))
    exps += [SSM_Q * j for j in range(SUBLANES)] + [SSM_Q << k for k in range(3, n_scan)]
    keep += [1.0] * (SUBLANES + n_scan - 3)
    tr, ti = lam_pow(exps)
    keep = jnp.asarray(np.asarray(keep, np.float32))[:, None, None]
    rows = jnp.stack([(tr * keep).reshape(-1, N_SLABS, SLAB_GROUPS // 2, 2 * SSM_P),
                      (ti * keep).reshape(-1, N_SLABS, SLAB_GROUPS // 2, 2 * SSM_P)], axis=3)
    decay = rows.transpose(1, 0, 2, 3, 4).reshape(N_SLABS, len(exps), SLAB_STATE)
    return cs, cl, skip, decay


def _expand_kernel(cs_ref, cl_ref, skip_ref, w1_ref, w2_ref):
    r16 = lax.broadcasted_iota(jnp.int32, (LANES, LANES), 0)
    c16 = lax.broadcasted_iota(jnp.int32, (LANES, LANES), 1)
    diag_h = (r16 // SSM_H) == (c16 // SSM_H)
    rs = lax.broadcasted_iota(jnp.int32, (LANES, SLAB_STATE), 0)
    cst = lax.broadcasted_iota(jnp.int32, (LANES, SLAB_STATE), 1)
    rep_s = ((rs // SSM_P == (cst // LANES) % 2) & (rs % SSM_P == cst % SSM_P)).astype(_BF16)
    diag_s = (rs // SSM_H) == (2 * (cst // (2 * LANES)) + (cst % LANES) // SSM_P)

    zero = jnp.zeros((LANES, LANES), _BF16)
    b_bar = cs_ref[0, SSM_Q - 1]
    blocks = []
    for d in range(SSM_Q):
        k_d = lax.dot_general(b_bar, cl_ref[0, d], (((1,), (1,)), ((), ())),
                              precision=lax.Precision.HIGHEST, preferred_element_type=_F32)
        if d == 0:
            k_d = k_d + jnp.where(r16 == c16, skip_ref[0], 0.0)
        blocks.append(jnp.where(diag_h, k_d, 0.0).astype(_BF16))
    for i in range(SSM_Q):
        for j in range(SSM_Q):
            w1_ref[0, i * LANES:(i + 1) * LANES, j * LANES:(j + 1) * LANES] = blocks[j - i] if j >= i else zero
        w_in = jnp.where(diag_s, jnp.dot(cs_ref[0, i].astype(_BF16), rep_s, preferred_element_type=_F32), 0.0)
        w1_ref[0, i * LANES:(i + 1) * LANES, SLAB_IN:] = w_in.astype(_BF16)
        w_out_t = jnp.where(diag_s, jnp.dot(cl_ref[0, i + 1].astype(_BF16), rep_s,
                                            preferred_element_type=_F32), 0.0)
        w2_ref[0, :, i * LANES:(i + 1) * LANES] = w_out_t.T.astype(_BF16)


def _expand(cs, cl, skip):
    blk = lambda a: pl.BlockSpec((1,) + a.shape[1:], lambda s: (s,) + (0,) * (a.ndim - 1))
    return pl.pallas_call(
        _expand_kernel,
        grid=(N_SLABS,),
        in_specs=[blk(cs), blk(cl), blk(skip)],
        out_specs=[pl.BlockSpec((1, SLAB_IN, SLAB_IN + SLAB_STATE), lambda s: (s, 0, 0)),
                   pl.BlockSpec((1, SLAB_STATE, SLAB_IN), lambda s: (s, 0, 0))],
        out_shape=[jax.ShapeDtypeStruct((N_SLABS, SLAB_IN, SLAB_IN + SLAB_STATE), _BF16),
                   jax.ShapeDtypeStruct((N_SLABS, SLAB_STATE, SLAB_IN), _BF16)],
        compiler_params=pltpu.CompilerParams(dimension_semantics=("arbitrary",)),
        name="s5_expand",
    )(cs, cl, skip)


def _ssm_kernel(x_ref, w1_ref, w2_ref, dec_ref, y_ref, h_ref, t_scr, y_scr):
    nc = x_ref.shape[0]
    nb = nc // SUBLANES
    piece = 2 * LANES
    powers = 3 * SUBLANES
    lvl2 = 4 * SUBLANES

    def cmul_add(acc_r, acc_i, pr, pi, tr, ti):
        return acc_r + pr * tr - pi * ti, acc_i + pr * ti + pi * tr

    def roll_in_block(a, shift):
        return pltpu.roll(a.reshape(nb, SUBLANES, LANES), shift, axis=1).reshape(nc, LANES)

    @pl.when(pl.program_id(2) == 0)
    def _():
        h_ref[...] = jnp.zeros_like(h_ref)

    x = x_ref[...]
    s_all = jnp.dot(x, w1_ref[0, :, SLAB_IN:], preferred_element_type=_F32)
    sub = lax.broadcasted_iota(jnp.int32, (nc, LANES), 0) % SUBLANES
    brow = lax.broadcasted_iota(jnp.int32, (nb, LANES), 0)
    y = None
    for p in range(SLAB_STATE // piece):
        re, im = slice(p * piece, p * piece + LANES), slice(p * piece + LANES, (p + 1) * piece)
        table = lambda row0, n: (dec_ref[0, row0:row0 + n, re], dec_ref[0, row0:row0 + n, im])
        tall = lambda a: jnp.concatenate([a] * nb, axis=0)

        sr, si = s_all[:, re], s_all[:, im]
        hin_r, hin_i = h_ref[0:1, re], h_ref[0:1, im]

        for k in range(3):
            mr, mi = table(k * SUBLANES, SUBLANES)
            sr, si = cmul_add(sr, si, tall(mr), tall(mi),
                              roll_in_block(sr, 1 << k), roll_in_block(si, 1 << k))

        t_scr[2 * p] = sr
        t_scr[2 * p + 1] = si
        br = t_scr[2 * p, pl.ds(SUBLANES - 1, nb, stride=SUBLANES), :]
        bi = t_scr[2 * p + 1, pl.ds(SUBLANES - 1, nb, stride=SUBLANES), :]
        pr, pi = table(lvl2, 1)
        br = br + jnp.where(brow == 0, pr * hin_r - pi * hin_i, 0.0)
        bi = bi + jnp.where(brow == 0, pr * hin_i + pi * hin_r, 0.0)
        for k in range(dec_ref.shape[1] - lvl2):
            shift = 1 << k
            tr = jnp.where(brow >= shift, pltpu.roll(br, shift, axis=0), 0.0)
            ti = jnp.where(brow >= shift, pltpu.roll(bi, shift, axis=0), 0.0)
            br, bi = cmul_add(br, bi, *table(lvl2 + k, 1), tr, ti)
        h_ref[0:1, re] = br[nb - 1:nb]
        h_ref[0:1, im] = bi[nb - 1:nb]
        bpr = jnp.where(brow == 0, hin_r, pltpu.roll(br, 1, axis=0))
        bpi = jnp.where(brow == 0, hin_i, pltpu.roll(bi, 1, axis=0))

        rep = lambda a: jnp.concatenate(
            [jnp.broadcast_to(a[k:k + 1], (SUBLANES, LANES)) for k in range(nb)], axis=0)
        fr, fi = table(powers, SUBLANES)
        tr = jnp.where(sub >= 1, roll_in_block(sr, 1), 0.0)
        ti = jnp.where(sub >= 1, roll_in_block(si, 1), 0.0)
        prev_r, prev_i = cmul_add(tr, ti, tall(fr), tall(fi), rep(bpr), rep(bpi))
        hprev = jnp.concatenate([prev_r, prev_i], axis=1).astype(_BF16)
        part = jnp.dot(hprev, w2_ref[0, p * piece:(p + 1) * piece, :], preferred_element_type=_F32)
        if y is None:
            y = jnp.dot(x, w1_ref[0, :, :SLAB_IN], preferred_element_type=_F32) + part
        else:
            y = y + part

    for i in range(SSM_Q):
        y_scr[pl.ds(i, nc, stride=SSM_Q), :] = y[:, i * LANES:(i + 1) * LANES]
    y_ref[...] = y_scr[...].astype(y_ref.dtype)


def _ssm(ucat, w1, w2, decay, batch, seq, nc):
    tiles = seq // (nc * SSM_Q)
    return pl.pallas_call(
        _ssm_kernel,
        grid=(N_SLABS, batch, tiles),
        in_specs=[pl.BlockSpec((nc, SLAB_IN), lambda s, b, t: (b * tiles + t, s)),
                  pl.BlockSpec((1,) + w1.shape[1:], lambda s, b, t: (s, 0, 0)),
                  pl.BlockSpec((1,) + w2.shape[1:], lambda s, b, t: (s, 0, 0)),
                  pl.BlockSpec((1,) + decay.shape[1:], lambda s, b, t: (s, 0, 0))],
        out_specs=pl.BlockSpec((nc * SSM_Q, LANES), lambda s, b, t: (b * tiles + t, s)),
        out_shape=jax.ShapeDtypeStruct((batch * seq, SSM_W), _BF16),
        scratch_shapes=[pltpu.VMEM((SUBLANES, SLAB_STATE), _F32),
                        pltpu.VMEM((SLAB_STATE // LANES, nc, LANES), _F32),
                        pltpu.VMEM((nc * SSM_Q, LANES), _F32)],
        compiler_params=pltpu.CompilerParams(
            dimension_semantics=("arbitrary", "arbitrary", "arbitrary"),
            vmem_limit_bytes=VMEM_LIMIT),
        name="s5_mixer",
    )(ucat, w1, w2, decay)


def _gelu_tanh(y):
    c = math.sqrt(2.0 / math.pi)
    return (0.5 * y) * (1.0 + jnp.tanh(y * (c + (c * 0.044715) * (y * y))))


def _mix_out_kernel(sinks_ref, x_ref, q_ref, kp_ref, kc_ref, vp_ref, vc_ref, za_ref, y_ref, zs_ref,
                    nw_ref, wg_ref, bg_ref, wo_ref, out_ref, acc_ref, *, tiles_per_seq):
    scores, vts = _attn_scores(q_ref, kp_ref, kc_ref, vp_ref, vc_ref)

    y = _gelu_tanh(y_ref[...].astype(_F32))
    gate = jnp.dot(y.astype(_BF16), wg_ref[...], preferred_element_type=_F32) + bg_ref[...]
    o = y * jax.nn.sigmoid(gate) * zs_ref[...].astype(_F32)
    rs = lax.rsqrt(jnp.mean(o * o, axis=-1, keepdims=True) + NORM_EPS)
    out_ref[...] = x_ref[...] + rs * jnp.dot(o.astype(_BF16), wo_ref[ATTN_W:, :], preferred_element_type=_F32)

    seq_start = pl.program_id(0) % tiles_per_seq == 0
    _attn_softmax_pv(sinks_ref, scores, vts, acc_ref, seq_start)
    g = acc_ref[...] * za_ref[...].astype(_F32)
    rs = lax.rsqrt(jnp.mean(g * g, axis=-1, keepdims=True) + NORM_EPS)
    oa = (g * rs * nw_ref[...]).astype(_BF16)
    out_ref[...] += jnp.dot(oa, wo_ref[:ATTN_W, :], preferred_element_type=_F32)


def _mix_out(sinks, x2, q, k, v, za, y, zs, nw, wg, bg, wo, seq, tm):
    t = x2.shape[0]
    blocks = tm // BLOCK
    row = lambda w: pl.BlockSpec((tm, w), lambda i: (i, 0))
    prev = lambda w: pl.BlockSpec((BLOCK, w), lambda i: (jnp.maximum(i * blocks - 1, 0), 0))
    return pl.pallas_call(
        functools.partial(_mix_out_kernel, tiles_per_seq=seq // tm),
        grid=(t // tm,),
        in_specs=[pl.BlockSpec(memory_space=pltpu.SMEM), row(D_MODEL), row(ATTN_W), prev(KV_W), row(KV_W),
                  prev(KV_W), row(KV_W), row(ATTN_W), row(SSM_W), row(SSM_W), _resident(nw.shape),
                  _resident(wg.shape), _resident(bg.shape), _resident(wo.shape)],
        out_specs=row(D_MODEL),
        out_shape=jax.ShapeDtypeStruct((t, D_MODEL), x2.dtype),
        scratch_shapes=[pltpu.VMEM((tm, ATTN_W), _F32)],
        compiler_params=pltpu.CompilerParams(dimension_semantics=("arbitrary",),
                                             vmem_limit_bytes=VMEM_LIMIT),
        name="mix_out",
    )(sinks, x2, q, k, k, v, v, za, y, zs, nw, wg, bg, wo)


def _pick_tile(n, target, multiple):
    best = multiple
    for cand in range(multiple, min(n, target) + 1, multiple):
        if n % cand == 0:
            best = cand
    return best


def kernel(x, positions, norm_w, w_in, q_norm_w, k_norm_w, sinks, a_re, a_im, log_step,
           b_re, b_im, c_re, c_im, d_skip, w_glu, b_glu, attn_out_norm_w, ssm_out_norm_w, w_out):
    batch, seq, _ = x.shape
    assert seq % BLOCK == 0 and x.shape[2] == D_MODEL and w_in.shape == (D_MODEL, IN_W)
    t = batch * seq
    f32 = _F32

    w_in_k = (norm_w.astype(f32)[:, None] * w_in.astype(f32)).astype(_BF16)
    w_out_k = jnp.concatenate(
        [_pair_heads(w_out[:ATTN_W].astype(f32)),
         ssm_out_norm_w.astype(f32)[:, None] * w_out[ATTN_W:].astype(f32)], axis=0).astype(_BF16)
    log2e = math.log2(math.e)
    qw = (jnp.tile(q_norm_w.astype(f32), N_HEADS) * (log2e / math.sqrt(HEAD_DIM)))[None, :]
    kw = jnp.tile(k_norm_w.astype(f32), N_KV_HEADS)[None, :]
    inv_freq = ROPE_THETA ** (-jnp.arange(0, HEAD_DIM, 2, dtype=f32) / HEAD_DIM)
    invf = jnp.tile(inv_freq, LANES // (HEAD_DIM // 2))[None, :]
    sgn = jnp.tile(jnp.concatenate([-jnp.ones(HEAD_DIM // 2, f32), jnp.ones(HEAD_DIM // 2, f32)]),
                   LANES // HEAD_DIM)[None, :]
    head_of_lane = np.arange(MXU_DIM) // HEAD_DIM
    ones = jnp.asarray(head_of_lane[:, None] == head_of_lane[None, :], _BF16)

    x2 = x.reshape(t, D_MODEL)
    pos2 = positions.reshape(t // LANES, LANES).astype(jnp.int32)

    tm = _pick_tile(seq, 512, BLOCK)
    q, k, v, za, ucat, zs = _in_proj(x2, pos2, w_in_k, qw, kw, invf, sgn, ones, tm)


    nc = _pick_tile(seq // SSM_Q, 512, 16)
    assert nc & (nc - 1) == 0, "chunk rows per tile must be a power of two for the scan"
    cs, cl, skip, decay = _ssm_weights(a_re, a_im, log_step, b_re, b_im, c_re, c_im, d_skip,
                                     n_scan=int(math.log2(nc)))
    w1, w2 = _expand(cs, cl, skip)
    y = _ssm(ucat, w1, w2, decay, batch, seq, nc)

    attn_nw = _pair_heads(attn_out_norm_w.astype(f32))[None, :]
    out = _mix_out(sinks.astype(f32) * log2e, x2, q, k, v, za, y, zs, attn_nw, w_glu.astype(_BF16),
                   b_glu.astype(f32)[None, :], w_out_k, seq, tm)
    return out.reshape(x.shape)
```

```python
import functools
import math

import jax
import jax.numpy as jnp
import numpy as np
from jax import lax
from jax.experimental import pallas as pl
from jax.experimental.pallas import tpu as pltpu

D_MODEL = 2048
ATTN_W = 1024
KV_W = 256
HEAD_DIM = 64
N_HEADS = 16
N_KV_HEADS = 4
KV_REP = N_HEADS // N_KV_HEADS
SSM_W = 1024
SSM_H = 16
SSM_G = 64
SSM_P = 64
BLOCK = 128
ROPE_THETA = 10000.0
NORM_EPS = 1e-6
IN_W = 2 * ATTN_W + 2 * KV_W + 2 * SSM_W

LANES = 128
SUBLANES = 8
MXU_DIM = 256
SSM_Q = 8
SLAB_GROUPS = LANES // SSM_H
N_SLABS = SSM_W // LANES
SLAB_IN = SSM_Q * LANES
SLAB_STATE = 2 * SLAB_GROUPS * SSM_P
VMEM_LIMIT = 56 * 1024 * 1024

_BF16 = jnp.bfloat16
_F32 = jnp.float32


def _resident(shape):
    nd = len(shape)
    return pl.BlockSpec(shape, lambda *_: (0,) * nd, pipeline_mode=pl.Buffered(1))


def _pair_heads(a):
    shape = a.shape
    a = a.reshape((N_KV_HEADS // 2, 2, KV_REP, HEAD_DIM) + shape[1:])
    return jnp.swapaxes(a, 1, 2).reshape(shape)


def _swap_halves(a):
    return pltpu.roll(a, HEAD_DIM, axis=1)


def _pair_slabs(n0, n1, n2, n3):
    low = lax.broadcasted_iota(jnp.int32, n0.shape, 1) < HEAD_DIM
    return (jnp.where(low, n0, _swap_halves(n2)), jnp.where(low, _swap_halves(n0), n2),
            jnp.where(low, n1, _swap_halves(n3)), jnp.where(low, _swap_halves(n1), n3))


def _rope(t, cos, sin_signed):
    lane = lax.broadcasted_iota(jnp.int32, t.shape, 1)
    first_half = (lane % HEAD_DIM) < (HEAD_DIM // 2)
    swapped = jnp.where(first_half,
                        pltpu.roll(t, LANES - HEAD_DIM // 2, axis=1),
                        pltpu.roll(t, HEAD_DIM // 2, axis=1))
    return t * cos + swapped * sin_signed


def _in_proj_kernel(x_ref, pos_ref, w_ref, qw_ref, kw_ref, invf_ref, sgn_ref, ones_ref,
                    q_ref, k_ref, v_ref, za_ref, ucat_ref, zs_ref, hn_ref, u_ref):
    tm = x_ref.shape[0]
    x = x_ref[...]
    hn_ref[...] = x.astype(_BF16)
    rs = jnp.broadcast_to(lax.rsqrt(jnp.mean(x * x, axis=-1, keepdims=True) + NORM_EPS), (tm, LANES))
    wide = 2 * MXU_DIM
    rs_wide = jnp.concatenate([rs] * (wide // LANES), axis=1)

    nsub = tm // LANES
    ang = jnp.concatenate(
        [jnp.broadcast_to(pos_ref[pl.ds(pl.program_id(0) * nsub + j, 1), :].astype(_F32),
                          (LANES, LANES)).T for j in range(nsub)], axis=0) * invf_ref[...]
    cos = jnp.cos(ang)
    sin_signed = jnp.sin(ang) * sgn_ref[...]

    def proj(c0):
        return jnp.dot(hn_ref[...], w_ref[:, c0:c0 + wide], preferred_element_type=_F32) * rs_wide

    def normed_rope(acc, nw):
        ssq = jnp.dot((acc * acc).astype(_BF16), ones_ref[...], preferred_element_type=_F32)
        t = acc * lax.rsqrt(ssq * (1.0 / HEAD_DIM) + NORM_EPS) * nw
        return [_rope(t[:, s * LANES:(s + 1) * LANES], cos, sin_signed) for s in range(MXU_DIM // LANES)]

    base = 2 * ATTN_W + 2 * KV_W
    nc = tm // SSM_Q
    for c in range(SSM_W // wide):
        acc = proj(base + c * wide)
        for s in range(wide // LANES):
            u_ref[c * (wide // LANES) + s] = acc[:, s * LANES:(s + 1) * LANES]
    for s in range(N_SLABS):
        for i in range(SSM_Q):
            c0 = s * SLAB_IN + i * LANES
            ucat_ref[:, c0:c0 + LANES] = u_ref[s, pl.ds(i, nc, stride=SSM_Q), :].astype(_BF16)

    for c in range(ATTN_W // wide):
        acc = proj(c * wide)
        slabs = []
        for h in range(wide // MXU_DIM):
            n0 = c * wide + h * MXU_DIM
            slabs += normed_rope(acc[:, h * MXU_DIM:(h + 1) * MXU_DIM], qw_ref[:, n0:n0 + MXU_DIM])
        for r, slab in enumerate(_pair_slabs(*slabs)):
            q_ref[:, c * wide + r * LANES:c * wide + (r + 1) * LANES] = slab.astype(_BF16)
    acc = proj(ATTN_W)
    for s, slab in enumerate(normed_rope(acc[:, :KV_W], kw_ref[...])):
        k_ref[:, s * LANES:(s + 1) * LANES] = slab.astype(_BF16)
    v_ref[...] = acc[:, KV_W:].astype(_BF16)
    for c in range(ATTN_W // wide):
        z = proj(ATTN_W + 2 * KV_W + c * wide)
        z = z * jax.nn.sigmoid(z)
        for r, slab in enumerate(_pair_slabs(*[z[:, s * LANES:(s + 1) * LANES] for s in range(wide // LANES)])):
            za_ref[:, c * wide + r * LANES:c * wide + (r + 1) * LANES] = slab.astype(_BF16)
    for c in range(SSM_W // wide):
        z = proj(2 * ATTN_W + 2 * KV_W + SSM_W + c * wide)
        zs_ref[:, c * wide:(c + 1) * wide] = (z * jax.nn.sigmoid(z)).astype(_BF16)


def _in_proj(x2, pos2, w_in, qw, kw, invf, sgn, ones, tm):
    t = x2.shape[0]
    row = lambda w: pl.BlockSpec((tm, w), lambda i: (i, 0))
    return pl.pallas_call(
        _in_proj_kernel,
        grid=(t // tm,),
        in_specs=[row(D_MODEL), _resident(pos2.shape), _resident(w_in.shape), _resident(qw.shape),
                  _resident(kw.shape), _resident(invf.shape), _resident(sgn.shape),
                  _resident(ones.shape)],
        out_specs=[row(ATTN_W), row(KV_W), row(KV_W), row(ATTN_W),
                   pl.BlockSpec((tm // SSM_Q, N_SLABS * SLAB_IN), lambda i: (i, 0)), row(SSM_W)],
        out_shape=[jax.ShapeDtypeStruct((t, ATTN_W), _BF16),
                   jax.ShapeDtypeStruct((t, KV_W), _BF16),
                   jax.ShapeDtypeStruct((t, KV_W), _BF16),
                   jax.ShapeDtypeStruct((t, ATTN_W), _BF16),
                   jax.ShapeDtypeStruct((t // SSM_Q, N_SLABS * SLAB_IN), _BF16),
                   jax.ShapeDtypeStruct((t, SSM_W), _BF16)],
        scratch_shapes=[pltpu.VMEM((tm, D_MODEL), _BF16), pltpu.VMEM((N_SLABS, tm, LANES), _F32)],
        compiler_params=pltpu.CompilerParams(dimension_semantics=("arbitrary",),
                                             vmem_limit_bytes=VMEM_LIMIT),
        name="in_proj",
    )(x2, pos2, w_in, qw, kw, invf, sgn, ones)


def _attn_scores(q_ref, kp_ref, kc_ref, vp_ref, vc_ref):
    blocks = q_ref.shape[0] // BLOCK
    two = 2 * BLOCK
    lane_kv = lax.broadcasted_iota(jnp.int32, (two, LANES), 1)
    halves = [lane_kv < HEAD_DIM, lane_kv >= HEAD_DIM]
    zero = jnp.zeros((two, LANES), _BF16)
    key_head = lax.broadcasted_iota(jnp.int32, (2 * SUBLANES, 2 * two), 1) // two
    count_rows = (lax.broadcasted_iota(jnp.int32, (2 * SUBLANES, 2 * two), 0) == key_head).astype(_F32)

    def keys(prev_ref, cur_ref, b, sl):
        if b == 0:
            return jnp.concatenate([prev_ref[:, sl], cur_ref[0:BLOCK, sl]], axis=0)
        return cur_ref[(b - 1) * BLOCK:(b + 1) * BLOCK, sl]

    scores, vts = {}, {}
    for b in range(blocks):
        rows = slice(b * BLOCK, (b + 1) * BLOCK)
        for j in range(N_KV_HEADS // 2):
            sl = slice(j * LANES, (j + 1) * LANES)
            k2 = keys(kp_ref, kc_ref, b, sl)
            v2 = keys(vp_ref, vc_ref, b, sl)
            vts[b, j] = jnp.concatenate(
                [jnp.concatenate([jnp.where(keep, v2, zero) for keep in halves], axis=0).astype(_F32).T,
                 count_rows], axis=0).astype(_BF16)
            for t in range(KV_REP // 2):
                r0 = KV_REP * j + 2 * t
                qq = q_ref[rows, r0 * LANES:(r0 + 2) * LANES]
                qq = jnp.concatenate([qq[:, :LANES], qq[:, LANES:]], axis=0)
                for half, keep in enumerate(halves):
                    scores[b, j, t, half] = lax.dot_general(
                        jnp.where(keep, k2, zero), qq, (((1,), (1,)), ((), ())),
                        preferred_element_type=_F32)
    return scores, vts


def _attn_softmax_pv(sinks_ref, scores, vts, acc_ref, seq_start, blocks):
    two = 2 * BLOCK
    ik = lax.broadcasted_iota(jnp.int32, (two, two), 0)
    iq = lax.broadcasted_iota(jnp.int32, (two, two), 1) % BLOCK
    rel = ik - iq
    band = (rel >= 1) & (rel <= BLOCK)
    band_first = band & ((ik >= BLOCK) | jnp.logical_not(seq_start))
    first_slab = lax.broadcasted_iota(jnp.int32, (1, two), 1) < BLOCK
    row_o = lax.broadcasted_iota(jnp.int32, (LANES, two), 0)
    for b in blocks:
        rows = slice(b * BLOCK, (b + 1) * BLOCK)
        allowed = band_first if b == 0 else band
        for j in range(N_KV_HEADS // 2):
            for t in range(KV_REP // 2):
                r0 = KV_REP * j + 2 * t
                probs, sink_terms = [], []
                for half in range(2):
                    h0 = 2 * KV_REP * j + KV_REP * half + 2 * t
                    sink = jnp.where(first_slab, sinks_ref[h0], sinks_ref[h0 + 1])
                    s = jnp.where(allowed, scores[b, j, t, half], -1e30)
                    m = jnp.maximum(jnp.max(s, axis=0, keepdims=True), sink)
                    probs.append(jnp.exp2(s - m).astype(_BF16))
                    sink_terms.append(jnp.exp2(sink - m))
                o_t = jnp.dot(vts[b, j], jnp.concatenate(probs, axis=0), preferred_element_type=_F32)
                rdens = [1.0 / (o_t[LANES + half:LANES + half + 1] + sink_terms[half]) for half in range(2)]
                o_t = o_t[:LANES] * jnp.where(row_o < HEAD_DIM, rdens[0], rdens[1])
                for c in range(2):
                    acc_ref[rows, (r0 + c) * LANES:(r0 + c + 1) * LANES] = o_t[:, c * BLOCK:(c + 1) * BLOCK].T


def _ssm_weights(a_re, a_im, log_step, b_re, b_im, c_re, c_im, d_skip, n_scan):
    f32 = _F32
    a_re, a_im = a_re.astype(f32), a_im.astype(f32)
    delta = jnp.exp(log_step.astype(f32))[:, None]
    zr, zi = a_re * delta, a_im * delta

    def lam_pow(exponents):
        m = jnp.asarray(np.asarray(exponents, np.float32))[:, None, None]
        mag = jnp.exp(m * zr)
        return mag * jnp.cos(m * zi), mag * jnp.sin(m * zi)

    pw_r, pw_i = lam_pow(range(SSM_Q + 1))
    rev_r, rev_i = lam_pow(range(SSM_Q - 1, -1, -1))
    lr, li = pw_r[1], pw_i[1]
    den = a_re * a_re + a_im * a_im
    fr = ((lr - 1.0) * a_re + li * a_im) / den
    fi = (li * a_re - (lr - 1.0) * a_im) / den
    b_re, b_im = b_re.astype(f32), b_im.astype(f32)
    bbr = fr[..., None] * b_re - fi[..., None] * b_im
    bbi = fr[..., None] * b_im + fi[..., None] * b_re
    c_re, c_im = c_re.astype(f32), c_im.astype(f32)

    cl_r = c_re[None] * pw_r[:, :, None, :] - c_im[None] * pw_i[:, :, None, :]
    cl_i = c_re[None] * pw_i[:, :, None, :] + c_im[None] * pw_r[:, :, None, :]

    def per_slab(a):
        a = a.reshape(a.shape[0], N_SLABS, SLAB_GROUPS * SSM_H, a.shape[-1])
        return a.transpose(1, 0, 2, 3)

    sin_r = rev_r[..., None] * bbr[None] - rev_i[..., None] * bbi[None]
    sin_i = rev_r[..., None] * bbi[None] + rev_i[..., None] * bbr[None]
    cs = per_slab(jnp.concatenate([sin_r, sin_i], axis=2).transpose(0, 1, 3, 2))

    cl = per_slab(jnp.concatenate([cl_r, -cl_i], axis=3))
    skip = d_skip.astype(f32).reshape(N_SLABS, 1, LANES)

    exps, keep = [], []
    for k in range(3):
        exps += [SSM_Q << k] * SUBLANES
        keep += [0.0] * (1 << k) + [1.0] * (SUBLANES - (1 << k))
    exps += [SSM_Q * j for j in range(SUBLANES)] + [SSM_Q << k for k in range(3, n_scan)]
    keep += [1.0] * (SUBLANES + n_scan - 3)
    tr, ti = lam_pow(exps)
    keep = jnp.asarray(np.asarray(keep, np.float32))[:, None, None]
    rows = jnp.stack([(tr * keep).reshape(-1, N_SLABS, SLAB_GROUPS // 2, 2 * SSM_P),
                      (ti * keep).reshape(-1, N_SLABS, SLAB_GROUPS // 2, 2 * SSM_P)], axis=3)
    decay = rows.transpose(1, 0, 2, 3, 4).reshape(N_SLABS, len(exps), SLAB_STATE)
    return cs, cl, skip, decay


def _expand_kernel(cs_ref, cl_ref, skip_ref, w1_ref, w2_ref):
    r16 = lax.broadcasted_iota(jnp.int32, (LANES, LANES), 0)
    c16 = lax.broadcasted_iota(jnp.int32, (LANES, LANES), 1)
    diag_h = (r16 // SSM_H) == (c16 // SSM_H)
    rs = lax.broadcasted_iota(jnp.int32, (LANES, SLAB_STATE), 0)
    cst = lax.broadcasted_iota(jnp.int32, (LANES, SLAB_STATE), 1)
    rep_s = ((rs // SSM_P == (cst // LANES) % 2) & (rs % SSM_P == cst % SSM_P)).astype(_BF16)
    diag_s = (rs // SSM_H) == (2 * (cst // (2 * LANES)) + (cst % LANES) // SSM_P)

    zero = jnp.zeros((LANES, LANES), _BF16)
    b_bar = cs_ref[0, SSM_Q - 1]
    blocks = []
    for d in range(SSM_Q):
        k_d = lax.dot_general(b_bar, cl_ref[0, d], (((1,), (1,)), ((), ())),
                              precision=lax.Precision.HIGHEST, preferred_element_type=_F32)
        if d == 0:
            k_d = k_d + jnp.where(r16 == c16, skip_ref[0], 0.0)
        blocks.append(jnp.where(diag_h, k_d, 0.0).astype(_BF16))
    for i in range(SSM_Q):
        for j in range(SSM_Q):
            w1_ref[0, i * LANES:(i + 1) * LANES, j * LANES:(j + 1) * LANES] = blocks[j - i] if j >= i else zero
        w_in = jnp.where(diag_s, jnp.dot(cs_ref[0, i].astype(_BF16), rep_s, preferred_element_type=_F32), 0.0)
        w1_ref[0, i * LANES:(i + 1) * LANES, SLAB_IN:] = w_in.astype(_BF16)
        w_out_t = jnp.where(diag_s, jnp.dot(cl_ref[0, i + 1].astype(_BF16), rep_s,
                                            preferred_element_type=_F32), 0.0)
        w2_ref[0, :, i * LANES:(i + 1) * LANES] = w_out_t.T.astype(_BF16)


def _expand(cs, cl, skip):
    blk = lambda a: pl.BlockSpec((1,) + a.shape[1:], lambda s: (s,) + (0,) * (a.ndim - 1))
    return pl.pallas_call(
        _expand_kernel,
        grid=(N_SLABS,),
        in_specs=[blk(cs), blk(cl), blk(skip)],
        out_specs=[pl.BlockSpec((1, SLAB_IN, SLAB_IN + SLAB_STATE), lambda s: (s, 0, 0)),
                   pl.BlockSpec((1, SLAB_STATE, SLAB_IN), lambda s: (s, 0, 0))],
        out_shape=[jax.ShapeDtypeStruct((N_SLABS, SLAB_IN, SLAB_IN + SLAB_STATE), _BF16),
                   jax.ShapeDtypeStruct((N_SLABS, SLAB_STATE, SLAB_IN), _BF16)],
        compiler_params=pltpu.CompilerParams(dimension_semantics=("arbitrary",)),
        name="s5_expand",
    )(cs, cl, skip)


def _ssm_kernel(x_ref, w1_ref, w2_ref, dec_ref, y_ref, h_ref, t_scr):
    nc = x_ref.shape[0]
    nb = nc // SUBLANES
    piece = 2 * LANES
    powers = 3 * SUBLANES
    lvl2 = 4 * SUBLANES

    def cmul_add(acc_r, acc_i, pr, pi, tr, ti):
        return acc_r + pr * tr - pi * ti, acc_i + pr * ti + pi * tr

    def roll_in_block(a, shift):
        return pltpu.roll(a.reshape(nb, SUBLANES, LANES), shift, axis=1).reshape(nc, LANES)

    @pl.when(pl.program_id(2) == 0)
    def _():
        h_ref[...] = jnp.zeros_like(h_ref)

    x = x_ref[...]
    s_all = jnp.dot(x, w1_ref[0, :, SLAB_IN:], preferred_element_type=_F32)
    sub = lax.broadcasted_iota(jnp.int32, (nc, LANES), 0) % SUBLANES
    brow = lax.broadcasted_iota(jnp.int32, (nb, LANES), 0)
    y = None
    for p in range(SLAB_STATE // piece):
        re, im = slice(p * piece, p * piece + LANES), slice(p * piece + LANES, (p + 1) * piece)
        table = lambda row0, n: (dec_ref[0, row0:row0 + n, re], dec_ref[0, row0:row0 + n, im])
        tall = lambda a: jnp.concatenate([a] * nb, axis=0)

        sr, si = s_all[:, re], s_all[:, im]
        hin_r, hin_i = h_ref[0:1, re], h_ref[0:1, im]

        for k in range(3):
            mr, mi = table(k * SUBLANES, SUBLANES)
            sr, si = cmul_add(sr, si, tall(mr), tall(mi),
                              roll_in_block(sr, 1 << k), roll_in_block(si, 1 << k))

        t_scr[2 * p] = sr
        t_scr[2 * p + 1] = si
        br = t_scr[2 * p, pl.ds(SUBLANES - 1, nb, stride=SUBLANES), :]
        bi = t_scr[2 * p + 1, pl.ds(SUBLANES - 1, nb, stride=SUBLANES), :]
        pr, pi = table(lvl2, 1)
        br = br + jnp.where(brow == 0, pr * hin_r - pi * hin_i, 0.0)
        bi = bi + jnp.where(brow == 0, pr * hin_i + pi * hin_r, 0.0)
        for k in range(dec_ref.shape[1] - lvl2):
            shift = 1 << k
            tr = jnp.where(brow >= shift, pltpu.roll(br, shift, axis=0), 0.0)
            ti = jnp.where(brow >= shift, pltpu.roll(bi, shift, axis=0), 0.0)
            br, bi = cmul_add(br, bi, *table(lvl2 + k, 1), tr, ti)
        h_ref[0:1, re] = br[nb - 1:nb]
        h_ref[0:1, im] = bi[nb - 1:nb]
        bpr = jnp.where(brow == 0, hin_r, pltpu.roll(br, 1, axis=0))
        bpi = jnp.where(brow == 0, hin_i, pltpu.roll(bi, 1, axis=0))

        rep = lambda a: jnp.concatenate(
            [jnp.broadcast_to(a[k:k + 1], (SUBLANES, LANES)) for k in range(nb)], axis=0)
        fr, fi = table(powers, SUBLANES)
        tr = jnp.where(sub >= 1, roll_in_block(sr, 1), 0.0)
        ti = jnp.where(sub >= 1, roll_in_block(si, 1), 0.0)
        prev_r, prev_i = cmul_add(tr, ti, tall(fr), tall(fi), rep(bpr), rep(bpi))
        hprev = jnp.concatenate([prev_r, prev_i], axis=1).astype(_BF16)
        part = jnp.dot(hprev, w2_ref[0, p * piece:(p + 1) * piece, :], preferred_element_type=_F32)
        if y is None:
            y = jnp.dot(x, w1_ref[0, :, :SLAB_IN], preferred_element_type=_F32) + part
        else:
            y = y + part

    for i in range(SSM_Q):
        y_ref[pl.ds(i, nc, stride=SSM_Q), :] = y[:, i * LANES:(i + 1) * LANES]


def _ssm(ucat, w1, w2, decay, batch, seq, nc):
    tiles = seq // (nc * SSM_Q)
    return pl.pallas_call(
        _ssm_kernel,
        grid=(N_SLABS, batch, tiles),
        in_specs=[pl.BlockSpec((nc, SLAB_IN), lambda s, b, t: (b * tiles + t, s)),
                  pl.BlockSpec((1,) + w1.shape[1:], lambda s, b, t: (s, 0, 0)),
                  pl.BlockSpec((1,) + w2.shape[1:], lambda s, b, t: (s, 0, 0)),
                  pl.BlockSpec((1,) + decay.shape[1:], lambda s, b, t: (s, 0, 0))],
        out_specs=pl.BlockSpec((nc * SSM_Q, LANES), lambda s, b, t: (b * tiles + t, s)),
        out_shape=jax.ShapeDtypeStruct((batch * seq, SSM_W), _F32),
        scratch_shapes=[pltpu.VMEM((SUBLANES, SLAB_STATE), _F32),
                        pltpu.VMEM((SLAB_STATE // LANES, nc, LANES), _F32)],
        compiler_params=pltpu.CompilerParams(
            dimension_semantics=("arbitrary", "arbitrary", "arbitrary"),
            vmem_limit_bytes=VMEM_LIMIT),
        name="s5_mixer",
    )(ucat, w1, w2, decay)


def _gelu_tanh(y):
    c = math.sqrt(2.0 / math.pi)
    return (0.5 * y) * (1.0 + jnp.tanh(y * (c + (c * 0.044715) * (y * y))))


def _mix_out_kernel(sinks_ref, x_ref, q_ref, kp_ref, kc_ref, vp_ref, vc_ref, za_ref, y_ref, zs_ref,
                    nw_ref, wg_ref, bg_ref, woa_ref, wos_ref, out_ref, acc_ref, *, tiles_per_seq):
    scores, vts = _attn_scores(q_ref, kp_ref, kc_ref, vp_ref, vc_ref)

    y = _gelu_tanh(y_ref[...].astype(_F32))
    gate = jnp.dot(y.astype(_BF16), wg_ref[...], preferred_element_type=_F32) + bg_ref[...]
    o = y * jax.nn.sigmoid(gate) * zs_ref[...].astype(_F32)
    rs = lax.rsqrt(jnp.mean(o * o, axis=-1, keepdims=True) + NORM_EPS)
    out_ref[...] = x_ref[...] + rs * jnp.dot(o.astype(_BF16), wos_ref[...], preferred_element_type=_F32)

    seq_start = pl.program_id(0) % tiles_per_seq == 0
    _attn_softmax_pv(sinks_ref, scores, vts, acc_ref, seq_start, range(acc_ref.shape[0] // BLOCK))
    g = acc_ref[...] * za_ref[...].astype(_F32)
    rs = lax.rsqrt(jnp.mean(g * g, axis=-1, keepdims=True) + NORM_EPS)
    oa = (g * rs * nw_ref[...]).astype(_BF16)
    out_ref[...] += jnp.dot(oa, woa_ref[...], preferred_element_type=_F32)


def _mix_out(sinks, x2, q, k, v, za, y, zs, nw, wg, bg, woa, wos, seq, tm):
    t = x2.shape[0]
    blocks = tm // BLOCK
    row = lambda w: pl.BlockSpec((tm, w), lambda i: (i, 0))
    prev = lambda w: pl.BlockSpec((BLOCK, w), lambda i: (jnp.maximum(i * blocks - 1, 0), 0))
    return pl.pallas_call(
        functools.partial(_mix_out_kernel, tiles_per_seq=seq // tm),
        grid=(t // tm,),
        in_specs=[pl.BlockSpec(memory_space=pltpu.SMEM), row(D_MODEL), row(ATTN_W), prev(KV_W), row(KV_W),
                  prev(KV_W), row(KV_W), row(ATTN_W), row(SSM_W), row(SSM_W), _resident(nw.shape),
                  _resident(wg.shape), _resident(bg.shape), _resident(woa.shape), _resident(wos.shape)],
        out_specs=row(D_MODEL),
        out_shape=jax.ShapeDtypeStruct((t, D_MODEL), x2.dtype),
        scratch_shapes=[pltpu.VMEM((tm, ATTN_W), _F32)],
        compiler_params=pltpu.CompilerParams(dimension_semantics=("arbitrary",),
                                             vmem_limit_bytes=VMEM_LIMIT),
        name="mix_out",
    )(sinks, x2, q, k, k, v, v, za, y, zs, nw, wg, bg, woa, wos)


def _pick_tile(n, target, multiple):
    best = multiple
    for cand in range(multiple, min(n, target) + 1, multiple):
        if n % cand == 0:
            best = cand
    return best


def kernel(x, positions, norm_w, w_in, q_norm_w, k_norm_w, sinks, a_re, a_im, log_step,
           b_re, b_im, c_re, c_im, d_skip, w_glu, b_glu, attn_out_norm_w, ssm_out_norm_w, w_out):
    batch, seq, _ = x.shape
    assert seq % BLOCK == 0 and x.shape[2] == D_MODEL and w_in.shape == (D_MODEL, IN_W)
    t = batch * seq
    f32 = _F32

    w_in_k = (norm_w.astype(f32)[:, None] * w_in.astype(f32)).astype(_BF16)
    wo_attn = _pair_heads(w_out[:ATTN_W]).astype(_BF16)
    wo_ssm = (ssm_out_norm_w.astype(f32)[:, None] * w_out[ATTN_W:].astype(f32)).astype(_BF16)
    log2e = math.log2(math.e)
    qw = (jnp.tile(q_norm_w.astype(f32), N_HEADS) * (log2e / math.sqrt(HEAD_DIM)))[None, :]
    kw = jnp.tile(k_norm_w.astype(f32), N_KV_HEADS)[None, :]
    inv_freq = ROPE_THETA ** (-jnp.arange(0, HEAD_DIM, 2, dtype=f32) / HEAD_DIM)
    invf = jnp.tile(inv_freq, LANES // (HEAD_DIM // 2))[None, :]
    sgn = jnp.tile(jnp.concatenate([-jnp.ones(HEAD_DIM // 2, f32), jnp.ones(HEAD_DIM // 2, f32)]),
                   LANES // HEAD_DIM)[None, :]
    head_of_lane = np.arange(MXU_DIM) // HEAD_DIM
    ones = jnp.asarray(head_of_lane[:, None] == head_of_lane[None, :], _BF16)

    x2 = x.reshape(t, D_MODEL)
    pos2 = positions.reshape(t // LANES, LANES).astype(jnp.int32)

    tm = _pick_tile(seq, 512, BLOCK)
    q, k, v, za, ucat, zs = _in_proj(x2, pos2, w_in_k, qw, kw, invf, sgn, ones, tm)


    nc = _pick_tile(seq // SSM_Q, 512, 16)
    assert nc & (nc - 1) == 0, "chunk rows per tile must be a power of two for the scan"
    cs, cl, skip, decay = _ssm_weights(a_re, a_im, log_step, b_re, b_im, c_re, c_im, d_skip,
                                     n_scan=int(math.log2(nc)))
    w1, w2 = _expand(cs, cl, skip)
    y = _ssm(ucat, w1, w2, decay, batch, seq, nc)

    attn_nw = _pair_heads(attn_out_norm_w.astype(f32))[None, :]
    out = _mix_out(sinks.astype(f32) * log2e, x2, q, k, v, za, y, zs, attn_nw, w_glu.astype(_BF16),
                   b_glu.astype(f32)[None, :], wo_attn, wo_ssm, seq, tm)
    return out.reshape(x.shape)
```

```python
import functools
import math

import jax
import jax.numpy as jnp
import numpy as np
from jax import lax
from jax.experimental import pallas as pl
from jax.experimental.pallas import tpu as pltpu

D_MODEL = 2048
ATTN_W = 1024
KV_W = 256
HEAD_DIM = 64
N_HEADS = 16
N_KV_HEADS = 4
KV_REP = N_HEADS // N_KV_HEADS
SSM_W = 1024
SSM_H = 16
SSM_G = 64
SSM_P = 64
BLOCK = 128
ROPE_THETA = 10000.0
NORM_EPS = 1e-6
IN_W = 2 * ATTN_W + 2 * KV_W + 2 * SSM_W

LANES = 128
SUBLANES = 8
MXU_DIM = 256
SSM_Q = 8
SLAB_GROUPS = LANES // SSM_H
N_SLABS = SSM_W // LANES
SLAB_IN = SSM_Q * LANES
SLAB_STATE = 2 * SLAB_GROUPS * SSM_P
VMEM_LIMIT = 56 * 1024 * 1024

_BF16 = jnp.bfloat16
_F32 = jnp.float32


def _resident(shape):
    nd = len(shape)
    return pl.BlockSpec(shape, lambda *_: (0,) * nd, pipeline_mode=pl.Buffered(1))


def _pair_heads(a):
    shape = a.shape
    a = a.reshape((N_KV_HEADS // 2, 2, KV_REP, HEAD_DIM) + shape[1:])
    return jnp.swapaxes(a, 1, 2).reshape(shape)


def _swap_halves(a):
    return pltpu.roll(a, HEAD_DIM, axis=1)


def _pair_slabs(n0, n1, n2, n3):
    low = lax.broadcasted_iota(jnp.int32, n0.shape, 1) < HEAD_DIM
    return (jnp.where(low, n0, _swap_halves(n2)), jnp.where(low, _swap_halves(n0), n2),
            jnp.where(low, n1, _swap_halves(n3)), jnp.where(low, _swap_halves(n1), n3))


def _rope(t, cos, sin_signed):
    lane = lax.broadcasted_iota(jnp.int32, t.shape, 1)
    first_half = (lane % HEAD_DIM) < (HEAD_DIM // 2)
    swapped = jnp.where(first_half,
                        pltpu.roll(t, LANES - HEAD_DIM // 2, axis=1),
                        pltpu.roll(t, HEAD_DIM // 2, axis=1))
    return t * cos + swapped * sin_signed


def _in_proj_kernel(x_ref, pos_ref, w_ref, qw_ref, kw_ref, invf_ref, sgn_ref,
                    q_ref, k_ref, v_ref, za_ref, ucat_ref, zs_ref, hn_ref, u_ref):
    tm = x_ref.shape[0]
    x = x_ref[...]
    hn_ref[...] = x.astype(_BF16)
    rs = jnp.broadcast_to(lax.rsqrt(jnp.mean(x * x, axis=-1, keepdims=True) + NORM_EPS), (tm, LANES))
    wide = 2 * MXU_DIM
    rs_wide = jnp.concatenate([rs] * (wide // LANES), axis=1)

    nsub = tm // LANES
    ang = jnp.concatenate(
        [jnp.broadcast_to(pos_ref[pl.ds(pl.program_id(0) * nsub + j, 1), :].astype(_F32),
                          (LANES, LANES)).T for j in range(nsub)], axis=0) * invf_ref[...]
    cos = jnp.cos(ang)
    sin_signed = jnp.sin(ang) * sgn_ref[...]

    def proj(c0):
        return jnp.dot(hn_ref[...], w_ref[:, c0:c0 + wide], preferred_element_type=_F32) * rs_wide

    def normed_rope(acc, nw):
        low = lax.broadcasted_iota(jnp.int32, (tm, LANES), 1) < HEAD_DIM
        out = []
        for s in range(MXU_DIM // LANES):
            a = acc[:, s * LANES:(s + 1) * LANES]
            sq = a * a
            ssq = jnp.where(low, jnp.sum(jnp.where(low, sq, 0.0), axis=-1, keepdims=True),
                            jnp.sum(jnp.where(low, 0.0, sq), axis=-1, keepdims=True))
            t = a * lax.rsqrt(ssq * (1.0 / HEAD_DIM) + NORM_EPS) * nw[:, s * LANES:(s + 1) * LANES]
            out.append(_rope(t, cos, sin_signed))
        return out

    base = 2 * ATTN_W + 2 * KV_W
    nc = tm // SSM_Q
    for c in range(SSM_W // wide):
        acc = proj(base + c * wide)
        for s in range(wide // LANES):
            u_ref[c * (wide // LANES) + s] = acc[:, s * LANES:(s + 1) * LANES]
    for s in range(N_SLABS):
        for i in range(SSM_Q):
            c0 = s * SLAB_IN + i * LANES
            ucat_ref[:, c0:c0 + LANES] = u_ref[s, pl.ds(i, nc, stride=SSM_Q), :].astype(_BF16)

    for c in range(ATTN_W // wide):
        acc = proj(c * wide)
        slabs = []
        for h in range(wide // MXU_DIM):
            n0 = c * wide + h * MXU_DIM
            slabs += normed_rope(acc[:, h * MXU_DIM:(h + 1) * MXU_DIM], qw_ref[:, n0:n0 + MXU_DIM])
        for r, slab in enumerate(_pair_slabs(*slabs)):
            q_ref[:, c * wide + r * LANES:c * wide + (r + 1) * LANES] = slab.astype(_BF16)
    acc = proj(ATTN_W)
    for s, slab in enumerate(normed_rope(acc[:, :KV_W], kw_ref[...])):
        k_ref[:, s * LANES:(s + 1) * LANES] = slab.astype(_BF16)
    v_ref[...] = acc[:, KV_W:].astype(_BF16)
    for c in range(ATTN_W // wide):
        z = proj(ATTN_W + 2 * KV_W + c * wide)
        z = z * jax.nn.sigmoid(z)
        for r, slab in enumerate(_pair_slabs(*[z[:, s * LANES:(s + 1) * LANES] for s in range(wide // LANES)])):
            za_ref[:, c * wide + r * LANES:c * wide + (r + 1) * LANES] = slab.astype(_BF16)
    for c in range(SSM_W // wide):
        z = proj(2 * ATTN_W + 2 * KV_W + SSM_W + c * wide)
        zs_ref[:, c * wide:(c + 1) * wide] = (z * jax.nn.sigmoid(z)).astype(_BF16)


def _in_proj(x2, pos2, w_in, qw, kw, invf, sgn, tm):
    t = x2.shape[0]
    row = lambda w: pl.BlockSpec((tm, w), lambda i: (i, 0))
    return pl.pallas_call(
        _in_proj_kernel,
        grid=(t // tm,),
        in_specs=[row(D_MODEL), _resident(pos2.shape), _resident(w_in.shape), _resident(qw.shape),
                  _resident(kw.shape), _resident(invf.shape), _resident(sgn.shape)],
        out_specs=[row(ATTN_W), row(KV_W), row(KV_W), row(ATTN_W),
                   pl.BlockSpec((tm // SSM_Q, N_SLABS * SLAB_IN), lambda i: (i, 0)), row(SSM_W)],
        out_shape=[jax.ShapeDtypeStruct((t, ATTN_W), _BF16),
                   jax.ShapeDtypeStruct((t, KV_W), _BF16),
                   jax.ShapeDtypeStruct((t, KV_W), _BF16),
                   jax.ShapeDtypeStruct((t, ATTN_W), _BF16),
                   jax.ShapeDtypeStruct((t // SSM_Q, N_SLABS * SLAB_IN), _BF16),
                   jax.ShapeDtypeStruct((t, SSM_W), _BF16)],
        scratch_shapes=[pltpu.VMEM((tm, D_MODEL), _BF16), pltpu.VMEM((N_SLABS, tm, LANES), _F32)],
        compiler_params=pltpu.CompilerParams(dimension_semantics=("arbitrary",),
                                             vmem_limit_bytes=VMEM_LIMIT),
        name="in_proj",
    )(x2, pos2, w_in, qw, kw, invf, sgn)


def _attn_scores(q_ref, kp_ref, kc_ref, vp_ref, vc_ref):
    blocks = q_ref.shape[0] // BLOCK
    two = 2 * BLOCK
    lane_kv = lax.broadcasted_iota(jnp.int32, (two, LANES), 1)
    halves = [lane_kv < HEAD_DIM, lane_kv >= HEAD_DIM]
    zero = jnp.zeros((two, LANES), _BF16)
    key_head = lax.broadcasted_iota(jnp.int32, (2 * SUBLANES, 2 * two), 1) // two
    count_rows = (lax.broadcasted_iota(jnp.int32, (2 * SUBLANES, 2 * two), 0) == key_head).astype(_F32)

    def keys(prev_ref, cur_ref, b, sl):
        if b == 0:
            return jnp.concatenate([prev_ref[:, sl], cur_ref[0:BLOCK, sl]], axis=0)
        return cur_ref[(b - 1) * BLOCK:(b + 1) * BLOCK, sl]

    scores, vts = {}, {}
    for b in range(blocks):
        rows = slice(b * BLOCK, (b + 1) * BLOCK)
        for j in range(N_KV_HEADS // 2):
            sl = slice(j * LANES, (j + 1) * LANES)
            k2 = keys(kp_ref, kc_ref, b, sl)
            v2 = keys(vp_ref, vc_ref, b, sl)
            vts[b, j] = jnp.concatenate(
                [jnp.concatenate([jnp.where(keep, v2, zero) for keep in halves], axis=0).astype(_F32).T,
                 count_rows], axis=0).astype(_BF16)
            for t in range(KV_REP // 2):
                r0 = KV_REP * j + 2 * t
                qq = q_ref[rows, r0 * LANES:(r0 + 2) * LANES]
                qq = jnp.concatenate([qq[:, :LANES], qq[:, LANES:]], axis=0)
                for half, keep in enumerate(halves):
                    scores[b, j, t, half] = lax.dot_general(
                        jnp.where(keep, k2, zero), qq, (((1,), (1,)), ((), ())),
                        preferred_element_type=_F32)
    return scores, vts


def _attn_softmax_pv(sinks_ref, scores, vts, acc_ref, seq_start, blocks):
    two = 2 * BLOCK
    ik = lax.broadcasted_iota(jnp.int32, (two, two), 0)
    iq = lax.broadcasted_iota(jnp.int32, (two, two), 1) % BLOCK
    rel = ik - iq
    band = (rel >= 1) & (rel <= BLOCK)
    band_first = band & ((ik >= BLOCK) | jnp.logical_not(seq_start))
    first_slab = lax.broadcasted_iota(jnp.int32, (1, two), 1) < BLOCK
    row_o = lax.broadcasted_iota(jnp.int32, (LANES, two), 0)
    for b in blocks:
        rows = slice(b * BLOCK, (b + 1) * BLOCK)
        allowed = band_first if b == 0 else band
        for j in range(N_KV_HEADS // 2):
            for t in range(KV_REP // 2):
                r0 = KV_REP * j + 2 * t
                probs, sink_terms = [], []
                for half in range(2):
                    h0 = 2 * KV_REP * j + KV_REP * half + 2 * t
                    sink = jnp.where(first_slab, sinks_ref[h0], sinks_ref[h0 + 1])
                    s = jnp.where(allowed, scores[b, j, t, half], -1e30)
                    m = jnp.maximum(jnp.max(s, axis=0, keepdims=True), sink)
                    probs.append(jnp.exp2(s - m).astype(_BF16))
                    sink_terms.append(jnp.exp2(sink - m))
                o_t = jnp.dot(vts[b, j], jnp.concatenate(probs, axis=0), preferred_element_type=_F32)
                rdens = [1.0 / (o_t[LANES + half:LANES + half + 1] + sink_terms[half]) for half in range(2)]
                o_t = o_t[:LANES] * jnp.where(row_o < HEAD_DIM, rdens[0], rdens[1])
                for c in range(2):
                    acc_ref[rows, (r0 + c) * LANES:(r0 + c + 1) * LANES] = o_t[:, c * BLOCK:(c + 1) * BLOCK].T


def _ssm_weights(a_re, a_im, log_step, b_re, b_im, c_re, c_im, d_skip, n_scan):
    f32 = _F32
    a_re, a_im = a_re.astype(f32), a_im.astype(f32)
    delta = jnp.exp(log_step.astype(f32))[:, None]
    zr, zi = a_re * delta, a_im * delta

    def lam_pow(exponents):
        m = jnp.asarray(np.asarray(exponents, np.float32))[:, None, None]
        mag = jnp.exp(m * zr)
        return mag * jnp.cos(m * zi), mag * jnp.sin(m * zi)

    pw_r, pw_i = lam_pow(range(SSM_Q + 1))
    rev_r, rev_i = lam_pow(range(SSM_Q - 1, -1, -1))
    lr, li = pw_r[1], pw_i[1]
    den = a_re * a_re + a_im * a_im
    fr = ((lr - 1.0) * a_re + li * a_im) / den
    fi = (li * a_re - (lr - 1.0) * a_im) / den
    b_re, b_im = b_re.astype(f32), b_im.astype(f32)
    bbr = fr[..., None] * b_re - fi[..., None] * b_im
    bbi = fr[..., None] * b_im + fi[..., None] * b_re
    c_re, c_im = c_re.astype(f32), c_im.astype(f32)

    cl_r = c_re[None] * pw_r[:, :, None, :] - c_im[None] * pw_i[:, :, None, :]
    cl_i = c_re[None] * pw_i[:, :, None, :] + c_im[None] * pw_r[:, :, None, :]

    def per_slab(a):
        a = a.reshape(a.shape[0], N_SLABS, SLAB_GROUPS * SSM_H, a.shape[-1])
        return a.transpose(1, 0, 2, 3)

    sin_r = rev_r[..., None] * bbr[None] - rev_i[..., None] * bbi[None]
    sin_i = rev_r[..., None] * bbi[None] + rev_i[..., None] * bbr[None]
    cs = per_slab(jnp.concatenate([sin_r, sin_i], axis=2).transpose(0, 1, 3, 2))

    cl = per_slab(jnp.concatenate([cl_r, -cl_i], axis=3))
    skip = d_skip.astype(f32).reshape(N_SLABS, 1, LANES)

    exps, keep = [], []
    for k in range(3):
        exps += [SSM_Q << k] * SUBLANES
        keep += [0.0] * (1 << k) + [1.0] * (SUBLANES - (1 << k))
    exps += [SSM_Q * j for j in range(SUBLANES)] + [SSM_Q << k for k in range(3, n_scan)]
    keep += [1.0] * (SUBLANES + n_scan - 3)
    tr, ti = lam_pow(exps)
    keep = jnp.asarray(np.asarray(keep, np.float32))[:, None, None]
    rows = jnp.stack([(tr * keep).reshape(-1, N_SLABS, SLAB_GROUPS // 2, 2 * SSM_P),
                      (ti * keep).reshape(-1, N_SLABS, SLAB_GROUPS // 2, 2 * SSM_P)], axis=3)
    decay = rows.transpose(1, 0, 2, 3, 4).reshape(N_SLABS, len(exps), SLAB_STATE)
    return cs, cl, skip, decay


def _expand_kernel(cs_ref, cl_ref, skip_ref, w1_ref, w2_ref):
    r16 = lax.broadcasted_iota(jnp.int32, (LANES, LANES), 0)
    c16 = lax.broadcasted_iota(jnp.int32, (LANES, LANES), 1)
    diag_h = (r16 // SSM_H) == (c16 // SSM_H)
    rs = lax.broadcasted_iota(jnp.int32, (LANES, SLAB_STATE), 0)
    cst = lax.broadcasted_iota(jnp.int32, (LANES, SLAB_STATE), 1)
    rep_s = ((rs // SSM_P == (cst // LANES) % 2) & (rs % SSM_P == cst % SSM_P)).astype(_BF16)
    diag_s = (rs // SSM_H) == (2 * (cst // (2 * LANES)) + (cst % LANES) // SSM_P)

    zero = jnp.zeros((LANES, LANES), _BF16)
    b_bar = cs_ref[0, SSM_Q - 1]
    blocks = []
    for d in range(SSM_Q):
        k_d = lax.dot_general(b_bar, cl_ref[0, d], (((1,), (1,)), ((), ())),
                              precision=lax.Precision.HIGHEST, preferred_element_type=_F32)
        if d == 0:
            k_d = k_d + jnp.where(r16 == c16, skip_ref[0], 0.0)
        blocks.append(jnp.where(diag_h, k_d, 0.0).astype(_BF16))
    for i in range(SSM_Q):
        for j in range(SSM_Q):
            w1_ref[0, i * LANES:(i + 1) * LANES, j * LANES:(j + 1) * LANES] = blocks[j - i] if j >= i else zero
        w_in = jnp.where(diag_s, jnp.dot(cs_ref[0, i].astype(_BF16), rep_s, preferred_element_type=_F32), 0.0)
        w1_ref[0, i * LANES:(i + 1) * LANES, SLAB_IN:] = w_in.astype(_BF16)
        w_out_t = jnp.where(diag_s, jnp.dot(cl_ref[0, i + 1].astype(_BF16), rep_s,
                                            preferred_element_type=_F32), 0.0)
        w2_ref[0, :, i * LANES:(i + 1) * LANES] = w_out_t.T.astype(_BF16)


def _expand(cs, cl, skip):
    blk = lambda a: pl.BlockSpec((1,) + a.shape[1:], lambda s: (s,) + (0,) * (a.ndim - 1))
    return pl.pallas_call(
        _expand_kernel,
        grid=(N_SLABS,),
        in_specs=[blk(cs), blk(cl), blk(skip)],
        out_specs=[pl.BlockSpec((1, SLAB_IN, SLAB_IN + SLAB_STATE), lambda s: (s, 0, 0)),
                   pl.BlockSpec((1, SLAB_STATE, SLAB_IN), lambda s: (s, 0, 0))],
        out_shape=[jax.ShapeDtypeStruct((N_SLABS, SLAB_IN, SLAB_IN + SLAB_STATE), _BF16),
                   jax.ShapeDtypeStruct((N_SLABS, SLAB_STATE, SLAB_IN), _BF16)],
        compiler_params=pltpu.CompilerParams(dimension_semantics=("arbitrary",)),
        name="s5_expand",
    )(cs, cl, skip)


def _ssm_kernel(x_ref, w1_ref, w2_ref, dec_ref, y_ref, h_ref, t_scr):
    nc = x_ref.shape[0]
    nb = nc // SUBLANES
    piece = 2 * LANES
    powers = 3 * SUBLANES
    lvl2 = 4 * SUBLANES

    def cmul_add(acc_r, acc_i, pr, pi, tr, ti):
        return acc_r + pr * tr - pi * ti, acc_i + pr * ti + pi * tr

    def roll_in_block(a, shift):
        return pltpu.roll(a.reshape(nb, SUBLANES, LANES), shift, axis=1).reshape(nc, LANES)

    @pl.when(pl.program_id(2) == 0)
    def _():
        h_ref[...] = jnp.zeros_like(h_ref)

    x = x_ref[...]
    s_all = jnp.dot(x, w1_ref[0, :, SLAB_IN:], preferred_element_type=_F32)
    sub = lax.broadcasted_iota(jnp.int32, (nc, LANES), 0) % SUBLANES
    brow = lax.broadcasted_iota(jnp.int32, (nb, LANES), 0)
    y = None
    for p in range(SLAB_STATE // piece):
        re, im = slice(p * piece, p * piece + LANES), slice(p * piece + LANES, (p + 1) * piece)
        table = lambda row0, n: (dec_ref[0, row0:row0 + n, re], dec_ref[0, row0:row0 + n, im])
        tall = lambda a: jnp.concatenate([a] * nb, axis=0)

        sr, si = s_all[:, re], s_all[:, im]
        hin_r, hin_i = h_ref[0:1, re], h_ref[0:1, im]

        for k in range(3):
            mr, mi = table(k * SUBLANES, SUBLANES)
            sr, si = cmul_add(sr, si, tall(mr), tall(mi),
                              roll_in_block(sr, 1 << k), roll_in_block(si, 1 << k))

        t_scr[2 * p] = sr
        t_scr[2 * p + 1] = si
        br = t_scr[2 * p, pl.ds(SUBLANES - 1, nb, stride=SUBLANES), :]
        bi = t_scr[2 * p + 1, pl.ds(SUBLANES - 1, nb, stride=SUBLANES), :]
        pr, pi = table(lvl2, 1)
        br = br + jnp.where(brow == 0, pr * hin_r - pi * hin_i, 0.0)
        bi = bi + jnp.where(brow == 0, pr * hin_i + pi * hin_r, 0.0)
        for k in range(dec_ref.shape[1] - lvl2):
            shift = 1 << k
            tr = jnp.where(brow >= shift, pltpu.roll(br, shift, axis=0), 0.0)
            ti = jnp.where(brow >= shift, pltpu.roll(bi, shift, axis=0), 0.0)
            br, bi = cmul_add(br, bi, *table(lvl2 + k, 1), tr, ti)
        h_ref[0:1, re] = br[nb - 1:nb]
        h_ref[0:1, im] = bi[nb - 1:nb]
        bpr = jnp.where(brow == 0, hin_r, pltpu.roll(br, 1, axis=0))
        bpi = jnp.where(brow == 0, hin_i, pltpu.roll(bi, 1, axis=0))

        rep = lambda a: jnp.concatenate(
            [jnp.broadcast_to(a[k:k + 1], (SUBLANES, LANES)) for k in range(nb)], axis=0)
        fr, fi = table(powers, SUBLANES)
        tr = jnp.where(sub >= 1, roll_in_block(sr, 1), 0.0)
        ti = jnp.where(sub >= 1, roll_in_block(si, 1), 0.0)
        prev_r, prev_i = cmul_add(tr, ti, tall(fr), tall(fi), rep(bpr), rep(bpi))
        hprev = jnp.concatenate([prev_r, prev_i], axis=1).astype(_BF16)
        part = jnp.dot(hprev, w2_ref[0, p * piece:(p + 1) * piece, :], preferred_element_type=_F32)
        if y is None:
            y = jnp.dot(x, w1_ref[0, :, :SLAB_IN], preferred_element_type=_F32) + part
        else:
            y = y + part

    for i in range(SSM_Q):
        y_ref[pl.ds(i, nc, stride=SSM_Q), :] = y[:, i * LANES:(i + 1) * LANES]


def _ssm(ucat, w1, w2, decay, batch, seq, nc):
    tiles = seq // (nc * SSM_Q)
    return pl.pallas_call(
        _ssm_kernel,
        grid=(N_SLABS, batch, tiles),
        in_specs=[pl.BlockSpec((nc, SLAB_IN), lambda s, b, t: (b * tiles + t, s)),
                  pl.BlockSpec((1,) + w1.shape[1:], lambda s, b, t: (s, 0, 0)),
                  pl.BlockSpec((1,) + w2.shape[1:], lambda s, b, t: (s, 0, 0)),
                  pl.BlockSpec((1,) + decay.shape[1:], lambda s, b, t: (s, 0, 0))],
        out_specs=pl.BlockSpec((nc * SSM_Q, LANES), lambda s, b, t: (b * tiles + t, s)),
        out_shape=jax.ShapeDtypeStruct((batch * seq, SSM_W), _F32),
        scratch_shapes=[pltpu.VMEM((SUBLANES, SLAB_STATE), _F32),
                        pltpu.VMEM((SLAB_STATE // LANES, nc, LANES), _F32)],
        compiler_params=pltpu.CompilerParams(
            dimension_semantics=("arbitrary", "arbitrary", "arbitrary"),
            vmem_limit_bytes=VMEM_LIMIT),
        name="s5_mixer",
    )(ucat, w1, w2, decay)


def _gelu_tanh(y):
    c = math.sqrt(2.0 / math.pi)
    return (0.5 * y) * (1.0 + jnp.tanh(y * (c + (c * 0.044715) * (y * y))))


def _mix_out_kernel(sinks_ref, x_ref, q_ref, kp_ref, kc_ref, vp_ref, vc_ref, za_ref, y_ref, zs_ref,
                    nw_ref, wg_ref, bg_ref, woa_ref, wos_ref, out_ref, acc_ref, *, tiles_per_seq):
    scores, vts = _attn_scores(q_ref, kp_ref, kc_ref, vp_ref, vc_ref)

    y = _gelu_tanh(y_ref[...].astype(_F32))
    gate = jnp.dot(y.astype(_BF16), wg_ref[...], preferred_element_type=_F32) + bg_ref[...]
    o = y * jax.nn.sigmoid(gate) * zs_ref[...].astype(_F32)
    rs = lax.rsqrt(jnp.mean(o * o, axis=-1, keepdims=True) + NORM_EPS)
    out_ref[...] = x_ref[...] + rs * jnp.dot(o.astype(_BF16), wos_ref[...], preferred_element_type=_F32)

    seq_start = pl.program_id(0) % tiles_per_seq == 0
    _attn_softmax_pv(sinks_ref, scores, vts, acc_ref, seq_start, range(acc_ref.shape[0] // BLOCK))
    g = acc_ref[...] * za_ref[...].astype(_F32)
    rs = lax.rsqrt(jnp.mean(g * g, axis=-1, keepdims=True) + NORM_EPS)
    oa = (g * rs * nw_ref[...]).astype(_BF16)
    out_ref[...] += jnp.dot(oa, woa_ref[...], preferred_element_type=_F32)


def _mix_out(sinks, x2, q, k, v, za, y, zs, nw, wg, bg, woa, wos, seq, tm):
    t = x2.shape[0]
    blocks = tm // BLOCK
    row = lambda w: pl.BlockSpec((tm, w), lambda i: (i, 0))
    prev = lambda w: pl.BlockSpec((BLOCK, w), lambda i: (jnp.maximum(i * blocks - 1, 0), 0))
    return pl.pallas_call(
        functools.partial(_mix_out_kernel, tiles_per_seq=seq // tm),
        grid=(t // tm,),
        in_specs=[pl.BlockSpec(memory_space=pltpu.SMEM), row(D_MODEL), row(ATTN_W), prev(KV_W), row(KV_W),
                  prev(KV_W), row(KV_W), row(ATTN_W), row(SSM_W), row(SSM_W), _resident(nw.shape),
                  _resident(wg.shape), _resident(bg.shape), _resident(woa.shape), _resident(wos.shape)],
        out_specs=row(D_MODEL),
        out_shape=jax.ShapeDtypeStruct((t, D_MODEL), x2.dtype),
        scratch_shapes=[pltpu.VMEM((tm, ATTN_W), _F32)],
        compiler_params=pltpu.CompilerParams(dimension_semantics=("arbitrary",),
                                             vmem_limit_bytes=VMEM_LIMIT),
        name="mix_out",
    )(sinks, x2, q, k, k, v, v, za, y, zs, nw, wg, bg, woa, wos)


def _pick_tile(n, target, multiple):
    best = multiple
    for cand in range(multiple, min(n, target) + 1, multiple):
        if n % cand == 0:
            best = cand
    return best


def kernel(x, positions, norm_w, w_in, q_norm_w, k_norm_w, sinks, a_re, a_im, log_step,
           b_re, b_im, c_re, c_im, d_skip, w_glu, b_glu, attn_out_norm_w, ssm_out_norm_w, w_out):
    batch, seq, _ = x.shape
    assert seq % BLOCK == 0 and x.shape[2] == D_MODEL and w_in.shape == (D_MODEL, IN_W)
    t = batch * seq
    f32 = _F32

    w_in_k = (norm_w.astype(f32)[:, None] * w_in.astype(f32)).astype(_BF16)
    wo_attn = _pair_heads(w_out[:ATTN_W]).astype(_BF16)
    wo_ssm = (ssm_out_norm_w.astype(f32)[:, None] * w_out[ATTN_W:].astype(f32)).astype(_BF16)
    log2e = math.log2(math.e)
    qw = (jnp.tile(q_norm_w.astype(f32), N_HEADS) * (log2e / math.sqrt(HEAD_DIM)))[None, :]
    kw = jnp.tile(k_norm_w.astype(f32), N_KV_HEADS)[None, :]
    inv_freq = ROPE_THETA ** (-jnp.arange(0, HEAD_DIM, 2, dtype=f32) / HEAD_DIM)
    invf = jnp.tile(inv_freq, LANES // (HEAD_DIM // 2))[None, :]
    sgn = jnp.tile(jnp.concatenate([-jnp.ones(HEAD_DIM // 2, f32), jnp.ones(HEAD_DIM // 2, f32)]),
                   LANES // HEAD_DIM)[None, :]

    x2 = x.reshape(t, D_MODEL)
    pos2 = positions.reshape(t // LANES, LANES).astype(jnp.int32)

    tm = _pick_tile(seq, 512, BLOCK)
    q, k, v, za, ucat, zs = _in_proj(x2, pos2, w_in_k, qw, kw, invf, sgn, tm)


    nc = _pick_tile(seq // SSM_Q, 512, 16)
    assert nc & (nc - 1) == 0, "chunk rows per tile must be a power of two for the scan"
    cs, cl, skip, decay = _ssm_weights(a_re, a_im, log_step, b_re, b_im, c_re, c_im, d_skip,
                                     n_scan=int(math.log2(nc)))
    w1, w2 = _expand(cs, cl, skip)
    y = _ssm(ucat, w1, w2, decay, batch, seq, nc)

    attn_nw = _pair_heads(attn_out_norm_w.astype(f32))[None, :]
    out = _mix_out(sinks.astype(f32) * log2e, x2, q, k, v, za, y, zs, attn_nw, w_glu.astype(_BF16),
                   b_glu.astype(f32)[None, :], wo_attn, wo_ssm, seq, tm)
    return out.reshape(x.shape)
```

```python
import functools
import math

import jax
import jax.numpy as jnp
import numpy as np
from jax import lax
from jax.experimental import pallas as pl
from jax.experimental.pallas import tpu as pltpu

D_MODEL = 2048
ATTN_W = 1024
KV_W = 256
HEAD_DIM = 64
N_HEADS = 16
N_KV_HEADS = 4
KV_REP = N_HEADS // N_KV_HEADS
SSM_W = 1024
SSM_H = 16
SSM_G = 64
SSM_P = 64
BLOCK = 128
ROPE_THETA = 10000.0
NORM_EPS = 1e-6
IN_W = 2 * ATTN_W + 2 * KV_W + 2 * SSM_W

LANES = 128
SUBLANES = 8
MXU_DIM = 256
SSM_Q = 8
SLAB_GROUPS = LANES // SSM_H
N_SLABS = SSM_W // LANES
SLAB_IN = SSM_Q * LANES
SLAB_STATE = 2 * SLAB_GROUPS * SSM_P
VMEM_LIMIT = 56 * 1024 * 1024

_BF16 = jnp.bfloat16
_F32 = jnp.float32


def _resident(shape):
    nd = len(shape)
    return pl.BlockSpec(shape, lambda *_: (0,) * nd, pipeline_mode=pl.Buffered(1))


def _pair_heads(a):
    shape = a.shape
    a = a.reshape((N_KV_HEADS // 2, 2, KV_REP, HEAD_DIM) + shape[1:])
    return jnp.swapaxes(a, 1, 2).reshape(shape)


def _swap_halves(a):
    return pltpu.roll(a, HEAD_DIM, axis=1)


def _pair_slabs(n0, n1, n2, n3):
    low = lax.broadcasted_iota(jnp.int32, n0.shape, 1) < HEAD_DIM
    return (jnp.where(low, n0, _swap_halves(n2)), jnp.where(low, _swap_halves(n0), n2),
            jnp.where(low, n1, _swap_halves(n3)), jnp.where(low, _swap_halves(n1), n3))


def _rope(t, cos, sin_signed):
    lane = lax.broadcasted_iota(jnp.int32, t.shape, 1)
    first_half = (lane % HEAD_DIM) < (HEAD_DIM // 2)
    swapped = jnp.where(first_half,
                        pltpu.roll(t, LANES - HEAD_DIM // 2, axis=1),
                        pltpu.roll(t, HEAD_DIM // 2, axis=1))
    return t * cos + swapped * sin_signed


def _in_proj_kernel(x_ref, pos_ref, w_ref, qw_ref, kw_ref, invf_ref, sgn_ref,
                    q_ref, k_ref, v_ref, za_ref, ucat_ref, zs_ref, hn_ref, u_ref):
    tm = x_ref.shape[0]
    x = x_ref[...]
    hn_ref[...] = x.astype(_BF16)
    rs = jnp.broadcast_to(lax.rsqrt(jnp.mean(x * x, axis=-1, keepdims=True) + NORM_EPS), (tm, LANES))
    wide = 2 * MXU_DIM
    rs_wide = jnp.concatenate([rs] * (wide // LANES), axis=1)

    nsub = tm // LANES
    ang = jnp.concatenate(
        [jnp.broadcast_to(pos_ref[pl.ds(pl.program_id(0) * nsub + j, 1), :].astype(_F32),
                          (LANES, LANES)).T for j in range(nsub)], axis=0) * invf_ref[...]
    cos = jnp.cos(ang)
    sin_signed = jnp.sin(ang) * sgn_ref[...]

    def proj(c0):
        return jnp.dot(hn_ref[...], w_ref[:, c0:c0 + wide], preferred_element_type=_F32) * rs_wide

    def normed_rope(acc, nw):
        low = lax.broadcasted_iota(jnp.int32, (tm, LANES), 1) < HEAD_DIM
        out = []
        for s in range(MXU_DIM // LANES):
            a = acc[:, s * LANES:(s + 1) * LANES]
            sq = a * a
            ssq = jnp.where(low, jnp.sum(jnp.where(low, sq, 0.0), axis=-1, keepdims=True),
                            jnp.sum(jnp.where(low, 0.0, sq), axis=-1, keepdims=True))
            t = a * lax.rsqrt(ssq * (1.0 / HEAD_DIM) + NORM_EPS) * nw[:, s * LANES:(s + 1) * LANES]
            out.append(_rope(t, cos, sin_signed))
        return out

    base = 2 * ATTN_W + 2 * KV_W
    nc = tm // SSM_Q
    for c in range(SSM_W // wide):
        acc = proj(base + c * wide)
        for s in range(wide // LANES):
            u_ref[c * (wide // LANES) + s] = acc[:, s * LANES:(s + 1) * LANES]
    for s in range(N_SLABS):
        for i in range(SSM_Q):
            c0 = s * SLAB_IN + i * LANES
            ucat_ref[:, c0:c0 + LANES] = u_ref[s, pl.ds(i, nc, stride=SSM_Q), :].astype(_BF16)

    for c in range(ATTN_W // wide):
        acc = proj(c * wide)
        slabs = []
        for h in range(wide // MXU_DIM):
            n0 = c * wide + h * MXU_DIM
            slabs += normed_rope(acc[:, h * MXU_DIM:(h + 1) * MXU_DIM], qw_ref[:, n0:n0 + MXU_DIM])
        for r, slab in enumerate(_pair_slabs(*slabs)):
            q_ref[:, c * wide + r * LANES:c * wide + (r + 1) * LANES] = slab.astype(_BF16)
    acc = proj(ATTN_W)
    for s, slab in enumerate(normed_rope(acc[:, :KV_W], kw_ref[...])):
        k_ref[:, s * LANES:(s + 1) * LANES] = slab.astype(_BF16)
    v_ref[...] = acc[:, KV_W:].astype(_BF16)
    for c in range(ATTN_W // wide):
        z = proj(ATTN_W + 2 * KV_W + c * wide)
        z = z * jax.nn.sigmoid(z)
        for r, slab in enumerate(_pair_slabs(*[z[:, s * LANES:(s + 1) * LANES] for s in range(wide // LANES)])):
            za_ref[:, c * wide + r * LANES:c * wide + (r + 1) * LANES] = slab.astype(_BF16)
    for c in range(SSM_W // wide):
        z = proj(2 * ATTN_W + 2 * KV_W + SSM_W + c * wide)
        zs_ref[:, c * wide:(c + 1) * wide] = (z * jax.nn.sigmoid(z)).astype(_BF16)


def _in_proj(x2, pos2, w_in, qw, kw, invf, sgn, tm):
    t = x2.shape[0]
    row = lambda w: pl.BlockSpec((tm, w), lambda i: (i, 0))
    return pl.pallas_call(
        _in_proj_kernel,
        grid=(t // tm,),
        in_specs=[row(D_MODEL), _resident(pos2.shape), _resident(w_in.shape), _resident(qw.shape),
                  _resident(kw.shape), _resident(invf.shape), _resident(sgn.shape)],
        out_specs=[row(ATTN_W), row(KV_W), row(KV_W), row(ATTN_W),
                   pl.BlockSpec((tm // SSM_Q, N_SLABS * SLAB_IN), lambda i: (i, 0)), row(SSM_W)],
        out_shape=[jax.ShapeDtypeStruct((t, ATTN_W), _BF16),
                   jax.ShapeDtypeStruct((t, KV_W), _BF16),
                   jax.ShapeDtypeStruct((t, KV_W), _BF16),
                   jax.ShapeDtypeStruct((t, ATTN_W), _BF16),
                   jax.ShapeDtypeStruct((t // SSM_Q, N_SLABS * SLAB_IN), _BF16),
                   jax.ShapeDtypeStruct((t, SSM_W), _BF16)],
        scratch_shapes=[pltpu.VMEM((tm, D_MODEL), _BF16), pltpu.VMEM((N_SLABS, tm, LANES), _F32)],
        compiler_params=pltpu.CompilerParams(dimension_semantics=("arbitrary",),
                                             vmem_limit_bytes=VMEM_LIMIT),
        name="in_proj",
    )(x2, pos2, w_in, qw, kw, invf, sgn)


def _attn_scores(q_ref, kp_ref, kc_ref, vp_ref, vc_ref):
    blocks = q_ref.shape[0] // BLOCK
    two = 2 * BLOCK
    lane_kv = lax.broadcasted_iota(jnp.int32, (two, LANES), 1)
    halves = [lane_kv < HEAD_DIM, lane_kv >= HEAD_DIM]
    zero = jnp.zeros((two, LANES), _BF16)
    key_head = lax.broadcasted_iota(jnp.int32, (2 * SUBLANES, 2 * two), 1) // two
    count_rows = (lax.broadcasted_iota(jnp.int32, (2 * SUBLANES, 2 * two), 0) == key_head).astype(_F32)

    def keys(prev_ref, cur_ref, b, sl):
        if b == 0:
            return jnp.concatenate([prev_ref[:, sl], cur_ref[0:BLOCK, sl]], axis=0)
        return cur_ref[(b - 1) * BLOCK:(b + 1) * BLOCK, sl]

    scores, vts = {}, {}
    for b in range(blocks):
        rows = slice(b * BLOCK, (b + 1) * BLOCK)
        for j in range(N_KV_HEADS // 2):
            sl = slice(j * LANES, (j + 1) * LANES)
            k2 = keys(kp_ref, kc_ref, b, sl)
            v2 = keys(vp_ref, vc_ref, b, sl)
            vts[b, j] = jnp.concatenate(
                [jnp.concatenate([jnp.where(keep, v2, zero) for keep in halves], axis=0).astype(_F32).T,
                 count_rows], axis=0).astype(_BF16)
            for t in range(KV_REP // 2):
                r0 = KV_REP * j + 2 * t
                qq = q_ref[rows, r0 * LANES:(r0 + 2) * LANES]
                qq = jnp.concatenate([qq[:, :LANES], qq[:, LANES:]], axis=0)
                for half, keep in enumerate(halves):
                    scores[b, j, t, half] = lax.dot_general(
                        jnp.where(keep, k2, zero), qq, (((1,), (1,)), ((), ())),
                        preferred_element_type=_F32)
    return scores, vts


def _attn_softmax_pv(sinks_ref, scores, vts, acc_ref, seq_start, blocks):
    two = 2 * BLOCK
    ik = lax.broadcasted_iota(jnp.int32, (two, two), 0)
    iq = lax.broadcasted_iota(jnp.int32, (two, two), 1) % BLOCK
    rel = ik - iq
    band = (rel >= 1) & (rel <= BLOCK)
    band_first = band & ((ik >= BLOCK) | jnp.logical_not(seq_start))
    first_slab = lax.broadcasted_iota(jnp.int32, (1, two), 1) < BLOCK
    row_o = lax.broadcasted_iota(jnp.int32, (LANES, two), 0)
    for b in blocks:
        rows = slice(b * BLOCK, (b + 1) * BLOCK)
        allowed = band_first if b == 0 else band
        for j in range(N_KV_HEADS // 2):
            for t in range(KV_REP // 2):
                r0 = KV_REP * j + 2 * t
                probs, sink_terms = [], []
                for half in range(2):
                    h0 = 2 * KV_REP * j + KV_REP * half + 2 * t
                    sink = jnp.where(first_slab, sinks_ref[h0], sinks_ref[h0 + 1])
                    s = jnp.where(allowed, scores[b, j, t, half], -1e30)
                    m = jnp.maximum(jnp.max(s, axis=0, keepdims=True), sink)
                    probs.append(jnp.exp2(s - m).astype(_BF16))
                    sink_terms.append(jnp.exp2(sink - m))
                o_t = jnp.dot(vts[b, j], jnp.concatenate(probs, axis=0), preferred_element_type=_F32)
                rdens = [1.0 / (o_t[LANES + half:LANES + half + 1] + sink_terms[half]) for half in range(2)]
                o_t = o_t[:LANES] * jnp.where(row_o < HEAD_DIM, rdens[0], rdens[1])
                for c in range(2):
                    acc_ref[rows, (r0 + c) * LANES:(r0 + c + 1) * LANES] = o_t[:, c * BLOCK:(c + 1) * BLOCK].T


def _ssm_weights(a_re, a_im, log_step, b_re, b_im, c_re, c_im, d_skip, n_scan):
    f32 = _F32
    a_re, a_im = a_re.astype(f32), a_im.astype(f32)
    delta = jnp.exp(log_step.astype(f32))[:, None]
    zr, zi = a_re * delta, a_im * delta

    def lam_pow(exponents):
        m = jnp.asarray(np.asarray(exponents, np.float32))[:, None, None]
        mag = jnp.exp(m * zr)
        return mag * jnp.cos(m * zi), mag * jnp.sin(m * zi)

    pw_r, pw_i = lam_pow(range(SSM_Q + 1))
    rev_r, rev_i = lam_pow(range(SSM_Q - 1, -1, -1))
    lr, li = pw_r[1], pw_i[1]
    den = a_re * a_re + a_im * a_im
    fr = ((lr - 1.0) * a_re + li * a_im) / den
    fi = (li * a_re - (lr - 1.0) * a_im) / den
    b_re, b_im = b_re.astype(f32), b_im.astype(f32)
    bbr = fr[..., None] * b_re - fi[..., None] * b_im
    bbi = fr[..., None] * b_im + fi[..., None] * b_re
    c_re, c_im = c_re.astype(f32), c_im.astype(f32)

    cl_r = c_re[None] * pw_r[:, :, None, :] - c_im[None] * pw_i[:, :, None, :]
    cl_i = c_re[None] * pw_i[:, :, None, :] + c_im[None] * pw_r[:, :, None, :]

    def per_slab(a):
        a = a.reshape(a.shape[0], N_SLABS, SLAB_GROUPS * SSM_H, a.shape[-1])
        return a.transpose(1, 0, 2, 3)

    sin_r = rev_r[..., None] * bbr[None] - rev_i[..., None] * bbi[None]
    sin_i = rev_r[..., None] * bbi[None] + rev_i[..., None] * bbr[None]
    cs = per_slab(jnp.concatenate([sin_r, sin_i], axis=2).transpose(0, 1, 3, 2))

    cl = per_slab(jnp.concatenate([cl_r, -cl_i], axis=3))
    skip = d_skip.astype(f32).reshape(N_SLABS, 1, LANES)

    exps, keep = [], []
    for k in range(3):
        exps += [SSM_Q << k] * SUBLANES
        keep += [0.0] * (1 << k) + [1.0] * (SUBLANES - (1 << k))
    exps += [SSM_Q * j for j in range(SUBLANES)] + [SSM_Q << k for k in range(3, n_scan)]
    keep += [1.0] * (SUBLANES + n_scan - 3)
    tr, ti = lam_pow(exps)
    keep = jnp.asarray(np.asarray(keep, np.float32))[:, None, None]
    rows = jnp.stack([(tr * keep).reshape(-1, N_SLABS, SLAB_GROUPS // 2, 2 * SSM_P),
                      (ti * keep).reshape(-1, N_SLABS, SLAB_GROUPS // 2, 2 * SSM_P)], axis=3)
    decay = rows.transpose(1, 0, 2, 3, 4).reshape(N_SLABS, len(exps), SLAB_STATE)
    return cs, cl, skip, decay


def _expand_kernel(cs_ref, cl_ref, skip_ref, w1_ref, w2_ref):
    r16 = lax.broadcasted_iota(jnp.int32, (LANES, LANES), 0)
    c16 = lax.broadcasted_iota(jnp.int32, (LANES, LANES), 1)
    diag_h = (r16 // SSM_H) == (c16 // SSM_H)
    rs = lax.broadcasted_iota(jnp.int32, (LANES, SLAB_STATE), 0)
    cst = lax.broadcasted_iota(jnp.int32, (LANES, SLAB_STATE), 1)
    rep_s = ((rs // SSM_P == (cst // LANES) % 2) & (rs % SSM_P == cst % SSM_P)).astype(_BF16)
    diag_s = (rs // SSM_H) == (2 * (cst // (2 * LANES)) + (cst % LANES) // SSM_P)

    zero = jnp.zeros((LANES, LANES), _BF16)
    b_bar = cs_ref[0, SSM_Q - 1]
    blocks = []
    for d in range(SSM_Q):
        k_d = lax.dot_general(b_bar, cl_ref[0, d], (((1,), (1,)), ((), ())),
                              precision=lax.Precision.HIGHEST, preferred_element_type=_F32)
        if d == 0:
            k_d = k_d + jnp.where(r16 == c16, skip_ref[0], 0.0)
        blocks.append(jnp.where(diag_h, k_d, 0.0).astype(_BF16))
    for i in range(SSM_Q):
        for j in range(SSM_Q):
            w1_ref[0, i * LANES:(i + 1) * LANES, j * LANES:(j + 1) * LANES] = blocks[j - i] if j >= i else zero
        w_in = jnp.where(diag_s, jnp.dot(cs_ref[0, i].astype(_BF16), rep_s, preferred_element_type=_F32), 0.0)
        w1_ref[0, i * LANES:(i + 1) * LANES, SLAB_IN:] = w_in.astype(_BF16)
        w_out_t = jnp.where(diag_s, jnp.dot(cl_ref[0, i + 1].astype(_BF16), rep_s,
                                            preferred_element_type=_F32), 0.0)
        w2_ref[0, i * LANES:(i + 1) * LANES, :] = w_out_t.astype(_BF16)


def _expand(cs, cl, skip):
    blk = lambda a: pl.BlockSpec((1,) + a.shape[1:], lambda s: (s,) + (0,) * (a.ndim - 1))
    return pl.pallas_call(
        _expand_kernel,
        grid=(N_SLABS,),
        in_specs=[blk(cs), blk(cl), blk(skip)],
        out_specs=[pl.BlockSpec((1, SLAB_IN, SLAB_IN + SLAB_STATE), lambda s: (s, 0, 0)),
                   pl.BlockSpec((1, SLAB_IN, SLAB_STATE), lambda s: (s, 0, 0))],
        out_shape=[jax.ShapeDtypeStruct((N_SLABS, SLAB_IN, SLAB_IN + SLAB_STATE), _BF16),
                   jax.ShapeDtypeStruct((N_SLABS, SLAB_IN, SLAB_STATE), _BF16)],
        compiler_params=pltpu.CompilerParams(dimension_semantics=("arbitrary",)),
        name="s5_expand",
    )(cs, cl, skip)


def _ssm_kernel(x_ref, w1_ref, w2_ref, dec_ref, y_ref, h_ref, t_scr):
    nc = x_ref.shape[0]
    nb = nc // SUBLANES
    piece = 2 * LANES
    powers = 3 * SUBLANES
    lvl2 = 4 * SUBLANES

    def cmul_add(acc_r, acc_i, pr, pi, tr, ti):
        return acc_r + pr * tr - pi * ti, acc_i + pr * ti + pi * tr

    def roll_in_block(a, shift):
        return pltpu.roll(a.reshape(nb, SUBLANES, LANES), shift, axis=1).reshape(nc, LANES)

    @pl.when(pl.program_id(2) == 0)
    def _():
        h_ref[...] = jnp.zeros_like(h_ref)

    x = x_ref[...]
    s_all = jnp.dot(x, w1_ref[0, :, SLAB_IN:], preferred_element_type=_F32)
    sub = lax.broadcasted_iota(jnp.int32, (nc, LANES), 0) % SUBLANES
    brow = lax.broadcasted_iota(jnp.int32, (nb, LANES), 0)
    y = None
    for p in range(SLAB_STATE // piece):
        re, im = slice(p * piece, p * piece + LANES), slice(p * piece + LANES, (p + 1) * piece)
        table = lambda row0, n: (dec_ref[0, row0:row0 + n, re], dec_ref[0, row0:row0 + n, im])
        tall = lambda a: jnp.concatenate([a] * nb, axis=0)

        sr, si = s_all[:, re], s_all[:, im]
        hin_r, hin_i = h_ref[0:1, re], h_ref[0:1, im]

        for k in range(3):
            mr, mi = table(k * SUBLANES, SUBLANES)
            sr, si = cmul_add(sr, si, tall(mr), tall(mi),
                              roll_in_block(sr, 1 << k), roll_in_block(si, 1 << k))

        t_scr[2 * p] = sr
        t_scr[2 * p + 1] = si
        br = t_scr[2 * p, pl.ds(SUBLANES - 1, nb, stride=SUBLANES), :]
        bi = t_scr[2 * p + 1, pl.ds(SUBLANES - 1, nb, stride=SUBLANES), :]
        pr, pi = table(lvl2, 1)
        br = br + jnp.where(brow == 0, pr * hin_r - pi * hin_i, 0.0)
        bi = bi + jnp.where(brow == 0, pr * hin_i + pi * hin_r, 0.0)
        for k in range(dec_ref.shape[1] - lvl2):
            shift = 1 << k
            tr = jnp.where(brow >= shift, pltpu.roll(br, shift, axis=0), 0.0)
            ti = jnp.where(brow >= shift, pltpu.roll(bi, shift, axis=0), 0.0)
            br, bi = cmul_add(br, bi, *table(lvl2 + k, 1), tr, ti)
        h_ref[0:1, re] = br[nb - 1:nb]
        h_ref[0:1, im] = bi[nb - 1:nb]
        bpr = jnp.where(brow == 0, hin_r, pltpu.roll(br, 1, axis=0))
        bpi = jnp.where(brow == 0, hin_i, pltpu.roll(bi, 1, axis=0))

        rep = lambda a: jnp.concatenate(
            [jnp.broadcast_to(a[k:k + 1], (SUBLANES, LANES)) for k in range(nb)], axis=0)
        fr, fi = table(powers, SUBLANES)
        tr = jnp.where(sub >= 1, roll_in_block(sr, 1), 0.0)
        ti = jnp.where(sub >= 1, roll_in_block(si, 1), 0.0)
        prev_r, prev_i = cmul_add(tr, ti, tall(fr), tall(fi), rep(bpr), rep(bpi))
        hprev = jnp.concatenate([prev_r, prev_i], axis=1).astype(_BF16)
        part = lax.dot_general(hprev, w2_ref[0, :, p * piece:(p + 1) * piece], (((1,), (1,)), ((), ())),
                               preferred_element_type=_F32)
        if y is None:
            y = jnp.dot(x, w1_ref[0, :, :SLAB_IN], preferred_element_type=_F32) + part
        else:
            y = y + part

    for i in range(SSM_Q):
        y_ref[pl.ds(i, nc, stride=SSM_Q), :] = y[:, i * LANES:(i + 1) * LANES]


def _ssm(ucat, w1, w2, decay, batch, seq, nc):
    tiles = seq // (nc * SSM_Q)
    return pl.pallas_call(
        _ssm_kernel,
        grid=(N_SLABS, batch, tiles),
        in_specs=[pl.BlockSpec((nc, SLAB_IN), lambda s, b, t: (b * tiles + t, s)),
                  pl.BlockSpec((1,) + w1.shape[1:], lambda s, b, t: (s, 0, 0)),
                  pl.BlockSpec((1,) + w2.shape[1:], lambda s, b, t: (s, 0, 0)),
                  pl.BlockSpec((1,) + decay.shape[1:], lambda s, b, t: (s, 0, 0))],
        out_specs=pl.BlockSpec((nc * SSM_Q, LANES), lambda s, b, t: (b * tiles + t, s)),
        out_shape=jax.ShapeDtypeStruct((batch * seq, SSM_W), _F32),
        scratch_shapes=[pltpu.VMEM((SUBLANES, SLAB_STATE), _F32),
                        pltpu.VMEM((SLAB_STATE // LANES, nc, LANES), _F32)],
        compiler_params=pltpu.CompilerParams(
            dimension_semantics=("arbitrary", "arbitrary", "arbitrary"),
            vmem_limit_bytes=VMEM_LIMIT),
        name="s5_mixer",
    )(ucat, w1, w2, decay)


def _gelu_tanh(y):
    c = math.sqrt(2.0 / math.pi)
    return (0.5 * y) * (1.0 + jnp.tanh(y * (c + (c * 0.044715) * (y * y))))


def _mix_out_kernel(sinks_ref, x_ref, q_ref, kp_ref, kc_ref, vp_ref, vc_ref, za_ref, y_ref, zs_ref,
                    nw_ref, wg_ref, bg_ref, woa_ref, wos_ref, out_ref, acc_ref, *, tiles_per_seq):
    scores, vts = _attn_scores(q_ref, kp_ref, kc_ref, vp_ref, vc_ref)

    y = _gelu_tanh(y_ref[...].astype(_F32))
    gate = jnp.dot(y.astype(_BF16), wg_ref[...], preferred_element_type=_F32) + bg_ref[...]
    o = y * jax.nn.sigmoid(gate) * zs_ref[...].astype(_F32)
    rs = lax.rsqrt(jnp.mean(o * o, axis=-1, keepdims=True) + NORM_EPS)
    out_ref[...] = x_ref[...] + rs * jnp.dot(o.astype(_BF16), wos_ref[...], preferred_element_type=_F32)

    seq_start = pl.program_id(0) % tiles_per_seq == 0
    _attn_softmax_pv(sinks_ref, scores, vts, acc_ref, seq_start, range(acc_ref.shape[0] // BLOCK))
    g = acc_ref[...] * za_ref[...].astype(_F32)
    rs = lax.rsqrt(jnp.mean(g * g, axis=-1, keepdims=True) + NORM_EPS)
    oa = (g * rs * nw_ref[...]).astype(_BF16)
    out_ref[...] += jnp.dot(oa, woa_ref[...], preferred_element_type=_F32)


def _mix_out(sinks, x2, q, k, v, za, y, zs, nw, wg, bg, woa, wos, seq, tm):
    t = x2.shape[0]
    blocks = tm // BLOCK
    row = lambda w: pl.BlockSpec((tm, w), lambda i: (i, 0))
    prev = lambda w: pl.BlockSpec((BLOCK, w), lambda i: (jnp.maximum(i * blocks - 1, 0), 0))
    return pl.pallas_call(
        functools.partial(_mix_out_kernel, tiles_per_seq=seq // tm),
        grid=(t // tm,),
        in_specs=[pl.BlockSpec(memory_space=pltpu.SMEM), row(D_MODEL), row(ATTN_W), prev(KV_W), row(KV_W),
                  prev(KV_W), row(KV_W), row(ATTN_W), row(SSM_W), row(SSM_W), _resident(nw.shape),
                  _resident(wg.shape), _resident(bg.shape), _resident(woa.shape), _resident(wos.shape)],
        out_specs=row(D_MODEL),
        out_shape=jax.ShapeDtypeStruct((t, D_MODEL), x2.dtype),
        scratch_shapes=[pltpu.VMEM((tm, ATTN_W), _F32)],
        compiler_params=pltpu.CompilerParams(dimension_semantics=("arbitrary",),
                                             vmem_limit_bytes=VMEM_LIMIT),
        name="mix_out",
    )(sinks, x2, q, k, k, v, v, za, y, zs, nw, wg, bg, woa, wos)


def _pick_tile(n, target, multiple):
    best = multiple
    for cand in range(multiple, min(n, target) + 1, multiple):
        if n % cand == 0:
            best = cand
    return best


def kernel(x, positions, norm_w, w_in, q_norm_w, k_norm_w, sinks, a_re, a_im, log_step,
           b_re, b_im, c_re, c_im, d_skip, w_glu, b_glu, attn_out_norm_w, ssm_out_norm_w, w_out):
    batch, seq, _ = x.shape
    assert seq % BLOCK == 0 and x.shape[2] == D_MODEL and w_in.shape == (D_MODEL, IN_W)
    t = batch * seq
    f32 = _F32

    w_in_k = (norm_w.astype(f32)[:, None] * w_in.astype(f32)).astype(_BF16)
    wo_attn = _pair_heads(w_out[:ATTN_W]).astype(_BF16)
    wo_ssm = (ssm_out_norm_w.astype(f32)[:, None] * w_out[ATTN_W:].astype(f32)).astype(_BF16)
    log2e = math.log2(math.e)
    qw = (jnp.tile(q_norm_w.astype(f32), N_HEADS) * (log2e / math.sqrt(HEAD_DIM)))[None, :]
    kw = jnp.tile(k_norm_w.astype(f32), N_KV_HEADS)[None, :]
    inv_freq = ROPE_THETA ** (-jnp.arange(0, HEAD_DIM, 2, dtype=f32) / HEAD_DIM)
    invf = jnp.tile(inv_freq, LANES // (HEAD_DIM // 2))[None, :]
    sgn = jnp.tile(jnp.concatenate([-jnp.ones(HEAD_DIM // 2, f32), jnp.ones(HEAD_DIM // 2, f32)]),
                   LANES // HEAD_DIM)[None, :]

    x2 = x.reshape(t, D_MODEL)
    pos2 = positions.reshape(t // LANES, LANES).astype(jnp.int32)

    tm = _pick_tile(seq, 512, BLOCK)
    q, k, v, za, ucat, zs = _in_proj(x2, pos2, w_in_k, qw, kw, invf, sgn, tm)


    nc = _pick_tile(seq // SSM_Q, 512, 16)
    assert nc & (nc - 1) == 0, "chunk rows per tile must be a power of two for the scan"
    cs, cl, skip, decay = _ssm_weights(a_re, a_im, log_step, b_re, b_im, c_re, c_im, d_skip,
                                     n_scan=int(math.log2(nc)))
    w1, w2 = _expand(cs, cl, skip)
    y = _ssm(ucat, w1, w2, decay, batch, seq, nc)

    attn_nw = _pair_heads(attn_out_norm_w.astype(f32))[None, :]
    out = _mix_out(sinks.astype(f32) * log2e, x2, q, k, v, za, y, zs, attn_nw, w_glu.astype(_BF16),
                   b_glu.astype(f32)[None, :], wo_attn, wo_ssm, seq, tm)
    return out.reshape(x.shape)
```

```python
import functools
import math

import jax
import jax.numpy as jnp
import numpy as np
from jax import lax
from jax.experimental import pallas as pl
from jax.experimental.pallas import tpu as pltpu

D_MODEL = 2048
ATTN_W = 1024
KV_W = 256
HEAD_DIM = 64
N_HEADS = 16
N_KV_HEADS = 4
KV_REP = N_HEADS // N_KV_HEADS
SSM_W = 1024
SSM_H = 16
SSM_G = 64
SSM_P = 64
BLOCK = 128
ROPE_THETA = 10000.0
NORM_EPS = 1e-6
IN_W = 2 * ATTN_W + 2 * KV_W + 2 * SSM_W

LANES = 128
SUBLANES = 8
MXU_DIM = 256
SSM_Q = 8
SLAB_GROUPS = LANES // SSM_H
N_SLABS = SSM_W // LANES
SLAB_IN = SSM_Q * LANES
SLAB_STATE = 2 * SLAB_GROUPS * SSM_P
SCAN_STEPS = 3
TABLE_POWERS = SCAN_STEPS * SUBLANES
TABLE_BLOCKS = TABLE_POWERS + SUBLANES
VMEM_LIMIT = 56 * 1024 * 1024

_BF16 = jnp.bfloat16
_F32 = jnp.float32


def _resident(shape):
    nd = len(shape)
    return pl.BlockSpec(shape, lambda *_: (0,) * nd, pipeline_mode=pl.Buffered(1))


def _swap_halves(a):
    return pltpu.roll(a, HEAD_DIM, axis=1)


def _pair_slabs(n0, n1, n2, n3):
    low = lax.broadcasted_iota(jnp.int32, n0.shape, 1) < HEAD_DIM
    return (jnp.where(low, n0, _swap_halves(n2)), jnp.where(low, _swap_halves(n0), n2),
            jnp.where(low, n1, _swap_halves(n3)), jnp.where(low, _swap_halves(n1), n3))


def _rope(t, cos, sin_signed):
    lane = lax.broadcasted_iota(jnp.int32, t.shape, 1)
    first_half = (lane % HEAD_DIM) < (HEAD_DIM // 2)
    swapped = jnp.where(first_half,
                        pltpu.roll(t, LANES - HEAD_DIM // 2, axis=1),
                        pltpu.roll(t, HEAD_DIM // 2, axis=1))
    return t * cos + swapped * sin_signed


def _in_proj_kernel(x_ref, pos_ref, w_ref, qw_ref, kw_ref, invf_ref, sgn_ref,
                    q_ref, k_ref, v_ref, za_ref, ucat_ref, zs_ref, hn_ref, u_ref):
    tm = x_ref.shape[0]
    x = x_ref[...]
    hn_ref[...] = x.astype(_BF16)
    rs = jnp.broadcast_to(lax.rsqrt(jnp.mean(x * x, axis=-1, keepdims=True) + NORM_EPS), (tm, LANES))
    wide = 2 * MXU_DIM
    rs_wide = jnp.concatenate([rs] * (wide // LANES), axis=1)

    nsub = tm // LANES
    ang = jnp.concatenate(
        [jnp.broadcast_to(pos_ref[pl.ds(pl.program_id(0) * nsub + j, 1), :].astype(_F32),
                          (LANES, LANES)).T for j in range(nsub)], axis=0) * invf_ref[...]
    cos = jnp.cos(ang)
    sin_signed = jnp.sin(ang) * sgn_ref[...]

    def proj(c0):
        return jnp.dot(hn_ref[...], w_ref[:, c0:c0 + wide], preferred_element_type=_F32) * rs_wide

    def normed_rope(acc, nw):
        low = lax.broadcasted_iota(jnp.int32, (tm, LANES), 1) < HEAD_DIM
        out = []
        for s in range(MXU_DIM // LANES):
            a = acc[:, s * LANES:(s + 1) * LANES]
            sq = a * a
            ssq = jnp.where(low, jnp.sum(jnp.where(low, sq, 0.0), axis=-1, keepdims=True),
                            jnp.sum(jnp.where(low, 0.0, sq), axis=-1, keepdims=True))
            t = a * lax.rsqrt(ssq * (1.0 / HEAD_DIM) + NORM_EPS) * nw[:, s * LANES:(s + 1) * LANES]
            out.append(_rope(t, cos, sin_signed))
        return out

    base = 2 * ATTN_W + 2 * KV_W
    nc = tm // SSM_Q
    for c in range(SSM_W // wide):
        acc = proj(base + c * wide)
        for s in range(wide // LANES):
            u_ref[c * (wide // LANES) + s] = acc[:, s * LANES:(s + 1) * LANES]
    for s in range(N_SLABS):
        for i in range(SSM_Q):
            c0 = s * SLAB_IN + i * LANES
            ucat_ref[:, c0:c0 + LANES] = u_ref[s, pl.ds(i, nc, stride=SSM_Q), :].astype(_BF16)

    for c in range(ATTN_W // wide):
        acc = proj(c * wide)
        slabs = []
        for h in range(wide // MXU_DIM):
            n0 = c * wide + h * MXU_DIM
            slabs += normed_rope(acc[:, h * MXU_DIM:(h + 1) * MXU_DIM], qw_ref[:, n0:n0 + MXU_DIM])
        for r, slab in enumerate(_pair_slabs(*slabs)):
            q_ref[:, c * wide + r * LANES:c * wide + (r + 1) * LANES] = slab.astype(_BF16)
    acc = proj(ATTN_W)
    for s, slab in enumerate(normed_rope(acc[:, :KV_W], kw_ref[...])):
        k_ref[:, s * LANES:(s + 1) * LANES] = slab.astype(_BF16)
    v_ref[...] = acc[:, KV_W:].astype(_BF16)
    for out_ref, base in ((za_ref, ATTN_W + 2 * KV_W), (zs_ref, 2 * ATTN_W + 2 * KV_W + SSM_W)):
        for c in range(ATTN_W // wide):
            z = proj(base + c * wide)
            out_ref[:, c * wide:(c + 1) * wide] = (z * jax.nn.sigmoid(z)).astype(_BF16)


def _in_proj(x2, pos2, w_in, qw, kw, invf, sgn, tm):
    t = x2.shape[0]
    row = lambda w: pl.BlockSpec((tm, w), lambda i: (i, 0))
    return pl.pallas_call(
        _in_proj_kernel,
        grid=(t // tm,),
        in_specs=[row(D_MODEL), _resident(pos2.shape), _resident(w_in.shape), _resident(qw.shape),
                  _resident(kw.shape), _resident(invf.shape), _resident(sgn.shape)],
        out_specs=[row(ATTN_W), row(KV_W), row(KV_W), row(ATTN_W),
                   pl.BlockSpec((tm // SSM_Q, N_SLABS * SLAB_IN), lambda i: (i, 0)), row(SSM_W)],
        out_shape=[jax.ShapeDtypeStruct((t, ATTN_W), _BF16),
                   jax.ShapeDtypeStruct((t, KV_W), _BF16),
                   jax.ShapeDtypeStruct((t, KV_W), _BF16),
                   jax.ShapeDtypeStruct((t, ATTN_W), _BF16),
                   jax.ShapeDtypeStruct((t // SSM_Q, N_SLABS * SLAB_IN), _BF16),
                   jax.ShapeDtypeStruct((t, SSM_W), _BF16)],
        scratch_shapes=[pltpu.VMEM((tm, D_MODEL), _BF16), pltpu.VMEM((N_SLABS, tm, LANES), _F32)],
        compiler_params=pltpu.CompilerParams(dimension_semantics=("arbitrary",),
                                             vmem_limit_bytes=VMEM_LIMIT),
        name="in_proj",
    )(x2, pos2, w_in, qw, kw, invf, sgn)


def _attn_scores(q_ref, kp_ref, kc_ref, vp_ref, vc_ref):
    blocks = q_ref.shape[0] // BLOCK
    two = 2 * BLOCK
    lane_kv = lax.broadcasted_iota(jnp.int32, (two, LANES), 1)
    halves = [lane_kv < HEAD_DIM, lane_kv >= HEAD_DIM]
    zero = jnp.zeros((two, LANES), _BF16)
    key_head = lax.broadcasted_iota(jnp.int32, (2 * SUBLANES, 2 * two), 1) // two
    count_rows = (lax.broadcasted_iota(jnp.int32, (2 * SUBLANES, 2 * two), 0) == key_head).astype(_BF16)

    def keys(prev_ref, cur_ref, b, sl):
        if b == 0:
            return jnp.concatenate([prev_ref[:, sl], cur_ref[0:BLOCK, sl]], axis=0)
        return cur_ref[(b - 1) * BLOCK:(b + 1) * BLOCK, sl]

    scores, vts = {}, {}
    for b in range(blocks):
        rows = slice(b * BLOCK, (b + 1) * BLOCK)
        for j in range(N_KV_HEADS // 2):
            sl = slice(j * LANES, (j + 1) * LANES)
            k2 = keys(kp_ref, kc_ref, b, sl)
            v2 = keys(vp_ref, vc_ref, b, sl)
            vts[b, j] = jnp.concatenate(
                [jnp.concatenate([jnp.where(keep, v2, zero) for keep in halves], axis=0).T,
                 count_rows], axis=0)
            for t in range(KV_REP // 2):
                r0 = KV_REP * j + 2 * t
                qq = q_ref[rows, r0 * LANES:(r0 + 2) * LANES]
                qq = jnp.concatenate([qq[:, :LANES], qq[:, LANES:]], axis=0)
                for half, keep in enumerate(halves):
                    scores[b, j, t, half] = lax.dot_general(
                        jnp.where(keep, k2, zero), qq, (((1,), (1,)), ((), ())),
                        preferred_element_type=_F32)
    return scores, vts


def _attn_softmax_pv(sinks_ref, scores, vts, acc_ref, seq_start, blocks):
    two = 2 * BLOCK
    ik = lax.broadcasted_iota(jnp.int32, (two, two), 0)
    iq = lax.broadcasted_iota(jnp.int32, (two, two), 1) % BLOCK
    rel = ik - iq
    band = (rel >= 1) & (rel <= BLOCK)
    band_first = band & ((ik >= BLOCK) | jnp.logical_not(seq_start))
    first_slab = lax.broadcasted_iota(jnp.int32, (1, two), 1) < BLOCK
    row_o = lax.broadcasted_iota(jnp.int32, (LANES, two), 0)
    for b in blocks:
        rows = slice(b * BLOCK, (b + 1) * BLOCK)
        allowed = band_first if b == 0 else band
        for j in range(N_KV_HEADS // 2):
            for t in range(KV_REP // 2):
                r0 = KV_REP * j + 2 * t
                probs, sink_terms = [], []
                for half in range(2):
                    h0 = 2 * KV_REP * j + KV_REP * half + 2 * t
                    sink = jnp.where(first_slab, sinks_ref[h0], sinks_ref[h0 + 1])
                    s = jnp.where(allowed, scores[b, j, t, half], -1e30)
                    m = jnp.maximum(jnp.max(s, axis=0, keepdims=True), sink)
                    probs.append(jnp.exp2(s - m).astype(_BF16))
                    sink_terms.append(jnp.exp2(sink - m))
                o_t = jnp.dot(vts[b, j], jnp.concatenate(probs, axis=0), preferred_element_type=_F32)
                rdens = [1.0 / (o_t[LANES + half:LANES + half + 1] + sink_terms[half]) for half in range(2)]
                o_t = o_t[:LANES] * jnp.where(row_o < HEAD_DIM, rdens[0], rdens[1])
                for half in range(2):
                    heads = slice(half * HEAD_DIM, (half + 1) * HEAD_DIM)
                    slab = KV_REP * j + (KV_REP // 2) * half + t
                    acc_ref[rows, slab * LANES:(slab + 1) * LANES] = jnp.concatenate(
                        [o_t[heads, :BLOCK], o_t[heads, BLOCK:]], axis=0).T


def _ssm_weights(a_re, a_im, log_step, b_re, b_im, c_re, c_im, d_skip, n_scan):
    f32 = _F32
    a_re, a_im = a_re.astype(f32), a_im.astype(f32)
    delta = jnp.exp(log_step.astype(f32))[:, None]
    zr, zi = a_re * delta, a_im * delta

    def lam_pow(exponents):
        m = jnp.asarray(np.asarray(exponents, np.float32))[:, None, None]
        mag = jnp.exp(m * zr)
        return mag * jnp.cos(m * zi), mag * jnp.sin(m * zi)

    pw_r, pw_i = lam_pow(range(SSM_Q + 1))
    rev_r, rev_i = lam_pow(range(SSM_Q - 1, -1, -1))
    lr, li = pw_r[1], pw_i[1]
    den = a_re * a_re + a_im * a_im
    fr = ((lr - 1.0) * a_re + li * a_im) / den
    fi = (li * a_re - (lr - 1.0) * a_im) / den
    b_re, b_im = b_re.astype(f32), b_im.astype(f32)
    bbr = fr[..., None] * b_re - fi[..., None] * b_im
    bbi = fr[..., None] * b_im + fi[..., None] * b_re
    c_re, c_im = c_re.astype(f32), c_im.astype(f32)

    cl_r = c_re[None] * pw_r[:, :, None, :] - c_im[None] * pw_i[:, :, None, :]
    cl_i = c_re[None] * pw_i[:, :, None, :] + c_im[None] * pw_r[:, :, None, :]

    def per_slab(a):
        a = a.reshape(a.shape[0], N_SLABS, SLAB_GROUPS * SSM_H, a.shape[-1])
        return a.transpose(1, 0, 2, 3)

    sin_r = rev_r[..., None] * bbr[None] - rev_i[..., None] * bbi[None]
    sin_i = rev_r[..., None] * bbi[None] + rev_i[..., None] * bbr[None]
    cs = per_slab(jnp.concatenate([sin_r, sin_i], axis=2).transpose(0, 1, 3, 2))

    cl = per_slab(jnp.concatenate([cl_r, -cl_i], axis=3))
    skip = d_skip.astype(f32).reshape(N_SLABS, 1, LANES)

    exps, keep = [], []
    for k in range(SCAN_STEPS):
        exps += [SSM_Q << k] * SUBLANES
        keep += [0.0] * (1 << k) + [1.0] * (SUBLANES - (1 << k))
    exps += [SSM_Q * j for j in range(SUBLANES)] + [SSM_Q << k for k in range(SCAN_STEPS, n_scan)]
    keep += [1.0] * (SUBLANES + n_scan - SCAN_STEPS)
    tr, ti = lam_pow(exps)
    keep = jnp.asarray(np.asarray(keep, np.float32))[:, None, None]
    rows = jnp.stack([(tr * keep).reshape(-1, N_SLABS, SLAB_GROUPS // 2, 2 * SSM_P),
                      (ti * keep).reshape(-1, N_SLABS, SLAB_GROUPS // 2, 2 * SSM_P)], axis=3)
    decay = rows.transpose(1, 0, 2, 3, 4).reshape(N_SLABS, len(exps), SLAB_STATE)
    return cs, cl, skip, decay


def _expand_kernel(cs_ref, cl_ref, skip_ref, w1_ref, w2_ref):
    r16 = lax.broadcasted_iota(jnp.int32, (LANES, LANES), 0)
    c16 = lax.broadcasted_iota(jnp.int32, (LANES, LANES), 1)
    diag_h = (r16 // SSM_H) == (c16 // SSM_H)
    rs = lax.broadcasted_iota(jnp.int32, (LANES, SLAB_STATE), 0)
    cst = lax.broadcasted_iota(jnp.int32, (LANES, SLAB_STATE), 1)
    rep_s = ((rs // SSM_P == (cst // LANES) % 2) & (rs % SSM_P == cst % SSM_P)).astype(_BF16)
    diag_s = (rs // SSM_H) == (2 * (cst // (2 * LANES)) + (cst % LANES) // SSM_P)

    zero = jnp.zeros((LANES, LANES), _BF16)
    b_bar = cs_ref[0, SSM_Q - 1]
    blocks = []
    for d in range(SSM_Q):
        k_d = lax.dot_general(b_bar, cl_ref[0, d], (((1,), (1,)), ((), ())),
                              precision=lax.Precision.HIGHEST, preferred_element_type=_F32)
        if d == 0:
            k_d = k_d + jnp.where(r16 == c16, skip_ref[0], 0.0)
        blocks.append(jnp.where(diag_h, k_d, 0.0).astype(_BF16))
    for i in range(SSM_Q):
        for j in range(SSM_Q):
            w1_ref[0, i * LANES:(i + 1) * LANES, j * LANES:(j + 1) * LANES] = blocks[j - i] if j >= i else zero
        w_in = jnp.where(diag_s, jnp.dot(cs_ref[0, i].astype(_BF16), rep_s, preferred_element_type=_F32), 0.0)
        w1_ref[0, i * LANES:(i + 1) * LANES, SLAB_IN:] = w_in.astype(_BF16)
        w_out_t = jnp.where(diag_s, jnp.dot(cl_ref[0, i + 1].astype(_BF16), rep_s,
                                            preferred_element_type=_F32), 0.0)
        w2_ref[0, i * LANES:(i + 1) * LANES, :] = w_out_t.astype(_BF16)


def _expand(cs, cl, skip):
    blk = lambda a: pl.BlockSpec((1,) + a.shape[1:], lambda s: (s,) + (0,) * (a.ndim - 1))
    return pl.pallas_call(
        _expand_kernel,
        grid=(N_SLABS,),
        in_specs=[blk(cs), blk(cl), blk(skip)],
        out_specs=[pl.BlockSpec((1, SLAB_IN, SLAB_IN + SLAB_STATE), lambda s: (s, 0, 0)),
                   pl.BlockSpec((1, SLAB_IN, SLAB_STATE), lambda s: (s, 0, 0))],
        out_shape=[jax.ShapeDtypeStruct((N_SLABS, SLAB_IN, SLAB_IN + SLAB_STATE), _BF16),
                   jax.ShapeDtypeStruct((N_SLABS, SLAB_IN, SLAB_STATE), _BF16)],
        compiler_params=pltpu.CompilerParams(dimension_semantics=("arbitrary",)),
        name="s5_expand",
    )(cs, cl, skip)


def _ssm_kernel(x_ref, w1_ref, w2_ref, dec_ref, y_ref, h_ref, t_scr):
    nc = x_ref.shape[0]
    nb = nc // SUBLANES
    piece = 2 * LANES

    def cmul_add(acc_r, acc_i, pr, pi, tr, ti):
        return acc_r + pr * tr - pi * ti, acc_i + pr * ti + pi * tr

    def roll_in_block(a, shift):
        return pltpu.roll(a.reshape(nb, SUBLANES, LANES), shift, axis=1).reshape(nc, LANES)

    @pl.when(pl.program_id(2) == 0)
    def _():
        h_ref[...] = jnp.zeros_like(h_ref)

    x = x_ref[...]
    s_all = jnp.dot(x, w1_ref[0, :, SLAB_IN:], preferred_element_type=_F32)
    sub = lax.broadcasted_iota(jnp.int32, (nc, LANES), 0) % SUBLANES
    brow = lax.broadcasted_iota(jnp.int32, (nb, LANES), 0)
    y = None
    for p in range(SLAB_STATE // piece):
        re, im = slice(p * piece, p * piece + LANES), slice(p * piece + LANES, (p + 1) * piece)
        table = lambda row0, n: (dec_ref[0, row0:row0 + n, re], dec_ref[0, row0:row0 + n, im])
        tall = lambda a: jnp.concatenate([a] * nb, axis=0)

        sr, si = s_all[:, re], s_all[:, im]
        hin_r, hin_i = h_ref[0:1, re], h_ref[0:1, im]

        for k in range(SCAN_STEPS):
            mr, mi = table(k * SUBLANES, SUBLANES)
            sr, si = cmul_add(sr, si, tall(mr), tall(mi),
                              roll_in_block(sr, 1 << k), roll_in_block(si, 1 << k))

        t_scr[2 * p] = sr
        t_scr[2 * p + 1] = si
        br = t_scr[2 * p, pl.ds(SUBLANES - 1, nb, stride=SUBLANES), :]
        bi = t_scr[2 * p + 1, pl.ds(SUBLANES - 1, nb, stride=SUBLANES), :]
        pr, pi = table(TABLE_BLOCKS, 1)
        br = br + jnp.where(brow == 0, pr * hin_r - pi * hin_i, 0.0)
        bi = bi + jnp.where(brow == 0, pr * hin_i + pi * hin_r, 0.0)
        for k in range(dec_ref.shape[1] - TABLE_BLOCKS):
            shift = 1 << k
            tr = jnp.where(brow >= shift, pltpu.roll(br, shift, axis=0), 0.0)
            ti = jnp.where(brow >= shift, pltpu.roll(bi, shift, axis=0), 0.0)
            br, bi = cmul_add(br, bi, *table(TABLE_BLOCKS + k, 1), tr, ti)
        h_ref[0:1, re] = br[nb - 1:nb]
        h_ref[0:1, im] = bi[nb - 1:nb]
        bpr = jnp.where(brow == 0, hin_r, pltpu.roll(br, 1, axis=0))
        bpi = jnp.where(brow == 0, hin_i, pltpu.roll(bi, 1, axis=0))

        rep = lambda a: jnp.concatenate(
            [jnp.broadcast_to(a[k:k + 1], (SUBLANES, LANES)) for k in range(nb)], axis=0)
        fr, fi = table(TABLE_POWERS, SUBLANES)
        tr = jnp.where(sub >= 1, roll_in_block(sr, 1), 0.0)
        ti = jnp.where(sub >= 1, roll_in_block(si, 1), 0.0)
        prev_r, prev_i = cmul_add(tr, ti, tall(fr), tall(fi), rep(bpr), rep(bpi))
        hprev = jnp.concatenate([prev_r, prev_i], axis=1).astype(_BF16)
        part = lax.dot_general(hprev, w2_ref[0, :, p * piece:(p + 1) * piece], (((1,), (1,)), ((), ())),
                               preferred_element_type=_F32)
        if y is None:
            y = jnp.dot(x, w1_ref[0, :, :SLAB_IN], preferred_element_type=_F32) + part
        else:
            y = y + part

    for i in range(SSM_Q):
        y_ref[pl.ds(i, nc, stride=SSM_Q), :] = y[:, i * LANES:(i + 1) * LANES]


def _ssm(ucat, w1, w2, decay, batch, seq, nc):
    tiles = seq // (nc * SSM_Q)
    return pl.pallas_call(
        _ssm_kernel,
        grid=(N_SLABS, batch, tiles),
        in_specs=[pl.BlockSpec((nc, SLAB_IN), lambda s, b, t: (b * tiles + t, s)),
                  pl.BlockSpec((1,) + w1.shape[1:], lambda s, b, t: (s, 0, 0)),
                  pl.BlockSpec((1,) + w2.shape[1:], lambda s, b, t: (s, 0, 0)),
                  pl.BlockSpec((1,) + decay.shape[1:], lambda s, b, t: (s, 0, 0))],
        out_specs=pl.BlockSpec((nc * SSM_Q, LANES), lambda s, b, t: (b * tiles + t, s)),
        out_shape=jax.ShapeDtypeStruct((batch * seq, SSM_W), _F32),
        scratch_shapes=[pltpu.VMEM((SUBLANES, SLAB_STATE), _F32),
                        pltpu.VMEM((SLAB_STATE // LANES, nc, LANES), _F32)],
        compiler_params=pltpu.CompilerParams(
            dimension_semantics=("arbitrary", "arbitrary", "arbitrary"),
            vmem_limit_bytes=VMEM_LIMIT),
        name="s5_mixer",
    )(ucat, w1, w2, decay)


def _gelu_tanh(y):
    c = math.sqrt(2.0 / math.pi)
    return (0.5 * y) * (1.0 + jnp.tanh(y * (c + (c * 0.044715) * (y * y))))


def _mix_out_kernel(sinks_ref, x_ref, q_ref, kp_ref, kc_ref, vp_ref, vc_ref, za_ref, y_ref, zs_ref,
                    nw_ref, wg_ref, bg_ref, woa_ref, wos_ref, out_ref, acc_ref, *, tiles_per_seq):
    scores, vts = _attn_scores(q_ref, kp_ref, kc_ref, vp_ref, vc_ref)

    y = _gelu_tanh(y_ref[...].astype(_F32))
    gate = jnp.dot(y.astype(_BF16), wg_ref[...], preferred_element_type=_F32) + bg_ref[...]
    o = y * jax.nn.sigmoid(gate) * zs_ref[...].astype(_F32)
    rs = lax.rsqrt(jnp.mean(o * o, axis=-1, keepdims=True) + NORM_EPS)
    out_ref[...] = x_ref[...] + rs * jnp.dot(o.astype(_BF16), wos_ref[...], preferred_element_type=_F32)

    seq_start = pl.program_id(0) % tiles_per_seq == 0
    _attn_softmax_pv(sinks_ref, scores, vts, acc_ref, seq_start, range(acc_ref.shape[0] // BLOCK))
    g = acc_ref[...] * za_ref[...].astype(_F32)
    rs = lax.rsqrt(jnp.mean(g * g, axis=-1, keepdims=True) + NORM_EPS)
    oa = (g * rs * nw_ref[...]).astype(_BF16)
    out_ref[...] += jnp.dot(oa, woa_ref[...], preferred_element_type=_F32)


def _mix_out(sinks, x2, q, k, v, za, y, zs, nw, wg, bg, woa, wos, seq, tm):
    t = x2.shape[0]
    blocks = tm // BLOCK
    row = lambda w: pl.BlockSpec((tm, w), lambda i: (i, 0))
    prev = lambda w: pl.BlockSpec((BLOCK, w), lambda i: (jnp.maximum(i * blocks - 1, 0), 0))
    return pl.pallas_call(
        functools.partial(_mix_out_kernel, tiles_per_seq=seq // tm),
        grid=(t // tm,),
        in_specs=[pl.BlockSpec(memory_space=pltpu.SMEM), row(D_MODEL), row(ATTN_W), prev(KV_W), row(KV_W),
                  prev(KV_W), row(KV_W), row(ATTN_W), row(SSM_W), row(SSM_W), _resident(nw.shape),
                  _resident(wg.shape), _resident(bg.shape), _resident(woa.shape), _resident(wos.shape)],
        out_specs=row(D_MODEL),
        out_shape=jax.ShapeDtypeStruct((t, D_MODEL), x2.dtype),
        scratch_shapes=[pltpu.VMEM((tm, ATTN_W), _F32)],
        compiler_params=pltpu.CompilerParams(dimension_semantics=("arbitrary",),
                                             vmem_limit_bytes=VMEM_LIMIT),
        name="mix_out",
    )(sinks, x2, q, k, k, v, v, za, y, zs, nw, wg, bg, woa, wos)


def _pick_tile(n, target, multiple):
    best = multiple
    for cand in range(multiple, min(n, target) + 1, multiple):
        if n % cand == 0:
            best = cand
    return best


def kernel(x, positions, norm_w, w_in, q_norm_w, k_norm_w, sinks, a_re, a_im, log_step,
           b_re, b_im, c_re, c_im, d_skip, w_glu, b_glu, attn_out_norm_w, ssm_out_norm_w, w_out):
    batch, seq, _ = x.shape
    assert seq % BLOCK == 0 and x.shape[2] == D_MODEL and w_in.shape == (D_MODEL, IN_W)
    t = batch * seq
    f32 = _F32

    w_in_k = (norm_w.astype(f32)[:, None] * w_in.astype(f32)).astype(_BF16)
    wo_attn = w_out[:ATTN_W].astype(_BF16)
    wo_ssm = (ssm_out_norm_w.astype(f32)[:, None] * w_out[ATTN_W:].astype(f32)).astype(_BF16)
    log2e = math.log2(math.e)
    qw = (jnp.tile(q_norm_w.astype(f32), N_HEADS) * (log2e / math.sqrt(HEAD_DIM)))[None, :]
    kw = jnp.tile(k_norm_w.astype(f32), N_KV_HEADS)[None, :]
    inv_freq = ROPE_THETA ** (-jnp.arange(0, HEAD_DIM, 2, dtype=f32) / HEAD_DIM)
    invf = jnp.tile(inv_freq, LANES // (HEAD_DIM // 2))[None, :]
    sgn = jnp.tile(jnp.concatenate([-jnp.ones(HEAD_DIM // 2, f32), jnp.ones(HEAD_DIM // 2, f32)]),
                   LANES // HEAD_DIM)[None, :]

    x2 = x.reshape(t, D_MODEL)
    pos2 = positions.reshape(t // LANES, LANES).astype(jnp.int32)

    tm = _pick_tile(seq, 512, BLOCK)
    q, k, v, za, ucat, zs = _in_proj(x2, pos2, w_in_k, qw, kw, invf, sgn, tm)


    nc = _pick_tile(seq // SSM_Q, 512, 16)
    assert nc & (nc - 1) == 0, "chunk rows per tile must be a power of two for the scan"
    cs, cl, skip, decay = _ssm_weights(a_re, a_im, log_step, b_re, b_im, c_re, c_im, d_skip,
                                     n_scan=int(math.log2(nc)))
    w1, w2 = _expand(cs, cl, skip)
    y = _ssm(ucat, w1, w2, decay, batch, seq, nc)

    attn_nw = attn_out_norm_w.astype(f32)[None, :]
    out = _mix_out(sinks.astype(f32) * log2e, x2, q, k, v, za, y, zs, attn_nw, w_glu.astype(_BF16),
                   b_glu.astype(f32)[None, :], wo_attn, wo_ssm, seq, tm)
    return out.reshape(x.shape)
```

```python
import functools
import math

import jax
import jax.numpy as jnp
import numpy as np
from jax import lax
from jax.experimental import pallas as pl
from jax.experimental.pallas import tpu as pltpu

D_MODEL = 2048
ATTN_W = 1024
KV_W = 256
HEAD_DIM = 64
N_HEADS = 16
N_KV_HEADS = 4
KV_REP = N_HEADS // N_KV_HEADS
SSM_W = 1024
SSM_H = 16
SSM_G = 64
SSM_P = 64
BLOCK = 128
ROPE_THETA = 10000.0
NORM_EPS = 1e-6
IN_W = 2 * ATTN_W + 2 * KV_W + 2 * SSM_W

LANES = 128
SUBLANES = 8
MXU_DIM = 256
SSM_Q = 8
SLAB_GROUPS = LANES // SSM_H
N_SLABS = SSM_W // LANES
SLAB_IN = SSM_Q * LANES
SLAB_STATE = 2 * SLAB_GROUPS * SSM_P
SCAN_STEPS = 3
TABLE_POWERS = SCAN_STEPS * SUBLANES
TABLE_BLOCKS = TABLE_POWERS + SUBLANES
VMEM_LIMIT = 56 * 1024 * 1024

_BF16 = jnp.bfloat16
_F32 = jnp.float32


def _resident(shape):
    nd = len(shape)
    return pl.BlockSpec(shape, lambda *_: (0,) * nd, pipeline_mode=pl.Buffered(1))


def _swap_halves(a):
    return pltpu.roll(a, HEAD_DIM, axis=1)


def _pair_slabs(n0, n1, n2, n3):
    low = lax.broadcasted_iota(jnp.int32, n0.shape, 1) < HEAD_DIM
    return (jnp.where(low, n0, _swap_halves(n2)), jnp.where(low, _swap_halves(n0), n2),
            jnp.where(low, n1, _swap_halves(n3)), jnp.where(low, _swap_halves(n1), n3))


def _rope(t, cos, sin_signed):
    lane = lax.broadcasted_iota(jnp.int32, t.shape, 1)
    first_half = (lane % HEAD_DIM) < (HEAD_DIM // 2)
    swapped = jnp.where(first_half,
                        pltpu.roll(t, LANES - HEAD_DIM // 2, axis=1),
                        pltpu.roll(t, HEAD_DIM // 2, axis=1))
    return t * cos + swapped * sin_signed


def _in_proj_kernel(x_ref, pos_ref, w_ref, qw_ref, kw_ref, invf_ref, sgn_ref,
                    q_ref, k_ref, v_ref, za_ref, ucat_ref, zs_ref, hn_ref, u_ref):
    tm = x_ref.shape[0]
    x = x_ref[...]
    hn_ref[...] = x.astype(_BF16)
    rs = jnp.broadcast_to(lax.rsqrt(jnp.mean(x * x, axis=-1, keepdims=True) + NORM_EPS), (tm, LANES))
    wide = 2 * MXU_DIM
    rs_wide = jnp.concatenate([rs] * (wide // LANES), axis=1)

    nsub = tm // LANES
    ang = jnp.concatenate(
        [jnp.broadcast_to(pos_ref[pl.ds(pl.program_id(0) * nsub + j, 1), :].astype(_F32),
                          (LANES, LANES)).T for j in range(nsub)], axis=0) * invf_ref[...]
    cos = jnp.cos(ang)
    sin_signed = jnp.sin(ang) * sgn_ref[...]

    def proj(c0):
        return jnp.dot(hn_ref[...], w_ref[:, c0:c0 + wide], preferred_element_type=_F32) * rs_wide

    def normed_rope(acc, nw):
        low = lax.broadcasted_iota(jnp.int32, (tm, LANES), 1) < HEAD_DIM
        out = []
        for s in range(MXU_DIM // LANES):
            a = acc[:, s * LANES:(s + 1) * LANES]
            sq = a * a
            ssq = jnp.where(low, jnp.sum(jnp.where(low, sq, 0.0), axis=-1, keepdims=True),
                            jnp.sum(jnp.where(low, 0.0, sq), axis=-1, keepdims=True))
            t = a * lax.rsqrt(ssq * (1.0 / HEAD_DIM) + NORM_EPS) * nw[:, s * LANES:(s + 1) * LANES]
            out.append(_rope(t, cos, sin_signed))
        return out

    base = 2 * ATTN_W + 2 * KV_W
    nc = tm // SSM_Q
    for c in range(SSM_W // wide):
        acc = proj(base + c * wide)
        for s in range(wide // LANES):
            u_ref[c * (wide // LANES) + s] = acc[:, s * LANES:(s + 1) * LANES]
    for s in range(N_SLABS):
        for i in range(SSM_Q):
            c0 = s * SLAB_IN + i * LANES
            ucat_ref[:, c0:c0 + LANES] = u_ref[s, pl.ds(i, nc, stride=SSM_Q), :].astype(_BF16)

    for c in range(ATTN_W // wide):
        acc = proj(c * wide)
        slabs = []
        for h in range(wide // MXU_DIM):
            n0 = c * wide + h * MXU_DIM
            slabs += normed_rope(acc[:, h * MXU_DIM:(h + 1) * MXU_DIM], qw_ref[:, n0:n0 + MXU_DIM])
        for r, slab in enumerate(_pair_slabs(*slabs)):
            q_ref[:, c * wide + r * LANES:c * wide + (r + 1) * LANES] = slab.astype(_BF16)
    acc = proj(ATTN_W)
    for s, slab in enumerate(normed_rope(acc[:, :KV_W], kw_ref[...])):
        k_ref[:, s * LANES:(s + 1) * LANES] = slab.astype(_BF16)
    v_ref[...] = acc[:, KV_W:].astype(_BF16)
    for out_ref, base in ((za_ref, ATTN_W + 2 * KV_W), (zs_ref, 2 * ATTN_W + 2 * KV_W + SSM_W)):
        for c in range(ATTN_W // wide):
            z = proj(base + c * wide)
            out_ref[:, c * wide:(c + 1) * wide] = (z * jax.nn.sigmoid(z)).astype(_BF16)


def _in_proj(x2, pos2, w_in, qw, kw, invf, sgn, tm):
    t = x2.shape[0]
    row = lambda w: pl.BlockSpec((tm, w), lambda i: (i, 0))
    return pl.pallas_call(
        _in_proj_kernel,
        grid=(t // tm,),
        in_specs=[row(D_MODEL), _resident(pos2.shape), _resident(w_in.shape), _resident(qw.shape),
                  _resident(kw.shape), _resident(invf.shape), _resident(sgn.shape)],
        out_specs=[row(ATTN_W), row(KV_W), row(KV_W), row(ATTN_W),
                   pl.BlockSpec((tm // SSM_Q, N_SLABS * SLAB_IN), lambda i: (i, 0)), row(SSM_W)],
        out_shape=[jax.ShapeDtypeStruct((t, ATTN_W), _BF16),
                   jax.ShapeDtypeStruct((t, KV_W), _BF16),
                   jax.ShapeDtypeStruct((t, KV_W), _BF16),
                   jax.ShapeDtypeStruct((t, ATTN_W), _BF16),
                   jax.ShapeDtypeStruct((t // SSM_Q, N_SLABS * SLAB_IN), _BF16),
                   jax.ShapeDtypeStruct((t, SSM_W), _BF16)],
        scratch_shapes=[pltpu.VMEM((tm, D_MODEL), _BF16), pltpu.VMEM((N_SLABS, tm, LANES), _F32)],
        compiler_params=pltpu.CompilerParams(dimension_semantics=("arbitrary",),
                                             vmem_limit_bytes=VMEM_LIMIT),
        name="in_proj",
    )(x2, pos2, w_in, qw, kw, invf, sgn)


def _attn_scores(q_ref, kp_ref, kc_ref, vp_ref, vc_ref):
    blocks = q_ref.shape[0] // BLOCK
    two = 2 * BLOCK
    lane_kv = lax.broadcasted_iota(jnp.int32, (two, LANES), 1)
    halves = [lane_kv < HEAD_DIM, lane_kv >= HEAD_DIM]
    zero = jnp.zeros((two, LANES), _BF16)
    key_head = lax.broadcasted_iota(jnp.int32, (2 * SUBLANES, 2 * two), 1) // two
    count_rows = (lax.broadcasted_iota(jnp.int32, (2 * SUBLANES, 2 * two), 0) == key_head).astype(_BF16)

    def keys(prev_ref, cur_ref, b, sl):
        if b == 0:
            return jnp.concatenate([prev_ref[:, sl], cur_ref[0:BLOCK, sl]], axis=0)
        return cur_ref[(b - 1) * BLOCK:(b + 1) * BLOCK, sl]

    scores, vts = {}, {}
    for b in range(blocks):
        rows = slice(b * BLOCK, (b + 1) * BLOCK)
        for j in range(N_KV_HEADS // 2):
            sl = slice(j * LANES, (j + 1) * LANES)
            k2 = keys(kp_ref, kc_ref, b, sl)
            v2 = keys(vp_ref, vc_ref, b, sl)
            vts[b, j] = jnp.concatenate(
                [jnp.concatenate([jnp.where(keep, v2, zero) for keep in halves], axis=0).T,
                 count_rows], axis=0)
            for t in range(KV_REP // 2):
                r0 = KV_REP * j + 2 * t
                qq = q_ref[rows, r0 * LANES:(r0 + 2) * LANES]
                qq = jnp.concatenate([qq[:, :LANES], qq[:, LANES:]], axis=0)
                for half, keep in enumerate(halves):
                    scores[b, j, t, half] = lax.dot_general(
                        jnp.where(keep, k2, zero), qq, (((1,), (1,)), ((), ())),
                        preferred_element_type=_F32)
    return scores, vts


def _attn_softmax_pv(sinks_ref, scores, vts, acc_ref, seq_start, blocks):
    two = 2 * BLOCK
    ik = lax.broadcasted_iota(jnp.int32, (two, two), 0)
    iq = lax.broadcasted_iota(jnp.int32, (two, two), 1) % BLOCK
    rel = ik - iq
    band = (rel >= 1) & (rel <= BLOCK)
    band_first = band & ((ik >= BLOCK) | jnp.logical_not(seq_start))
    first_slab = lax.broadcasted_iota(jnp.int32, (1, two), 1) < BLOCK
    row_o = lax.broadcasted_iota(jnp.int32, (LANES, two), 0)
    for b in blocks:
        rows = slice(b * BLOCK, (b + 1) * BLOCK)
        allowed = band_first if b == 0 else band
        for j in range(N_KV_HEADS // 2):
            for t in range(KV_REP // 2):
                r0 = KV_REP * j + 2 * t
                probs, sink_terms = [], []
                for half in range(2):
                    h0 = 2 * KV_REP * j + KV_REP * half + 2 * t
                    sink = jnp.where(first_slab, sinks_ref[h0], sinks_ref[h0 + 1])
                    s = jnp.where(allowed, scores[b, j, t, half], -1e30)
                    m = jnp.maximum(jnp.max(s, axis=0, keepdims=True), sink)
                    probs.append(jnp.exp2(s - m).astype(_BF16))
                    sink_terms.append(jnp.exp2(sink - m))
                o_t = jnp.dot(vts[b, j], jnp.concatenate(probs, axis=0), preferred_element_type=_F32)
                rdens = [1.0 / (o_t[LANES + half:LANES + half + 1] + sink_terms[half]) for half in range(2)]
                o_t = o_t[:LANES] * jnp.where(row_o < HEAD_DIM, rdens[0], rdens[1])
                for half in range(2):
                    heads = slice(half * HEAD_DIM, (half + 1) * HEAD_DIM)
                    slab = KV_REP * j + (KV_REP // 2) * half + t
                    acc_ref[rows, slab * LANES:(slab + 1) * LANES] = jnp.concatenate(
                        [o_t[heads, :BLOCK], o_t[heads, BLOCK:]], axis=0).T


def _ssm_weights(a_re, a_im, log_step, b_re, b_im, c_re, c_im, d_skip, n_scan):
    f32 = _F32
    a_re, a_im = a_re.astype(f32), a_im.astype(f32)
    delta = jnp.exp(log_step.astype(f32))[:, None]
    zr, zi = a_re * delta, a_im * delta

    def lam_pow(exponents):
        m = jnp.asarray(np.asarray(exponents, np.float32))[:, None, None]
        mag = jnp.exp(m * zr)
        return mag * jnp.cos(m * zi), mag * jnp.sin(m * zi)

    pw_r, pw_i = lam_pow(range(SSM_Q + 1))
    rev_r, rev_i = lam_pow(range(SSM_Q - 1, -1, -1))
    lr, li = pw_r[1], pw_i[1]
    den = a_re * a_re + a_im * a_im
    fr = ((lr - 1.0) * a_re + li * a_im) / den
    fi = (li * a_re - (lr - 1.0) * a_im) / den
    b_re, b_im = b_re.astype(f32), b_im.astype(f32)
    bbr = fr[..., None] * b_re - fi[..., None] * b_im
    bbi = fr[..., None] * b_im + fi[..., None] * b_re
    c_re, c_im = c_re.astype(f32), c_im.astype(f32)

    cl_r = c_re[None] * pw_r[:, :, None, :] - c_im[None] * pw_i[:, :, None, :]
    cl_i = c_re[None] * pw_i[:, :, None, :] + c_im[None] * pw_r[:, :, None, :]

    def per_slab(a):
        a = a.reshape(a.shape[0], N_SLABS, SLAB_GROUPS * SSM_H, a.shape[-1])
        return a.transpose(1, 0, 2, 3)

    sin_r = rev_r[..., None] * bbr[None] - rev_i[..., None] * bbi[None]
    sin_i = rev_r[..., None] * bbi[None] + rev_i[..., None] * bbr[None]
    cs = per_slab(jnp.concatenate([sin_r, sin_i], axis=2).transpose(0, 1, 3, 2))

    cl = per_slab(jnp.concatenate([cl_r, -cl_i], axis=3))
    skip = d_skip.astype(f32).reshape(N_SLABS, 1, LANES)

    exps, keep = [], []
    for k in range(SCAN_STEPS):
        exps += [SSM_Q << k] * SUBLANES
        keep += [0.0] * (1 << k) + [1.0] * (SUBLANES - (1 << k))
    exps += [SSM_Q * j for j in range(SUBLANES)] + [SSM_Q << k for k in range(SCAN_STEPS, n_scan)]
    keep += [1.0] * (SUBLANES + n_scan - SCAN_STEPS)
    tr, ti = lam_pow(exps)
    keep = jnp.asarray(np.asarray(keep, np.float32))[:, None, None]
    rows = jnp.stack([(tr * keep).reshape(-1, N_SLABS, SLAB_GROUPS // 2, 2 * SSM_P),
                      (ti * keep).reshape(-1, N_SLABS, SLAB_GROUPS // 2, 2 * SSM_P)], axis=3)
    decay = rows.transpose(1, 0, 2, 3, 4).reshape(N_SLABS, len(exps), SLAB_STATE)
    return cs, cl, skip, decay


def _expand_kernel(cs_ref, cl_ref, skip_ref, w1_ref, w2_ref):
    r16 = lax.broadcasted_iota(jnp.int32, (LANES, LANES), 0)
    c16 = lax.broadcasted_iota(jnp.int32, (LANES, LANES), 1)
    diag_h = (r16 // SSM_H) == (c16 // SSM_H)
    rs = lax.broadcasted_iota(jnp.int32, (LANES, SLAB_STATE), 0)
    cst = lax.broadcasted_iota(jnp.int32, (LANES, SLAB_STATE), 1)
    rep_s = ((rs // SSM_P == (cst // LANES) % 2) & (rs % SSM_P == cst % SSM_P)).astype(_BF16)
    diag_s = (rs // SSM_H) == (2 * (cst // (2 * LANES)) + (cst % LANES) // SSM_P)

    zero = jnp.zeros((LANES, LANES), _BF16)
    b_bar = cs_ref[0, SSM_Q - 1]
    blocks = []
    for d in range(SSM_Q):
        k_d = lax.dot_general(b_bar, cl_ref[0, d], (((1,), (1,)), ((), ())),
                              precision=lax.Precision.HIGHEST, preferred_element_type=_F32)
        if d == 0:
            k_d = k_d + jnp.where(r16 == c16, skip_ref[0], 0.0)
        blocks.append(jnp.where(diag_h, k_d, 0.0).astype(_BF16))
    for i in range(SSM_Q):
        for j in range(SSM_Q):
            w1_ref[0, i * LANES:(i + 1) * LANES, j * LANES:(j + 1) * LANES] = blocks[j - i] if j >= i else zero
        w_in = jnp.where(diag_s, jnp.dot(cs_ref[0, i].astype(_BF16), rep_s, preferred_element_type=_F32), 0.0)
        w1_ref[0, i * LANES:(i + 1) * LANES, SLAB_IN:] = w_in.astype(_BF16)
        w_out_t = jnp.where(diag_s, jnp.dot(cl_ref[0, i + 1].astype(_BF16), rep_s,
                                            preferred_element_type=_F32), 0.0)
        w2_ref[0, i * LANES:(i + 1) * LANES, :] = w_out_t.astype(_BF16)


def _expand(cs, cl, skip):
    blk = lambda a: pl.BlockSpec((1,) + a.shape[1:], lambda s: (s,) + (0,) * (a.ndim - 1))
    return pl.pallas_call(
        _expand_kernel,
        grid=(N_SLABS,),
        in_specs=[blk(cs), blk(cl), blk(skip)],
        out_specs=[pl.BlockSpec((1, SLAB_IN, SLAB_IN + SLAB_STATE), lambda s: (s, 0, 0)),
                   pl.BlockSpec((1, SLAB_IN, SLAB_STATE), lambda s: (s, 0, 0))],
        out_shape=[jax.ShapeDtypeStruct((N_SLABS, SLAB_IN, SLAB_IN + SLAB_STATE), _BF16),
                   jax.ShapeDtypeStruct((N_SLABS, SLAB_IN, SLAB_STATE), _BF16)],
        compiler_params=pltpu.CompilerParams(dimension_semantics=("arbitrary",)),
        name="s5_expand",
    )(cs, cl, skip)


def _ssm_kernel(x_ref, w1_ref, w2_ref, dec_ref, y_ref, h_ref, t_scr):
    nc = x_ref.shape[0]
    nb = nc // SUBLANES
    piece = 2 * LANES

    def cmul_add(acc_r, acc_i, pr, pi, tr, ti):
        return acc_r + pr * tr - pi * ti, acc_i + pr * ti + pi * tr

    def roll_in_block(a, shift):
        return pltpu.roll(a.reshape(nb, SUBLANES, LANES), shift, axis=1).reshape(nc, LANES)

    @pl.when(pl.program_id(2) == 0)
    def _():
        h_ref[...] = jnp.zeros_like(h_ref)

    x = x_ref[...]
    s_all = jnp.dot(x, w1_ref[0, :, SLAB_IN:], preferred_element_type=_F32)
    sub = lax.broadcasted_iota(jnp.int32, (nc, LANES), 0) % SUBLANES
    brow = lax.broadcasted_iota(jnp.int32, (nb, LANES), 0)
    y = None
    for p in range(SLAB_STATE // piece):
        re, im = slice(p * piece, p * piece + LANES), slice(p * piece + LANES, (p + 1) * piece)
        table = lambda row0, n: (dec_ref[0, row0:row0 + n, re], dec_ref[0, row0:row0 + n, im])
        tall = lambda a: jnp.concatenate([a] * nb, axis=0)

        sr, si = s_all[:, re], s_all[:, im]
        hin_r, hin_i = h_ref[0:1, re], h_ref[0:1, im]

        for k in range(SCAN_STEPS):
            mr, mi = table(k * SUBLANES, SUBLANES)
            sr, si = cmul_add(sr, si, tall(mr), tall(mi),
                              roll_in_block(sr, 1 << k), roll_in_block(si, 1 << k))

        t_scr[2 * p] = sr
        t_scr[2 * p + 1] = si
        br = t_scr[2 * p, pl.ds(SUBLANES - 1, nb, stride=SUBLANES), :]
        bi = t_scr[2 * p + 1, pl.ds(SUBLANES - 1, nb, stride=SUBLANES), :]
        pr, pi = table(TABLE_BLOCKS, 1)
        br = br + jnp.where(brow == 0, pr * hin_r - pi * hin_i, 0.0)
        bi = bi + jnp.where(brow == 0, pr * hin_i + pi * hin_r, 0.0)
        for k in range(dec_ref.shape[1] - TABLE_BLOCKS):
            shift = 1 << k
            tr = jnp.where(brow >= shift, pltpu.roll(br, shift, axis=0), 0.0)
            ti = jnp.where(brow >= shift, pltpu.roll(bi, shift, axis=0), 0.0)
            br, bi = cmul_add(br, bi, *table(TABLE_BLOCKS + k, 1), tr, ti)
        h_ref[0:1, re] = br[nb - 1:nb]
        h_ref[0:1, im] = bi[nb - 1:nb]
        bpr = jnp.where(brow == 0, hin_r, pltpu.roll(br, 1, axis=0))
        bpi = jnp.where(brow == 0, hin_i, pltpu.roll(bi, 1, axis=0))

        rep = lambda a: jnp.concatenate(
            [jnp.broadcast_to(a[k:k + 1], (SUBLANES, LANES)) for k in range(nb)], axis=0)
        fr, fi = table(TABLE_POWERS, SUBLANES)
        tr = jnp.where(sub >= 1, roll_in_block(sr, 1), 0.0)
        ti = jnp.where(sub >= 1, roll_in_block(si, 1), 0.0)
        prev_r, prev_i = cmul_add(tr, ti, tall(fr), tall(fi), rep(bpr), rep(bpi))
        hprev = jnp.concatenate([prev_r, prev_i], axis=1).astype(_BF16)
        part = lax.dot_general(hprev, w2_ref[0, :, p * piece:(p + 1) * piece], (((1,), (1,)), ((), ())),
                               preferred_element_type=_F32)
        if y is None:
            y = jnp.dot(x, w1_ref[0, :, :SLAB_IN], preferred_element_type=_F32) + part
        else:
            y = y + part

    for i in range(SSM_Q):
        y_ref[pl.ds(i, nc, stride=SSM_Q), :] = y[:, i * LANES:(i + 1) * LANES]


def _ssm(ucat, w1, w2, decay, batch, seq, nc):
    tiles = seq // (nc * SSM_Q)
    return pl.pallas_call(
        _ssm_kernel,
        grid=(N_SLABS, batch, tiles),
        in_specs=[pl.BlockSpec((nc, SLAB_IN), lambda s, b, t: (b * tiles + t, s)),
                  pl.BlockSpec((1,) + w1.shape[1:], lambda s, b, t: (s, 0, 0)),
                  pl.BlockSpec((1,) + w2.shape[1:], lambda s, b, t: (s, 0, 0)),
                  pl.BlockSpec((1,) + decay.shape[1:], lambda s, b, t: (s, 0, 0))],
        out_specs=pl.BlockSpec((nc * SSM_Q, LANES), lambda s, b, t: (b * tiles + t, s)),
        out_shape=jax.ShapeDtypeStruct((batch * seq, SSM_W), _F32),
        scratch_shapes=[pltpu.VMEM((SUBLANES, SLAB_STATE), _F32),
                        pltpu.VMEM((SLAB_STATE // LANES, nc, LANES), _F32)],
        compiler_params=pltpu.CompilerParams(
            dimension_semantics=("arbitrary", "arbitrary", "arbitrary"),
            vmem_limit_bytes=VMEM_LIMIT),
        name="s5_mixer",
    )(ucat, w1, w2, decay)


def _gelu_tanh(y):
    c = math.sqrt(2.0 / math.pi)
    return (0.5 * y) * (1.0 + jnp.tanh(y * (c + (c * 0.044715) * (y * y))))


def _mix_out_kernel(sinks_ref, x_ref, q_ref, kp_ref, kc_ref, vp_ref, vc_ref, za_ref, y_ref, zs_ref,
                    wg_ref, bg_ref, wo_ref, out_ref, acc_ref, *, tiles_per_seq):
    scores, vts = _attn_scores(q_ref, kp_ref, kc_ref, vp_ref, vc_ref)

    y = _gelu_tanh(y_ref[...].astype(_F32))
    gate = jnp.dot(y.astype(_BF16), wg_ref[...], preferred_element_type=_F32) + bg_ref[...]
    o = y * jax.nn.sigmoid(gate) * zs_ref[...].astype(_F32)
    rs = lax.rsqrt(jnp.mean(o * o, axis=-1, keepdims=True) + NORM_EPS)
    out_ref[...] = x_ref[...] + rs * jnp.dot(o.astype(_BF16), wo_ref[ATTN_W:, :], preferred_element_type=_F32)

    seq_start = pl.program_id(0) % tiles_per_seq == 0
    _attn_softmax_pv(sinks_ref, scores, vts, acc_ref, seq_start, range(acc_ref.shape[0] // BLOCK))
    g = acc_ref[...] * za_ref[...].astype(_F32)
    rs = lax.rsqrt(jnp.mean(g * g, axis=-1, keepdims=True) + NORM_EPS)
    out_ref[...] += rs * jnp.dot(g.astype(_BF16), wo_ref[:ATTN_W, :], preferred_element_type=_F32)


def _mix_out(sinks, x2, q, k, v, za, y, zs, wg, bg, wo, seq, tm):
    t = x2.shape[0]
    blocks = tm // BLOCK
    row = lambda w: pl.BlockSpec((tm, w), lambda i: (i, 0))
    prev = lambda w: pl.BlockSpec((BLOCK, w), lambda i: (jnp.maximum(i * blocks - 1, 0), 0))
    return pl.pallas_call(
        functools.partial(_mix_out_kernel, tiles_per_seq=seq // tm),
        grid=(t // tm,),
        in_specs=[pl.BlockSpec(memory_space=pltpu.SMEM), row(D_MODEL), row(ATTN_W), prev(KV_W), row(KV_W),
                  prev(KV_W), row(KV_W), row(ATTN_W), row(SSM_W), row(SSM_W),
                  _resident(wg.shape), _resident(bg.shape), _resident(wo.shape)],
        out_specs=row(D_MODEL),
        out_shape=jax.ShapeDtypeStruct((t, D_MODEL), x2.dtype),
        scratch_shapes=[pltpu.VMEM((tm, ATTN_W), _F32)],
        compiler_params=pltpu.CompilerParams(dimension_semantics=("arbitrary",),
                                             vmem_limit_bytes=VMEM_LIMIT),
        name="mix_out",
    )(sinks, x2, q, k, k, v, v, za, y, zs, wg, bg, wo)


def _pick_tile(n, target, multiple):
    best = multiple
    for cand in range(multiple, min(n, target) + 1, multiple):
        if n % cand == 0:
            best = cand
    return best


def kernel(x, positions, norm_w, w_in, q_norm_w, k_norm_w, sinks, a_re, a_im, log_step,
           b_re, b_im, c_re, c_im, d_skip, w_glu, b_glu, attn_out_norm_w, ssm_out_norm_w, w_out):
    batch, seq, _ = x.shape
    assert seq % BLOCK == 0 and x.shape[2] == D_MODEL and w_in.shape == (D_MODEL, IN_W)
    t = batch * seq
    f32 = _F32

    w_in_k = (norm_w.astype(f32)[:, None] * w_in.astype(f32)).astype(_BF16)
    out_gain = jnp.concatenate([attn_out_norm_w.astype(f32), ssm_out_norm_w.astype(f32)])
    w_out_k = (out_gain[:, None] * w_out.astype(f32)).astype(_BF16)
    log2e = math.log2(math.e)
    qw = (jnp.tile(q_norm_w.astype(f32), N_HEADS) * (log2e / math.sqrt(HEAD_DIM)))[None, :]
    kw = jnp.tile(k_norm_w.astype(f32), N_KV_HEADS)[None, :]
    inv_freq = ROPE_THETA ** (-jnp.arange(0, HEAD_DIM, 2, dtype=f32) / HEAD_DIM)
    invf = jnp.tile(inv_freq, LANES // (HEAD_DIM // 2))[None, :]
    sgn = jnp.tile(jnp.concatenate([-jnp.ones(HEAD_DIM // 2, f32), jnp.ones(HEAD_DIM // 2, f32)]),
                   LANES // HEAD_DIM)[None, :]

    x2 = x.reshape(t, D_MODEL)
    pos2 = positions.reshape(t // LANES, LANES).astype(jnp.int32)

    tm = _pick_tile(seq, 512, BLOCK)
    q, k, v, za, ucat, zs = _in_proj(x2, pos2, w_in_k, qw, kw, invf, sgn, tm)


    nc = _pick_tile(seq // SSM_Q, 512, 16)
    assert nc & (nc - 1) == 0, "chunk rows per tile must be a power of two for the scan"
    cs, cl, skip, decay = _ssm_weights(a_re, a_im, log_step, b_re, b_im, c_re, c_im, d_skip,
                                     n_scan=int(math.log2(nc)))
    w1, w2 = _expand(cs, cl, skip)
    y = _ssm(ucat, w1, w2, decay, batch, seq, nc)

    out = _mix_out(sinks.astype(f32) * log2e, x2, q, k, v, za, y, zs, w_glu.astype(_BF16),
                   b_glu.astype(f32)[None, :], w_out_k, seq, tm)
    return out.reshape(x.shape)
```

```python
import functools
import math

import jax
import jax.numpy as jnp
import numpy as np
from jax import lax
from jax.experimental import pallas as pl
from jax.experimental.pallas import tpu as pltpu

D_MODEL = 2048
ATTN_W = 1024
KV_W = 256
HEAD_DIM = 64
N_HEADS = 16
N_KV_HEADS = 4
KV_REP = N_HEADS // N_KV_HEADS
SSM_W = 1024
SSM_H = 16
SSM_G = 64
SSM_P = 64
BLOCK = 128
ROPE_THETA = 10000.0
NORM_EPS = 1e-6
IN_W = 2 * ATTN_W + 2 * KV_W + 2 * SSM_W

LANES = 128
SUBLANES = 8
MXU_DIM = 256
SSM_Q = 8
SLAB_GROUPS = LANES // SSM_H
N_SLABS = SSM_W // LANES
SLAB_IN = SSM_Q * LANES
SLAB_STATE = 2 * SLAB_GROUPS * SSM_P
SCAN_STEPS = 3
TABLE_POWERS = SCAN_STEPS * SUBLANES
TABLE_BLOCKS = TABLE_POWERS + SUBLANES
VMEM_LIMIT = 56 * 1024 * 1024

_BF16 = jnp.bfloat16
_F32 = jnp.float32


def _resident(shape):
    nd = len(shape)
    return pl.BlockSpec(shape, lambda *_: (0,) * nd, pipeline_mode=pl.Buffered(1))


def _swap_halves(a):
    return pltpu.roll(a, HEAD_DIM, axis=1)


def _pair_slabs(n0, n1, n2, n3):
    low = lax.broadcasted_iota(jnp.int32, n0.shape, 1) < HEAD_DIM
    return (jnp.where(low, n0, _swap_halves(n2)), jnp.where(low, _swap_halves(n0), n2),
            jnp.where(low, n1, _swap_halves(n3)), jnp.where(low, _swap_halves(n1), n3))


def _rope(t, cos, sin_signed):
    lane = lax.broadcasted_iota(jnp.int32, t.shape, 1)
    first_half = (lane % HEAD_DIM) < (HEAD_DIM // 2)
    swapped = jnp.where(first_half,
                        pltpu.roll(t, LANES - HEAD_DIM // 2, axis=1),
                        pltpu.roll(t, HEAD_DIM // 2, axis=1))
    return t * cos + swapped * sin_signed


def _in_proj_kernel(x_ref, pos_ref, w_ref, qw_ref, kw_ref, invf_ref, sgn_ref,
                    q_ref, k_ref, v_ref, za_ref, ucat_ref, zs_ref, hn_ref, u_ref):
    tm = x_ref.shape[0]
    x = x_ref[...]
    hn_ref[...] = x.astype(_BF16)
    rs = jnp.broadcast_to(lax.rsqrt(jnp.mean(x * x, axis=-1, keepdims=True) + NORM_EPS), (tm, LANES))
    wide = 2 * MXU_DIM
    rs_wide = jnp.concatenate([rs] * (wide // LANES), axis=1)

    nsub = tm // LANES
    ang = jnp.concatenate(
        [jnp.broadcast_to(pos_ref[pl.ds(pl.program_id(0) * nsub + j, 1), :].astype(_F32),
                          (LANES, LANES)).T for j in range(nsub)], axis=0) * invf_ref[...]
    cos = jnp.cos(ang)
    sin_signed = jnp.sin(ang) * sgn_ref[...]

    def proj(c0):
        return jnp.dot(hn_ref[...], w_ref[:, c0:c0 + wide], preferred_element_type=_F32) * rs_wide

    def normed_rope(acc, nw):
        low = lax.broadcasted_iota(jnp.int32, (tm, LANES), 1) < HEAD_DIM
        out = []
        for s in range(MXU_DIM // LANES):
            a = acc[:, s * LANES:(s + 1) * LANES]
            sq = a * a
            ssq = jnp.where(low, jnp.sum(jnp.where(low, sq, 0.0), axis=-1, keepdims=True),
                            jnp.sum(jnp.where(low, 0.0, sq), axis=-1, keepdims=True))
            t = a * lax.rsqrt(ssq * (1.0 / HEAD_DIM) + NORM_EPS) * nw[:, s * LANES:(s + 1) * LANES]
            out.append(_rope(t, cos, sin_signed))
        return out

    base = 2 * ATTN_W + 2 * KV_W
    nc = tm // SSM_Q
    for c in range(SSM_W // wide):
        acc = proj(base + c * wide)
        for s in range(wide // LANES):
            u_ref[c * (wide // LANES) + s] = acc[:, s * LANES:(s + 1) * LANES]
    for s in range(N_SLABS):
        for i in range(SSM_Q):
            c0 = s * SLAB_IN + i * LANES
            ucat_ref[:, c0:c0 + LANES] = u_ref[s, pl.ds(i, nc, stride=SSM_Q), :].astype(_BF16)

    for c in range(ATTN_W // wide):
        acc = proj(c * wide)
        slabs = []
        for h in range(wide // MXU_DIM):
            n0 = c * wide + h * MXU_DIM
            slabs += normed_rope(acc[:, h * MXU_DIM:(h + 1) * MXU_DIM], qw_ref[:, n0:n0 + MXU_DIM])
        for r, slab in enumerate(_pair_slabs(*slabs)):
            q_ref[:, c * wide + r * LANES:c * wide + (r + 1) * LANES] = slab.astype(_BF16)
    acc = proj(ATTN_W)
    for s, slab in enumerate(normed_rope(acc[:, :KV_W], kw_ref[...])):
        k_ref[:, s * LANES:(s + 1) * LANES] = slab.astype(_BF16)
    v_ref[...] = acc[:, KV_W:].astype(_BF16)
    for out_ref, base in ((za_ref, ATTN_W + 2 * KV_W), (zs_ref, 2 * ATTN_W + 2 * KV_W + SSM_W)):
        for c in range(ATTN_W // wide):
            z = proj(base + c * wide)
            out_ref[:, c * wide:(c + 1) * wide] = (z * jax.nn.sigmoid(z)).astype(_BF16)


def _in_proj(x2, pos2, w_in, qw, kw, invf, sgn, tm):
    t = x2.shape[0]
    row = lambda w: pl.BlockSpec((tm, w), lambda i: (i, 0))
    return pl.pallas_call(
        _in_proj_kernel,
        grid=(t // tm,),
        in_specs=[row(D_MODEL), _resident(pos2.shape), _resident(w_in.shape), _resident(qw.shape),
                  _resident(kw.shape), _resident(invf.shape), _resident(sgn.shape)],
        out_specs=[row(ATTN_W), row(KV_W), row(KV_W), row(ATTN_W),
                   pl.BlockSpec((tm // SSM_Q, N_SLABS * SLAB_IN), lambda i: (i, 0)), row(SSM_W)],
        out_shape=[jax.ShapeDtypeStruct((t, ATTN_W), _BF16),
                   jax.ShapeDtypeStruct((t, KV_W), _BF16),
                   jax.ShapeDtypeStruct((t, KV_W), _BF16),
                   jax.ShapeDtypeStruct((t, ATTN_W), _BF16),
                   jax.ShapeDtypeStruct((t // SSM_Q, N_SLABS * SLAB_IN), _BF16),
                   jax.ShapeDtypeStruct((t, SSM_W), _BF16)],
        scratch_shapes=[pltpu.VMEM((tm, D_MODEL), _BF16), pltpu.VMEM((N_SLABS, tm, LANES), _F32)],
        compiler_params=pltpu.CompilerParams(dimension_semantics=("arbitrary",),
                                             vmem_limit_bytes=VMEM_LIMIT),
        name="in_proj",
    )(x2, pos2, w_in, qw, kw, invf, sgn)


def _attn_scores(q_ref, kp_ref, kc_ref, vp_ref, vc_ref):
    blocks = q_ref.shape[0] // BLOCK
    two = 2 * BLOCK
    lane_kv = lax.broadcasted_iota(jnp.int32, (two, LANES), 1)
    halves = [lane_kv < HEAD_DIM, lane_kv >= HEAD_DIM]
    zero = jnp.zeros((two, LANES), _BF16)
    key_head = lax.broadcasted_iota(jnp.int32, (2 * SUBLANES, 2 * two), 1) // two
    count_rows = (lax.broadcasted_iota(jnp.int32, (2 * SUBLANES, 2 * two), 0) == key_head).astype(_BF16)

    def keys(prev_ref, cur_ref, b, sl):
        if b == 0:
            return jnp.concatenate([prev_ref[:, sl], cur_ref[0:BLOCK, sl]], axis=0)
        return cur_ref[(b - 1) * BLOCK:(b + 1) * BLOCK, sl]

    scores, vts = {}, {}
    for b in range(blocks):
        rows = slice(b * BLOCK, (b + 1) * BLOCK)
        for j in range(N_KV_HEADS // 2):
            sl = slice(j * LANES, (j + 1) * LANES)
            k2 = keys(kp_ref, kc_ref, b, sl)
            v2 = keys(vp_ref, vc_ref, b, sl)
            vts[b, j] = jnp.concatenate(
                [jnp.concatenate([jnp.where(keep, v2, zero) for keep in halves], axis=0).T,
                 count_rows], axis=0)
            for t in range(KV_REP // 2):
                r0 = KV_REP * j + 2 * t
                qq = q_ref[rows, r0 * LANES:(r0 + 2) * LANES]
                qq = jnp.concatenate([qq[:, :LANES], qq[:, LANES:]], axis=0)
                for half, keep in enumerate(halves):
                    scores[b, j, t, half] = lax.dot_general(
                        jnp.where(keep, k2, zero), qq, (((1,), (1,)), ((), ())),
                        preferred_element_type=_F32)
    return scores, vts


def _attn_softmax_pv(sinks_ref, scores, vts, acc_ref, seq_start, blocks):
    two = 2 * BLOCK
    ik = lax.broadcasted_iota(jnp.int32, (two, two), 0)
    iq = lax.broadcasted_iota(jnp.int32, (two, two), 1) % BLOCK
    rel = ik - iq
    band = (rel >= 1) & (rel <= BLOCK)
    band_first = band & ((ik >= BLOCK) | jnp.logical_not(seq_start))
    first_slab = lax.broadcasted_iota(jnp.int32, (1, two), 1) < BLOCK
    row_o = lax.broadcasted_iota(jnp.int32, (LANES, two), 0)
    for b in blocks:
        rows = slice(b * BLOCK, (b + 1) * BLOCK)
        allowed = band_first if b == 0 else band
        for j in range(N_KV_HEADS // 2):
            for t in range(KV_REP // 2):
                r0 = KV_REP * j + 2 * t
                probs, sink_terms = [], []
                for half in range(2):
                    h0 = 2 * KV_REP * j + KV_REP * half + 2 * t
                    sink = jnp.where(first_slab, sinks_ref[h0], sinks_ref[h0 + 1])
                    s = jnp.where(allowed, scores[b, j, t, half], -1e30)
                    m = jnp.maximum(jnp.max(s, axis=0, keepdims=True), sink)
                    probs.append(jnp.exp2(s - m).astype(_BF16))
                    sink_terms.append(jnp.exp2(sink - m))
                o_t = jnp.dot(vts[b, j], jnp.concatenate(probs, axis=0), preferred_element_type=_F32)
                rdens = [1.0 / (o_t[LANES + half:LANES + half + 1] + sink_terms[half]) for half in range(2)]
                o_t = o_t[:LANES] * jnp.where(row_o < HEAD_DIM, rdens[0], rdens[1])
                for half in range(2):
                    heads = slice(half * HEAD_DIM, (half + 1) * HEAD_DIM)
                    slab = KV_REP * j + (KV_REP // 2) * half + t
                    acc_ref[rows, slab * LANES:(slab + 1) * LANES] = jnp.concatenate(
                        [o_t[heads, :BLOCK], o_t[heads, BLOCK:]], axis=0).T


def _ssm_weights(a_re, a_im, log_step, b_re, b_im, c_re, c_im, d_skip, n_scan):
    f32 = _F32
    a_re, a_im = a_re.astype(f32), a_im.astype(f32)
    delta = jnp.exp(log_step.astype(f32))[:, None]
    zr, zi = a_re * delta, a_im * delta

    def lam_pow(exponents):
        m = jnp.asarray(np.asarray(exponents, np.float32))[:, None, None]
        mag = jnp.exp(m * zr)
        return mag * jnp.cos(m * zi), mag * jnp.sin(m * zi)

    pw_r, pw_i = lam_pow(range(SSM_Q + 1))
    rev_r, rev_i = lam_pow(range(SSM_Q - 1, -1, -1))
    lr, li = pw_r[1], pw_i[1]
    den = a_re * a_re + a_im * a_im
    fr = ((lr - 1.0) * a_re + li * a_im) / den
    fi = (li * a_re - (lr - 1.0) * a_im) / den
    b_re, b_im = b_re.astype(f32), b_im.astype(f32)
    bbr = fr[..., None] * b_re - fi[..., None] * b_im
    bbi = fr[..., None] * b_im + fi[..., None] * b_re
    c_re, c_im = c_re.astype(f32), c_im.astype(f32)

    cl_r = c_re[None] * pw_r[:, :, None, :] - c_im[None] * pw_i[:, :, None, :]
    cl_i = c_re[None] * pw_i[:, :, None, :] + c_im[None] * pw_r[:, :, None, :]

    def per_slab(a):
        a = a.reshape(a.shape[0], N_SLABS, SLAB_GROUPS * SSM_H, a.shape[-1])
        return a.transpose(1, 0, 2, 3)

    sin_r = rev_r[..., None] * bbr[None] - rev_i[..., None] * bbi[None]
    sin_i = rev_r[..., None] * bbi[None] + rev_i[..., None] * bbr[None]
    cs = per_slab(jnp.concatenate([sin_r, sin_i], axis=2).transpose(0, 1, 3, 2))

    cl = per_slab(jnp.concatenate([cl_r, -cl_i], axis=3))
    skip = d_skip.astype(f32).reshape(N_SLABS, 1, LANES)

    exps, keep = [], []
    for k in range(SCAN_STEPS):
        exps += [SSM_Q << k] * SUBLANES
        keep += [0.0] * (1 << k) + [1.0] * (SUBLANES - (1 << k))
    exps += [SSM_Q * j for j in range(SUBLANES)] + [SSM_Q << k for k in range(SCAN_STEPS, n_scan)]
    keep += [1.0] * (SUBLANES + n_scan - SCAN_STEPS)
    tr, ti = lam_pow(exps)
    keep = jnp.asarray(np.asarray(keep, np.float32))[:, None, None]
    rows = jnp.stack([(tr * keep).reshape(-1, N_SLABS, SLAB_GROUPS // 2, 2 * SSM_P),
                      (ti * keep).reshape(-1, N_SLABS, SLAB_GROUPS // 2, 2 * SSM_P)], axis=3)
    decay = rows.transpose(1, 0, 2, 3, 4).reshape(N_SLABS, len(exps), SLAB_STATE)
    return cs, cl, skip, decay


def _expand_kernel(cs_ref, cl_ref, skip_ref, w1_ref, w2_ref):
    r16 = lax.broadcasted_iota(jnp.int32, (LANES, LANES), 0)
    c16 = lax.broadcasted_iota(jnp.int32, (LANES, LANES), 1)
    diag_h = (r16 // SSM_H) == (c16 // SSM_H)
    rs = lax.broadcasted_iota(jnp.int32, (LANES, SLAB_STATE), 0)
    cst = lax.broadcasted_iota(jnp.int32, (LANES, SLAB_STATE), 1)
    rep_s = ((rs // SSM_P == (cst // LANES) % 2) & (rs % SSM_P == cst % SSM_P)).astype(_BF16)
    diag_s = (rs // SSM_H) == (2 * (cst // (2 * LANES)) + (cst % LANES) // SSM_P)

    zero = jnp.zeros((LANES, LANES), _BF16)
    b_bar = cs_ref[0, SSM_Q - 1]
    blocks = []
    for d in range(SSM_Q):
        k_d = lax.dot_general(b_bar, cl_ref[0, d], (((1,), (1,)), ((), ())),
                              precision=lax.Precision.HIGHEST, preferred_element_type=_F32)
        if d == 0:
            k_d = k_d + jnp.where(r16 == c16, skip_ref[0], 0.0)
        blocks.append(jnp.where(diag_h, k_d, 0.0).astype(_BF16))
    for i in range(SSM_Q):
        for j in range(SSM_Q):
            w1_ref[0, i * LANES:(i + 1) * LANES, j * LANES:(j + 1) * LANES] = blocks[j - i] if j >= i else zero
        w_in = jnp.where(diag_s, jnp.dot(cs_ref[0, i].astype(_BF16), rep_s, preferred_element_type=_F32), 0.0)
        w1_ref[0, i * LANES:(i + 1) * LANES, SLAB_IN:] = w_in.astype(_BF16)
        w_out_t = jnp.where(diag_s, jnp.dot(cl_ref[0, i + 1].astype(_BF16), rep_s,
                                            preferred_element_type=_F32), 0.0)
        w2_ref[0, i * LANES:(i + 1) * LANES, :] = w_out_t.astype(_BF16)


def _expand(cs, cl, skip):
    blk = lambda a: pl.BlockSpec((1,) + a.shape[1:], lambda s: (s,) + (0,) * (a.ndim - 1))
    return pl.pallas_call(
        _expand_kernel,
        grid=(N_SLABS,),
        in_specs=[blk(cs), blk(cl), blk(skip)],
        out_specs=[pl.BlockSpec((1, SLAB_IN, SLAB_IN + SLAB_STATE), lambda s: (s, 0, 0)),
                   pl.BlockSpec((1, SLAB_IN, SLAB_STATE), lambda s: (s, 0, 0))],
        out_shape=[jax.ShapeDtypeStruct((N_SLABS, SLAB_IN, SLAB_IN + SLAB_STATE), _BF16),
                   jax.ShapeDtypeStruct((N_SLABS, SLAB_IN, SLAB_STATE), _BF16)],
        compiler_params=pltpu.CompilerParams(dimension_semantics=("arbitrary",)),
        name="s5_expand",
    )(cs, cl, skip)


def _ssm_kernel(x_ref, w1_ref, w2_ref, dec_ref, y_ref, h_ref, t_scr):
    nc = x_ref.shape[0]
    nb = nc // SUBLANES
    piece = 2 * LANES

    def cmul_add(acc_r, acc_i, pr, pi, tr, ti):
        return acc_r + pr * tr - pi * ti, acc_i + pr * ti + pi * tr

    def roll_in_block(a, shift):
        return pltpu.roll(a.reshape(nb, SUBLANES, LANES), shift, axis=1).reshape(nc, LANES)

    @pl.when(pl.program_id(2) == 0)
    def _():
        h_ref[...] = jnp.zeros_like(h_ref)

    x = x_ref[...]
    s_all = jnp.dot(x, w1_ref[0, :, SLAB_IN:], preferred_element_type=_F32)
    sub = lax.broadcasted_iota(jnp.int32, (nc, LANES), 0) % SUBLANES
    brow = lax.broadcasted_iota(jnp.int32, (nb, LANES), 0)
    y = None
    for p in range(SLAB_STATE // piece):
        re, im = slice(p * piece, p * piece + LANES), slice(p * piece + LANES, (p + 1) * piece)
        table = lambda row0, n: (dec_ref[0, row0:row0 + n, re], dec_ref[0, row0:row0 + n, im])
        tall = lambda a: jnp.concatenate([a] * nb, axis=0)

        sr, si = s_all[:, re], s_all[:, im]
        hin_r, hin_i = h_ref[0:1, re], h_ref[0:1, im]

        for k in range(SCAN_STEPS):
            mr, mi = table(k * SUBLANES, SUBLANES)
            sr, si = cmul_add(sr, si, tall(mr), tall(mi),
                              roll_in_block(sr, 1 << k), roll_in_block(si, 1 << k))

        t_scr[2 * p] = sr
        t_scr[2 * p + 1] = si
        br = t_scr[2 * p, pl.ds(SUBLANES - 1, nb, stride=SUBLANES), :]
        bi = t_scr[2 * p + 1, pl.ds(SUBLANES - 1, nb, stride=SUBLANES), :]
        pr, pi = table(TABLE_BLOCKS, 1)
        br = br + jnp.where(brow == 0, pr * hin_r - pi * hin_i, 0.0)
        bi = bi + jnp.where(brow == 0, pr * hin_i + pi * hin_r, 0.0)
        for k in range(dec_ref.shape[1] - TABLE_BLOCKS):
            shift = 1 << k
            tr = jnp.where(brow >= shift, pltpu.roll(br, shift, axis=0), 0.0)
            ti = jnp.where(brow >= shift, pltpu.roll(bi, shift, axis=0), 0.0)
            br, bi = cmul_add(br, bi, *table(TABLE_BLOCKS + k, 1), tr, ti)
        h_ref[0:1, re] = br[nb - 1:nb]
        h_ref[0:1, im] = bi[nb - 1:nb]
        bpr = jnp.where(brow == 0, hin_r, pltpu.roll(br, 1, axis=0))
        bpi = jnp.where(brow == 0, hin_i, pltpu.roll(bi, 1, axis=0))

        rep = lambda a: jnp.concatenate(
            [jnp.broadcast_to(a[k:k + 1], (SUBLANES, LANES)) for k in range(nb)], axis=0)
        fr, fi = table(TABLE_POWERS, SUBLANES)
        tr = jnp.where(sub >= 1, roll_in_block(sr, 1), 0.0)
        ti = jnp.where(sub >= 1, roll_in_block(si, 1), 0.0)
        prev_r, prev_i = cmul_add(tr, ti, tall(fr), tall(fi), rep(bpr), rep(bpi))
        hprev = jnp.concatenate([prev_r, prev_i], axis=1).astype(_BF16)
        part = lax.dot_general(hprev, w2_ref[0, :, p * piece:(p + 1) * piece], (((1,), (1,)), ((), ())),
                               preferred_element_type=_F32)
        if y is None:
            mid = SLAB_IN // 2
            y = jnp.concatenate(
                [jnp.dot(x[:, :mid], w1_ref[0, :mid, :mid], preferred_element_type=_F32),
                 jnp.dot(x, w1_ref[0, :, mid:SLAB_IN], preferred_element_type=_F32)], axis=1) + part
        else:
            y = y + part

    for i in range(SSM_Q):
        y_ref[pl.ds(i, nc, stride=SSM_Q), :] = y[:, i * LANES:(i + 1) * LANES]


def _ssm(ucat, w1, w2, decay, batch, seq, nc):
    tiles = seq // (nc * SSM_Q)
    return pl.pallas_call(
        _ssm_kernel,
        grid=(N_SLABS, batch, tiles),
        in_specs=[pl.BlockSpec((nc, SLAB_IN), lambda s, b, t: (b * tiles + t, s)),
                  pl.BlockSpec((1,) + w1.shape[1:], lambda s, b, t: (s, 0, 0)),
                  pl.BlockSpec((1,) + w2.shape[1:], lambda s, b, t: (s, 0, 0)),
                  pl.BlockSpec((1,) + decay.shape[1:], lambda s, b, t: (s, 0, 0))],
        out_specs=pl.BlockSpec((nc * SSM_Q, LANES), lambda s, b, t: (b * tiles + t, s)),
        out_shape=jax.ShapeDtypeStruct((batch * seq, SSM_W), _F32),
        scratch_shapes=[pltpu.VMEM((SUBLANES, SLAB_STATE), _F32),
                        pltpu.VMEM((SLAB_STATE // LANES, nc, LANES), _F32)],
        compiler_params=pltpu.CompilerParams(
            dimension_semantics=("arbitrary", "arbitrary", "arbitrary"),
            vmem_limit_bytes=VMEM_LIMIT),
        name="s5_mixer",
    )(ucat, w1, w2, decay)


def _gelu_tanh(y):
    c = math.sqrt(2.0 / math.pi)
    return (0.5 * y) * (1.0 + jnp.tanh(y * (c + (c * 0.044715) * (y * y))))


def _mix_out_kernel(sinks_ref, x_ref, q_ref, kp_ref, kc_ref, vp_ref, vc_ref, za_ref, y_ref, zs_ref,
                    wg_ref, bg_ref, wo_ref, out_ref, acc_ref, *, tiles_per_seq):
    scores, vts = _attn_scores(q_ref, kp_ref, kc_ref, vp_ref, vc_ref)

    y = _gelu_tanh(y_ref[...].astype(_F32))
    gate = jnp.dot(y.astype(_BF16), wg_ref[...], preferred_element_type=_F32) + bg_ref[...]
    o = y * jax.nn.sigmoid(gate) * zs_ref[...].astype(_F32)
    rs = lax.rsqrt(jnp.mean(o * o, axis=-1, keepdims=True) + NORM_EPS)
    out_ref[...] = x_ref[...] + rs * jnp.dot(o.astype(_BF16), wo_ref[ATTN_W:, :], preferred_element_type=_F32)

    seq_start = pl.program_id(0) % tiles_per_seq == 0
    _attn_softmax_pv(sinks_ref, scores, vts, acc_ref, seq_start, range(acc_ref.shape[0] // BLOCK))
    g = acc_ref[...] * za_ref[...].astype(_F32)
    rs = lax.rsqrt(jnp.mean(g * g, axis=-1, keepdims=True) + NORM_EPS)
    out_ref[...] += rs * jnp.dot(g.astype(_BF16), wo_ref[:ATTN_W, :], preferred_element_type=_F32)


def _mix_out(sinks, x2, q, k, v, za, y, zs, wg, bg, wo, seq, tm):
    t = x2.shape[0]
    blocks = tm // BLOCK
    row = lambda w: pl.BlockSpec((tm, w), lambda i: (i, 0))
    prev = lambda w: pl.BlockSpec((BLOCK, w), lambda i: (jnp.maximum(i * blocks - 1, 0), 0))
    return pl.pallas_call(
        functools.partial(_mix_out_kernel, tiles_per_seq=seq // tm),
        grid=(t // tm,),
        in_specs=[pl.BlockSpec(memory_space=pltpu.SMEM), row(D_MODEL), row(ATTN_W), prev(KV_W), row(KV_W),
                  prev(KV_W), row(KV_W), row(ATTN_W), row(SSM_W), row(SSM_W),
                  _resident(wg.shape), _resident(bg.shape), _resident(wo.shape)],
        out_specs=row(D_MODEL),
        out_shape=jax.ShapeDtypeStruct((t, D_MODEL), x2.dtype),
        scratch_shapes=[pltpu.VMEM((tm, ATTN_W), _F32)],
        compiler_params=pltpu.CompilerParams(dimension_semantics=("arbitrary",),
                                             vmem_limit_bytes=VMEM_LIMIT),
        name="mix_out",
    )(sinks, x2, q, k, k, v, v, za, y, zs, wg, bg, wo)


def _pick_tile(n, target, multiple):
    best = multiple
    for cand in range(multiple, min(n, target) + 1, multiple):
        if n % cand == 0:
            best = cand
    return best


def kernel(x, positions, norm_w, w_in, q_norm_w, k_norm_w, sinks, a_re, a_im, log_step,
           b_re, b_im, c_re, c_im, d_skip, w_glu, b_glu, attn_out_norm_w, ssm_out_norm_w, w_out):
    batch, seq, _ = x.shape
    assert seq % BLOCK == 0 and x.shape[2] == D_MODEL and w_in.shape == (D_MODEL, IN_W)
    t = batch * seq
    f32 = _F32

    w_in_k = (norm_w.astype(f32)[:, None] * w_in.astype(f32)).astype(_BF16)
    out_gain = jnp.concatenate([attn_out_norm_w.astype(f32), ssm_out_norm_w.astype(f32)])
    w_out_k = (out_gain[:, None] * w_out.astype(f32)).astype(_BF16)
    log2e = math.log2(math.e)
    qw = (jnp.tile(q_norm_w.astype(f32), N_HEADS) * (log2e / math.sqrt(HEAD_DIM)))[None, :]
    kw = jnp.tile(k_norm_w.astype(f32), N_KV_HEADS)[None, :]
    inv_freq = ROPE_THETA ** (-jnp.arange(0, HEAD_DIM, 2, dtype=f32) / HEAD_DIM)
    invf = jnp.tile(inv_freq, LANES // (HEAD_DIM // 2))[None, :]
    sgn = jnp.tile(jnp.concatenate([-jnp.ones(HEAD_DIM // 2, f32), jnp.ones(HEAD_DIM // 2, f32)]),
                   LANES // HEAD_DIM)[None, :]

    x2 = x.reshape(t, D_MODEL)
    pos2 = positions.reshape(t // LANES, LANES).astype(jnp.int32)

    tm = _pick_tile(seq, 512, BLOCK)
    q, k, v, za, ucat, zs = _in_proj(x2, pos2, w_in_k, qw, kw, invf, sgn, tm)


    nc = _pick_tile(seq // SSM_Q, 512, 16)
    assert nc & (nc - 1) == 0, "chunk rows per tile must be a power of two for the scan"
    cs, cl, skip, decay = _ssm_weights(a_re, a_im, log_step, b_re, b_im, c_re, c_im, d_skip,
                                     n_scan=int(math.log2(nc)))
    w1, w2 = _expand(cs, cl, skip)
    y = _ssm(ucat, w1, w2, decay, batch, seq, nc)

    out = _mix_out(sinks.astype(f32) * log2e, x2, q, k, v, za, y, zs, w_glu.astype(_BF16),
                   b_glu.astype(f32)[None, :], w_out_k, seq, tm)
    return out.reshape(x.shape)
```

```python
import functools
import math

import jax
import jax.numpy as jnp
import numpy as np
from jax import lax
from jax.experimental import pallas as pl
from jax.experimental.pallas import tpu as pltpu

D_MODEL = 2048
ATTN_W = 1024
KV_W = 256
HEAD_DIM = 64
N_HEADS = 16
N_KV_HEADS = 4
KV_REP = N_HEADS // N_KV_HEADS
SSM_W = 1024
SSM_H = 16
SSM_G = 64
SSM_P = 64
BLOCK = 128
ROPE_THETA = 10000.0
NORM_EPS = 1e-6
IN_W = 2 * ATTN_W + 2 * KV_W + 2 * SSM_W

LANES = 128
SUBLANES = 8
MXU_DIM = 256
SSM_Q = 8
SLAB_GROUPS = LANES // SSM_H
N_SLABS = SSM_W // LANES
SLAB_IN = SSM_Q * LANES
SLAB_STATE = 2 * SLAB_GROUPS * SSM_P
SCAN_STEPS = 3
TABLE_POWERS = SCAN_STEPS * SUBLANES
TABLE_BLOCKS = TABLE_POWERS + SUBLANES
VMEM_LIMIT = 56 * 1024 * 1024

_BF16 = jnp.bfloat16
_F32 = jnp.float32


def _resident(shape):
    nd = len(shape)
    return pl.BlockSpec(shape, lambda *_: (0,) * nd, pipeline_mode=pl.Buffered(1))


def _swap_halves(a):
    return pltpu.roll(a, HEAD_DIM, axis=1)


def _pair_slabs(n0, n1, n2, n3):
    low = lax.broadcasted_iota(jnp.int32, n0.shape, 1) < HEAD_DIM
    return (jnp.where(low, n0, _swap_halves(n2)), jnp.where(low, _swap_halves(n0), n2),
            jnp.where(low, n1, _swap_halves(n3)), jnp.where(low, _swap_halves(n1), n3))


def _rope(t, cos, sin_signed):
    lane = lax.broadcasted_iota(jnp.int32, t.shape, 1)
    first_half = (lane % HEAD_DIM) < (HEAD_DIM // 2)
    swapped = jnp.where(first_half,
                        pltpu.roll(t, LANES - HEAD_DIM // 2, axis=1),
                        pltpu.roll(t, HEAD_DIM // 2, axis=1))
    return t * cos + swapped * sin_signed


def _in_proj_kernel(x_ref, pos_ref, w_ref, qw_ref, kw_ref, invf_ref, sgn_ref,
                    q_ref, k_ref, v_ref, za_ref, ucat_ref, zs_ref, hn_ref, u_ref):
    tm = x_ref.shape[0]
    x = x_ref[...]
    hn_ref[...] = x.astype(_BF16)
    rs = jnp.broadcast_to(lax.rsqrt(jnp.mean(x * x, axis=-1, keepdims=True) + NORM_EPS), (tm, LANES))
    wide = 2 * MXU_DIM
    rs_wide = jnp.concatenate([rs] * (wide // LANES), axis=1)

    nsub = tm // LANES
    ang = jnp.concatenate(
        [jnp.broadcast_to(pos_ref[pl.ds(pl.program_id(0) * nsub + j, 1), :].astype(_F32),
                          (LANES, LANES)).T for j in range(nsub)], axis=0) * invf_ref[...]
    cos = jnp.cos(ang)
    sin_signed = jnp.sin(ang) * sgn_ref[...]

    def proj(c0):
        return jnp.dot(hn_ref[...], w_ref[:, c0:c0 + wide], preferred_element_type=_F32) * rs_wide

    def normed_rope(acc, nw):
        low = lax.broadcasted_iota(jnp.int32, (tm, LANES), 1) < HEAD_DIM
        out = []
        for s in range(MXU_DIM // LANES):
            a = acc[:, s * LANES:(s + 1) * LANES]
            sq = a * a
            ssq = jnp.where(low, jnp.sum(jnp.where(low, sq, 0.0), axis=-1, keepdims=True),
                            jnp.sum(jnp.where(low, 0.0, sq), axis=-1, keepdims=True))
            t = a * lax.rsqrt(ssq * (1.0 / HEAD_DIM) + NORM_EPS) * nw[:, s * LANES:(s + 1) * LANES]
            out.append(_rope(t, cos, sin_signed))
        return out

    base = 2 * ATTN_W + 2 * KV_W
    nc = tm // SSM_Q
    for c in range(SSM_W // wide):
        acc = proj(base + c * wide)
        for s in range(wide // LANES):
            u_ref[c * (wide // LANES) + s] = acc[:, s * LANES:(s + 1) * LANES]
    for s in range(N_SLABS):
        for i in range(SSM_Q):
            c0 = s * SLAB_IN + i * LANES
            ucat_ref[:, c0:c0 + LANES] = u_ref[s, pl.ds(i, nc, stride=SSM_Q), :].astype(_BF16)

    for c in range(ATTN_W // wide):
        acc = proj(c * wide)
        slabs = []
        for h in range(wide // MXU_DIM):
            n0 = c * wide + h * MXU_DIM
            slabs += normed_rope(acc[:, h * MXU_DIM:(h + 1) * MXU_DIM], qw_ref[:, n0:n0 + MXU_DIM])
        for r, slab in enumerate(_pair_slabs(*slabs)):
            q_ref[:, c * wide + r * LANES:c * wide + (r + 1) * LANES] = slab.astype(_BF16)
    acc = proj(ATTN_W)
    for s, slab in enumerate(normed_rope(acc[:, :KV_W], kw_ref[...])):
        k_ref[:, s * LANES:(s + 1) * LANES] = slab.astype(_BF16)
    v_ref[...] = acc[:, KV_W:].astype(_BF16)
    for out_ref, base in ((za_ref, ATTN_W + 2 * KV_W), (zs_ref, 2 * ATTN_W + 2 * KV_W + SSM_W)):
        for c in range(ATTN_W // wide):
            z = proj(base + c * wide)
            out_ref[:, c * wide:(c + 1) * wide] = (z * jax.nn.sigmoid(z)).astype(_BF16)


def _in_proj(x2, pos2, w_in, qw, kw, invf, sgn, tm):
    t = x2.shape[0]
    row = lambda w: pl.BlockSpec((tm, w), lambda i: (i, 0))
    return pl.pallas_call(
        _in_proj_kernel,
        grid=(t // tm,),
        in_specs=[row(D_MODEL), _resident(pos2.shape), _resident(w_in.shape), _resident(qw.shape),
                  _resident(kw.shape), _resident(invf.shape), _resident(sgn.shape)],
        out_specs=[row(ATTN_W), row(KV_W), row(KV_W), row(ATTN_W),
                   pl.BlockSpec((tm // SSM_Q, N_SLABS * SLAB_IN), lambda i: (i, 0)), row(SSM_W)],
        out_shape=[jax.ShapeDtypeStruct((t, ATTN_W), _BF16),
                   jax.ShapeDtypeStruct((t, KV_W), _BF16),
                   jax.ShapeDtypeStruct((t, KV_W), _BF16),
                   jax.ShapeDtypeStruct((t, ATTN_W), _BF16),
                   jax.ShapeDtypeStruct((t // SSM_Q, N_SLABS * SLAB_IN), _BF16),
                   jax.ShapeDtypeStruct((t, SSM_W), _BF16)],
        scratch_shapes=[pltpu.VMEM((tm, D_MODEL), _BF16), pltpu.VMEM((N_SLABS, tm, LANES), _F32)],
        compiler_params=pltpu.CompilerParams(dimension_semantics=("arbitrary",),
                                             vmem_limit_bytes=VMEM_LIMIT),
        name="in_proj",
    )(x2, pos2, w_in, qw, kw, invf, sgn)


def _attn_scores(q_ref, kp_ref, kc_ref, vp_ref, vc_ref):
    blocks = q_ref.shape[0] // BLOCK
    two = 2 * BLOCK
    lane_kv = lax.broadcasted_iota(jnp.int32, (two, LANES), 1)
    halves = [lane_kv < HEAD_DIM, lane_kv >= HEAD_DIM]
    zero = jnp.zeros((two, LANES), _BF16)
    key_head = lax.broadcasted_iota(jnp.int32, (2 * SUBLANES, 2 * two), 1) // two
    count_rows = (lax.broadcasted_iota(jnp.int32, (2 * SUBLANES, 2 * two), 0) == key_head).astype(_BF16)

    def keys(prev_ref, cur_ref, b, sl):
        if b == 0:
            return jnp.concatenate([prev_ref[:, sl], cur_ref[0:BLOCK, sl]], axis=0)
        return cur_ref[(b - 1) * BLOCK:(b + 1) * BLOCK, sl]

    scores, vts = {}, {}
    for b in range(blocks):
        rows = slice(b * BLOCK, (b + 1) * BLOCK)
        for j in range(N_KV_HEADS // 2):
            sl = slice(j * LANES, (j + 1) * LANES)
            k2 = keys(kp_ref, kc_ref, b, sl)
            v2 = keys(vp_ref, vc_ref, b, sl)
            vts[b, j] = jnp.concatenate(
                [jnp.concatenate([jnp.where(keep, v2, zero) for keep in halves], axis=0).T,
                 count_rows], axis=0)
            for t in range(KV_REP // 2):
                r0 = KV_REP * j + 2 * t
                qq = q_ref[rows, r0 * LANES:(r0 + 2) * LANES]
                qq = jnp.concatenate([qq[:, :LANES], qq[:, LANES:]], axis=0)
                for half, keep in enumerate(halves):
                    scores[b, j, t, half] = lax.dot_general(
                        jnp.where(keep, k2, zero), qq, (((1,), (1,)), ((), ())),
                        preferred_element_type=_F32)
    return scores, vts


def _attn_softmax_pv(sinks_ref, scores, vts, acc_ref, seq_start, blocks):
    two = 2 * BLOCK
    ik = lax.broadcasted_iota(jnp.int32, (two, two), 0)
    iq = lax.broadcasted_iota(jnp.int32, (two, two), 1) % BLOCK
    rel = ik - iq
    band = (rel >= 1) & (rel <= BLOCK)
    band_first = band & ((ik >= BLOCK) | jnp.logical_not(seq_start))
    first_slab = lax.broadcasted_iota(jnp.int32, (1, two), 1) < BLOCK
    row_o = lax.broadcasted_iota(jnp.int32, (LANES, two), 0)
    for b in blocks:
        rows = slice(b * BLOCK, (b + 1) * BLOCK)
        allowed = band_first if b == 0 else band
        for j in range(N_KV_HEADS // 2):
            for t in range(KV_REP // 2):
                r0 = KV_REP * j + 2 * t
                probs, sink_terms = [], []
                for half in range(2):
                    h0 = 2 * KV_REP * j + KV_REP * half + 2 * t
                    sink = jnp.where(first_slab, sinks_ref[h0], sinks_ref[h0 + 1])
                    s = jnp.where(allowed, scores[b, j, t, half], -1e30)
                    m = jnp.maximum(jnp.max(s, axis=0, keepdims=True), sink)
                    probs.append(jnp.exp2(s - m).astype(_BF16))
                    sink_terms.append(jnp.exp2(sink - m))
                o_t = jnp.dot(vts[b, j], jnp.concatenate(probs, axis=0), preferred_element_type=_F32)
                rdens = [1.0 / (o_t[LANES + half:LANES + half + 1] + sink_terms[half]) for half in range(2)]
                o_t = o_t[:LANES] * jnp.where(row_o < HEAD_DIM, rdens[0], rdens[1])
                for half in range(2):
                    heads = slice(half * HEAD_DIM, (half + 1) * HEAD_DIM)
                    slab = KV_REP * j + (KV_REP // 2) * half + t
                    acc_ref[rows, slab * LANES:(slab + 1) * LANES] = jnp.concatenate(
                        [o_t[heads, :BLOCK], o_t[heads, BLOCK:]], axis=0).T


def _ssm_weights(a_re, a_im, log_step, b_re, b_im, c_re, c_im, d_skip, n_scan):
    f32 = _F32
    a_re, a_im = a_re.astype(f32), a_im.astype(f32)
    delta = jnp.exp(log_step.astype(f32))[:, None]
    zr, zi = a_re * delta, a_im * delta

    def lam_pow(exponents):
        m = jnp.asarray(np.asarray(exponents, np.float32))[:, None, None]
        mag = jnp.exp(m * zr)
        return mag * jnp.cos(m * zi), mag * jnp.sin(m * zi)

    pw_r, pw_i = lam_pow(range(SSM_Q + 1))
    rev_r, rev_i = lam_pow(range(SSM_Q - 1, -1, -1))
    lr, li = pw_r[1], pw_i[1]
    den = a_re * a_re + a_im * a_im
    fr = ((lr - 1.0) * a_re + li * a_im) / den
    fi = (li * a_re - (lr - 1.0) * a_im) / den
    b_re, b_im = b_re.astype(f32), b_im.astype(f32)
    bbr = fr[..., None] * b_re - fi[..., None] * b_im
    bbi = fr[..., None] * b_im + fi[..., None] * b_re
    c_re, c_im = c_re.astype(f32), c_im.astype(f32)

    cl_r = c_re[None] * pw_r[:, :, None, :] - c_im[None] * pw_i[:, :, None, :]
    cl_i = c_re[None] * pw_i[:, :, None, :] + c_im[None] * pw_r[:, :, None, :]

    def per_slab(a):
        a = a.reshape(a.shape[0], N_SLABS, SLAB_GROUPS * SSM_H, a.shape[-1])
        return a.transpose(1, 0, 2, 3)

    sin_r = rev_r[..., None] * bbr[None] - rev_i[..., None] * bbi[None]
    sin_i = rev_r[..., None] * bbi[None] + rev_i[..., None] * bbr[None]
    cs = per_slab(jnp.concatenate([sin_r, sin_i], axis=2).transpose(0, 1, 3, 2))

    cl = per_slab(jnp.concatenate([cl_r, -cl_i], axis=3))
    skip = d_skip.astype(f32).reshape(N_SLABS, 1, LANES)

    exps, keep = [], []
    for k in range(SCAN_STEPS):
        exps += [SSM_Q << k] * SUBLANES
        keep += [0.0] * (1 << k) + [1.0] * (SUBLANES - (1 << k))
    exps += [SSM_Q * j for j in range(SUBLANES)] + [SSM_Q << k for k in range(SCAN_STEPS, n_scan)]
    keep += [1.0] * (SUBLANES + n_scan - SCAN_STEPS)
    tr, ti = lam_pow(exps)
    keep = jnp.asarray(np.asarray(keep, np.float32))[:, None, None]
    rows = jnp.stack([(tr * keep).reshape(-1, N_SLABS, SLAB_GROUPS // 2, 2 * SSM_P),
                      (ti * keep).reshape(-1, N_SLABS, SLAB_GROUPS // 2, 2 * SSM_P)], axis=3)
    decay = rows.transpose(1, 0, 2, 3, 4).reshape(N_SLABS, len(exps), SLAB_STATE)
    return cs, cl, skip, decay


def _expand_kernel(cs_ref, cl_ref, skip_ref, w1_ref, w2_ref):
    r16 = lax.broadcasted_iota(jnp.int32, (LANES, LANES), 0)
    c16 = lax.broadcasted_iota(jnp.int32, (LANES, LANES), 1)
    diag_h = (r16 // SSM_H) == (c16 // SSM_H)
    rs = lax.broadcasted_iota(jnp.int32, (LANES, SLAB_STATE), 0)
    cst = lax.broadcasted_iota(jnp.int32, (LANES, SLAB_STATE), 1)
    rep_s = ((rs // SSM_P == (cst // LANES) % 2) & (rs % SSM_P == cst % SSM_P)).astype(_BF16)
    diag_s = (rs // SSM_H) == (2 * (cst // (2 * LANES)) + (cst % LANES) // SSM_P)

    zero = jnp.zeros((LANES, LANES), _BF16)
    b_bar = cs_ref[0, SSM_Q - 1]
    blocks = []
    for d in range(SSM_Q):
        k_d = lax.dot_general(b_bar, cl_ref[0, d], (((1,), (1,)), ((), ())),
                              precision=lax.Precision.HIGHEST, preferred_element_type=_F32)
        if d == 0:
            k_d = k_d + jnp.where(r16 == c16, skip_ref[0], 0.0)
        blocks.append(jnp.where(diag_h, k_d, 0.0).astype(_BF16))
    for i in range(SSM_Q):
        for j in range(SSM_Q):
            w1_ref[0, i * LANES:(i + 1) * LANES, j * LANES:(j + 1) * LANES] = blocks[j - i] if j >= i else zero
        w_in = jnp.where(diag_s, jnp.dot(cs_ref[0, i].astype(_BF16), rep_s, preferred_element_type=_F32), 0.0)
        w1_ref[0, i * LANES:(i + 1) * LANES, SLAB_IN:] = w_in.astype(_BF16)
        w_out_t = jnp.where(diag_s, jnp.dot(cl_ref[0, i + 1].astype(_BF16), rep_s,
                                            preferred_element_type=_F32), 0.0)
        w2_ref[0, i * LANES:(i + 1) * LANES, :] = w_out_t.astype(_BF16)


def _expand(cs, cl, skip):
    blk = lambda a: pl.BlockSpec((1,) + a.shape[1:], lambda s: (s,) + (0,) * (a.ndim - 1))
    return pl.pallas_call(
        _expand_kernel,
        grid=(N_SLABS,),
        in_specs=[blk(cs), blk(cl), blk(skip)],
        out_specs=[pl.BlockSpec((1, SLAB_IN, SLAB_IN + SLAB_STATE), lambda s: (s, 0, 0)),
                   pl.BlockSpec((1, SLAB_IN, SLAB_STATE), lambda s: (s, 0, 0))],
        out_shape=[jax.ShapeDtypeStruct((N_SLABS, SLAB_IN, SLAB_IN + SLAB_STATE), _BF16),
                   jax.ShapeDtypeStruct((N_SLABS, SLAB_IN, SLAB_STATE), _BF16)],
        compiler_params=pltpu.CompilerParams(dimension_semantics=("arbitrary",)),
        name="s5_expand",
    )(cs, cl, skip)


def _ssm_kernel(x_ref, w1_ref, w2_ref, dec_ref, y_ref, h_ref, t_scr):
    nc = x_ref.shape[0]
    nb = nc // SUBLANES
    piece = 2 * LANES

    def cmul_add(acc_r, acc_i, pr, pi, tr, ti):
        return acc_r + pr * tr - pi * ti, acc_i + pr * ti + pi * tr

    def roll_in_block(a, shift):
        return pltpu.roll(a.reshape(nb, SUBLANES, LANES), shift, axis=1).reshape(nc, LANES)

    @pl.when(pl.program_id(2) == 0)
    def _():
        h_ref[...] = jnp.zeros_like(h_ref)

    x = x_ref[...]
    s_all = jnp.dot(x, w1_ref[0, :, SLAB_IN:], preferred_element_type=_F32)
    sub = lax.broadcasted_iota(jnp.int32, (nc, LANES), 0) % SUBLANES
    brow = lax.broadcasted_iota(jnp.int32, (nb, LANES), 0)
    y = None
    for p in range(SLAB_STATE // piece):
        re, im = slice(p * piece, p * piece + LANES), slice(p * piece + LANES, (p + 1) * piece)
        table = lambda row0, n: (dec_ref[0, row0:row0 + n, re], dec_ref[0, row0:row0 + n, im])
        tall = lambda a: jnp.concatenate([a] * nb, axis=0)

        sr, si = s_all[:, re], s_all[:, im]
        hin_r, hin_i = h_ref[0:1, re], h_ref[0:1, im]

        for k in range(SCAN_STEPS):
            mr, mi = table(k * SUBLANES, SUBLANES)
            sr, si = cmul_add(sr, si, tall(mr), tall(mi),
                              roll_in_block(sr, 1 << k), roll_in_block(si, 1 << k))

        t_scr[2 * p] = sr
        t_scr[2 * p + 1] = si
        br = t_scr[2 * p, pl.ds(SUBLANES - 1, nb, stride=SUBLANES), :]
        bi = t_scr[2 * p + 1, pl.ds(SUBLANES - 1, nb, stride=SUBLANES), :]
        pr, pi = table(TABLE_BLOCKS, 1)
        br = br + jnp.where(brow == 0, pr * hin_r - pi * hin_i, 0.0)
        bi = bi + jnp.where(brow == 0, pr * hin_i + pi * hin_r, 0.0)
        for k in range(dec_ref.shape[1] - TABLE_BLOCKS):
            shift = 1 << k
            tr = jnp.where(brow >= shift, pltpu.roll(br, shift, axis=0), 0.0)
            ti = jnp.where(brow >= shift, pltpu.roll(bi, shift, axis=0), 0.0)
            br, bi = cmul_add(br, bi, *table(TABLE_BLOCKS + k, 1), tr, ti)
        h_ref[0:1, re] = br[nb - 1:nb]
        h_ref[0:1, im] = bi[nb - 1:nb]
        bpr = jnp.where(brow == 0, hin_r, pltpu.roll(br, 1, axis=0))
        bpi = jnp.where(brow == 0, hin_i, pltpu.roll(bi, 1, axis=0))

        rep = lambda a: jnp.concatenate(
            [jnp.broadcast_to(a[k:k + 1], (SUBLANES, LANES)) for k in range(nb)], axis=0)
        fr, fi = table(TABLE_POWERS, SUBLANES)
        tr = jnp.where(sub >= 1, roll_in_block(sr, 1), 0.0)
        ti = jnp.where(sub >= 1, roll_in_block(si, 1), 0.0)
        prev_r, prev_i = cmul_add(tr, ti, tall(fr), tall(fi), rep(bpr), rep(bpi))
        hprev = jnp.concatenate([prev_r, prev_i], axis=1).astype(_BF16)
        part = lax.dot_general(hprev, w2_ref[0, :, p * piece:(p + 1) * piece], (((1,), (1,)), ((), ())),
                               preferred_element_type=_F32)
        if y is None:
            q4 = SLAB_IN // 4
            y = jnp.concatenate(
                [jnp.dot(x[:, :(c + 1) * q4], w1_ref[0, :(c + 1) * q4, c * q4:(c + 1) * q4],
                         preferred_element_type=_F32) for c in range(4)], axis=1) + part
        else:
            y = y + part

    for i in range(SSM_Q):
        y_ref[pl.ds(i, nc, stride=SSM_Q), :] = y[:, i * LANES:(i + 1) * LANES]


def _ssm(ucat, w1, w2, decay, batch, seq, nc):
    tiles = seq // (nc * SSM_Q)
    return pl.pallas_call(
        _ssm_kernel,
        grid=(N_SLABS, batch, tiles),
        in_specs=[pl.BlockSpec((nc, SLAB_IN), lambda s, b, t: (b * tiles + t, s)),
                  pl.BlockSpec((1,) + w1.shape[1:], lambda s, b, t: (s, 0, 0)),
                  pl.BlockSpec((1,) + w2.shape[1:], lambda s, b, t: (s, 0, 0)),
                  pl.BlockSpec((1,) + decay.shape[1:], lambda s, b, t: (s, 0, 0))],
        out_specs=pl.BlockSpec((nc * SSM_Q, LANES), lambda s, b, t: (b * tiles + t, s)),
        out_shape=jax.ShapeDtypeStruct((batch * seq, SSM_W), _F32),
        scratch_shapes=[pltpu.VMEM((SUBLANES, SLAB_STATE), _F32),
                        pltpu.VMEM((SLAB_STATE // LANES, nc, LANES), _F32)],
        compiler_params=pltpu.CompilerParams(
            dimension_semantics=("arbitrary", "arbitrary", "arbitrary"),
            vmem_limit_bytes=VMEM_LIMIT),
        name="s5_mixer",
    )(ucat, w1, w2, decay)


def _gelu_tanh(y):
    c = math.sqrt(2.0 / math.pi)
    return (0.5 * y) * (1.0 + jnp.tanh(y * (c + (c * 0.044715) * (y * y))))


def _mix_out_kernel(sinks_ref, x_ref, q_ref, kp_ref, kc_ref, vp_ref, vc_ref, za_ref, y_ref, zs_ref,
                    wg_ref, bg_ref, wo_ref, out_ref, acc_ref, *, tiles_per_seq):
    scores, vts = _attn_scores(q_ref, kp_ref, kc_ref, vp_ref, vc_ref)

    y = _gelu_tanh(y_ref[...].astype(_F32))
    gate = jnp.dot(y.astype(_BF16), wg_ref[...], preferred_element_type=_F32) + bg_ref[...]
    o = y * jax.nn.sigmoid(gate) * zs_ref[...].astype(_F32)
    rs = lax.rsqrt(jnp.mean(o * o, axis=-1, keepdims=True) + NORM_EPS)
    out_ref[...] = x_ref[...] + rs * jnp.dot(o.astype(_BF16), wo_ref[ATTN_W:, :], preferred_element_type=_F32)

    seq_start = pl.program_id(0) % tiles_per_seq == 0
    _attn_softmax_pv(sinks_ref, scores, vts, acc_ref, seq_start, range(acc_ref.shape[0] // BLOCK))
    g = acc_ref[...] * za_ref[...].astype(_F32)
    rs = lax.rsqrt(jnp.mean(g * g, axis=-1, keepdims=True) + NORM_EPS)
    out_ref[...] += rs * jnp.dot(g.astype(_BF16), wo_ref[:ATTN_W, :], preferred_element_type=_F32)


def _mix_out(sinks, x2, q, k, v, za, y, zs, wg, bg, wo, seq, tm):
    t = x2.shape[0]
    blocks = tm // BLOCK
    row = lambda w: pl.BlockSpec((tm, w), lambda i: (i, 0))
    prev = lambda w: pl.BlockSpec((BLOCK, w), lambda i: (jnp.maximum(i * blocks - 1, 0), 0))
    return pl.pallas_call(
        functools.partial(_mix_out_kernel, tiles_per_seq=seq // tm),
        grid=(t // tm,),
        in_specs=[pl.BlockSpec(memory_space=pltpu.SMEM), row(D_MODEL), row(ATTN_W), prev(KV_W), row(KV_W),
                  prev(KV_W), row(KV_W), row(ATTN_W), row(SSM_W), row(SSM_W),
                  _resident(wg.shape), _resident(bg.shape), _resident(wo.shape)],
        out_specs=row(D_MODEL),
        out_shape=jax.ShapeDtypeStruct((t, D_MODEL), x2.dtype),
        scratch_shapes=[pltpu.VMEM((tm, ATTN_W), _F32)],
        compiler_params=pltpu.CompilerParams(dimension_semantics=("arbitrary",),
                                             vmem_limit_bytes=VMEM_LIMIT),
        name="mix_out",
    )(sinks, x2, q, k, k, v, v, za, y, zs, wg, bg, wo)


def _pick_tile(n, target, multiple):
    best = multiple
    for cand in range(multiple, min(n, target) + 1, multiple):
        if n % cand == 0:
            best = cand
    return best


def kernel(x, positions, norm_w, w_in, q_norm_w, k_norm_w, sinks, a_re, a_im, log_step,
           b_re, b_im, c_re, c_im, d_skip, w_glu, b_glu, attn_out_norm_w, ssm_out_norm_w, w_out):
    batch, seq, _ = x.shape
    assert seq % BLOCK == 0 and x.shape[2] == D_MODEL and w_in.shape == (D_MODEL, IN_W)
    t = batch * seq
    f32 = _F32

    w_in_k = (norm_w.astype(f32)[:, None] * w_in.astype(f32)).astype(_BF16)
    out_gain = jnp.concatenate([attn_out_norm_w.astype(f32), ssm_out_norm_w.astype(f32)])
    w_out_k = (out_gain[:, None] * w_out.astype(f32)).astype(_BF16)
    log2e = math.log2(math.e)
    qw = (jnp.tile(q_norm_w.astype(f32), N_HEADS) * (log2e / math.sqrt(HEAD_DIM)))[None, :]
    kw = jnp.tile(k_norm_w.astype(f32), N_KV_HEADS)[None, :]
    inv_freq = ROPE_THETA ** (-jnp.arange(0, HEAD_DIM, 2, dtype=f32) / HEAD_DIM)
    invf = jnp.tile(inv_freq, LANES // (HEAD_DIM // 2))[None, :]
    sgn = jnp.tile(jnp.concatenate([-jnp.ones(HEAD_DIM // 2, f32), jnp.ones(HEAD_DIM // 2, f32)]),
                   LANES // HEAD_DIM)[None, :]

    x2 = x.reshape(t, D_MODEL)
    pos2 = positions.reshape(t // LANES, LANES).astype(jnp.int32)

    tm = _pick_tile(seq, 512, BLOCK)
    q, k, v, za, ucat, zs = _in_proj(x2, pos2, w_in_k, qw, kw, invf, sgn, tm)


    nc = _pick_tile(seq // SSM_Q, 512, 16)
    assert nc & (nc - 1) == 0, "chunk rows per tile must be a power of two for the scan"
    cs, cl, skip, decay = _ssm_weights(a_re, a_im, log_step, b_re, b_im, c_re, c_im, d_skip,
                                     n_scan=int(math.log2(nc)))
    w1, w2 = _expand(cs, cl, skip)
    y = _ssm(ucat, w1, w2, decay, batch, seq, nc)

    out = _mix_out(sinks.astype(f32) * log2e, x2, q, k, v, za, y, zs, w_glu.astype(_BF16),
                   b_glu.astype(f32)[None, :], w_out_k, seq, tm)
    return out.reshape(x.shape)
```

```python
import functools
import math

import jax
import jax.numpy as jnp
import numpy as np
from jax import lax
from jax.experimental import pallas as pl
from jax.experimental.pallas import tpu as pltpu

D_MODEL = 2048
ATTN_W = 1024
KV_W = 256
HEAD_DIM = 64
N_HEADS = 16
N_KV_HEADS = 4
KV_REP = N_HEADS // N_KV_HEADS
SSM_W = 1024
SSM_H = 16
SSM_G = 64
SSM_P = 64
BLOCK = 128
ROPE_THETA = 10000.0
NORM_EPS = 1e-6
IN_W = 2 * ATTN_W + 2 * KV_W + 2 * SSM_W

LANES = 128
SUBLANES = 8
MXU_DIM = 256
SSM_Q = 8
N_SLABS = SSM_W // LANES
HALF = LANES // 2
HALF_GROUPS = HALF // SSM_H
N_HALVES = SSM_W // HALF
HALF_IN = SSM_Q * HALF
HALF_STATE = 2 * HALF_GROUPS * SSM_P
SLAB_IN = 2 * HALF_IN
SLAB_STATE = 2 * HALF_STATE
SCAN_STEPS = 3
TABLE_POWERS = SCAN_STEPS * SUBLANES
TABLE_BLOCKS = TABLE_POWERS + SUBLANES
VMEM_LIMIT = 56 * 1024 * 1024

_BF16 = jnp.bfloat16
_F32 = jnp.float32


def _resident(shape):
    nd = len(shape)
    return pl.BlockSpec(shape, lambda *_: (0,) * nd, pipeline_mode=pl.Buffered(1))


def _swap_halves(a):
    return pltpu.roll(a, HEAD_DIM, axis=1)


def _pair_slabs(n0, n1, n2, n3):
    low = lax.broadcasted_iota(jnp.int32, n0.shape, 1) < HEAD_DIM
    return (jnp.where(low, n0, _swap_halves(n2)), jnp.where(low, _swap_halves(n0), n2),
            jnp.where(low, n1, _swap_halves(n3)), jnp.where(low, _swap_halves(n1), n3))


def _rope(t, cos, sin_signed):
    lane = lax.broadcasted_iota(jnp.int32, t.shape, 1)
    first_half = (lane % HEAD_DIM) < (HEAD_DIM // 2)
    swapped = jnp.where(first_half,
                        pltpu.roll(t, LANES - HEAD_DIM // 2, axis=1),
                        pltpu.roll(t, HEAD_DIM // 2, axis=1))
    return t * cos + swapped * sin_signed


def _in_proj_kernel(x_ref, pos_ref, w_ref, qw_ref, kw_ref, invf_ref, sgn_ref,
                    q_ref, k_ref, v_ref, za_ref, ucat_ref, zs_ref, hn_ref, u_ref):
    tm = x_ref.shape[0]
    x = x_ref[...]
    hn_ref[...] = x.astype(_BF16)
    rs = jnp.broadcast_to(lax.rsqrt(jnp.mean(x * x, axis=-1, keepdims=True) + NORM_EPS), (tm, LANES))
    wide = 2 * MXU_DIM
    rs_wide = jnp.concatenate([rs] * (wide // LANES), axis=1)

    nsub = tm // LANES
    ang = jnp.concatenate(
        [jnp.broadcast_to(pos_ref[pl.ds(pl.program_id(0) * nsub + j, 1), :].astype(_F32),
                          (LANES, LANES)).T for j in range(nsub)], axis=0) * invf_ref[...]
    cos = jnp.cos(ang)
    sin_signed = jnp.sin(ang) * sgn_ref[...]

    def proj(c0):
        return jnp.dot(hn_ref[...], w_ref[:, c0:c0 + wide], preferred_element_type=_F32) * rs_wide

    def normed_rope(acc, nw):
        low = lax.broadcasted_iota(jnp.int32, (tm, LANES), 1) < HEAD_DIM
        out = []
        for s in range(MXU_DIM // LANES):
            a = acc[:, s * LANES:(s + 1) * LANES]
            sq = a * a
            ssq = jnp.where(low, jnp.sum(jnp.where(low, sq, 0.0), axis=-1, keepdims=True),
                            jnp.sum(jnp.where(low, 0.0, sq), axis=-1, keepdims=True))
            t = a * lax.rsqrt(ssq * (1.0 / HEAD_DIM) + NORM_EPS) * nw[:, s * LANES:(s + 1) * LANES]
            out.append(_rope(t, cos, sin_signed))
        return out

    base = 2 * ATTN_W + 2 * KV_W
    nc = tm // SSM_Q
    for c in range(SSM_W // wide):
        acc = proj(base + c * wide)
        for s in range(wide // LANES):
            u_ref[c * (wide // LANES) + s] = acc[:, s * LANES:(s + 1) * LANES]
    low_half = lax.broadcasted_iota(jnp.int32, (nc, LANES), 1) < HALF
    for s in range(N_SLABS):
        for b in range(SSM_Q // 2):
            even = u_ref[s, pl.ds(2 * b, nc, stride=SSM_Q), :]
            odd = u_ref[s, pl.ds(2 * b + 1, nc, stride=SSM_Q), :]
            c0 = s * SLAB_IN + b * LANES
            ucat_ref[:, c0:c0 + LANES] = jnp.where(low_half, even, _swap_halves(odd)).astype(_BF16)
            c0 += HALF_IN
            ucat_ref[:, c0:c0 + LANES] = jnp.where(low_half, _swap_halves(even), odd).astype(_BF16)

    for c in range(ATTN_W // wide):
        acc = proj(c * wide)
        slabs = []
        for h in range(wide // MXU_DIM):
            n0 = c * wide + h * MXU_DIM
            slabs += normed_rope(acc[:, h * MXU_DIM:(h + 1) * MXU_DIM], qw_ref[:, n0:n0 + MXU_DIM])
        for r, slab in enumerate(_pair_slabs(*slabs)):
            q_ref[:, c * wide + r * LANES:c * wide + (r + 1) * LANES] = slab.astype(_BF16)
    acc = proj(ATTN_W)
    for s, slab in enumerate(normed_rope(acc[:, :KV_W], kw_ref[...])):
        k_ref[:, s * LANES:(s + 1) * LANES] = slab.astype(_BF16)
    v_ref[...] = acc[:, KV_W:].astype(_BF16)
    for out_ref, base in ((za_ref, ATTN_W + 2 * KV_W), (zs_ref, 2 * ATTN_W + 2 * KV_W + SSM_W)):
        for c in range(ATTN_W // wide):
            z = proj(base + c * wide)
            out_ref[:, c * wide:(c + 1) * wide] = (z * jax.nn.sigmoid(z)).astype(_BF16)


def _in_proj(x2, pos2, w_in, qw, kw, invf, sgn, tm):
    t = x2.shape[0]
    row = lambda w: pl.BlockSpec((tm, w), lambda i: (i, 0))
    return pl.pallas_call(
        _in_proj_kernel,
        grid=(t // tm,),
        in_specs=[row(D_MODEL), _resident(pos2.shape), _resident(w_in.shape), _resident(qw.shape),
                  _resident(kw.shape), _resident(invf.shape), _resident(sgn.shape)],
        out_specs=[row(ATTN_W), row(KV_W), row(KV_W), row(ATTN_W),
                   pl.BlockSpec((tm // SSM_Q, N_SLABS * SLAB_IN), lambda i: (i, 0)), row(SSM_W)],
        out_shape=[jax.ShapeDtypeStruct((t, ATTN_W), _BF16),
                   jax.ShapeDtypeStruct((t, KV_W), _BF16),
                   jax.ShapeDtypeStruct((t, KV_W), _BF16),
                   jax.ShapeDtypeStruct((t, ATTN_W), _BF16),
                   jax.ShapeDtypeStruct((t // SSM_Q, N_SLABS * SLAB_IN), _BF16),
                   jax.ShapeDtypeStruct((t, SSM_W), _BF16)],
        scratch_shapes=[pltpu.VMEM((tm, D_MODEL), _BF16), pltpu.VMEM((N_SLABS, tm, LANES), _F32)],
        compiler_params=pltpu.CompilerParams(dimension_semantics=("arbitrary",),
                                             vmem_limit_bytes=VMEM_LIMIT),
        name="in_proj",
    )(x2, pos2, w_in, qw, kw, invf, sgn)


def _attn_scores(q_ref, kp_ref, kc_ref, vp_ref, vc_ref):
    blocks = q_ref.shape[0] // BLOCK
    two = 2 * BLOCK
    lane_kv = lax.broadcasted_iota(jnp.int32, (two, LANES), 1)
    halves = [lane_kv < HEAD_DIM, lane_kv >= HEAD_DIM]
    zero = jnp.zeros((two, LANES), _BF16)
    key_head = lax.broadcasted_iota(jnp.int32, (2 * SUBLANES, 2 * two), 1) // two
    count_rows = (lax.broadcasted_iota(jnp.int32, (2 * SUBLANES, 2 * two), 0) == key_head).astype(_BF16)

    def keys(prev_ref, cur_ref, b, sl):
        if b == 0:
            return jnp.concatenate([prev_ref[:, sl], cur_ref[0:BLOCK, sl]], axis=0)
        return cur_ref[(b - 1) * BLOCK:(b + 1) * BLOCK, sl]

    scores, vts = {}, {}
    for b in range(blocks):
        rows = slice(b * BLOCK, (b + 1) * BLOCK)
        for j in range(N_KV_HEADS // 2):
            sl = slice(j * LANES, (j + 1) * LANES)
            k2 = keys(kp_ref, kc_ref, b, sl)
            v2 = keys(vp_ref, vc_ref, b, sl)
            vts[b, j] = jnp.concatenate(
                [jnp.concatenate([jnp.where(keep, v2, zero) for keep in halves], axis=0).T,
                 count_rows], axis=0)
            for t in range(KV_REP // 2):
                r0 = KV_REP * j + 2 * t
                qq = q_ref[rows, r0 * LANES:(r0 + 2) * LANES]
                qq = jnp.concatenate([qq[:, :LANES], qq[:, LANES:]], axis=0)
                for half, keep in enumerate(halves):
                    scores[b, j, t, half] = lax.dot_general(
                        jnp.where(keep, k2, zero), qq, (((1,), (1,)), ((), ())),
                        preferred_element_type=_F32)
    return scores, vts


def _attn_softmax_pv(sinks_ref, scores, vts, acc_ref, seq_start, blocks):
    two = 2 * BLOCK
    ik = lax.broadcasted_iota(jnp.int32, (two, two), 0)
    iq = lax.broadcasted_iota(jnp.int32, (two, two), 1) % BLOCK
    rel = ik - iq
    band = (rel >= 1) & (rel <= BLOCK)
    band_first = band & ((ik >= BLOCK) | jnp.logical_not(seq_start))
    first_slab = lax.broadcasted_iota(jnp.int32, (1, two), 1) < BLOCK
    row_o = lax.broadcasted_iota(jnp.int32, (LANES, two), 0)
    for b in blocks:
        rows = slice(b * BLOCK, (b + 1) * BLOCK)
        allowed = band_first if b == 0 else band
        for j in range(N_KV_HEADS // 2):
            for t in range(KV_REP // 2):
                r0 = KV_REP * j + 2 * t
                probs, sink_terms = [], []
                for half in range(2):
                    h0 = 2 * KV_REP * j + KV_REP * half + 2 * t
                    sink = jnp.where(first_slab, sinks_ref[h0], sinks_ref[h0 + 1])
                    s = jnp.where(allowed, scores[b, j, t, half], -1e30)
                    m = jnp.maximum(jnp.max(s, axis=0, keepdims=True), sink)
                    probs.append(jnp.exp2(s - m).astype(_BF16))
                    sink_terms.append(jnp.exp2(sink - m))
                o_t = jnp.dot(vts[b, j], jnp.concatenate(probs, axis=0), preferred_element_type=_F32)
                rdens = [1.0 / (o_t[LANES + half:LANES + half + 1] + sink_terms[half]) for half in range(2)]
                o_t = o_t[:LANES] * jnp.where(row_o < HEAD_DIM, rdens[0], rdens[1])
                for half in range(2):
                    heads = slice(half * HEAD_DIM, (half + 1) * HEAD_DIM)
                    slab = KV_REP * j + (KV_REP // 2) * half + t
                    acc_ref[rows, slab * LANES:(slab + 1) * LANES] = jnp.concatenate(
                        [o_t[heads, :BLOCK], o_t[heads, BLOCK:]], axis=0).T


def _ssm_weights(a_re, a_im, log_step, b_re, b_im, c_re, c_im, d_skip, n_scan):
    f32 = _F32
    a_re, a_im = a_re.astype(f32), a_im.astype(f32)
    delta = jnp.exp(log_step.astype(f32))[:, None]
    zr, zi = a_re * delta, a_im * delta

    def lam_pow(exponents):
        m = jnp.asarray(np.asarray(exponents, np.float32))[:, None, None]
        mag = jnp.exp(m * zr)
        return mag * jnp.cos(m * zi), mag * jnp.sin(m * zi)

    pw_r, pw_i = lam_pow(range(SSM_Q + 1))
    rev_r, rev_i = lam_pow(range(SSM_Q - 1, -1, -1))
    lr, li = pw_r[1], pw_i[1]
    den = a_re * a_re + a_im * a_im
    fr = ((lr - 1.0) * a_re + li * a_im) / den
    fi = (li * a_re - (lr - 1.0) * a_im) / den
    b_re, b_im = b_re.astype(f32), b_im.astype(f32)
    bbr = fr[..., None] * b_re - fi[..., None] * b_im
    bbi = fr[..., None] * b_im + fi[..., None] * b_re
    c_re, c_im = c_re.astype(f32), c_im.astype(f32)

    cl_r = c_re[None] * pw_r[:, :, None, :] - c_im[None] * pw_i[:, :, None, :]
    cl_i = c_re[None] * pw_i[:, :, None, :] + c_im[None] * pw_r[:, :, None, :]

    def per_slab(a):
        a = a.reshape(a.shape[0], N_HALVES, HALF, a.shape[-1])
        return a.transpose(1, 0, 2, 3)

    sin_r = rev_r[..., None] * bbr[None] - rev_i[..., None] * bbi[None]
    sin_i = rev_r[..., None] * bbi[None] + rev_i[..., None] * bbr[None]
    cs = per_slab(jnp.concatenate([sin_r, sin_i], axis=2).transpose(0, 1, 3, 2))

    cl = per_slab(jnp.concatenate([cl_r, -cl_i], axis=3))
    skip = d_skip.astype(f32).reshape(N_HALVES, 1, HALF)

    exps, keep = [], []
    for k in range(SCAN_STEPS):
        exps += [SSM_Q << k] * SUBLANES
        keep += [0.0] * (1 << k) + [1.0] * (SUBLANES - (1 << k))
    exps += [SSM_Q * j for j in range(SUBLANES)] + [SSM_Q << k for k in range(SCAN_STEPS, n_scan)]
    keep += [1.0] * (SUBLANES + n_scan - SCAN_STEPS)
    tr, ti = lam_pow(exps)
    keep = jnp.asarray(np.asarray(keep, np.float32))[:, None, None]
    rows = jnp.stack([(tr * keep).reshape(-1, N_HALVES, HALF_GROUPS // 2, 2 * SSM_P),
                      (ti * keep).reshape(-1, N_HALVES, HALF_GROUPS // 2, 2 * SSM_P)], axis=3)
    decay = rows.transpose(1, 0, 2, 3, 4).reshape(N_HALVES, len(exps), HALF_STATE)
    return cs, cl, skip, decay


def _expand_kernel(cs_ref, cl_ref, skip_ref, w1_ref, w2_ref):
    r = lax.broadcasted_iota(jnp.int32, (LANES, LANES), 0)
    c = lax.broadcasted_iota(jnp.int32, (LANES, LANES), 1)
    same_group = ((r % HALF) // SSM_H) == ((c % HALF) // SSM_H)
    rs = lax.broadcasted_iota(jnp.int32, (LANES, HALF_STATE), 0)
    cst = lax.broadcasted_iota(jnp.int32, (LANES, HALF_STATE), 1)
    rep_s = ((rs // SSM_P == (cst // LANES) % 2) & (rs % SSM_P == cst % SSM_P)).astype(_BF16)
    diag_s = ((rs % HALF) // SSM_H) == (2 * (cst // (2 * LANES)) + (cst % LANES) // SSM_P)

    none = jnp.zeros((HALF, LANES), _F32)
    b_bar = cs_ref[0, SSM_Q - 1]
    lhs = jnp.concatenate([jnp.concatenate([b_bar, none], axis=1),
                           jnp.concatenate([none, b_bar], axis=1)], axis=0)
    c_lam = lambda d: cl_ref[0, d] if d >= 0 else none
    skip = jnp.concatenate([skip_ref[0], skip_ref[0]], axis=1)
    tiles = []
    for delta in range(SSM_Q // 2):
        rhs = jnp.concatenate(
            [jnp.concatenate([c_lam(2 * delta), c_lam(2 * delta - 1)], axis=1),
             jnp.concatenate([c_lam(2 * delta + 1), c_lam(2 * delta)], axis=1)], axis=0)
        k = lax.dot_general(lhs, rhs, (((1,), (1,)), ((), ())),
                            precision=lax.Precision.HIGHEST, preferred_element_type=_F32)
        if delta == 0:
            k = k + jnp.where(r == c, skip, 0.0)
        tiles.append(jnp.where(same_group, k, 0.0).astype(_BF16))
    zero = jnp.zeros((LANES, LANES), _BF16)
    two_offsets = lambda ref, i: jnp.concatenate([ref[0, i], ref[0, i + 1]], axis=0).astype(_BF16)
    for b in range(SSM_Q // 2):
        rows = slice(b * LANES, (b + 1) * LANES)
        for b2 in range(SSM_Q // 2):
            w1_ref[0, rows, b2 * LANES:(b2 + 1) * LANES] = tiles[b2 - b] if b2 >= b else zero
        w_in = jnp.where(diag_s, jnp.dot(two_offsets(cs_ref, 2 * b), rep_s, preferred_element_type=_F32), 0.0)
        w1_ref[0, rows, HALF_IN:] = w_in.astype(_BF16)
        w_out_t = jnp.where(diag_s, jnp.dot(two_offsets(cl_ref, 2 * b + 1), rep_s,
                                            preferred_element_type=_F32), 0.0)
        w2_ref[0, rows, :] = w_out_t.astype(_BF16)


def _expand(cs, cl, skip):
    blk = lambda a: pl.BlockSpec((1,) + a.shape[1:], lambda s: (s,) + (0,) * (a.ndim - 1))
    return pl.pallas_call(
        _expand_kernel,
        grid=(N_HALVES,),
        in_specs=[blk(cs), blk(cl), blk(skip)],
        out_specs=[pl.BlockSpec((1, HALF_IN, HALF_IN + HALF_STATE), lambda s: (s, 0, 0)),
                   pl.BlockSpec((1, HALF_IN, HALF_STATE), lambda s: (s, 0, 0))],
        out_shape=[jax.ShapeDtypeStruct((N_HALVES, HALF_IN, HALF_IN + HALF_STATE), _BF16),
                   jax.ShapeDtypeStruct((N_HALVES, HALF_IN, HALF_STATE), _BF16)],
        compiler_params=pltpu.CompilerParams(dimension_semantics=("arbitrary",)),
        name="s5_expand",
    )(cs, cl, skip)


def _ssm_kernel(x_ref, w1_ref, w2_ref, dec_ref, y_ref, h_ref, t_scr):
    nc = x_ref.shape[0]
    nb = nc // SUBLANES
    piece = 2 * LANES

    def cmul_add(acc_r, acc_i, pr, pi, tr, ti):
        return acc_r + pr * tr - pi * ti, acc_i + pr * ti + pi * tr

    def roll_in_block(a, shift):
        return pltpu.roll(a.reshape(nb, SUBLANES, LANES), shift, axis=1).reshape(nc, LANES)

    @pl.when(pl.program_id(2) == 0)
    def _():
        h_ref[...] = jnp.zeros_like(h_ref)

    xs = [x_ref[:, e * HALF_IN:(e + 1) * HALF_IN] for e in range(2)]
    s_all = [jnp.dot(xs[e], w1_ref[e, :, HALF_IN:], preferred_element_type=_F32) for e in range(2)]
    sub = lax.broadcasted_iota(jnp.int32, (nc, LANES), 0) % SUBLANES
    brow = lax.broadcasted_iota(jnp.int32, (nb, LANES), 0)
    ys = [None, None]
    for e in range(2):
        for p in range(HALF_STATE // piece):
            re, im = slice(p * piece, p * piece + LANES), slice(p * piece + LANES, (p + 1) * piece)
            table = lambda row0, n: (dec_ref[e, row0:row0 + n, re], dec_ref[e, row0:row0 + n, im])
            tall = lambda a: jnp.concatenate([a] * nb, axis=0)
            h_re = slice(e * HALF_STATE + p * piece, e * HALF_STATE + p * piece + LANES)
            h_im = slice(h_re.stop, h_re.stop + LANES)
            slot = 2 * (e * (HALF_STATE // piece) + p)

            sr, si = s_all[e][:, re], s_all[e][:, im]
            hin_r, hin_i = h_ref[0:1, h_re], h_ref[0:1, h_im]

            for k in range(SCAN_STEPS):
                mr, mi = table(k * SUBLANES, SUBLANES)
                sr, si = cmul_add(sr, si, tall(mr), tall(mi),
                                  roll_in_block(sr, 1 << k), roll_in_block(si, 1 << k))

            t_scr[slot] = sr
            t_scr[slot + 1] = si
            br = t_scr[slot, pl.ds(SUBLANES - 1, nb, stride=SUBLANES), :]
            bi = t_scr[slot + 1, pl.ds(SUBLANES - 1, nb, stride=SUBLANES), :]
            pr, pi = table(TABLE_BLOCKS, 1)
            br = br + jnp.where(brow == 0, pr * hin_r - pi * hin_i, 0.0)
            bi = bi + jnp.where(brow == 0, pr * hin_i + pi * hin_r, 0.0)
            for k in range(dec_ref.shape[1] - TABLE_BLOCKS):
                shift = 1 << k
                tr = jnp.where(brow >= shift, pltpu.roll(br, shift, axis=0), 0.0)
                ti = jnp.where(brow >= shift, pltpu.roll(bi, shift, axis=0), 0.0)
                br, bi = cmul_add(br, bi, *table(TABLE_BLOCKS + k, 1), tr, ti)
            h_ref[0:1, h_re] = br[nb - 1:nb]
            h_ref[0:1, h_im] = bi[nb - 1:nb]
            bpr = jnp.where(brow == 0, hin_r, pltpu.roll(br, 1, axis=0))
            bpi = jnp.where(brow == 0, hin_i, pltpu.roll(bi, 1, axis=0))

            rep = lambda a: jnp.concatenate(
                [jnp.broadcast_to(a[k:k + 1], (SUBLANES, LANES)) for k in range(nb)], axis=0)
            fr, fi = table(TABLE_POWERS, SUBLANES)
            tr = jnp.where(sub >= 1, roll_in_block(sr, 1), 0.0)
            ti = jnp.where(sub >= 1, roll_in_block(si, 1), 0.0)
            prev_r, prev_i = cmul_add(tr, ti, tall(fr), tall(fi), rep(bpr), rep(bpi))
            hprev = jnp.concatenate([prev_r, prev_i], axis=1).astype(_BF16)
            part = lax.dot_general(hprev, w2_ref[e, :, p * piece:(p + 1) * piece], (((1,), (1,)), ((), ())),
                                   preferred_element_type=_F32)
            if ys[e] is None:
                mid = HALF_IN // 2
                ys[e] = jnp.concatenate(
                    [jnp.dot(xs[e][:, :mid], w1_ref[e, :mid, :mid], preferred_element_type=_F32),
                     jnp.dot(xs[e], w1_ref[e, :, mid:HALF_IN], preferred_element_type=_F32)], axis=1) + part
            else:
                ys[e] = ys[e] + part

    low = lax.broadcasted_iota(jnp.int32, (nc, LANES), 1) < HALF
    for b in range(SSM_Q // 2):
        y0, y1 = (ys[e][:, b * LANES:(b + 1) * LANES] for e in range(2))
        y_ref[pl.ds(2 * b, nc, stride=SSM_Q), :] = jnp.where(low, y0, _swap_halves(y1))
        y_ref[pl.ds(2 * b + 1, nc, stride=SSM_Q), :] = jnp.where(low, _swap_halves(y0), y1)


def _ssm(ucat, w1, w2, decay, batch, seq, nc):
    tiles = seq // (nc * SSM_Q)
    pair = lambda a: pl.BlockSpec((2,) + a.shape[1:], lambda s, b, t: (s, 0, 0))
    return pl.pallas_call(
        _ssm_kernel,
        grid=(N_SLABS, batch, tiles),
        in_specs=[pl.BlockSpec((nc, SLAB_IN), lambda s, b, t: (b * tiles + t, s)),
                  pair(w1), pair(w2), pair(decay)],
        out_specs=pl.BlockSpec((nc * SSM_Q, LANES), lambda s, b, t: (b * tiles + t, s)),
        out_shape=jax.ShapeDtypeStruct((batch * seq, SSM_W), _F32),
        scratch_shapes=[pltpu.VMEM((SUBLANES, SLAB_STATE), _F32),
                        pltpu.VMEM((SLAB_STATE // LANES, nc, LANES), _F32)],
        compiler_params=pltpu.CompilerParams(
            dimension_semantics=("arbitrary", "arbitrary", "arbitrary"),
            vmem_limit_bytes=VMEM_LIMIT),
        name="s5_mixer",
    )(ucat, w1, w2, decay)


def _gelu_tanh(y):
    c = math.sqrt(2.0 / math.pi)
    return (0.5 * y) * (1.0 + jnp.tanh(y * (c + (c * 0.044715) * (y * y))))


def _mix_out_kernel(sinks_ref, x_ref, q_ref, kp_ref, kc_ref, vp_ref, vc_ref, za_ref, y_ref, zs_ref,
                    wg_ref, bg_ref, wo_ref, out_ref, acc_ref, *, tiles_per_seq):
    scores, vts = _attn_scores(q_ref, kp_ref, kc_ref, vp_ref, vc_ref)

    y = _gelu_tanh(y_ref[...].astype(_F32))
    gate = jnp.dot(y.astype(_BF16), wg_ref[...], preferred_element_type=_F32) + bg_ref[...]
    o = y * jax.nn.sigmoid(gate) * zs_ref[...].astype(_F32)
    rs = lax.rsqrt(jnp.mean(o * o, axis=-1, keepdims=True) + NORM_EPS)
    out_ref[...] = x_ref[...] + rs * jnp.dot(o.astype(_BF16), wo_ref[ATTN_W:, :], preferred_element_type=_F32)

    seq_start = pl.program_id(0) % tiles_per_seq == 0
    _attn_softmax_pv(sinks_ref, scores, vts, acc_ref, seq_start, range(acc_ref.shape[0] // BLOCK))
    g = acc_ref[...] * za_ref[...].astype(_F32)
    rs = lax.rsqrt(jnp.mean(g * g, axis=-1, keepdims=True) + NORM_EPS)
    out_ref[...] += rs * jnp.dot(g.astype(_BF16), wo_ref[:ATTN_W, :], preferred_element_type=_F32)


def _mix_out(sinks, x2, q, k, v, za, y, zs, wg, bg, wo, seq, tm):
    t = x2.shape[0]
    blocks = tm // BLOCK
    row = lambda w: pl.BlockSpec((tm, w), lambda i: (i, 0))
    prev = lambda w: pl.BlockSpec((BLOCK, w), lambda i: (jnp.maximum(i * blocks - 1, 0), 0))
    return pl.pallas_call(
        functools.partial(_mix_out_kernel, tiles_per_seq=seq // tm),
        grid=(t // tm,),
        in_specs=[pl.BlockSpec(memory_space=pltpu.SMEM), row(D_MODEL), row(ATTN_W), prev(KV_W), row(KV_W),
                  prev(KV_W), row(KV_W), row(ATTN_W), row(SSM_W), row(SSM_W),
                  _resident(wg.shape), _resident(bg.shape), _resident(wo.shape)],
        out_specs=row(D_MODEL),
        out_shape=jax.ShapeDtypeStruct((t, D_MODEL), x2.dtype),
        scratch_shapes=[pltpu.VMEM((tm, ATTN_W), _F32)],
        compiler_params=pltpu.CompilerParams(dimension_semantics=("arbitrary",),
                                             vmem_limit_bytes=VMEM_LIMIT),
        name="mix_out",
    )(sinks, x2, q, k, k, v, v, za, y, zs, wg, bg, wo)


def _pick_tile(n, target, multiple):
    best = multiple
    for cand in range(multiple, min(n, target) + 1, multiple):
        if n % cand == 0:
            best = cand
    return best


def kernel(x, positions, norm_w, w_in, q_norm_w, k_norm_w, sinks, a_re, a_im, log_step,
           b_re, b_im, c_re, c_im, d_skip, w_glu, b_glu, attn_out_norm_w, ssm_out_norm_w, w_out):
    batch, seq, _ = x.shape
    assert seq % BLOCK == 0 and x.shape[2] == D_MODEL and w_in.shape == (D_MODEL, IN_W)
    t = batch * seq
    f32 = _F32

    w_in_k = (norm_w.astype(f32)[:, None] * w_in.astype(f32)).astype(_BF16)
    out_gain = jnp.concatenate([attn_out_norm_w.astype(f32), ssm_out_norm_w.astype(f32)])
    w_out_k = (out_gain[:, None] * w_out.astype(f32)).astype(_BF16)
    log2e = math.log2(math.e)
    qw = (jnp.tile(q_norm_w.astype(f32), N_HEADS) * (log2e / math.sqrt(HEAD_DIM)))[None, :]
    kw = jnp.tile(k_norm_w.astype(f32), N_KV_HEADS)[None, :]
    inv_freq = ROPE_THETA ** (-jnp.arange(0, HEAD_DIM, 2, dtype=f32) / HEAD_DIM)
    invf = jnp.tile(inv_freq, LANES // (HEAD_DIM // 2))[None, :]
    sgn = jnp.tile(jnp.concatenate([-jnp.ones(HEAD_DIM // 2, f32), jnp.ones(HEAD_DIM // 2, f32)]),
                   LANES // HEAD_DIM)[None, :]

    x2 = x.reshape(t, D_MODEL)
    pos2 = positions.reshape(t // LANES, LANES).astype(jnp.int32)

    tm = _pick_tile(seq, 512, BLOCK)
    q, k, v, za, ucat, zs = _in_proj(x2, pos2, w_in_k, qw, kw, invf, sgn, tm)


    nc = _pick_tile(seq // SSM_Q, 512, 16)
    assert nc & (nc - 1) == 0, "chunk rows per tile must be a power of two for the scan"
    cs, cl, skip, decay = _ssm_weights(a_re, a_im, log_step, b_re, b_im, c_re, c_im, d_skip,
                                     n_scan=int(math.log2(nc)))
    w1, w2 = _expand(cs, cl, skip)
    y = _ssm(ucat, w1, w2, decay, batch, seq, nc)

    out = _mix_out(sinks.astype(f32) * log2e, x2, q, k, v, za, y, zs, w_glu.astype(_BF16),
                   b_glu.astype(f32)[None, :], w_out_k, seq, tm)
    return out.reshape(x.shape)
```

```python
import functools
import math

import jax
import jax.numpy as jnp
import numpy as np
from jax import lax
from jax.experimental import pallas as pl
from jax.experimental.pallas import tpu as pltpu

D_MODEL = 2048
ATTN_W = 1024
KV_W = 256
HEAD_DIM = 64
N_HEADS = 16
N_KV_HEADS = 4
KV_REP = N_HEADS // N_KV_HEADS
SSM_W = 1024
SSM_H = 16
SSM_G = 64
SSM_P = 64
BLOCK = 128
ROPE_THETA = 10000.0
NORM_EPS = 1e-6
IN_W = 2 * ATTN_W + 2 * KV_W + 2 * SSM_W

LANES = 128
SUBLANES = 8
MXU_DIM = 256
SSM_Q = 8
N_SLABS = SSM_W // LANES
HALF = LANES // 2
HALF_GROUPS = HALF // SSM_H
N_HALVES = SSM_W // HALF
HALF_IN = SSM_Q * HALF
HALF_STATE = 2 * HALF_GROUPS * SSM_P
SLAB_IN = 2 * HALF_IN
SLAB_STATE = 2 * HALF_STATE
SCAN_STEPS = 3
TABLE_POWERS = SCAN_STEPS * SUBLANES
TABLE_BLOCKS = TABLE_POWERS + SUBLANES
VMEM_LIMIT = 56 * 1024 * 1024

_BF16 = jnp.bfloat16
_F32 = jnp.float32


def _resident(shape):
    nd = len(shape)
    return pl.BlockSpec(shape, lambda *_: (0,) * nd, pipeline_mode=pl.Buffered(1))


def _swap_halves(a):
    return pltpu.roll(a, HEAD_DIM, axis=1)


def _pair_slabs(n0, n1, n2, n3):
    low = lax.broadcasted_iota(jnp.int32, n0.shape, 1) < HEAD_DIM
    return (jnp.where(low, n0, _swap_halves(n2)), jnp.where(low, _swap_halves(n0), n2),
            jnp.where(low, n1, _swap_halves(n3)), jnp.where(low, _swap_halves(n1), n3))


def _rope(t, cos, sin_signed):
    lane = lax.broadcasted_iota(jnp.int32, t.shape, 1)
    first_half = (lane % HEAD_DIM) < (HEAD_DIM // 2)
    swapped = jnp.where(first_half,
                        pltpu.roll(t, LANES - HEAD_DIM // 2, axis=1),
                        pltpu.roll(t, HEAD_DIM // 2, axis=1))
    return t * cos + swapped * sin_signed


def _in_proj_kernel(x_ref, pos_ref, w_ref, qw_ref, kw_ref, invf_ref, sgn_ref,
                    q_ref, k_ref, v_ref, za_ref, ucat_ref, zs_ref, hn_ref, u_ref):
    tm = x_ref.shape[0]
    x = x_ref[...]
    hn_ref[...] = x.astype(_BF16)
    rs = jnp.broadcast_to(lax.rsqrt(jnp.mean(x * x, axis=-1, keepdims=True) + NORM_EPS), (tm, LANES))
    wide = 2 * MXU_DIM
    rs_wide = jnp.concatenate([rs] * (wide // LANES), axis=1)

    nsub = tm // LANES
    ang = jnp.concatenate(
        [jnp.broadcast_to(pos_ref[pl.ds(pl.program_id(0) * nsub + j, 1), :].astype(_F32),
                          (LANES, LANES)).T for j in range(nsub)], axis=0) * invf_ref[...]
    cos = jnp.cos(ang)
    sin_signed = jnp.sin(ang) * sgn_ref[...]

    def proj(c0):
        return jnp.dot(hn_ref[...], w_ref[:, c0:c0 + wide], preferred_element_type=_F32) * rs_wide

    def normed_rope(acc, nw):
        low = lax.broadcasted_iota(jnp.int32, (tm, LANES), 1) < HEAD_DIM
        out = []
        for s in range(MXU_DIM // LANES):
            a = acc[:, s * LANES:(s + 1) * LANES]
            sq = a * a
            ssq = jnp.where(low, jnp.sum(jnp.where(low, sq, 0.0), axis=-1, keepdims=True),
                            jnp.sum(jnp.where(low, 0.0, sq), axis=-1, keepdims=True))
            t = a * lax.rsqrt(ssq * (1.0 / HEAD_DIM) + NORM_EPS) * nw[:, s * LANES:(s + 1) * LANES]
            out.append(_rope(t, cos, sin_signed))
        return out

    base = 2 * ATTN_W + 2 * KV_W
    nc = tm // SSM_Q
    for c in range(SSM_W // wide):
        acc = proj(base + c * wide)
        for s in range(wide // LANES):
            u_ref[c * (wide // LANES) + s] = acc[:, s * LANES:(s + 1) * LANES]
    low_half = lax.broadcasted_iota(jnp.int32, (nc, LANES), 1) < HALF
    for s in range(N_SLABS):
        for b in range(SSM_Q // 2):
            even = u_ref[s, pl.ds(2 * b, nc, stride=SSM_Q), :]
            odd = u_ref[s, pl.ds(2 * b + 1, nc, stride=SSM_Q), :]
            c0 = s * SLAB_IN + b * LANES
            ucat_ref[:, c0:c0 + LANES] = jnp.where(low_half, even, _swap_halves(odd)).astype(_BF16)
            c0 += HALF_IN
            ucat_ref[:, c0:c0 + LANES] = jnp.where(low_half, _swap_halves(even), odd).astype(_BF16)

    for c in range(ATTN_W // wide):
        acc = proj(c * wide)
        slabs = []
        for h in range(wide // MXU_DIM):
            n0 = c * wide + h * MXU_DIM
            slabs += normed_rope(acc[:, h * MXU_DIM:(h + 1) * MXU_DIM], qw_ref[:, n0:n0 + MXU_DIM])
        for r, slab in enumerate(_pair_slabs(*slabs)):
            q_ref[:, c * wide + r * LANES:c * wide + (r + 1) * LANES] = slab.astype(_BF16)
    acc = proj(ATTN_W)
    for s, slab in enumerate(normed_rope(acc[:, :KV_W], kw_ref[...])):
        k_ref[:, s * LANES:(s + 1) * LANES] = slab.astype(_BF16)
    v_ref[...] = acc[:, KV_W:].astype(_BF16)
    for out_ref, base in ((za_ref, ATTN_W + 2 * KV_W), (zs_ref, 2 * ATTN_W + 2 * KV_W + SSM_W)):
        for c in range(ATTN_W // wide):
            z = proj(base + c * wide)
            out_ref[:, c * wide:(c + 1) * wide] = (z * jax.nn.sigmoid(z)).astype(_BF16)


def _in_proj(x2, pos2, w_in, qw, kw, invf, sgn, tm):
    t = x2.shape[0]
    row = lambda w: pl.BlockSpec((tm, w), lambda i: (i, 0))
    return pl.pallas_call(
        _in_proj_kernel,
        grid=(t // tm,),
        in_specs=[row(D_MODEL), _resident(pos2.shape), _resident(w_in.shape), _resident(qw.shape),
                  _resident(kw.shape), _resident(invf.shape), _resident(sgn.shape)],
        out_specs=[row(ATTN_W), row(KV_W), row(KV_W), row(ATTN_W),
                   pl.BlockSpec((tm // SSM_Q, N_SLABS * SLAB_IN), lambda i: (i, 0)), row(SSM_W)],
        out_shape=[jax.ShapeDtypeStruct((t, ATTN_W), _BF16),
                   jax.ShapeDtypeStruct((t, KV_W), _BF16),
                   jax.ShapeDtypeStruct((t, KV_W), _BF16),
                   jax.ShapeDtypeStruct((t, ATTN_W), _BF16),
                   jax.ShapeDtypeStruct((t // SSM_Q, N_SLABS * SLAB_IN), _BF16),
                   jax.ShapeDtypeStruct((t, SSM_W), _BF16)],
        scratch_shapes=[pltpu.VMEM((tm, D_MODEL), _BF16), pltpu.VMEM((N_SLABS, tm, LANES), _F32)],
        compiler_params=pltpu.CompilerParams(dimension_semantics=("arbitrary",),
                                             vmem_limit_bytes=VMEM_LIMIT),
        name="in_proj",
    )(x2, pos2, w_in, qw, kw, invf, sgn)


def _attn_scores(q_ref, kp_ref, kc_ref, vp_ref, vc_ref):
    blocks = q_ref.shape[0] // BLOCK
    two = 2 * BLOCK
    lane_kv = lax.broadcasted_iota(jnp.int32, (two, LANES), 1)
    halves = [lane_kv < HEAD_DIM, lane_kv >= HEAD_DIM]
    zero = jnp.zeros((two, LANES), _BF16)
    count_rows = (lax.broadcasted_iota(jnp.int32, (2 * SUBLANES, two), 0) == 0).astype(_BF16)

    def keys(prev_ref, cur_ref, b, sl):
        if b == 0:
            return jnp.concatenate([prev_ref[:, sl], cur_ref[0:BLOCK, sl]], axis=0)
        return cur_ref[(b - 1) * BLOCK:(b + 1) * BLOCK, sl]

    scores, vts = {}, {}
    for b in range(blocks):
        rows = slice(b * BLOCK, (b + 1) * BLOCK)
        for j in range(N_KV_HEADS // 2):
            sl = slice(j * LANES, (j + 1) * LANES)
            k2 = keys(kp_ref, kc_ref, b, sl)
            v2 = keys(vp_ref, vc_ref, b, sl)
            v_t = v2.T
            vts[b, j] = [jnp.concatenate([v_t[half * HEAD_DIM:(half + 1) * HEAD_DIM], count_rows], axis=0)
                         for half in range(2)]
            for t in range(KV_REP // 2):
                r0 = KV_REP * j + 2 * t
                qq = q_ref[rows, r0 * LANES:(r0 + 2) * LANES]
                qq = jnp.concatenate([qq[:, :LANES], qq[:, LANES:]], axis=0)
                for half, keep in enumerate(halves):
                    scores[b, j, t, half] = lax.dot_general(
                        jnp.where(keep, k2, zero), qq, (((1,), (1,)), ((), ())),
                        preferred_element_type=_F32)
    return scores, vts


def _attn_softmax_pv(sinks_ref, scores, vts, acc_ref, seq_start, blocks):
    two = 2 * BLOCK
    ik = lax.broadcasted_iota(jnp.int32, (two, two), 0)
    iq = lax.broadcasted_iota(jnp.int32, (two, two), 1) % BLOCK
    rel = ik - iq
    band = (rel >= 1) & (rel <= BLOCK)
    band_first = band & ((ik >= BLOCK) | jnp.logical_not(seq_start))
    first_slab = lax.broadcasted_iota(jnp.int32, (1, two), 1) < BLOCK
    for b in blocks:
        rows = slice(b * BLOCK, (b + 1) * BLOCK)
        allowed = band_first if b == 0 else band
        for j in range(N_KV_HEADS // 2):
            for t in range(KV_REP // 2):
                r0 = KV_REP * j + 2 * t
                probs, sink_terms = [], []
                for half in range(2):
                    h0 = 2 * KV_REP * j + KV_REP * half + 2 * t
                    sink = jnp.where(first_slab, sinks_ref[h0], sinks_ref[h0 + 1])
                    s = jnp.where(allowed, scores[b, j, t, half], -1e30)
                    m = jnp.maximum(jnp.max(s, axis=0, keepdims=True), sink)
                    probs.append(jnp.exp2(s - m).astype(_BF16))
                    sink_terms.append(jnp.exp2(sink - m))
                outs = []
                for half in range(2):
                    o_h = jnp.dot(vts[b, j][half], probs[half], preferred_element_type=_F32)
                    outs.append(o_h[:HEAD_DIM] * (1.0 / (o_h[HEAD_DIM:HEAD_DIM + 1] + sink_terms[half])))
                o_t = jnp.concatenate(outs, axis=0)
                for half in range(2):
                    heads = slice(half * HEAD_DIM, (half + 1) * HEAD_DIM)
                    slab = KV_REP * j + (KV_REP // 2) * half + t
                    acc_ref[rows, slab * LANES:(slab + 1) * LANES] = jnp.concatenate(
                        [o_t[heads, :BLOCK], o_t[heads, BLOCK:]], axis=0).T


def _ssm_weights(a_re, a_im, log_step, b_re, b_im, c_re, c_im, d_skip, n_scan):
    f32 = _F32
    grouped = lambda a: a.astype(f32).reshape((N_HALVES, HALF_GROUPS) + a.shape[1:])
    a_re, a_im = grouped(a_re), grouped(a_im)
    delta = jnp.exp(grouped(log_step))[..., None]
    zr, zi = a_re * delta, a_im * delta

    def lam_pow(exponents):
        m = jnp.asarray(np.asarray(exponents, np.float32))[None, :, None, None]
        mag = jnp.exp(m * zr[:, None])
        return mag * jnp.cos(m * zi[:, None]), mag * jnp.sin(m * zi[:, None])

    pw_r, pw_i = lam_pow(range(SSM_Q + 1))
    rev_r, rev_i = lam_pow(range(SSM_Q - 1, -1, -1))
    lr, li = pw_r[:, 1], pw_i[:, 1]
    den = a_re * a_re + a_im * a_im
    fr = ((lr - 1.0) * a_re + li * a_im) / den
    fi = (li * a_re - (lr - 1.0) * a_im) / den
    b_re, b_im = grouped(b_re), grouped(b_im)
    bbr = fr[..., None] * b_re - fi[..., None] * b_im
    bbi = fr[..., None] * b_im + fi[..., None] * b_re
    c_re, c_im = grouped(c_re), grouped(c_im)

    def blocks(re, im):
        a = jnp.concatenate([re, im], axis=-1)
        return a.reshape(a.shape[:2] + (HALF, 2 * SSM_P))

    rev_r, rev_i = rev_r[:, :, :, None, :], rev_i[:, :, :, None, :]
    bt_r, bt_i = jnp.swapaxes(bbr, 2, 3)[:, None], jnp.swapaxes(bbi, 2, 3)[:, None]
    cs = blocks(rev_r * bt_r - rev_i * bt_i, rev_r * bt_i + rev_i * bt_r)

    pw_r, pw_i = pw_r[:, :, :, None, :], pw_i[:, :, :, None, :]
    cl = blocks(c_re[:, None] * pw_r - c_im[:, None] * pw_i, -(c_re[:, None] * pw_i + c_im[:, None] * pw_r))
    skip = d_skip.astype(f32).reshape(N_HALVES, 1, HALF)

    exps, keep = [], []
    for k in range(SCAN_STEPS):
        exps += [SSM_Q << k] * SUBLANES
        keep += [0.0] * (1 << k) + [1.0] * (SUBLANES - (1 << k))
    exps += [SSM_Q * j for j in range(SUBLANES)] + [SSM_Q << k for k in range(SCAN_STEPS, n_scan)]
    keep += [1.0] * (SUBLANES + n_scan - SCAN_STEPS)
    tr, ti = lam_pow(exps)
    keep = jnp.asarray(np.asarray(keep, np.float32))[None, :, None, None]
    pairs = lambda a: (a * keep).reshape(N_HALVES, len(exps), HALF_GROUPS // 2, 2 * SSM_P)
    decay = jnp.stack([pairs(tr), pairs(ti)], axis=3).reshape(N_HALVES, len(exps), HALF_STATE)
    return cs, cl, skip, decay


def _expand_kernel(cs_ref, cl_ref, skip_ref, w1_ref, w2_ref):
    r = lax.broadcasted_iota(jnp.int32, (LANES, LANES), 0)
    c = lax.broadcasted_iota(jnp.int32, (LANES, LANES), 1)
    same_group = ((r % HALF) // SSM_H) == ((c % HALF) // SSM_H)
    rs = lax.broadcasted_iota(jnp.int32, (LANES, HALF_STATE), 0)
    cst = lax.broadcasted_iota(jnp.int32, (LANES, HALF_STATE), 1)
    rep_s = ((rs // SSM_P == (cst // LANES) % 2) & (rs % SSM_P == cst % SSM_P)).astype(_BF16)
    diag_s = ((rs % HALF) // SSM_H) == (2 * (cst // (2 * LANES)) + (cst % LANES) // SSM_P)

    none = jnp.zeros((HALF, LANES), _F32)
    b_bar = cs_ref[0, SSM_Q - 1]
    lhs = jnp.concatenate([jnp.concatenate([b_bar, none], axis=1),
                           jnp.concatenate([none, b_bar], axis=1)], axis=0)
    c_lam = lambda d: cl_ref[0, d] if d >= 0 else none
    skip = jnp.concatenate([skip_ref[0], skip_ref[0]], axis=1)
    tiles = []
    for delta in range(SSM_Q // 2):
        rhs = jnp.concatenate(
            [jnp.concatenate([c_lam(2 * delta), c_lam(2 * delta - 1)], axis=1),
             jnp.concatenate([c_lam(2 * delta + 1), c_lam(2 * delta)], axis=1)], axis=0)
        k = lax.dot_general(lhs, rhs, (((1,), (1,)), ((), ())),
                            precision=lax.Precision.HIGHEST, preferred_element_type=_F32)
        if delta == 0:
            k = k + jnp.where(r == c, skip, 0.0)
        tiles.append(jnp.where(same_group, k, 0.0).astype(_BF16))
    zero = jnp.zeros((LANES, LANES), _BF16)
    two_offsets = lambda ref, i: jnp.concatenate([ref[0, i], ref[0, i + 1]], axis=0).astype(_BF16)
    for b in range(SSM_Q // 2):
        rows = slice(b * LANES, (b + 1) * LANES)
        for b2 in range(SSM_Q // 2):
            w1_ref[0, rows, b2 * LANES:(b2 + 1) * LANES] = tiles[b2 - b] if b2 >= b else zero
        w_in = jnp.where(diag_s, jnp.dot(two_offsets(cs_ref, 2 * b), rep_s, preferred_element_type=_F32), 0.0)
        w1_ref[0, rows, HALF_IN:] = w_in.astype(_BF16)
        w_out_t = jnp.where(diag_s, jnp.dot(two_offsets(cl_ref, 2 * b + 1), rep_s,
                                            preferred_element_type=_F32), 0.0)
        w2_ref[0, rows, :] = w_out_t.astype(_BF16)


def _expand(cs, cl, skip):
    blk = lambda a: pl.BlockSpec((1,) + a.shape[1:], lambda s: (s,) + (0,) * (a.ndim - 1))
    return pl.pallas_call(
        _expand_kernel,
        grid=(N_HALVES,),
        in_specs=[blk(cs), blk(cl), blk(skip)],
        out_specs=[pl.BlockSpec((1, HALF_IN, HALF_IN + HALF_STATE), lambda s: (s, 0, 0)),
                   pl.BlockSpec((1, HALF_IN, HALF_STATE), lambda s: (s, 0, 0))],
        out_shape=[jax.ShapeDtypeStruct((N_HALVES, HALF_IN, HALF_IN + HALF_STATE), _BF16),
                   jax.ShapeDtypeStruct((N_HALVES, HALF_IN, HALF_STATE), _BF16)],
        compiler_params=pltpu.CompilerParams(dimension_semantics=("arbitrary",)),
        name="s5_expand",
    )(cs, cl, skip)


def _ssm_kernel(x_ref, w1_ref, w2_ref, dec_ref, y_ref, h_ref, t_scr):
    nc = x_ref.shape[0]
    nb = nc // SUBLANES
    piece = 2 * LANES

    def cmul_add(acc_r, acc_i, pr, pi, tr, ti):
        return acc_r + pr * tr - pi * ti, acc_i + pr * ti + pi * tr

    def roll_in_block(a, shift):
        return pltpu.roll(a.reshape(nb, SUBLANES, LANES), shift, axis=1).reshape(nc, LANES)

    @pl.when(pl.program_id(2) == 0)
    def _():
        h_ref[...] = jnp.zeros_like(h_ref)

    xs = [x_ref[:, e * HALF_IN:(e + 1) * HALF_IN] for e in range(2)]
    s_all = [jnp.dot(xs[e], w1_ref[e, :, HALF_IN:], preferred_element_type=_F32) for e in range(2)]
    sub = lax.broadcasted_iota(jnp.int32, (nc, LANES), 0) % SUBLANES
    brow = lax.broadcasted_iota(jnp.int32, (nb, LANES), 0)
    ys = [None, None]
    for e in range(2):
        for p in range(HALF_STATE // piece):
            re, im = slice(p * piece, p * piece + LANES), slice(p * piece + LANES, (p + 1) * piece)
            table = lambda row0, n: (dec_ref[e, row0:row0 + n, re], dec_ref[e, row0:row0 + n, im])
            tall = lambda a: jnp.concatenate([a] * nb, axis=0)
            h_re = slice(e * HALF_STATE + p * piece, e * HALF_STATE + p * piece + LANES)
            h_im = slice(h_re.stop, h_re.stop + LANES)
            slot = 2 * (e * (HALF_STATE // piece) + p)

            sr, si = s_all[e][:, re], s_all[e][:, im]
            hin_r, hin_i = h_ref[0:1, h_re], h_ref[0:1, h_im]

            for k in range(SCAN_STEPS):
                mr, mi = table(k * SUBLANES, SUBLANES)
                sr, si = cmul_add(sr, si, tall(mr), tall(mi),
                                  roll_in_block(sr, 1 << k), roll_in_block(si, 1 << k))

            t_scr[slot] = sr
            t_scr[slot + 1] = si
            br = t_scr[slot, pl.ds(SUBLANES - 1, nb, stride=SUBLANES), :]
            bi = t_scr[slot + 1, pl.ds(SUBLANES - 1, nb, stride=SUBLANES), :]
            pr, pi = table(TABLE_BLOCKS, 1)
            br = br + jnp.where(brow == 0, pr * hin_r - pi * hin_i, 0.0)
            bi = bi + jnp.where(brow == 0, pr * hin_i + pi * hin_r, 0.0)
            for k in range(dec_ref.shape[1] - TABLE_BLOCKS):
                shift = 1 << k
                tr = jnp.where(brow >= shift, pltpu.roll(br, shift, axis=0), 0.0)
                ti = jnp.where(brow >= shift, pltpu.roll(bi, shift, axis=0), 0.0)
                br, bi = cmul_add(br, bi, *table(TABLE_BLOCKS + k, 1), tr, ti)
            h_ref[0:1, h_re] = br[nb - 1:nb]
            h_ref[0:1, h_im] = bi[nb - 1:nb]
            bpr = jnp.where(brow == 0, hin_r, pltpu.roll(br, 1, axis=0))
            bpi = jnp.where(brow == 0, hin_i, pltpu.roll(bi, 1, axis=0))

            rep = lambda a: jnp.concatenate(
                [jnp.broadcast_to(a[k:k + 1], (SUBLANES, LANES)) for k in range(nb)], axis=0)
            fr, fi = table(TABLE_POWERS, SUBLANES)
            tr = jnp.where(sub >= 1, roll_in_block(sr, 1), 0.0)
            ti = jnp.where(sub >= 1, roll_in_block(si, 1), 0.0)
            prev_r, prev_i = cmul_add(tr, ti, tall(fr), tall(fi), rep(bpr), rep(bpi))
            hprev = jnp.concatenate([prev_r, prev_i], axis=1).astype(_BF16)
            part = lax.dot_general(hprev, w2_ref[e, :, p * piece:(p + 1) * piece], (((1,), (1,)), ((), ())),
                                   preferred_element_type=_F32)
            if ys[e] is None:
                mid = HALF_IN // 2
                ys[e] = jnp.concatenate(
                    [jnp.dot(xs[e][:, :mid], w1_ref[e, :mid, :mid], preferred_element_type=_F32),
                     jnp.dot(xs[e], w1_ref[e, :, mid:HALF_IN], preferred_element_type=_F32)], axis=1) + part
            else:
                ys[e] = ys[e] + part

    low = lax.broadcasted_iota(jnp.int32, (nc, LANES), 1) < HALF
    for b in range(SSM_Q // 2):
        y0, y1 = (ys[e][:, b * LANES:(b + 1) * LANES] for e in range(2))
        y_ref[pl.ds(2 * b, nc, stride=SSM_Q), :] = jnp.where(low, y0, _swap_halves(y1))
        y_ref[pl.ds(2 * b + 1, nc, stride=SSM_Q), :] = jnp.where(low, _swap_halves(y0), y1)


def _ssm(ucat, w1, w2, decay, batch, seq, nc):
    tiles = seq // (nc * SSM_Q)
    pair = lambda a: pl.BlockSpec((2,) + a.shape[1:], lambda s, b, t: (s, 0, 0))
    return pl.pallas_call(
        _ssm_kernel,
        grid=(N_SLABS, batch, tiles),
        in_specs=[pl.BlockSpec((nc, SLAB_IN), lambda s, b, t: (b * tiles + t, s)),
                  pair(w1), pair(w2), pair(decay)],
        out_specs=pl.BlockSpec((nc * SSM_Q, LANES), lambda s, b, t: (b * tiles + t, s)),
        out_shape=jax.ShapeDtypeStruct((batch * seq, SSM_W), _F32),
        scratch_shapes=[pltpu.VMEM((SUBLANES, SLAB_STATE), _F32),
                        pltpu.VMEM((SLAB_STATE // LANES, nc, LANES), _F32)],
        compiler_params=pltpu.CompilerParams(
            dimension_semantics=("arbitrary", "arbitrary", "arbitrary"),
            vmem_limit_bytes=VMEM_LIMIT),
        name="s5_mixer",
    )(ucat, w1, w2, decay)


def _gelu_tanh(y):
    c = math.sqrt(2.0 / math.pi)
    return (0.5 * y) * (1.0 + jnp.tanh(y * (c + (c * 0.044715) * (y * y))))


def _mix_out_kernel(sinks_ref, x_ref, q_ref, kp_ref, kc_ref, vp_ref, vc_ref, za_ref, y_ref, zs_ref,
                    wg_ref, bg_ref, wo_ref, out_ref, acc_ref, *, tiles_per_seq):
    scores, vts = _attn_scores(q_ref, kp_ref, kc_ref, vp_ref, vc_ref)

    y = _gelu_tanh(y_ref[...].astype(_F32))
    gate = jnp.dot(y.astype(_BF16), wg_ref[...], preferred_element_type=_F32) + bg_ref[...]
    o = y * jax.nn.sigmoid(gate) * zs_ref[...].astype(_F32)
    rs = lax.rsqrt(jnp.mean(o * o, axis=-1, keepdims=True) + NORM_EPS)
    out_ref[...] = x_ref[...] + rs * jnp.dot(o.astype(_BF16), wo_ref[ATTN_W:, :], preferred_element_type=_F32)

    seq_start = pl.program_id(0) % tiles_per_seq == 0
    _attn_softmax_pv(sinks_ref, scores, vts, acc_ref, seq_start, range(acc_ref.shape[0] // BLOCK))
    g = acc_ref[...] * za_ref[...].astype(_F32)
    rs = lax.rsqrt(jnp.mean(g * g, axis=-1, keepdims=True) + NORM_EPS)
    out_ref[...] += rs * jnp.dot(g.astype(_BF16), wo_ref[:ATTN_W, :], preferred_element_type=_F32)


def _mix_out(sinks, x2, q, k, v, za, y, zs, wg, bg, wo, seq, tm):
    t = x2.shape[0]
    blocks = tm // BLOCK
    row = lambda w: pl.BlockSpec((tm, w), lambda i: (i, 0))
    prev = lambda w: pl.BlockSpec((BLOCK, w), lambda i: (jnp.maximum(i * blocks - 1, 0), 0))
    return pl.pallas_call(
        functools.partial(_mix_out_kernel, tiles_per_seq=seq // tm),
        grid=(t // tm,),
        in_specs=[pl.BlockSpec(memory_space=pltpu.SMEM), row(D_MODEL), row(ATTN_W), prev(KV_W), row(KV_W),
                  prev(KV_W), row(KV_W), row(ATTN_W), row(SSM_W), row(SSM_W),
                  _resident(wg.shape), _resident(bg.shape), _resident(wo.shape)],
        out_specs=row(D_MODEL),
        out_shape=jax.ShapeDtypeStruct((t, D_MODEL), x2.dtype),
        scratch_shapes=[pltpu.VMEM((tm, ATTN_W), _F32)],
        compiler_params=pltpu.CompilerParams(dimension_semantics=("arbitrary",),
                                             vmem_limit_bytes=VMEM_LIMIT),
        name="mix_out",
    )(sinks, x2, q, k, k, v, v, za, y, zs, wg, bg, wo)


def _pick_tile(n, target, multiple):
    best = multiple
    for cand in range(multiple, min(n, target) + 1, multiple):
        if n % cand == 0:
            best = cand
    return best


def kernel(x, positions, norm_w, w_in, q_norm_w, k_norm_w, sinks, a_re, a_im, log_step,
           b_re, b_im, c_re, c_im, d_skip, w_glu, b_glu, attn_out_norm_w, ssm_out_norm_w, w_out):
    batch, seq, _ = x.shape
    assert seq % BLOCK == 0 and x.shape[2] == D_MODEL and w_in.shape == (D_MODEL, IN_W)
    t = batch * seq
    f32 = _F32

    w_in_k = (norm_w.astype(f32)[:, None] * w_in.astype(f32)).astype(_BF16)
    out_gain = jnp.concatenate([attn_out_norm_w.astype(f32), ssm_out_norm_w.astype(f32)])
    w_out_k = (out_gain[:, None] * w_out.astype(f32)).astype(_BF16)
    log2e = math.log2(math.e)
    qw = (jnp.tile(q_norm_w.astype(f32), N_HEADS) * (log2e / math.sqrt(HEAD_DIM)))[None, :]
    kw = jnp.tile(k_norm_w.astype(f32), N_KV_HEADS)[None, :]
    inv_freq = ROPE_THETA ** (-jnp.arange(0, HEAD_DIM, 2, dtype=f32) / HEAD_DIM)
    invf = jnp.tile(inv_freq, LANES // (HEAD_DIM // 2))[None, :]
    sgn = jnp.tile(jnp.concatenate([-jnp.ones(HEAD_DIM // 2, f32), jnp.ones(HEAD_DIM // 2, f32)]),
                   LANES // HEAD_DIM)[None, :]

    x2 = x.reshape(t, D_MODEL)
    pos2 = positions.reshape(t // LANES, LANES).astype(jnp.int32)

    tm = _pick_tile(seq, 512, BLOCK)
    q, k, v, za, ucat, zs = _in_proj(x2, pos2, w_in_k, qw, kw, invf, sgn, tm)


    nc = _pick_tile(seq // SSM_Q, 512, 16)
    assert nc & (nc - 1) == 0, "chunk rows per tile must be a power of two for the scan"
    cs, cl, skip, decay = _ssm_weights(a_re, a_im, log_step, b_re, b_im, c_re, c_im, d_skip,
                                     n_scan=int(math.log2(nc)))
    w1, w2 = _expand(cs, cl, skip)
    y = _ssm(ucat, w1, w2, decay, batch, seq, nc)

    out = _mix_out(sinks.astype(f32) * log2e, x2, q, k, v, za, y, zs, w_glu.astype(_BF16),
                   b_glu.astype(f32)[None, :], w_out_k, seq, tm)
    return out.reshape(x.shape)
```

```python
import functools
import math

import jax
import jax.numpy as jnp
import numpy as np
from jax import lax
from jax.experimental import pallas as pl
from jax.experimental.pallas import tpu as pltpu

D_MODEL = 2048
ATTN_W = 1024
KV_W = 256
HEAD_DIM = 64
N_HEADS = 16
N_KV_HEADS = 4
KV_REP = N_HEADS // N_KV_HEADS
SSM_W = 1024
SSM_H = 16
SSM_G = 64
SSM_P = 64
BLOCK = 128
ROPE_THETA = 10000.0
NORM_EPS = 1e-6
IN_W = 2 * ATTN_W + 2 * KV_W + 2 * SSM_W

LANES = 128
SUBLANES = 8
MXU_DIM = 256
SSM_Q = 8
N_SLABS = SSM_W // LANES
HALF = LANES // 2
HALF_GROUPS = HALF // SSM_H
N_HALVES = SSM_W // HALF
HALF_IN = SSM_Q * HALF
HALF_STATE = 2 * HALF_GROUPS * SSM_P
SLAB_IN = 2 * HALF_IN
SLAB_STATE = 2 * HALF_STATE
SCAN_STEPS = 3
TABLE_POWERS = SCAN_STEPS * SUBLANES
TABLE_BLOCKS = TABLE_POWERS + SUBLANES
VMEM_LIMIT = 56 * 1024 * 1024

_BF16 = jnp.bfloat16
_F32 = jnp.float32


def _resident(shape):
    nd = len(shape)
    return pl.BlockSpec(shape, lambda *_: (0,) * nd, pipeline_mode=pl.Buffered(1))


def _swap_halves(a):
    return pltpu.roll(a, HEAD_DIM, axis=1)


def _pair_slabs(n0, n1, n2, n3):
    low = lax.broadcasted_iota(jnp.int32, n0.shape, 1) < HEAD_DIM
    return (jnp.where(low, n0, _swap_halves(n2)), jnp.where(low, _swap_halves(n0), n2),
            jnp.where(low, n1, _swap_halves(n3)), jnp.where(low, _swap_halves(n1), n3))


def _rope(t, cos, sin_signed):
    lane = lax.broadcasted_iota(jnp.int32, t.shape, 1)
    first_half = (lane % HEAD_DIM) < (HEAD_DIM // 2)
    swapped = jnp.where(first_half,
                        pltpu.roll(t, LANES - HEAD_DIM // 2, axis=1),
                        pltpu.roll(t, HEAD_DIM // 2, axis=1))
    return t * cos + swapped * sin_signed


def _in_proj_kernel(x_ref, pos_ref, w_ref, qw_ref, kw_ref, invf_ref, sgn_ref,
                    q_ref, k_ref, v_ref, za_ref, ucat_ref, zs_ref, hn_ref, u_ref):
    tm = x_ref.shape[0]
    x = x_ref[...]
    hn_ref[...] = x.astype(_BF16)
    rs = jnp.broadcast_to(lax.rsqrt(jnp.mean(x * x, axis=-1, keepdims=True) + NORM_EPS), (tm, LANES))
    wide = 2 * MXU_DIM
    rs_wide = jnp.concatenate([rs] * (wide // LANES), axis=1)

    nsub = tm // LANES
    ang = jnp.concatenate(
        [jnp.broadcast_to(pos_ref[pl.ds(pl.program_id(0) * nsub + j, 1), :].astype(_F32),
                          (LANES, LANES)).T for j in range(nsub)], axis=0) * invf_ref[...]
    cos = jnp.cos(ang)
    sin_signed = jnp.sin(ang) * sgn_ref[...]

    def proj(c0):
        return jnp.dot(hn_ref[...], w_ref[:, c0:c0 + wide], preferred_element_type=_F32) * rs_wide

    def normed_rope(acc, nw):
        low = lax.broadcasted_iota(jnp.int32, (tm, LANES), 1) < HEAD_DIM
        out = []
        for s in range(MXU_DIM // LANES):
            a = acc[:, s * LANES:(s + 1) * LANES]
            sq = a * a
            ssq = jnp.where(low, jnp.sum(jnp.where(low, sq, 0.0), axis=-1, keepdims=True),
                            jnp.sum(jnp.where(low, 0.0, sq), axis=-1, keepdims=True))
            t = a * lax.rsqrt(ssq * (1.0 / HEAD_DIM) + NORM_EPS) * nw[:, s * LANES:(s + 1) * LANES]
            out.append(_rope(t, cos, sin_signed))
        return out

    base = 2 * ATTN_W + 2 * KV_W
    nc = tm // SSM_Q
    for c in range(SSM_W // wide):
        acc = proj(base + c * wide)
        for s in range(wide // LANES):
            u_ref[c * (wide // LANES) + s] = acc[:, s * LANES:(s + 1) * LANES]
    low_half = lax.broadcasted_iota(jnp.int32, (nc, LANES), 1) < HALF
    for s in range(N_SLABS):
        for b in range(SSM_Q // 2):
            even = u_ref[s, pl.ds(2 * b, nc, stride=SSM_Q), :]
            odd = u_ref[s, pl.ds(2 * b + 1, nc, stride=SSM_Q), :]
            c0 = s * SLAB_IN + b * LANES
            ucat_ref[:, c0:c0 + LANES] = jnp.where(low_half, even, _swap_halves(odd)).astype(_BF16)
            c0 += HALF_IN
            ucat_ref[:, c0:c0 + LANES] = jnp.where(low_half, _swap_halves(even), odd).astype(_BF16)

    for c in range(ATTN_W // wide):
        acc = proj(c * wide)
        slabs = []
        for h in range(wide // MXU_DIM):
            n0 = c * wide + h * MXU_DIM
            slabs += normed_rope(acc[:, h * MXU_DIM:(h + 1) * MXU_DIM], qw_ref[:, n0:n0 + MXU_DIM])
        for r, slab in enumerate(_pair_slabs(*slabs)):
            q_ref[:, c * wide + r * LANES:c * wide + (r + 1) * LANES] = slab.astype(_BF16)
    acc = proj(ATTN_W)
    for s, slab in enumerate(normed_rope(acc[:, :KV_W], kw_ref[...])):
        k_ref[:, s * LANES:(s + 1) * LANES] = slab.astype(_BF16)
    v_ref[...] = acc[:, KV_W:].astype(_BF16)
    for out_ref, base in ((za_ref, ATTN_W + 2 * KV_W), (zs_ref, 2 * ATTN_W + 2 * KV_W + SSM_W)):
        for c in range(ATTN_W // wide):
            z = proj(base + c * wide)
            out_ref[:, c * wide:(c + 1) * wide] = (z * jax.nn.sigmoid(z)).astype(_BF16)


def _in_proj(x2, pos2, w_in, qw, kw, invf, sgn, tm):
    t = x2.shape[0]
    row = lambda w: pl.BlockSpec((tm, w), lambda i: (i, 0))
    return pl.pallas_call(
        _in_proj_kernel,
        grid=(t // tm,),
        in_specs=[row(D_MODEL), _resident(pos2.shape), _resident(w_in.shape), _resident(qw.shape),
                  _resident(kw.shape), _resident(invf.shape), _resident(sgn.shape)],
        out_specs=[row(ATTN_W), row(KV_W), row(KV_W), row(ATTN_W),
                   pl.BlockSpec((tm // SSM_Q, N_SLABS * SLAB_IN), lambda i: (i, 0)), row(SSM_W)],
        out_shape=[jax.ShapeDtypeStruct((t, ATTN_W), _BF16),
                   jax.ShapeDtypeStruct((t, KV_W), _BF16),
                   jax.ShapeDtypeStruct((t, KV_W), _BF16),
                   jax.ShapeDtypeStruct((t, ATTN_W), _BF16),
                   jax.ShapeDtypeStruct((t // SSM_Q, N_SLABS * SLAB_IN), _BF16),
                   jax.ShapeDtypeStruct((t, SSM_W), _BF16)],
        scratch_shapes=[pltpu.VMEM((tm, D_MODEL), _BF16), pltpu.VMEM((N_SLABS, tm, LANES), _F32)],
        compiler_params=pltpu.CompilerParams(dimension_semantics=("arbitrary",),
                                             vmem_limit_bytes=VMEM_LIMIT),
        name="in_proj",
    )(x2, pos2, w_in, qw, kw, invf, sgn)


def _attn_scores(q_ref, kp_ref, kc_ref, vp_ref, vc_ref):
    blocks = q_ref.shape[0] // BLOCK
    two = 2 * BLOCK
    lane_kv = lax.broadcasted_iota(jnp.int32, (two, LANES), 1)
    halves = [lane_kv < HEAD_DIM, lane_kv >= HEAD_DIM]
    zero = jnp.zeros((two, LANES), _BF16)
    key_head = lax.broadcasted_iota(jnp.int32, (2 * SUBLANES, 2 * two), 1) // two
    count_rows = (lax.broadcasted_iota(jnp.int32, (2 * SUBLANES, 2 * two), 0) == key_head).astype(_BF16)

    def keys(prev_ref, cur_ref, b, sl):
        if b == 0:
            return jnp.concatenate([prev_ref[:, sl], cur_ref[0:BLOCK, sl]], axis=0)
        return cur_ref[(b - 1) * BLOCK:(b + 1) * BLOCK, sl]

    scores, vts = {}, {}
    for b in range(blocks):
        rows = slice(b * BLOCK, (b + 1) * BLOCK)
        for j in range(N_KV_HEADS // 2):
            sl = slice(j * LANES, (j + 1) * LANES)
            k2 = keys(kp_ref, kc_ref, b, sl)
            v2 = keys(vp_ref, vc_ref, b, sl)
            vts[b, j] = jnp.concatenate(
                [jnp.concatenate([jnp.where(keep, v2, zero) for keep in halves], axis=0).T,
                 count_rows], axis=0)
            for t in range(KV_REP // 2):
                r0 = KV_REP * j + 2 * t
                qq = q_ref[rows, r0 * LANES:(r0 + 2) * LANES]
                qq = jnp.concatenate([qq[:, :LANES], qq[:, LANES:]], axis=0)
                for half, keep in enumerate(halves):
                    scores[b, j, t, half] = lax.dot_general(
                        jnp.where(keep, k2, zero), qq, (((1,), (1,)), ((), ())),
                        preferred_element_type=_F32)
    return scores, vts


def _attn_softmax_pv(sinks_ref, scores, vts, acc_ref, seq_start, blocks):
    two = 2 * BLOCK
    ik = lax.broadcasted_iota(jnp.int32, (two, two), 0)
    iq = lax.broadcasted_iota(jnp.int32, (two, two), 1) % BLOCK
    rel = ik - iq
    band = (rel >= 1) & (rel <= BLOCK)
    band_first = band & ((ik >= BLOCK) | jnp.logical_not(seq_start))
    first_slab = lax.broadcasted_iota(jnp.int32, (1, two), 1) < BLOCK
    row_o = lax.broadcasted_iota(jnp.int32, (LANES, two), 0)
    for b in blocks:
        rows = slice(b * BLOCK, (b + 1) * BLOCK)
        allowed = band_first if b == 0 else band
        for j in range(N_KV_HEADS // 2):
            for t in range(KV_REP // 2):
                r0 = KV_REP * j + 2 * t
                probs, sink_terms = [], []
                for half in range(2):
                    h0 = 2 * KV_REP * j + KV_REP * half + 2 * t
                    sink = jnp.where(first_slab, sinks_ref[h0], sinks_ref[h0 + 1])
                    s = jnp.where(allowed, scores[b, j, t, half], -1e30)
                    m = jnp.maximum(jnp.max(s, axis=0, keepdims=True), sink)
                    probs.append(jnp.exp2(s - m).astype(_BF16))
                    sink_terms.append(jnp.exp2(sink - m))
                o_t = jnp.dot(vts[b, j], jnp.concatenate(probs, axis=0), preferred_element_type=_F32)
                rdens = [1.0 / (o_t[LANES + half:LANES + half + 1] + sink_terms[half]) for half in range(2)]
                o_t = o_t[:LANES] * jnp.where(row_o < HEAD_DIM, rdens[0], rdens[1])
                for half in range(2):
                    heads = slice(half * HEAD_DIM, (half + 1) * HEAD_DIM)
                    slab = KV_REP * j + (KV_REP // 2) * half + t
                    acc_ref[rows, slab * LANES:(slab + 1) * LANES] = jnp.concatenate(
                        [o_t[heads, :BLOCK], o_t[heads, BLOCK:]], axis=0).T


def _ssm_weights(a_re, a_im, log_step, b_re, b_im, c_re, c_im, d_skip, n_scan):
    f32 = _F32
    a_re, a_im = a_re.astype(f32), a_im.astype(f32)
    delta = jnp.exp(log_step.astype(f32))[:, None]
    zr, zi = a_re * delta, a_im * delta

    def lam_pow(exponents):
        m = jnp.asarray(np.asarray(exponents, np.float32))[:, None, None]
        mag = jnp.exp(m * zr)
        return mag * jnp.cos(m * zi), mag * jnp.sin(m * zi)

    pw_r, pw_i = lam_pow(range(SSM_Q + 1))
    rev_r, rev_i = lam_pow(range(SSM_Q - 1, -1, -1))
    lr, li = pw_r[1], pw_i[1]
    den = a_re * a_re + a_im * a_im
    fr = ((lr - 1.0) * a_re + li * a_im) / den
    fi = (li * a_re - (lr - 1.0) * a_im) / den
    b_re, b_im = b_re.astype(f32), b_im.astype(f32)
    bbr = fr[..., None] * b_re - fi[..., None] * b_im
    bbi = fr[..., None] * b_im + fi[..., None] * b_re
    c_re, c_im = c_re.astype(f32), c_im.astype(f32)

    cl_r = c_re[None] * pw_r[:, :, None, :] - c_im[None] * pw_i[:, :, None, :]
    cl_i = c_re[None] * pw_i[:, :, None, :] + c_im[None] * pw_r[:, :, None, :]

    def per_slab(a):
        a = a.reshape(a.shape[0], N_HALVES, HALF, a.shape[-1])
        return a.transpose(1, 0, 2, 3)

    sin_r = rev_r[..., None] * bbr[None] - rev_i[..., None] * bbi[None]
    sin_i = rev_r[..., None] * bbi[None] + rev_i[..., None] * bbr[None]
    cs = per_slab(jnp.concatenate([sin_r, sin_i], axis=2).transpose(0, 1, 3, 2))

    cl = per_slab(jnp.concatenate([cl_r, -cl_i], axis=3))
    skip = d_skip.astype(f32).reshape(N_HALVES, 1, HALF)

    exps, keep = [], []
    for k in range(SCAN_STEPS):
        exps += [SSM_Q << k] * SUBLANES
        keep += [0.0] * (1 << k) + [1.0] * (SUBLANES - (1 << k))
    exps += [SSM_Q * j for j in range(SUBLANES)] + [SSM_Q << k for k in range(SCAN_STEPS, n_scan)]
    keep += [1.0] * (SUBLANES + n_scan - SCAN_STEPS)
    tr, ti = lam_pow(exps)
    keep = jnp.asarray(np.asarray(keep, np.float32))[:, None, None]
    rows = jnp.stack([(tr * keep).reshape(-1, N_HALVES, HALF_GROUPS // 2, 2 * SSM_P),
                      (ti * keep).reshape(-1, N_HALVES, HALF_GROUPS // 2, 2 * SSM_P)], axis=3)
    decay = rows.transpose(1, 0, 2, 3, 4).reshape(N_HALVES, len(exps), HALF_STATE)
    return cs, cl, skip, decay


def _expand_half(cs_ref, cl_ref, skip_ref, w1_ref, w2_ref, e):
    r = lax.broadcasted_iota(jnp.int32, (LANES, LANES), 0)
    c = lax.broadcasted_iota(jnp.int32, (LANES, LANES), 1)
    same_group = ((r % HALF) // SSM_H) == ((c % HALF) // SSM_H)
    rs = lax.broadcasted_iota(jnp.int32, (LANES, HALF_STATE), 0)
    cst = lax.broadcasted_iota(jnp.int32, (LANES, HALF_STATE), 1)
    rep_s = ((rs // SSM_P == (cst // LANES) % 2) & (rs % SSM_P == cst % SSM_P)).astype(_BF16)
    diag_s = ((rs % HALF) // SSM_H) == (2 * (cst // (2 * LANES)) + (cst % LANES) // SSM_P)

    none = jnp.zeros((HALF, LANES), _F32)
    b_bar = cs_ref[e, SSM_Q - 1]
    lhs = jnp.concatenate([jnp.concatenate([b_bar, none], axis=1),
                           jnp.concatenate([none, b_bar], axis=1)], axis=0)
    c_lam = lambda d: cl_ref[e, d] if d >= 0 else none
    skip = jnp.concatenate([skip_ref[e], skip_ref[e]], axis=1)
    tiles = []
    for delta in range(SSM_Q // 2):
        rhs = jnp.concatenate(
            [jnp.concatenate([c_lam(2 * delta), c_lam(2 * delta - 1)], axis=1),
             jnp.concatenate([c_lam(2 * delta + 1), c_lam(2 * delta)], axis=1)], axis=0)
        k = lax.dot_general(lhs, rhs, (((1,), (1,)), ((), ())),
                            precision=lax.Precision.HIGHEST, preferred_element_type=_F32)
        if delta == 0:
            k = k + jnp.where(r == c, skip, 0.0)
        tiles.append(jnp.where(same_group, k, 0.0).astype(_BF16))
    zero = jnp.zeros((LANES, LANES), _BF16)
    two_offsets = lambda ref, i: jnp.concatenate([ref[e, i], ref[e, i + 1]], axis=0).astype(_BF16)
    for b in range(SSM_Q // 2):
        rows = slice(b * LANES, (b + 1) * LANES)
        for b2 in range(SSM_Q // 2):
            w1_ref[e, rows, b2 * LANES:(b2 + 1) * LANES] = tiles[b2 - b] if b2 >= b else zero
        w_in = jnp.where(diag_s, jnp.dot(two_offsets(cs_ref, 2 * b), rep_s, preferred_element_type=_F32), 0.0)
        w1_ref[e, rows, HALF_IN:] = w_in.astype(_BF16)
        w_out_t = jnp.where(diag_s, jnp.dot(two_offsets(cl_ref, 2 * b + 1), rep_s,
                                            preferred_element_type=_F32), 0.0)
        w2_ref[e, rows, :] = w_out_t.astype(_BF16)


def _ssm_kernel(x_ref, cs_ref, cl_ref, skip_ref, dec_ref, y_ref, w1_ref, w2_ref, h_ref, t_scr):
    nc = x_ref.shape[0]
    nb = nc // SUBLANES
    piece = 2 * LANES

    def cmul_add(acc_r, acc_i, pr, pi, tr, ti):
        return acc_r + pr * tr - pi * ti, acc_i + pr * ti + pi * tr

    def roll_in_block(a, shift):
        return pltpu.roll(a.reshape(nb, SUBLANES, LANES), shift, axis=1).reshape(nc, LANES)

    @pl.when((pl.program_id(1) == 0) & (pl.program_id(2) == 0))
    def _():
        for e in range(2):
            _expand_half(cs_ref, cl_ref, skip_ref, w1_ref, w2_ref, e)

    @pl.when(pl.program_id(2) == 0)
    def _():
        h_ref[...] = jnp.zeros_like(h_ref)

    xs = [x_ref[:, e * HALF_IN:(e + 1) * HALF_IN] for e in range(2)]
    s_all = [jnp.dot(xs[e], w1_ref[e, :, HALF_IN:], preferred_element_type=_F32) for e in range(2)]
    sub = lax.broadcasted_iota(jnp.int32, (nc, LANES), 0) % SUBLANES
    brow = lax.broadcasted_iota(jnp.int32, (nb, LANES), 0)
    ys = [None, None]
    for e in range(2):
        for p in range(HALF_STATE // piece):
            re, im = slice(p * piece, p * piece + LANES), slice(p * piece + LANES, (p + 1) * piece)
            table = lambda row0, n: (dec_ref[e, row0:row0 + n, re], dec_ref[e, row0:row0 + n, im])
            tall = lambda a: jnp.concatenate([a] * nb, axis=0)
            h_re = slice(e * HALF_STATE + p * piece, e * HALF_STATE + p * piece + LANES)
            h_im = slice(h_re.stop, h_re.stop + LANES)
            slot = 2 * (e * (HALF_STATE // piece) + p)

            sr, si = s_all[e][:, re], s_all[e][:, im]
            hin_r, hin_i = h_ref[0:1, h_re], h_ref[0:1, h_im]

            for k in range(SCAN_STEPS):
                mr, mi = table(k * SUBLANES, SUBLANES)
                sr, si = cmul_add(sr, si, tall(mr), tall(mi),
                                  roll_in_block(sr, 1 << k), roll_in_block(si, 1 << k))

            t_scr[slot] = sr
            t_scr[slot + 1] = si
            br = t_scr[slot, pl.ds(SUBLANES - 1, nb, stride=SUBLANES), :]
            bi = t_scr[slot + 1, pl.ds(SUBLANES - 1, nb, stride=SUBLANES), :]
            pr, pi = table(TABLE_BLOCKS, 1)
            br = br + jnp.where(brow == 0, pr * hin_r - pi * hin_i, 0.0)
            bi = bi + jnp.where(brow == 0, pr * hin_i + pi * hin_r, 0.0)
            for k in range(dec_ref.shape[1] - TABLE_BLOCKS):
                shift = 1 << k
                tr = jnp.where(brow >= shift, pltpu.roll(br, shift, axis=0), 0.0)
                ti = jnp.where(brow >= shift, pltpu.roll(bi, shift, axis=0), 0.0)
                br, bi = cmul_add(br, bi, *table(TABLE_BLOCKS + k, 1), tr, ti)
            h_ref[0:1, h_re] = br[nb - 1:nb]
            h_ref[0:1, h_im] = bi[nb - 1:nb]
            bpr = jnp.where(brow == 0, hin_r, pltpu.roll(br, 1, axis=0))
            bpi = jnp.where(brow == 0, hin_i, pltpu.roll(bi, 1, axis=0))

            rep = lambda a: jnp.concatenate(
                [jnp.broadcast_to(a[k:k + 1], (SUBLANES, LANES)) for k in range(nb)], axis=0)
            fr, fi = table(TABLE_POWERS, SUBLANES)
            tr = jnp.where(sub >= 1, roll_in_block(sr, 1), 0.0)
            ti = jnp.where(sub >= 1, roll_in_block(si, 1), 0.0)
            prev_r, prev_i = cmul_add(tr, ti, tall(fr), tall(fi), rep(bpr), rep(bpi))
            hprev = jnp.concatenate([prev_r, prev_i], axis=1).astype(_BF16)
            part = lax.dot_general(hprev, w2_ref[e, :, p * piece:(p + 1) * piece], (((1,), (1,)), ((), ())),
                                   preferred_element_type=_F32)
            if ys[e] is None:
                mid = HALF_IN // 2
                ys[e] = jnp.concatenate(
                    [jnp.dot(xs[e][:, :mid], w1_ref[e, :mid, :mid], preferred_element_type=_F32),
                     jnp.dot(xs[e], w1_ref[e, :, mid:HALF_IN], preferred_element_type=_F32)], axis=1) + part
            else:
                ys[e] = ys[e] + part

    low = lax.broadcasted_iota(jnp.int32, (nc, LANES), 1) < HALF
    for b in range(SSM_Q // 2):
        y0, y1 = (ys[e][:, b * LANES:(b + 1) * LANES] for e in range(2))
        y_ref[pl.ds(2 * b, nc, stride=SSM_Q), :] = jnp.where(low, y0, _swap_halves(y1))
        y_ref[pl.ds(2 * b + 1, nc, stride=SSM_Q), :] = jnp.where(low, _swap_halves(y0), y1)


def _ssm(ucat, cs, cl, skip, decay, batch, seq, nc):
    tiles = seq // (nc * SSM_Q)
    pair = lambda a: pl.BlockSpec((2,) + a.shape[1:], lambda s, b, t: (s,) + (0,) * (a.ndim - 1))
    return pl.pallas_call(
        _ssm_kernel,
        grid=(N_SLABS, batch, tiles),
        in_specs=[pl.BlockSpec((nc, SLAB_IN), lambda s, b, t: (b * tiles + t, s)),
                  pair(cs), pair(cl), pair(skip), pair(decay)],
        out_specs=pl.BlockSpec((nc * SSM_Q, LANES), lambda s, b, t: (b * tiles + t, s)),
        out_shape=jax.ShapeDtypeStruct((batch * seq, SSM_W), _F32),
        scratch_shapes=[pltpu.VMEM((2, HALF_IN, HALF_IN + HALF_STATE), _BF16),
                        pltpu.VMEM((2, HALF_IN, HALF_STATE), _BF16),
                        pltpu.VMEM((SUBLANES, SLAB_STATE), _F32),
                        pltpu.VMEM((SLAB_STATE // LANES, nc, LANES), _F32)],
        compiler_params=pltpu.CompilerParams(
            dimension_semantics=("arbitrary", "arbitrary", "arbitrary"),
            vmem_limit_bytes=VMEM_LIMIT),
        name="s5_mixer",
    )(ucat, cs, cl, skip, decay)


def _gelu_tanh(y):
    c = math.sqrt(2.0 / math.pi)
    return (0.5 * y) * (1.0 + jnp.tanh(y * (c + (c * 0.044715) * (y * y))))


def _mix_out_kernel(sinks_ref, x_ref, q_ref, kp_ref, kc_ref, vp_ref, vc_ref, za_ref, y_ref, zs_ref,
                    wg_ref, bg_ref, wo_ref, out_ref, acc_ref, *, tiles_per_seq):
    scores, vts = _attn_scores(q_ref, kp_ref, kc_ref, vp_ref, vc_ref)

    y = _gelu_tanh(y_ref[...].astype(_F32))
    gate = jnp.dot(y.astype(_BF16), wg_ref[...], preferred_element_type=_F32) + bg_ref[...]
    o = y * jax.nn.sigmoid(gate) * zs_ref[...].astype(_F32)
    rs = lax.rsqrt(jnp.mean(o * o, axis=-1, keepdims=True) + NORM_EPS)
    out_ref[...] = x_ref[...] + rs * jnp.dot(o.astype(_BF16), wo_ref[ATTN_W:, :], preferred_element_type=_F32)

    seq_start = pl.program_id(0) % tiles_per_seq == 0
    _attn_softmax_pv(sinks_ref, scores, vts, acc_ref, seq_start, range(acc_ref.shape[0] // BLOCK))
    g = acc_ref[...] * za_ref[...].astype(_F32)
    rs = lax.rsqrt(jnp.mean(g * g, axis=-1, keepdims=True) + NORM_EPS)
    out_ref[...] += rs * jnp.dot(g.astype(_BF16), wo_ref[:ATTN_W, :], preferred_element_type=_F32)


def _mix_out(sinks, x2, q, k, v, za, y, zs, wg, bg, wo, seq, tm):
    t = x2.shape[0]
    blocks = tm // BLOCK
    row = lambda w: pl.BlockSpec((tm, w), lambda i: (i, 0))
    prev = lambda w: pl.BlockSpec((BLOCK, w), lambda i: (jnp.maximum(i * blocks - 1, 0), 0))
    return pl.pallas_call(
        functools.partial(_mix_out_kernel, tiles_per_seq=seq // tm),
        grid=(t // tm,),
        in_specs=[pl.BlockSpec(memory_space=pltpu.SMEM), row(D_MODEL), row(ATTN_W), prev(KV_W), row(KV_W),
                  prev(KV_W), row(KV_W), row(ATTN_W), row(SSM_W), row(SSM_W),
                  _resident(wg.shape), _resident(bg.shape), _resident(wo.shape)],
        out_specs=row(D_MODEL),
        out_shape=jax.ShapeDtypeStruct((t, D_MODEL), x2.dtype),
        scratch_shapes=[pltpu.VMEM((tm, ATTN_W), _F32)],
        compiler_params=pltpu.CompilerParams(dimension_semantics=("arbitrary",),
                                             vmem_limit_bytes=VMEM_LIMIT),
        name="mix_out",
    )(sinks, x2, q, k, k, v, v, za, y, zs, wg, bg, wo)


def _pick_tile(n, target, multiple):
    best = multiple
    for cand in range(multiple, min(n, target) + 1, multiple):
        if n % cand == 0:
            best = cand
    return best


def kernel(x, positions, norm_w, w_in, q_norm_w, k_norm_w, sinks, a_re, a_im, log_step,
           b_re, b_im, c_re, c_im, d_skip, w_glu, b_glu, attn_out_norm_w, ssm_out_norm_w, w_out):
    batch, seq, _ = x.shape
    assert seq % BLOCK == 0 and x.shape[2] == D_MODEL and w_in.shape == (D_MODEL, IN_W)
    t = batch * seq
    f32 = _F32

    w_in_k = (norm_w.astype(f32)[:, None] * w_in.astype(f32)).astype(_BF16)
    out_gain = jnp.concatenate([attn_out_norm_w.astype(f32), ssm_out_norm_w.astype(f32)])
    w_out_k = (out_gain[:, None] * w_out.astype(f32)).astype(_BF16)
    log2e = math.log2(math.e)
    qw = (jnp.tile(q_norm_w.astype(f32), N_HEADS) * (log2e / math.sqrt(HEAD_DIM)))[None, :]
    kw = jnp.tile(k_norm_w.astype(f32), N_KV_HEADS)[None, :]
    inv_freq = ROPE_THETA ** (-jnp.arange(0, HEAD_DIM, 2, dtype=f32) / HEAD_DIM)
    invf = jnp.tile(inv_freq, LANES // (HEAD_DIM // 2))[None, :]
    sgn = jnp.tile(jnp.concatenate([-jnp.ones(HEAD_DIM // 2, f32), jnp.ones(HEAD_DIM // 2, f32)]),
                   LANES // HEAD_DIM)[None, :]

    x2 = x.reshape(t, D_MODEL)
    pos2 = positions.reshape(t // LANES, LANES).astype(jnp.int32)

    tm = _pick_tile(seq, 512, BLOCK)
    q, k, v, za, ucat, zs = _in_proj(x2, pos2, w_in_k, qw, kw, invf, sgn, tm)


    nc = _pick_tile(seq // SSM_Q, 512, 16)
    assert nc & (nc - 1) == 0, "chunk rows per tile must be a power of two for the scan"
    cs, cl, skip, decay = _ssm_weights(a_re, a_im, log_step, b_re, b_im, c_re, c_im, d_skip,
                                     n_scan=int(math.log2(nc)))
    y = _ssm(ucat, cs, cl, skip, decay, batch, seq, nc)

    out = _mix_out(sinks.astype(f32) * log2e, x2, q, k, v, za, y, zs, w_glu.astype(_BF16),
                   b_glu.astype(f32)[None, :], w_out_k, seq, tm)
    return out.reshape(x.shape)
```

```python
import functools
import math

import jax
import jax.numpy as jnp
import numpy as np
from jax import lax
from jax.experimental import pallas as pl
from jax.experimental.pallas import tpu as pltpu

D_MODEL = 2048
ATTN_W = 1024
KV_W = 256
HEAD_DIM = 64
N_HEADS = 16
N_KV_HEADS = 4
KV_REP = N_HEADS // N_KV_HEADS
SSM_W = 1024
SSM_H = 16
SSM_G = 64
SSM_P = 64
BLOCK = 128
ROPE_THETA = 10000.0
NORM_EPS = 1e-6
IN_W = 2 * ATTN_W + 2 * KV_W + 2 * SSM_W

LANES = 128
SUBLANES = 8
MXU_DIM = 256
SSM_Q = 8
N_SLABS = SSM_W // LANES
HALF = LANES // 2
HALF_GROUPS = HALF // SSM_H
N_HALVES = SSM_W // HALF
HALF_IN = SSM_Q * HALF
HALF_STATE = 2 * HALF_GROUPS * SSM_P
SLAB_IN = 2 * HALF_IN
SLAB_STATE = 2 * HALF_STATE
SCAN_STEPS = 3
TABLE_POWERS = SCAN_STEPS * SUBLANES
TABLE_BLOCKS = TABLE_POWERS + SUBLANES
VMEM_LIMIT = 56 * 1024 * 1024

_BF16 = jnp.bfloat16
_F32 = jnp.float32


def _resident(shape):
    nd = len(shape)
    return pl.BlockSpec(shape, lambda *_: (0,) * nd, pipeline_mode=pl.Buffered(1))


def _swap_halves(a):
    return pltpu.roll(a, HEAD_DIM, axis=1)


def _pair_slabs(n0, n1, n2, n3):
    low = lax.broadcasted_iota(jnp.int32, n0.shape, 1) < HEAD_DIM
    return (jnp.where(low, n0, _swap_halves(n2)), jnp.where(low, _swap_halves(n0), n2),
            jnp.where(low, n1, _swap_halves(n3)), jnp.where(low, _swap_halves(n1), n3))


def _rope(t, cos, sin_signed):
    lane = lax.broadcasted_iota(jnp.int32, t.shape, 1)
    first_half = (lane % HEAD_DIM) < (HEAD_DIM // 2)
    swapped = jnp.where(first_half,
                        pltpu.roll(t, LANES - HEAD_DIM // 2, axis=1),
                        pltpu.roll(t, HEAD_DIM // 2, axis=1))
    return t * cos + swapped * sin_signed


def _in_proj_kernel(x_ref, pos_ref, w_ref, qw_ref, kw_ref, invf_ref, sgn_ref,
                    q_ref, k_ref, v_ref, za_ref, ucat_ref, zs_ref, hn_ref, u_ref):
    tm = x_ref.shape[0]
    x = x_ref[...]
    hn_ref[...] = x.astype(_BF16)
    rs = jnp.broadcast_to(lax.rsqrt(jnp.mean(x * x, axis=-1, keepdims=True) + NORM_EPS), (tm, LANES))
    wide = 2 * MXU_DIM
    rs_wide = jnp.concatenate([rs] * (wide // LANES), axis=1)

    nsub = tm // LANES
    ang = jnp.concatenate(
        [jnp.broadcast_to(pos_ref[pl.ds(pl.program_id(0) * nsub + j, 1), :].astype(_F32),
                          (LANES, LANES)).T for j in range(nsub)], axis=0) * invf_ref[...]
    cos = jnp.cos(ang)
    sin_signed = jnp.sin(ang) * sgn_ref[...]

    def proj(c0):
        return jnp.dot(hn_ref[...], w_ref[:, c0:c0 + wide], preferred_element_type=_F32) * rs_wide

    def normed_rope(acc, nw):
        low = lax.broadcasted_iota(jnp.int32, (tm, LANES), 1) < HEAD_DIM
        out = []
        for s in range(MXU_DIM // LANES):
            a = acc[:, s * LANES:(s + 1) * LANES]
            sq = a * a
            ssq = jnp.where(low, jnp.sum(jnp.where(low, sq, 0.0), axis=-1, keepdims=True),
                            jnp.sum(jnp.where(low, 0.0, sq), axis=-1, keepdims=True))
            t = a * lax.rsqrt(ssq * (1.0 / HEAD_DIM) + NORM_EPS) * nw[:, s * LANES:(s + 1) * LANES]
            out.append(_rope(t, cos, sin_signed))
        return out

    base = 2 * ATTN_W + 2 * KV_W
    nc = tm // SSM_Q
    for c in range(SSM_W // wide):
        acc = proj(base + c * wide)
        for s in range(wide // LANES):
            u_ref[c * (wide // LANES) + s] = acc[:, s * LANES:(s + 1) * LANES]
    low_half = lax.broadcasted_iota(jnp.int32, (nc, LANES), 1) < HALF
    for s in range(N_SLABS):
        for b in range(SSM_Q // 2):
            even = u_ref[s, pl.ds(2 * b, nc, stride=SSM_Q), :]
            odd = u_ref[s, pl.ds(2 * b + 1, nc, stride=SSM_Q), :]
            c0 = s * SLAB_IN + b * LANES
            ucat_ref[:, c0:c0 + LANES] = jnp.where(low_half, even, _swap_halves(odd)).astype(_BF16)
            c0 += HALF_IN
            ucat_ref[:, c0:c0 + LANES] = jnp.where(low_half, _swap_halves(even), odd).astype(_BF16)

    for c in range(ATTN_W // wide):
        acc = proj(c * wide)
        slabs = []
        for h in range(wide // MXU_DIM):
            n0 = c * wide + h * MXU_DIM
            slabs += normed_rope(acc[:, h * MXU_DIM:(h + 1) * MXU_DIM], qw_ref[:, n0:n0 + MXU_DIM])
        for r, slab in enumerate(_pair_slabs(*slabs)):
            q_ref[:, c * wide + r * LANES:c * wide + (r + 1) * LANES] = slab.astype(_BF16)
    acc = proj(ATTN_W)
    for s, slab in enumerate(normed_rope(acc[:, :KV_W], kw_ref[...])):
        k_ref[:, s * LANES:(s + 1) * LANES] = slab.astype(_BF16)
    v_ref[...] = acc[:, KV_W:].astype(_BF16)
    for out_ref, base in ((za_ref, ATTN_W + 2 * KV_W), (zs_ref, 2 * ATTN_W + 2 * KV_W + SSM_W)):
        for c in range(ATTN_W // wide):
            z = proj(base + c * wide)
            out_ref[:, c * wide:(c + 1) * wide] = (z * jax.nn.sigmoid(z)).astype(_BF16)


def _in_proj(x2, pos2, w_in, qw, kw, invf, sgn, tm):
    t = x2.shape[0]
    row = lambda w: pl.BlockSpec((tm, w), lambda i: (i, 0))
    return pl.pallas_call(
        _in_proj_kernel,
        grid=(t // tm,),
        in_specs=[row(D_MODEL), _resident(pos2.shape), _resident(w_in.shape), _resident(qw.shape),
                  _resident(kw.shape), _resident(invf.shape), _resident(sgn.shape)],
        out_specs=[row(ATTN_W), row(KV_W), row(KV_W), row(ATTN_W),
                   pl.BlockSpec((tm // SSM_Q, N_SLABS * SLAB_IN), lambda i: (i, 0)), row(SSM_W)],
        out_shape=[jax.ShapeDtypeStruct((t, ATTN_W), _BF16),
                   jax.ShapeDtypeStruct((t, KV_W), _BF16),
                   jax.ShapeDtypeStruct((t, KV_W), _BF16),
                   jax.ShapeDtypeStruct((t, ATTN_W), _BF16),
                   jax.ShapeDtypeStruct((t // SSM_Q, N_SLABS * SLAB_IN), _BF16),
                   jax.ShapeDtypeStruct((t, SSM_W), _BF16)],
        scratch_shapes=[pltpu.VMEM((tm, D_MODEL), _BF16), pltpu.VMEM((N_SLABS, tm, LANES), _F32)],
        compiler_params=pltpu.CompilerParams(dimension_semantics=("arbitrary",),
                                             vmem_limit_bytes=VMEM_LIMIT),
        name="in_proj",
    )(x2, pos2, w_in, qw, kw, invf, sgn)


def _attn_scores(q_ref, kp_ref, kc_ref, vp_ref, vc_ref):
    blocks = q_ref.shape[0] // BLOCK
    two = 2 * BLOCK
    lane_kv = lax.broadcasted_iota(jnp.int32, (two, LANES), 1)
    halves = [lane_kv < HEAD_DIM, lane_kv >= HEAD_DIM]
    zero = jnp.zeros((two, LANES), _BF16)
    key_head = lax.broadcasted_iota(jnp.int32, (2 * SUBLANES, 2 * two), 1) // two
    count_rows = (lax.broadcasted_iota(jnp.int32, (2 * SUBLANES, 2 * two), 0) == key_head).astype(_BF16)

    def keys(prev_ref, cur_ref, b, sl):
        if b == 0:
            return jnp.concatenate([prev_ref[:, sl], cur_ref[0:BLOCK, sl]], axis=0)
        return cur_ref[(b - 1) * BLOCK:(b + 1) * BLOCK, sl]

    scores, vts = {}, {}
    for b in range(blocks):
        rows = slice(b * BLOCK, (b + 1) * BLOCK)
        for j in range(N_KV_HEADS // 2):
            sl = slice(j * LANES, (j + 1) * LANES)
            k2 = keys(kp_ref, kc_ref, b, sl)
            v2 = keys(vp_ref, vc_ref, b, sl)
            vts[b, j] = jnp.concatenate(
                [jnp.concatenate([jnp.where(keep, v2, zero) for keep in halves], axis=0).T,
                 count_rows], axis=0)
            for t in range(KV_REP // 2):
                r0 = KV_REP * j + 2 * t
                qq = q_ref[rows, r0 * LANES:(r0 + 2) * LANES]
                qq = jnp.concatenate([qq[:, :LANES], qq[:, LANES:]], axis=0)
                for half, keep in enumerate(halves):
                    scores[b, j, t, half] = lax.dot_general(
                        jnp.where(keep, k2, zero), qq, (((1,), (1,)), ((), ())),
                        preferred_element_type=_F32)
    return scores, vts


def _attn_softmax_pv(sinks_ref, scores, vts, acc_ref, seq_start, blocks):
    two = 2 * BLOCK
    ik = lax.broadcasted_iota(jnp.int32, (two, two), 0)
    iq = lax.broadcasted_iota(jnp.int32, (two, two), 1) % BLOCK
    rel = ik - iq
    band = (rel >= 1) & (rel <= BLOCK)
    band_first = band & ((ik >= BLOCK) | jnp.logical_not(seq_start))
    first_slab = lax.broadcasted_iota(jnp.int32, (1, two), 1) < BLOCK
    row_o = lax.broadcasted_iota(jnp.int32, (LANES, two), 0)
    for b in blocks:
        rows = slice(b * BLOCK, (b + 1) * BLOCK)
        allowed = band_first if b == 0 else band
        for j in range(N_KV_HEADS // 2):
            for t in range(KV_REP // 2):
                r0 = KV_REP * j + 2 * t
                probs, sink_terms = [], []
                for half in range(2):
                    h0 = 2 * KV_REP * j + KV_REP * half + 2 * t
                    sink = jnp.where(first_slab, sinks_ref[h0], sinks_ref[h0 + 1])
                    s = jnp.where(allowed, scores[b, j, t, half], -1e30)
                    m = jnp.maximum(jnp.max(s, axis=0, keepdims=True), sink)
                    probs.append(jnp.exp2(s - m).astype(_BF16))
                    sink_terms.append(jnp.exp2(sink - m))
                o_t = jnp.dot(vts[b, j], jnp.concatenate(probs, axis=0), preferred_element_type=_F32)
                rdens = [1.0 / (o_t[LANES + half:LANES + half + 1] + sink_terms[half]) for half in range(2)]
                o_t = o_t[:LANES] * jnp.where(row_o < HEAD_DIM, rdens[0], rdens[1])
                for half in range(2):
                    heads = slice(half * HEAD_DIM, (half + 1) * HEAD_DIM)
                    slab = KV_REP * j + (KV_REP // 2) * half + t
                    acc_ref[rows, slab * LANES:(slab + 1) * LANES] = jnp.concatenate(
                        [o_t[heads, :BLOCK], o_t[heads, BLOCK:]], axis=0).T


def _ssm_weights(a_re, a_im, log_step, b_re, b_im, c_re, c_im, d_skip, n_scan):
    f32 = _F32
    a_re, a_im = a_re.astype(f32), a_im.astype(f32)
    delta = jnp.exp(log_step.astype(f32))[:, None]
    zr, zi = a_re * delta, a_im * delta

    def lam_pow(exponents):
        m = jnp.asarray(np.asarray(exponents, np.float32))[:, None, None]
        mag = jnp.exp(m * zr)
        return mag * jnp.cos(m * zi), mag * jnp.sin(m * zi)

    pw_r, pw_i = lam_pow(range(SSM_Q + 1))
    rev_r, rev_i = lam_pow(range(SSM_Q - 1, -1, -1))
    lr, li = pw_r[1], pw_i[1]
    den = a_re * a_re + a_im * a_im
    fr = ((lr - 1.0) * a_re + li * a_im) / den
    fi = (li * a_re - (lr - 1.0) * a_im) / den
    b_re, b_im = b_re.astype(f32), b_im.astype(f32)
    bbr = fr[..., None] * b_re - fi[..., None] * b_im
    bbi = fr[..., None] * b_im + fi[..., None] * b_re
    c_re, c_im = c_re.astype(f32), c_im.astype(f32)

    cl_r = c_re[None] * pw_r[:, :, None, :] - c_im[None] * pw_i[:, :, None, :]
    cl_i = c_re[None] * pw_i[:, :, None, :] + c_im[None] * pw_r[:, :, None, :]

    def per_slab(a):
        a = a.reshape(a.shape[0], N_HALVES, HALF, a.shape[-1])
        return a.transpose(1, 0, 2, 3)

    sin_r = rev_r[..., None] * bbr[None] - rev_i[..., None] * bbi[None]
    sin_i = rev_r[..., None] * bbi[None] + rev_i[..., None] * bbr[None]
    cs = per_slab(jnp.concatenate([sin_r, sin_i], axis=2).transpose(0, 1, 3, 2))

    cl = per_slab(jnp.concatenate([cl_r, -cl_i], axis=3))
    skip = d_skip.astype(f32).reshape(N_HALVES, 1, HALF)

    exps, keep = [], []
    for k in range(SCAN_STEPS):
        exps += [SSM_Q << k] * SUBLANES
        keep += [0.0] * (1 << k) + [1.0] * (SUBLANES - (1 << k))
    exps += [SSM_Q * j for j in range(SUBLANES)] + [SSM_Q << k for k in range(SCAN_STEPS, n_scan)]
    keep += [1.0] * (SUBLANES + n_scan - SCAN_STEPS)
    tr, ti = lam_pow(exps)
    keep = jnp.asarray(np.asarray(keep, np.float32))[:, None, None]
    rows = jnp.stack([(tr * keep).reshape(-1, N_HALVES, HALF_GROUPS // 2, 2 * SSM_P),
                      (ti * keep).reshape(-1, N_HALVES, HALF_GROUPS // 2, 2 * SSM_P)], axis=3)
    decay = rows.transpose(1, 0, 2, 3, 4).reshape(N_HALVES, len(exps), HALF_STATE)
    return cs, cl, skip, decay


def _expand_half(cs_ref, cl_ref, skip_ref, w1_ref, w2_ref, e):
    r = lax.broadcasted_iota(jnp.int32, (LANES, LANES), 0)
    c = lax.broadcasted_iota(jnp.int32, (LANES, LANES), 1)
    same_group = ((r % HALF) // SSM_H) == ((c % HALF) // SSM_H)
    rs = lax.broadcasted_iota(jnp.int32, (LANES, HALF_STATE), 0)
    cst = lax.broadcasted_iota(jnp.int32, (LANES, HALF_STATE), 1)
    rep_s = ((rs // SSM_P == (cst // LANES) % 2) & (rs % SSM_P == cst % SSM_P)).astype(_BF16)
    diag_s = ((rs % HALF) // SSM_H) == (2 * (cst // (2 * LANES)) + (cst % LANES) // SSM_P)

    none = jnp.zeros((HALF, LANES), _F32)
    b_bar = cs_ref[e, SSM_Q - 1]
    lhs = jnp.concatenate([jnp.concatenate([b_bar, none], axis=1),
                           jnp.concatenate([none, b_bar], axis=1)], axis=0)
    c_lam = lambda d: cl_ref[e, d] if d >= 0 else none
    skip = jnp.concatenate([skip_ref[e], skip_ref[e]], axis=1)
    tiles = []
    for delta in range(SSM_Q // 2):
        rhs = jnp.concatenate(
            [jnp.concatenate([c_lam(2 * delta), c_lam(2 * delta - 1)], axis=1),
             jnp.concatenate([c_lam(2 * delta + 1), c_lam(2 * delta)], axis=1)], axis=0)
        k = lax.dot_general(lhs, rhs, (((1,), (1,)), ((), ())),
                            precision=lax.Precision.HIGHEST, preferred_element_type=_F32)
        if delta == 0:
            k = k + jnp.where(r == c, skip, 0.0)
        tiles.append(jnp.where(same_group, k, 0.0).astype(_BF16))
    zero = jnp.zeros((LANES, LANES), _BF16)
    two_offsets = lambda ref, i: jnp.concatenate([ref[e, i], ref[e, i + 1]], axis=0).astype(_BF16)
    for b in range(SSM_Q // 2):
        rows = slice(b * LANES, (b + 1) * LANES)
        for b2 in range(SSM_Q // 2):
            w1_ref[e, rows, b2 * LANES:(b2 + 1) * LANES] = tiles[b2 - b] if b2 >= b else zero
        w_in = jnp.where(diag_s, jnp.dot(two_offsets(cs_ref, 2 * b), rep_s, preferred_element_type=_F32), 0.0)
        w1_ref[e, rows, HALF_IN:] = w_in.astype(_BF16)
        w_out_t = jnp.where(diag_s, jnp.dot(two_offsets(cl_ref, 2 * b + 1), rep_s,
                                            preferred_element_type=_F32), 0.0)
        w2_ref[e, rows, :] = w_out_t.astype(_BF16)


def _ssm_kernel(x_ref, cs_ref, cl_ref, skip_ref, dec_ref, y_ref, w1_ref, w2_ref, h_ref, t_scr):
    nc = x_ref.shape[0]
    nb = nc // SUBLANES
    piece = 2 * LANES

    def cmul_add(acc_r, acc_i, pr, pi, tr, ti):
        return acc_r + pr * tr - pi * ti, acc_i + pr * ti + pi * tr

    def roll_in_block(a, shift):
        return pltpu.roll(a.reshape(nb, SUBLANES, LANES), shift, axis=1).reshape(nc, LANES)

    @pl.when((pl.program_id(1) == 0) & (pl.program_id(2) == 0))
    def _():
        for e in range(2):
            _expand_half(cs_ref, cl_ref, skip_ref, w1_ref, w2_ref, e)

    @pl.when(pl.program_id(2) == 0)
    def _():
        h_ref[...] = jnp.zeros_like(h_ref)

    xs = [x_ref[:, e * HALF_IN:(e + 1) * HALF_IN] for e in range(2)]
    s_all = [jnp.dot(xs[e], w1_ref[e, :, HALF_IN:], preferred_element_type=_F32) for e in range(2)]
    sub = lax.broadcasted_iota(jnp.int32, (nc, LANES), 0) % SUBLANES
    brow = lax.broadcasted_iota(jnp.int32, (nb, LANES), 0)
    ys = [None, None]
    for e in range(2):
        for p in range(HALF_STATE // piece):
            re, im = slice(p * piece, p * piece + LANES), slice(p * piece + LANES, (p + 1) * piece)
            table = lambda row0, n: (dec_ref[e, row0:row0 + n, re], dec_ref[e, row0:row0 + n, im])
            tall = lambda a: jnp.concatenate([a] * nb, axis=0)
            h_re = slice(e * HALF_STATE + p * piece, e * HALF_STATE + p * piece + LANES)
            h_im = slice(h_re.stop, h_re.stop + LANES)
            slot = 2 * (e * (HALF_STATE // piece) + p)

            sr, si = s_all[e][:, re], s_all[e][:, im]
            hin_r, hin_i = h_ref[0:1, h_re], h_ref[0:1, h_im]

            for k in range(SCAN_STEPS):
                mr, mi = table(k * SUBLANES, SUBLANES)
                sr, si = cmul_add(sr, si, tall(mr), tall(mi),
                                  roll_in_block(sr, 1 << k), roll_in_block(si, 1 << k))

            t_scr[slot] = sr
            t_scr[slot + 1] = si
            br = t_scr[slot, pl.ds(SUBLANES - 1, nb, stride=SUBLANES), :]
            bi = t_scr[slot + 1, pl.ds(SUBLANES - 1, nb, stride=SUBLANES), :]
            pr, pi = table(TABLE_BLOCKS, 1)
            br = br + jnp.where(brow == 0, pr * hin_r - pi * hin_i, 0.0)
            bi = bi + jnp.where(brow == 0, pr * hin_i + pi * hin_r, 0.0)
            for k in range(dec_ref.shape[1] - TABLE_BLOCKS):
                shift = 1 << k
                tr = jnp.where(brow >= shift, pltpu.roll(br, shift, axis=0), 0.0)
                ti = jnp.where(brow >= shift, pltpu.roll(bi, shift, axis=0), 0.0)
                br, bi = cmul_add(br, bi, *table(TABLE_BLOCKS + k, 1), tr, ti)
            h_ref[0:1, h_re] = br[nb - 1:nb]
            h_ref[0:1, h_im] = bi[nb - 1:nb]
            bpr = jnp.where(brow == 0, hin_r, pltpu.roll(br, 1, axis=0))
            bpi = jnp.where(brow == 0, hin_i, pltpu.roll(bi, 1, axis=0))

            rep = lambda a: jnp.concatenate(
                [jnp.broadcast_to(a[k:k + 1], (SUBLANES, LANES)) for k in range(nb)], axis=0)
            fr, fi = table(TABLE_POWERS, SUBLANES)
            tr = jnp.where(sub >= 1, roll_in_block(sr, 1), 0.0)
            ti = jnp.where(sub >= 1, roll_in_block(si, 1), 0.0)
            prev_r, prev_i = cmul_add(tr, ti, tall(fr), tall(fi), rep(bpr), rep(bpi))
            hprev = jnp.concatenate([prev_r, prev_i], axis=1).astype(_BF16)
            part = lax.dot_general(hprev, w2_ref[e, :, p * piece:(p + 1) * piece], (((1,), (1,)), ((), ())),
                                   preferred_element_type=_F32)
            if ys[e] is None:
                mid = HALF_IN // 2
                ys[e] = jnp.concatenate(
                    [jnp.dot(xs[e][:, :mid], w1_ref[e, :mid, :mid], preferred_element_type=_F32),
                     jnp.dot(xs[e], w1_ref[e, :, mid:HALF_IN], preferred_element_type=_F32)], axis=1) + part
            else:
                ys[e] = ys[e] + part

    low = lax.broadcasted_iota(jnp.int32, (nc, LANES), 1) < HALF
    for b in range(SSM_Q // 2):
        y0, y1 = (ys[e][:, b * LANES:(b + 1) * LANES] for e in range(2))
        y_ref[pl.ds(2 * b, nc, stride=SSM_Q), :] = jnp.where(low, y0, _swap_halves(y1))
        y_ref[pl.ds(2 * b + 1, nc, stride=SSM_Q), :] = jnp.where(low, _swap_halves(y0), y1)


def _ssm(ucat, cs, cl, skip, decay, batch, seq, nc):
    tiles = seq // (nc * SSM_Q)
    pair = lambda a: pl.BlockSpec((2,) + a.shape[1:], lambda s, b, t: (s,) + (0,) * (a.ndim - 1))
    return pl.pallas_call(
        _ssm_kernel,
        grid=(N_SLABS, batch, tiles),
        in_specs=[pl.BlockSpec((nc, SLAB_IN), lambda s, b, t: (b * tiles + t, s)),
                  pair(cs), pair(cl), pair(skip), pair(decay)],
        out_specs=pl.BlockSpec((nc * SSM_Q, LANES), lambda s, b, t: (b * tiles + t, s)),
        out_shape=jax.ShapeDtypeStruct((batch * seq, SSM_W), _F32),
        scratch_shapes=[pltpu.VMEM((2, HALF_IN, HALF_IN + HALF_STATE), _BF16),
                        pltpu.VMEM((2, HALF_IN, HALF_STATE), _BF16),
                        pltpu.VMEM((SUBLANES, SLAB_STATE), _F32),
                        pltpu.VMEM((SLAB_STATE // LANES, nc, LANES), _F32)],
        compiler_params=pltpu.CompilerParams(
            dimension_semantics=("arbitrary", "arbitrary", "arbitrary"),
            vmem_limit_bytes=VMEM_LIMIT),
        name="s5_mixer",
    )(ucat, cs, cl, skip, decay)


def _gelu_tanh(y):
    c = math.sqrt(2.0 / math.pi)
    return (0.5 * y) * (1.0 + jnp.tanh(y * (c + (c * 0.044715) * (y * y))))


def _mix_out_kernel(sinks_ref, x_ref, q_ref, kp_ref, kc_ref, vp_ref, vc_ref, za_ref, y_ref, zs_ref,
                    wg_ref, bg_ref, wo_ref, out_ref, acc_ref, *, tiles_per_seq):
    scores, vts = _attn_scores(q_ref, kp_ref, kc_ref, vp_ref, vc_ref)

    y = _gelu_tanh(y_ref[...].astype(_F32))
    gate = jnp.dot(y.astype(_BF16), wg_ref[...], preferred_element_type=_F32) + bg_ref[...]
    o = y * jax.nn.sigmoid(gate) * zs_ref[...].astype(_F32)
    rs = lax.rsqrt(jnp.mean(o * o, axis=-1, keepdims=True) + NORM_EPS)
    out_ref[...] = x_ref[...] + rs * jnp.dot(o.astype(_BF16), wo_ref[ATTN_W:, :], preferred_element_type=_F32)

    seq_start = pl.program_id(0) % tiles_per_seq == 0
    _attn_softmax_pv(sinks_ref, scores, vts, acc_ref, seq_start, range(acc_ref.shape[0] // BLOCK))
    g = acc_ref[...] * za_ref[...].astype(_F32)
    rs = lax.rsqrt(jnp.mean(g * g, axis=-1, keepdims=True) + NORM_EPS)
    out_ref[...] += rs * jnp.dot(g.astype(_BF16), wo_ref[:ATTN_W, :], preferred_element_type=_F32)


def _mix_out(sinks, x2, q, k, v, za, y, zs, wg, bg, wo, seq, tm):
    t = x2.shape[0]
    blocks = tm // BLOCK
    row = lambda w: pl.BlockSpec((tm, w), lambda i: (i, 0))
    prev = lambda w: pl.BlockSpec((BLOCK, w), lambda i: (jnp.maximum(i * blocks - 1, 0), 0))
    return pl.pallas_call(
        functools.partial(_mix_out_kernel, tiles_per_seq=seq // tm),
        grid=(t // tm,),
        in_specs=[pl.BlockSpec(memory_space=pltpu.SMEM), row(D_MODEL), row(ATTN_W), prev(KV_W), row(KV_W),
                  prev(KV_W), row(KV_W), row(ATTN_W), row(SSM_W), row(SSM_W),
                  _resident(wg.shape), _resident(bg.shape), _resident(wo.shape)],
        out_specs=row(D_MODEL),
        out_shape=jax.ShapeDtypeStruct((t, D_MODEL), x2.dtype),
        scratch_shapes=[pltpu.VMEM((tm, ATTN_W), _F32)],
        compiler_params=pltpu.CompilerParams(dimension_semantics=("arbitrary",),
                                             vmem_limit_bytes=VMEM_LIMIT),
        name="mix_out",
    )(sinks, x2, q, k, k, v, v, za, y, zs, wg, bg, wo)


def _pick_tile(n, target, multiple):
    best = multiple
    for cand in range(multiple, min(n, target) + 1, multiple):
        if n % cand == 0:
            best = cand
    return best


def kernel(x, positions, norm_w, w_in, q_norm_w, k_norm_w, sinks, a_re, a_im, log_step,
           b_re, b_im, c_re, c_im, d_skip, w_glu, b_glu, attn_out_norm_w, ssm_out_norm_w, w_out):
    batch, seq, _ = x.shape
    assert seq % BLOCK == 0 and x.shape[2] == D_MODEL and positions.shape == (batch, seq)
    assert w_in.shape == (D_MODEL, IN_W) and w_out.shape == (ATTN_W + SSM_W, D_MODEL)
    assert a_re.shape == (SSM_G, SSM_P) and b_re.shape == (SSM_G, SSM_P, SSM_H) and c_re.shape == (SSM_G, SSM_H, SSM_P)
    assert sinks.shape == (N_HEADS,) and w_glu.shape == (SSM_W, SSM_W)
    t = batch * seq
    f32 = _F32

    w_in_k = (norm_w.astype(f32)[:, None] * w_in.astype(f32)).astype(_BF16)
    out_gain = jnp.concatenate([attn_out_norm_w.astype(f32), ssm_out_norm_w.astype(f32)])
    w_out_k = (out_gain[:, None] * w_out.astype(f32)).astype(_BF16)
    log2e = math.log2(math.e)
    qw = (jnp.tile(q_norm_w.astype(f32), N_HEADS) * (log2e / math.sqrt(HEAD_DIM)))[None, :]
    kw = jnp.tile(k_norm_w.astype(f32), N_KV_HEADS)[None, :]
    inv_freq = ROPE_THETA ** (-jnp.arange(0, HEAD_DIM, 2, dtype=f32) / HEAD_DIM)
    invf = jnp.tile(inv_freq, LANES // (HEAD_DIM // 2))[None, :]
    sgn = jnp.tile(jnp.concatenate([-jnp.ones(HEAD_DIM // 2, f32), jnp.ones(HEAD_DIM // 2, f32)]),
                   LANES // HEAD_DIM)[None, :]

    x2 = x.reshape(t, D_MODEL)
    pos2 = positions.reshape(t // LANES, LANES).astype(jnp.int32)

    tm = _pick_tile(seq, 512, BLOCK)
    q, k, v, za, ucat, zs = _in_proj(x2, pos2, w_in_k, qw, kw, invf, sgn, tm)


    nc = _pick_tile(seq // SSM_Q, 512, 16)
    assert nc & (nc - 1) == 0, "chunk rows per tile must be a power of two for the scan"
    cs, cl, skip, decay = _ssm_weights(a_re, a_im, log_step, b_re, b_im, c_re, c_im, d_skip,
                                     n_scan=int(math.log2(nc)))
    y = _ssm(ucat, cs, cl, skip, decay, batch, seq, nc)

    out = _mix_out(sinks.astype(f32) * log2e, x2, q, k, v, za, y, zs, w_glu.astype(_BF16),
                   b_glu.astype(f32)[None, :], w_out_k, seq, tm)
    return out.reshape(x.shape)
```

```python
import functools
import math

import jax
import jax.numpy as jnp
import numpy as np
from jax import lax
from jax.experimental import pallas as pl
from jax.experimental.pallas import tpu as pltpu

D_MODEL = 2048
ATTN_W = 1024
KV_W = 256
HEAD_DIM = 64
N_HEADS = 16
N_KV_HEADS = 4
KV_REP = N_HEADS // N_KV_HEADS
SSM_W = 1024
SSM_H = 16
SSM_G = 64
SSM_P = 64
BLOCK = 128
ROPE_THETA = 10000.0
NORM_EPS = 1e-6
IN_W = 2 * ATTN_W + 2 * KV_W + 2 * SSM_W

LANES = 128
SUBLANES = 8
MXU_DIM = 256
SSM_Q = 8
N_SLABS = SSM_W // LANES
HALF = LANES // 2
HALF_GROUPS = HALF // SSM_H
N_HALVES = SSM_W // HALF
HALF_IN = SSM_Q * HALF
HALF_STATE = 2 * HALF_GROUPS * SSM_P
SLAB_IN = 2 * HALF_IN
SLAB_STATE = 2 * HALF_STATE
SCAN_STEPS = 3
TABLE_POWERS = SCAN_STEPS * SUBLANES
TABLE_BLOCKS = TABLE_POWERS + SUBLANES
VMEM_LIMIT = 56 * 1024 * 1024

_BF16 = jnp.bfloat16
_F32 = jnp.float32


def _resident(shape):
    nd = len(shape)
    return pl.BlockSpec(shape, lambda *_: (0,) * nd, pipeline_mode=pl.Buffered(1))


def _swap_halves(a):
    return pltpu.roll(a, HEAD_DIM, axis=1)


def _pair_slabs(n0, n1, n2, n3):
    low = lax.broadcasted_iota(jnp.int32, n0.shape, 1) < HEAD_DIM
    return (jnp.where(low, n0, _swap_halves(n2)), jnp.where(low, _swap_halves(n0), n2),
            jnp.where(low, n1, _swap_halves(n3)), jnp.where(low, _swap_halves(n1), n3))


def _rope(t, cos, sin_signed):
    lane = lax.broadcasted_iota(jnp.int32, t.shape, 1)
    first_half = (lane % HEAD_DIM) < (HEAD_DIM // 2)
    swapped = jnp.where(first_half,
                        pltpu.roll(t, LANES - HEAD_DIM // 2, axis=1),
                        pltpu.roll(t, HEAD_DIM // 2, axis=1))
    return t * cos + swapped * sin_signed


def _in_proj_kernel(x_ref, pos_ref, w_ref, qw_ref, kw_ref, invf_ref, sgn_ref,
                    q_ref, k_ref, v_ref, za_ref, ucat_ref, zs_ref, hn_ref, u_ref):
    tm = x_ref.shape[0]
    x = x_ref[...]
    hn_ref[...] = x.astype(_BF16)
    rs = jnp.broadcast_to(lax.rsqrt(jnp.mean(x * x, axis=-1, keepdims=True) + NORM_EPS), (tm, LANES))
    wide = 2 * MXU_DIM
    rs_wide = jnp.concatenate([rs] * (wide // LANES), axis=1)

    nsub = tm // LANES
    ang = jnp.concatenate(
        [jnp.broadcast_to(pos_ref[pl.ds(pl.program_id(0) * nsub + j, 1), :].astype(_F32),
                          (LANES, LANES)).T for j in range(nsub)], axis=0) * invf_ref[...]
    cos = jnp.cos(ang)
    sin_signed = jnp.sin(ang) * sgn_ref[...]

    def proj(c0):
        return jnp.dot(hn_ref[...], w_ref[:, c0:c0 + wide], preferred_element_type=_F32) * rs_wide

    def normed_rope(acc, nw):
        low = lax.broadcasted_iota(jnp.int32, (tm, LANES), 1) < HEAD_DIM
        out = []
        for s in range(MXU_DIM // LANES):
            a = acc[:, s * LANES:(s + 1) * LANES]
            sq = a * a
            ssq = jnp.where(low, jnp.sum(jnp.where(low, sq, 0.0), axis=-1, keepdims=True),
                            jnp.sum(jnp.where(low, 0.0, sq), axis=-1, keepdims=True))
            t = a * lax.rsqrt(ssq * (1.0 / HEAD_DIM) + NORM_EPS) * nw[:, s * LANES:(s + 1) * LANES]
            out.append(_rope(t, cos, sin_signed))
        return out

    base = 2 * ATTN_W + 2 * KV_W
    nc = tm // SSM_Q
    for c in range(SSM_W // wide):
        acc = proj(base + c * wide)
        for s in range(wide // LANES):
            u_ref[c * (wide // LANES) + s] = acc[:, s * LANES:(s + 1) * LANES]
    low_half = lax.broadcasted_iota(jnp.int32, (nc, LANES), 1) < HALF
    for s in range(N_SLABS):
        for b in range(SSM_Q // 2):
            even = u_ref[s, pl.ds(2 * b, nc, stride=SSM_Q), :]
            odd = u_ref[s, pl.ds(2 * b + 1, nc, stride=SSM_Q), :]
            c0 = s * SLAB_IN + b * LANES
            ucat_ref[:, c0:c0 + LANES] = jnp.where(low_half, even, _swap_halves(odd)).astype(_BF16)
            c0 += HALF_IN
            ucat_ref[:, c0:c0 + LANES] = jnp.where(low_half, _swap_halves(even), odd).astype(_BF16)

    for c in range(ATTN_W // wide):
        acc = proj(c * wide)
        slabs = []
        for h in range(wide // MXU_DIM):
            n0 = c * wide + h * MXU_DIM
            slabs += normed_rope(acc[:, h * MXU_DIM:(h + 1) * MXU_DIM], qw_ref[:, n0:n0 + MXU_DIM])
        for r, slab in enumerate(_pair_slabs(*slabs)):
            q_ref[:, c * wide + r * LANES:c * wide + (r + 1) * LANES] = slab.astype(_BF16)
    acc = proj(ATTN_W)
    for s, slab in enumerate(normed_rope(acc[:, :KV_W], kw_ref[...])):
        k_ref[:, s * LANES:(s + 1) * LANES] = slab.astype(_BF16)
    v_ref[...] = acc[:, KV_W:].astype(_BF16)
    for out_ref, base in ((za_ref, ATTN_W + 2 * KV_W), (zs_ref, 2 * ATTN_W + 2 * KV_W + SSM_W)):
        for c in range(ATTN_W // wide):
            z = proj(base + c * wide)
            out_ref[:, c * wide:(c + 1) * wide] = (z * jax.nn.sigmoid(z)).astype(_BF16)


def _in_proj(x2, pos2, w_in, qw, kw, invf, sgn, tm):
    t = x2.shape[0]
    row = lambda w: pl.BlockSpec((tm, w), lambda i: (i, 0))
    return pl.pallas_call(
        _in_proj_kernel,
        grid=(t // tm,),
        in_specs=[row(D_MODEL), _resident(pos2.shape), _resident(w_in.shape), _resident(qw.shape),
                  _resident(kw.shape), _resident(invf.shape), _resident(sgn.shape)],
        out_specs=[row(ATTN_W), row(KV_W), row(KV_W), row(ATTN_W),
                   pl.BlockSpec((tm // SSM_Q, N_SLABS * SLAB_IN), lambda i: (i, 0)), row(SSM_W)],
        out_shape=[jax.ShapeDtypeStruct((t, ATTN_W), _BF16),
                   jax.ShapeDtypeStruct((t, KV_W), _BF16),
                   jax.ShapeDtypeStruct((t, KV_W), _BF16),
                   jax.ShapeDtypeStruct((t, ATTN_W), _BF16),
                   jax.ShapeDtypeStruct((t // SSM_Q, N_SLABS * SLAB_IN), _BF16),
                   jax.ShapeDtypeStruct((t, SSM_W), _BF16)],
        scratch_shapes=[pltpu.VMEM((tm, D_MODEL), _BF16), pltpu.VMEM((N_SLABS, tm, LANES), _F32)],
        compiler_params=pltpu.CompilerParams(dimension_semantics=("arbitrary",),
                                             vmem_limit_bytes=VMEM_LIMIT),
        name="in_proj",
    )(x2, pos2, w_in, qw, kw, invf, sgn)


def _attn_scores(q_ref, kp_ref, kc_ref, vp_ref, vc_ref):
    blocks = q_ref.shape[0] // BLOCK
    two = 2 * BLOCK
    lane_kv = lax.broadcasted_iota(jnp.int32, (two, LANES), 1)
    halves = [lane_kv < HEAD_DIM, lane_kv >= HEAD_DIM]
    zero = jnp.zeros((two, LANES), _BF16)
    key_head = lax.broadcasted_iota(jnp.int32, (2 * SUBLANES, 2 * two), 1) // two
    count_rows = (lax.broadcasted_iota(jnp.int32, (2 * SUBLANES, 2 * two), 0) == key_head).astype(_BF16)

    def keys(prev_ref, cur_ref, b, sl):
        if b == 0:
            return jnp.concatenate([prev_ref[:, sl], cur_ref[0:BLOCK, sl]], axis=0)
        return cur_ref[(b - 1) * BLOCK:(b + 1) * BLOCK, sl]

    scores, vts = {}, {}
    for b in range(blocks):
        rows = slice(b * BLOCK, (b + 1) * BLOCK)
        for j in range(N_KV_HEADS // 2):
            sl = slice(j * LANES, (j + 1) * LANES)
            k2 = keys(kp_ref, kc_ref, b, sl)
            v2 = keys(vp_ref, vc_ref, b, sl)
            vts[b, j] = jnp.concatenate(
                [jnp.concatenate([jnp.where(keep, v2, zero) for keep in halves], axis=0).T,
                 count_rows], axis=0)
            for t in range(KV_REP // 2):
                r0 = KV_REP * j + 2 * t
                qq = q_ref[rows, r0 * LANES:(r0 + 2) * LANES]
                qq = jnp.concatenate([qq[:, :LANES], qq[:, LANES:]], axis=0)
                for half, keep in enumerate(halves):
                    scores[b, j, t, half] = lax.dot_general(
                        jnp.where(keep, k2, zero), qq, (((1,), (1,)), ((), ())),
                        preferred_element_type=_F32)
    return scores, vts


def _attn_softmax_pv(sinks_ref, scores, vts, acc_ref, seq_start, blocks):
    two = 2 * BLOCK
    ik = lax.broadcasted_iota(jnp.int32, (two, two), 0)
    iq = lax.broadcasted_iota(jnp.int32, (two, two), 1) % BLOCK
    rel = ik - iq
    band = (rel >= 1) & (rel <= BLOCK)
    band_first = band & ((ik >= BLOCK) | jnp.logical_not(seq_start))
    first_slab = lax.broadcasted_iota(jnp.int32, (1, two), 1) < BLOCK
    row_o = lax.broadcasted_iota(jnp.int32, (LANES, two), 0)
    for b in blocks:
        rows = slice(b * BLOCK, (b + 1) * BLOCK)
        allowed = band_first if b == 0 else band
        for j in range(N_KV_HEADS // 2):
            for t in range(KV_REP // 2):
                r0 = KV_REP * j + 2 * t
                probs, sink_terms = [], []
                for half in range(2):
                    h0 = 2 * KV_REP * j + KV_REP * half + 2 * t
                    sink = jnp.where(first_slab, sinks_ref[h0], sinks_ref[h0 + 1])
                    s = jnp.where(allowed, scores[b, j, t, half], -1e30)
                    m = jnp.maximum(jnp.max(s, axis=0, keepdims=True), sink)
                    probs.append(jnp.exp2(s - m).astype(_BF16))
                    sink_terms.append(jnp.exp2(sink - m))
                o_t = jnp.dot(vts[b, j], jnp.concatenate(probs, axis=0), preferred_element_type=_F32)
                rdens = [1.0 / (o_t[LANES + half:LANES + half + 1] + sink_terms[half]) for half in range(2)]
                o_t = o_t[:LANES] * jnp.where(row_o < HEAD_DIM, rdens[0], rdens[1])
                for half in range(2):
                    heads = slice(half * HEAD_DIM, (half + 1) * HEAD_DIM)
                    slab = KV_REP * j + (KV_REP // 2) * half + t
                    acc_ref[rows, slab * LANES:(slab + 1) * LANES] = jnp.concatenate(
                        [o_t[heads, :BLOCK], o_t[heads, BLOCK:]], axis=0).T


def _ssm_weights(a_re, a_im, log_step, b_re, b_im, c_re, c_im, d_skip, n_scan):
    f32 = _F32
    exps, keep = [], []
    for k in range(SCAN_STEPS):
        exps += [SSM_Q << k] * SUBLANES
        keep += [0.0] * (1 << k) + [1.0] * (SUBLANES - (1 << k))
    exps += [SSM_Q * j for j in range(SUBLANES)] + [SSM_Q << k for k in range(SCAN_STEPS, n_scan)]
    keep += [1.0] * (SUBLANES + n_scan - SCAN_STEPS)
    n_rows = len(exps)
    row_consts = jnp.asarray(np.stack([np.repeat(np.asarray(v, np.float32)[:, None], HALF_STATE, axis=1)
                                       for v in (exps, keep)]))

    lam = jnp.stack([a_re.astype(f32), a_im.astype(f32),
                     jnp.broadcast_to(log_step.astype(f32)[:, None], (SSM_G, SSM_P))])
    lam_lanes = jnp.broadcast_to(lam.reshape(3, N_HALVES, HALF_GROUPS // 2, 1, 2 * SSM_P),
                                 (3, N_HALVES, HALF_GROUPS // 2, 2, 2 * SSM_P))
    lam_lanes = lam_lanes.reshape(3, N_HALVES, HALF_STATE).transpose(1, 0, 2)
    lam_rows = jnp.concatenate([lam, lam], axis=2).reshape(3, N_HALVES, HALF_GROUPS, LANES)
    bt = jnp.stack([b_re, b_im]).astype(f32).transpose(0, 1, 3, 2)
    bt = jnp.concatenate([bt, bt], axis=3).reshape(2, N_HALVES, HALF, LANES)
    ct = jnp.stack([c_re, c_im]).astype(f32)
    ct = jnp.concatenate([ct, ct], axis=3).reshape(2, N_HALVES, HALF, LANES)
    skip = d_skip.astype(f32).reshape(N_HALVES, 1, HALF)

    def per_half(shape):
        return pl.BlockSpec((shape[0], 1) + shape[2:], lambda e: (0, e) + (0,) * (len(shape) - 2))

    cs, cl, decay = pl.pallas_call(
        _ssm_tables_kernel,
        grid=(N_HALVES,),
        in_specs=[per_half(lam_rows.shape), pl.BlockSpec((1, 3, HALF_STATE), lambda e: (e, 0, 0)),
                  per_half(bt.shape), per_half(ct.shape), _resident(row_consts.shape)],
        out_specs=[pl.BlockSpec((1, SSM_Q, HALF, LANES), lambda e: (e, 0, 0, 0)),
                   pl.BlockSpec((1, SSM_Q + 1, HALF, LANES), lambda e: (e, 0, 0, 0)),
                   pl.BlockSpec((1, n_rows, HALF_STATE), lambda e: (e, 0, 0))],
        out_shape=[jax.ShapeDtypeStruct((N_HALVES, SSM_Q, HALF, LANES), f32),
                   jax.ShapeDtypeStruct((N_HALVES, SSM_Q + 1, HALF, LANES), f32),
                   jax.ShapeDtypeStruct((N_HALVES, n_rows, HALF_STATE), f32)],
        compiler_params=pltpu.CompilerParams(dimension_semantics=("arbitrary",)),
        name="s5_tables",
    )(lam_rows, lam_lanes, bt, ct, row_consts)
    return cs, cl, skip, decay


def _ssm_tables_kernel(lam_ref, lanes_ref, b_ref, c_ref, rows_ref, cs_ref, cl_ref, dec_ref):
    a_re, a_im = lam_ref[0, 0], lam_ref[1, 0]
    delta = jnp.exp(lam_ref[2, 0])
    zr, zi = a_re * delta, a_im * delta

    def group_rows(a):
        return jnp.concatenate([jnp.broadcast_to(a[g:g + 1], (SSM_H, LANES)) for g in range(HALF_GROUPS)], axis=0)

    def lam_pow(d):
        mag = jnp.exp(float(d) * zr)
        return mag * jnp.cos(float(d) * zi), mag * jnp.sin(float(d) * zi)

    lr, li = lam_pow(1)
    den = a_re * a_re + a_im * a_im
    fr = group_rows(((lr - 1.0) * a_re + li * a_im) / den)
    fi = group_rows((li * a_re - (lr - 1.0) * a_im) / den)
    b_re, b_im = b_ref[0, 0], b_ref[1, 0]
    bbr = fr * b_re - fi * b_im
    bbi = fr * b_im + fi * b_re
    c_re, c_im = c_ref[0, 0], c_ref[1, 0]
    re_lanes = lax.broadcasted_iota(jnp.int32, (HALF, LANES), 1) < SSM_P

    for d in range(SSM_Q + 1):
        pr, pi = lam_pow(d)
        pr, pi = group_rows(pr), group_rows(pi)
        cl_ref[0, d] = jnp.where(re_lanes, c_re * pr - c_im * pi, -(c_re * pi + c_im * pr))
        if d < SSM_Q:
            cs_ref[0, SSM_Q - 1 - d] = jnp.where(re_lanes, pr * bbr - pi * bbi, pr * bbi + pi * bbr)

    a_re, a_im = lanes_ref[0, 0:1, :], lanes_ref[0, 1:2, :]
    delta = jnp.exp(lanes_ref[0, 2:3, :])
    m, keep = rows_ref[0], rows_ref[1]
    mag = jnp.exp(m * (a_re * delta))
    ang = m * (a_im * delta)
    im_lanes = (lax.broadcasted_iota(jnp.int32, ang.shape, 1) // LANES) % 2 == 1
    dec_ref[0] = mag * jnp.where(im_lanes, jnp.sin(ang), jnp.cos(ang)) * keep


def _expand_half(cs_ref, cl_ref, skip_ref, w1_ref, w2_ref, e):
    r = lax.broadcasted_iota(jnp.int32, (LANES, LANES), 0)
    c = lax.broadcasted_iota(jnp.int32, (LANES, LANES), 1)
    same_group = ((r % HALF) // SSM_H) == ((c % HALF) // SSM_H)
    rs = lax.broadcasted_iota(jnp.int32, (LANES, HALF_STATE), 0)
    cst = lax.broadcasted_iota(jnp.int32, (LANES, HALF_STATE), 1)
    rep_s = ((rs // SSM_P == (cst // LANES) % 2) & (rs % SSM_P == cst % SSM_P)).astype(_BF16)
    diag_s = ((rs % HALF) // SSM_H) == (2 * (cst // (2 * LANES)) + (cst % LANES) // SSM_P)

    none = jnp.zeros((HALF, LANES), _F32)
    b_bar = cs_ref[e, SSM_Q - 1]
    lhs = jnp.concatenate([jnp.concatenate([b_bar, none], axis=1),
                           jnp.concatenate([none, b_bar], axis=1)], axis=0)
    c_lam = lambda d: cl_ref[e, d] if d >= 0 else none
    skip = jnp.concatenate([skip_ref[e], skip_ref[e]], axis=1)
    tiles = []
    for delta in range(SSM_Q // 2):
        rhs = jnp.concatenate(
            [jnp.concatenate([c_lam(2 * delta), c_lam(2 * delta - 1)], axis=1),
             jnp.concatenate([c_lam(2 * delta + 1), c_lam(2 * delta)], axis=1)], axis=0)
        k = lax.dot_general(lhs, rhs, (((1,), (1,)), ((), ())),
                            precision=lax.Precision.HIGHEST, preferred_element_type=_F32)
        if delta == 0:
            k = k + jnp.where(r == c, skip, 0.0)
        tiles.append(jnp.where(same_group, k, 0.0).astype(_BF16))
    zero = jnp.zeros((LANES, LANES), _BF16)
    two_offsets = lambda ref, i: jnp.concatenate([ref[e, i], ref[e, i + 1]], axis=0).astype(_BF16)
    for b in range(SSM_Q // 2):
        rows = slice(b * LANES, (b + 1) * LANES)
        for b2 in range(SSM_Q // 2):
            w1_ref[e, rows, b2 * LANES:(b2 + 1) * LANES] = tiles[b2 - b] if b2 >= b else zero
        w_in = jnp.where(diag_s, jnp.dot(two_offsets(cs_ref, 2 * b), rep_s, preferred_element_type=_F32), 0.0)
        w1_ref[e, rows, HALF_IN:] = w_in.astype(_BF16)
        w_out_t = jnp.where(diag_s, jnp.dot(two_offsets(cl_ref, 2 * b + 1), rep_s,
                                            preferred_element_type=_F32), 0.0)
        w2_ref[e, rows, :] = w_out_t.astype(_BF16)


def _ssm_kernel(x_ref, cs_ref, cl_ref, skip_ref, dec_ref, y_ref, w1_ref, w2_ref, h_ref, t_scr):
    nc = x_ref.shape[0]
    nb = nc // SUBLANES
    piece = 2 * LANES

    def cmul_add(acc_r, acc_i, pr, pi, tr, ti):
        return acc_r + pr * tr - pi * ti, acc_i + pr * ti + pi * tr

    def roll_in_block(a, shift):
        return pltpu.roll(a.reshape(nb, SUBLANES, LANES), shift, axis=1).reshape(nc, LANES)

    @pl.when((pl.program_id(1) == 0) & (pl.program_id(2) == 0))
    def _():
        for e in range(2):
            _expand_half(cs_ref, cl_ref, skip_ref, w1_ref, w2_ref, e)

    @pl.when(pl.program_id(2) == 0)
    def _():
        h_ref[...] = jnp.zeros_like(h_ref)

    xs = [x_ref[:, e * HALF_IN:(e + 1) * HALF_IN] for e in range(2)]
    s_all = [jnp.dot(xs[e], w1_ref[e, :, HALF_IN:], preferred_element_type=_F32) for e in range(2)]
    sub = lax.broadcasted_iota(jnp.int32, (nc, LANES), 0) % SUBLANES
    brow = lax.broadcasted_iota(jnp.int32, (nb, LANES), 0)
    ys = [None, None]
    for e in range(2):
        for p in range(HALF_STATE // piece):
            re, im = slice(p * piece, p * piece + LANES), slice(p * piece + LANES, (p + 1) * piece)
            table = lambda row0, n: (dec_ref[e, row0:row0 + n, re], dec_ref[e, row0:row0 + n, im])
            tall = lambda a: jnp.concatenate([a] * nb, axis=0)
            h_re = slice(e * HALF_STATE + p * piece, e * HALF_STATE + p * piece + LANES)
            h_im = slice(h_re.stop, h_re.stop + LANES)
            slot = 2 * (e * (HALF_STATE // piece) + p)

            sr, si = s_all[e][:, re], s_all[e][:, im]
            hin_r, hin_i = h_ref[0:1, h_re], h_ref[0:1, h_im]

            for k in range(SCAN_STEPS):
                mr, mi = table(k * SUBLANES, SUBLANES)
                sr, si = cmul_add(sr, si, tall(mr), tall(mi),
                                  roll_in_block(sr, 1 << k), roll_in_block(si, 1 << k))

            t_scr[slot] = sr
            t_scr[slot + 1] = si
            br = t_scr[slot, pl.ds(SUBLANES - 1, nb, stride=SUBLANES), :]
            bi = t_scr[slot + 1, pl.ds(SUBLANES - 1, nb, stride=SUBLANES), :]
            pr, pi = table(TABLE_BLOCKS, 1)
            br = br + jnp.where(brow == 0, pr * hin_r - pi * hin_i, 0.0)
            bi = bi + jnp.where(brow == 0, pr * hin_i + pi * hin_r, 0.0)
            for k in range(dec_ref.shape[1] - TABLE_BLOCKS):
                shift = 1 << k
                tr = jnp.where(brow >= shift, pltpu.roll(br, shift, axis=0), 0.0)
                ti = jnp.where(brow >= shift, pltpu.roll(bi, shift, axis=0), 0.0)
                br, bi = cmul_add(br, bi, *table(TABLE_BLOCKS + k, 1), tr, ti)
            h_ref[0:1, h_re] = br[nb - 1:nb]
            h_ref[0:1, h_im] = bi[nb - 1:nb]
            bpr = jnp.where(brow == 0, hin_r, pltpu.roll(br, 1, axis=0))
            bpi = jnp.where(brow == 0, hin_i, pltpu.roll(bi, 1, axis=0))

            rep = lambda a: jnp.concatenate(
                [jnp.broadcast_to(a[k:k + 1], (SUBLANES, LANES)) for k in range(nb)], axis=0)
            fr, fi = table(TABLE_POWERS, SUBLANES)
            tr = jnp.where(sub >= 1, roll_in_block(sr, 1), 0.0)
            ti = jnp.where(sub >= 1, roll_in_block(si, 1), 0.0)
            prev_r, prev_i = cmul_add(tr, ti, tall(fr), tall(fi), rep(bpr), rep(bpi))
            hprev = jnp.concatenate([prev_r, prev_i], axis=1).astype(_BF16)
            part = lax.dot_general(hprev, w2_ref[e, :, p * piece:(p + 1) * piece], (((1,), (1,)), ((), ())),
                                   preferred_element_type=_F32)
            if ys[e] is None:
                mid = HALF_IN // 2
                ys[e] = jnp.concatenate(
                    [jnp.dot(xs[e][:, :mid], w1_ref[e, :mid, :mid], preferred_element_type=_F32),
                     jnp.dot(xs[e], w1_ref[e, :, mid:HALF_IN], preferred_element_type=_F32)], axis=1) + part
            else:
                ys[e] = ys[e] + part

    low = lax.broadcasted_iota(jnp.int32, (nc, LANES), 1) < HALF
    for b in range(SSM_Q // 2):
        y0, y1 = (ys[e][:, b * LANES:(b + 1) * LANES] for e in range(2))
        y_ref[pl.ds(2 * b, nc, stride=SSM_Q), :] = jnp.where(low, y0, _swap_halves(y1))
        y_ref[pl.ds(2 * b + 1, nc, stride=SSM_Q), :] = jnp.where(low, _swap_halves(y0), y1)


def _ssm(ucat, cs, cl, skip, decay, batch, seq, nc):
    tiles = seq // (nc * SSM_Q)
    pair = lambda a: pl.BlockSpec((2,) + a.shape[1:], lambda s, b, t: (s,) + (0,) * (a.ndim - 1))
    return pl.pallas_call(
        _ssm_kernel,
        grid=(N_SLABS, batch, tiles),
        in_specs=[pl.BlockSpec((nc, SLAB_IN), lambda s, b, t: (b * tiles + t, s)),
                  pair(cs), pair(cl), pair(skip), pair(decay)],
        out_specs=pl.BlockSpec((nc * SSM_Q, LANES), lambda s, b, t: (b * tiles + t, s)),
        out_shape=jax.ShapeDtypeStruct((batch * seq, SSM_W), _F32),
        scratch_shapes=[pltpu.VMEM((2, HALF_IN, HALF_IN + HALF_STATE), _BF16),
                        pltpu.VMEM((2, HALF_IN, HALF_STATE), _BF16),
                        pltpu.VMEM((SUBLANES, SLAB_STATE), _F32),
                        pltpu.VMEM((SLAB_STATE // LANES, nc, LANES), _F32)],
        compiler_params=pltpu.CompilerParams(
            dimension_semantics=("arbitrary", "arbitrary", "arbitrary"),
            vmem_limit_bytes=VMEM_LIMIT),
        name="s5_mixer",
    )(ucat, cs, cl, skip, decay)


def _gelu_tanh(y):
    c = math.sqrt(2.0 / math.pi)
    return (0.5 * y) * (1.0 + jnp.tanh(y * (c + (c * 0.044715) * (y * y))))


def _mix_out_kernel(sinks_ref, x_ref, q_ref, kp_ref, kc_ref, vp_ref, vc_ref, za_ref, y_ref, zs_ref,
                    wg_ref, bg_ref, wo_ref, out_ref, acc_ref, *, tiles_per_seq):
    scores, vts = _attn_scores(q_ref, kp_ref, kc_ref, vp_ref, vc_ref)

    y = _gelu_tanh(y_ref[...].astype(_F32))
    gate = jnp.dot(y.astype(_BF16), wg_ref[...], preferred_element_type=_F32) + bg_ref[...]
    o = y * jax.nn.sigmoid(gate) * zs_ref[...].astype(_F32)
    rs = lax.rsqrt(jnp.mean(o * o, axis=-1, keepdims=True) + NORM_EPS)
    out_ref[...] = x_ref[...] + rs * jnp.dot(o.astype(_BF16), wo_ref[ATTN_W:, :], preferred_element_type=_F32)

    seq_start = pl.program_id(0) % tiles_per_seq == 0
    _attn_softmax_pv(sinks_ref, scores, vts, acc_ref, seq_start, range(acc_ref.shape[0] // BLOCK))
    g = acc_ref[...] * za_ref[...].astype(_F32)
    rs = lax.rsqrt(jnp.mean(g * g, axis=-1, keepdims=True) + NORM_EPS)
    out_ref[...] += rs * jnp.dot(g.astype(_BF16), wo_ref[:ATTN_W, :], preferred_element_type=_F32)


def _mix_out(sinks, x2, q, k, v, za, y, zs, wg, bg, wo, seq, tm):
    t = x2.shape[0]
    blocks = tm // BLOCK
    row = lambda w: pl.BlockSpec((tm, w), lambda i: (i, 0))
    prev = lambda w: pl.BlockSpec((BLOCK, w), lambda i: (jnp.maximum(i * blocks - 1, 0), 0))
    return pl.pallas_call(
        functools.partial(_mix_out_kernel, tiles_per_seq=seq // tm),
        grid=(t // tm,),
        in_specs=[pl.BlockSpec(memory_space=pltpu.SMEM), row(D_MODEL), row(ATTN_W), prev(KV_W), row(KV_W),
                  prev(KV_W), row(KV_W), row(ATTN_W), row(SSM_W), row(SSM_W),
                  _resident(wg.shape), _resident(bg.shape), _resident(wo.shape)],
        out_specs=row(D_MODEL),
        out_shape=jax.ShapeDtypeStruct((t, D_MODEL), x2.dtype),
        scratch_shapes=[pltpu.VMEM((tm, ATTN_W), _F32)],
        compiler_params=pltpu.CompilerParams(dimension_semantics=("arbitrary",),
                                             vmem_limit_bytes=VMEM_LIMIT),
        name="mix_out",
    )(sinks, x2, q, k, k, v, v, za, y, zs, wg, bg, wo)


def _pick_tile(n, target, multiple):
    best = multiple
    for cand in range(multiple, min(n, target) + 1, multiple):
        if n % cand == 0:
            best = cand
    return best


def kernel(x, positions, norm_w, w_in, q_norm_w, k_norm_w, sinks, a_re, a_im, log_step,
           b_re, b_im, c_re, c_im, d_skip, w_glu, b_glu, attn_out_norm_w, ssm_out_norm_w, w_out):
    batch, seq, _ = x.shape
    assert seq % BLOCK == 0 and x.shape[2] == D_MODEL and positions.shape == (batch, seq)
    assert w_in.shape == (D_MODEL, IN_W) and w_out.shape == (ATTN_W + SSM_W, D_MODEL)
    assert a_re.shape == (SSM_G, SSM_P) and b_re.shape == (SSM_G, SSM_P, SSM_H) and c_re.shape == (SSM_G, SSM_H, SSM_P)
    assert sinks.shape == (N_HEADS,) and w_glu.shape == (SSM_W, SSM_W)
    t = batch * seq
    f32 = _F32

    w_in_k = (norm_w.astype(f32)[:, None] * w_in.astype(f32)).astype(_BF16)
    out_gain = jnp.concatenate([attn_out_norm_w.astype(f32), ssm_out_norm_w.astype(f32)])
    w_out_k = (out_gain[:, None] * w_out.astype(f32)).astype(_BF16)
    log2e = math.log2(math.e)
    qw = (jnp.tile(q_norm_w.astype(f32), N_HEADS) * (log2e / math.sqrt(HEAD_DIM)))[None, :]
    kw = jnp.tile(k_norm_w.astype(f32), N_KV_HEADS)[None, :]
    inv_freq = ROPE_THETA ** (-jnp.arange(0, HEAD_DIM, 2, dtype=f32) / HEAD_DIM)
    invf = jnp.tile(inv_freq, LANES // (HEAD_DIM // 2))[None, :]
    sgn = jnp.tile(jnp.concatenate([-jnp.ones(HEAD_DIM // 2, f32), jnp.ones(HEAD_DIM // 2, f32)]),
                   LANES // HEAD_DIM)[None, :]

    x2 = x.reshape(t, D_MODEL)
    pos2 = positions.reshape(t // LANES, LANES).astype(jnp.int32)

    tm = _pick_tile(seq, 512, BLOCK)
    q, k, v, za, ucat, zs = _in_proj(x2, pos2, w_in_k, qw, kw, invf, sgn, tm)


    nc = _pick_tile(seq // SSM_Q, 512, 16)
    assert nc & (nc - 1) == 0, "chunk rows per tile must be a power of two for the scan"
    cs, cl, skip, decay = _ssm_weights(a_re, a_im, log_step, b_re, b_im, c_re, c_im, d_skip,
                                     n_scan=int(math.log2(nc)))
    y = _ssm(ucat, cs, cl, skip, decay, batch, seq, nc)

    out = _mix_out(sinks.astype(f32) * log2e, x2, q, k, v, za, y, zs, w_glu.astype(_BF16),
                   b_glu.astype(f32)[None, :], w_out_k, seq, tm)
    return out.reshape(x.shape)
```

```python
import functools
import math

import jax
import jax.numpy as jnp
import numpy as np
from jax import lax
from jax.experimental import pallas as pl
from jax.experimental.pallas import tpu as pltpu

D_MODEL = 2048
ATTN_W = 1024
KV_W = 256
HEAD_DIM = 64
N_HEADS = 16
N_KV_HEADS = 4
KV_REP = N_HEADS // N_KV_HEADS
SSM_W = 1024
SSM_H = 16
SSM_G = 64
SSM_P = 64
BLOCK = 128
ROPE_THETA = 10000.0
NORM_EPS = 1e-6
IN_W = 2 * ATTN_W + 2 * KV_W + 2 * SSM_W

LANES = 128
SUBLANES = 8
MXU_DIM = 256
SSM_Q = 8
N_SLABS = SSM_W // LANES
HALF = LANES // 2
HALF_GROUPS = HALF // SSM_H
N_HALVES = SSM_W // HALF
HALF_IN = SSM_Q * HALF
HALF_STATE = 2 * HALF_GROUPS * SSM_P
SLAB_IN = 2 * HALF_IN
SLAB_STATE = 2 * HALF_STATE
SCAN_STEPS = 3
TABLE_POWERS = SCAN_STEPS * SUBLANES
TABLE_BLOCKS = TABLE_POWERS + SUBLANES
VMEM_LIMIT = 56 * 1024 * 1024

_BF16 = jnp.bfloat16
_F32 = jnp.float32


def _resident(shape):
    nd = len(shape)
    return pl.BlockSpec(shape, lambda *_: (0,) * nd, pipeline_mode=pl.Buffered(1))


def _swap_halves(a):
    return pltpu.roll(a, HEAD_DIM, axis=1)


def _pair_slabs(n0, n1, n2, n3):
    low = lax.broadcasted_iota(jnp.int32, n0.shape, 1) < HEAD_DIM
    return (jnp.where(low, n0, _swap_halves(n2)), jnp.where(low, _swap_halves(n0), n2),
            jnp.where(low, n1, _swap_halves(n3)), jnp.where(low, _swap_halves(n1), n3))


def _rope(t, cos, sin_signed):
    lane = lax.broadcasted_iota(jnp.int32, t.shape, 1)
    first_half = (lane % HEAD_DIM) < (HEAD_DIM // 2)
    swapped = jnp.where(first_half,
                        pltpu.roll(t, LANES - HEAD_DIM // 2, axis=1),
                        pltpu.roll(t, HEAD_DIM // 2, axis=1))
    return t * cos + swapped * sin_signed


def _in_proj_kernel(x_ref, pos_ref, w_ref, qw_ref, kw_ref, invf_ref, sgn_ref,
                    q_ref, k_ref, v_ref, za_ref, ucat_ref, zs_ref, hn_ref, u_ref):
    tm = x_ref.shape[0]
    x = x_ref[...]
    hn_ref[...] = x.astype(_BF16)
    rs = jnp.broadcast_to(lax.rsqrt(jnp.mean(x * x, axis=-1, keepdims=True) + NORM_EPS), (tm, LANES))
    wide = 2 * MXU_DIM
    rs_wide = jnp.concatenate([rs] * (wide // LANES), axis=1)

    nsub = tm // LANES
    ang = jnp.concatenate(
        [jnp.broadcast_to(pos_ref[pl.ds(pl.program_id(0) * nsub + j, 1), :].astype(_F32),
                          (LANES, LANES)).T for j in range(nsub)], axis=0) * invf_ref[...]
    cos = jnp.cos(ang)
    sin_signed = jnp.sin(ang) * sgn_ref[...]

    def proj(c0):
        return jnp.dot(hn_ref[...], w_ref[:, c0:c0 + wide], preferred_element_type=_F32) * rs_wide

    def normed_rope(acc, nw):
        low = lax.broadcasted_iota(jnp.int32, (tm, LANES), 1) < HEAD_DIM
        out = []
        for s in range(MXU_DIM // LANES):
            a = acc[:, s * LANES:(s + 1) * LANES]
            sq = a * a
            ssq = jnp.where(low, jnp.sum(jnp.where(low, sq, 0.0), axis=-1, keepdims=True),
                            jnp.sum(jnp.where(low, 0.0, sq), axis=-1, keepdims=True))
            t = a * lax.rsqrt(ssq * (1.0 / HEAD_DIM) + NORM_EPS) * nw[:, s * LANES:(s + 1) * LANES]
            out.append(_rope(t, cos, sin_signed))
        return out

    base = 2 * ATTN_W + 2 * KV_W
    nc = tm // SSM_Q
    for c in range(SSM_W // wide):
        acc = proj(base + c * wide)
        for s in range(wide // LANES):
            u_ref[c * (wide // LANES) + s] = acc[:, s * LANES:(s + 1) * LANES]
    low_half = lax.broadcasted_iota(jnp.int32, (nc, LANES), 1) < HALF
    for s in range(N_SLABS):
        for b in range(SSM_Q // 2):
            even = u_ref[s, pl.ds(2 * b, nc, stride=SSM_Q), :]
            odd = u_ref[s, pl.ds(2 * b + 1, nc, stride=SSM_Q), :]
            c0 = s * SLAB_IN + b * LANES
            ucat_ref[:, c0:c0 + LANES] = jnp.where(low_half, even, _swap_halves(odd)).astype(_BF16)
            c0 += HALF_IN
            ucat_ref[:, c0:c0 + LANES] = jnp.where(low_half, _swap_halves(even), odd).astype(_BF16)

    for c in range(ATTN_W // wide):
        acc = proj(c * wide)
        slabs = []
        for h in range(wide // MXU_DIM):
            n0 = c * wide + h * MXU_DIM
            slabs += normed_rope(acc[:, h * MXU_DIM:(h + 1) * MXU_DIM], qw_ref[:, n0:n0 + MXU_DIM])
        for r, slab in enumerate(_pair_slabs(*slabs)):
            q_ref[:, c * wide + r * LANES:c * wide + (r + 1) * LANES] = slab.astype(_BF16)
    acc = proj(ATTN_W)
    for s, slab in enumerate(normed_rope(acc[:, :KV_W], kw_ref[...])):
        k_ref[:, s * LANES:(s + 1) * LANES] = slab.astype(_BF16)
    v_ref[...] = acc[:, KV_W:].astype(_BF16)
    for out_ref, base in ((za_ref, ATTN_W + 2 * KV_W), (zs_ref, 2 * ATTN_W + 2 * KV_W + SSM_W)):
        for c in range(ATTN_W // wide):
            z = proj(base + c * wide)
            out_ref[:, c * wide:(c + 1) * wide] = (z * jax.nn.sigmoid(z)).astype(_BF16)


def _in_proj(x2, pos2, w_in, qw, kw, invf, sgn, tm):
    t = x2.shape[0]
    row = lambda w: pl.BlockSpec((tm, w), lambda i: (i, 0))
    return pl.pallas_call(
        _in_proj_kernel,
        grid=(t // tm,),
        in_specs=[row(D_MODEL), _resident(pos2.shape), _resident(w_in.shape), _resident(qw.shape),
                  _resident(kw.shape), _resident(invf.shape), _resident(sgn.shape)],
        out_specs=[row(ATTN_W), row(KV_W), row(KV_W), row(ATTN_W),
                   pl.BlockSpec((tm // SSM_Q, N_SLABS * SLAB_IN), lambda i: (i, 0)), row(SSM_W)],
        out_shape=[jax.ShapeDtypeStruct((t, ATTN_W), _BF16),
                   jax.ShapeDtypeStruct((t, KV_W), _BF16),
                   jax.ShapeDtypeStruct((t, KV_W), _BF16),
                   jax.ShapeDtypeStruct((t, ATTN_W), _BF16),
                   jax.ShapeDtypeStruct((t // SSM_Q, N_SLABS * SLAB_IN), _BF16),
                   jax.ShapeDtypeStruct((t, SSM_W), _BF16)],
        scratch_shapes=[pltpu.VMEM((tm, D_MODEL), _BF16), pltpu.VMEM((N_SLABS, tm, LANES), _F32)],
        compiler_params=pltpu.CompilerParams(dimension_semantics=("arbitrary",),
                                             vmem_limit_bytes=VMEM_LIMIT),
        name="in_proj",
    )(x2, pos2, w_in, qw, kw, invf, sgn)


def _attn_scores(q_ref, kp_ref, kc_ref, vp_ref, vc_ref):
    blocks = q_ref.shape[0] // BLOCK
    two = 2 * BLOCK
    lane_kv = lax.broadcasted_iota(jnp.int32, (two, LANES), 1)
    halves = [lane_kv < HEAD_DIM, lane_kv >= HEAD_DIM]
    zero = jnp.zeros((two, LANES), _BF16)
    key_head = lax.broadcasted_iota(jnp.int32, (2 * SUBLANES, 2 * two), 1) // two
    count_rows = (lax.broadcasted_iota(jnp.int32, (2 * SUBLANES, 2 * two), 0) == key_head).astype(_BF16)

    def keys(prev_ref, cur_ref, b, sl):
        if b == 0:
            return jnp.concatenate([prev_ref[:, sl], cur_ref[0:BLOCK, sl]], axis=0)
        return cur_ref[(b - 1) * BLOCK:(b + 1) * BLOCK, sl]

    scores, vts = {}, {}
    for b in range(blocks):
        rows = slice(b * BLOCK, (b + 1) * BLOCK)
        for j in range(N_KV_HEADS // 2):
            sl = slice(j * LANES, (j + 1) * LANES)
            k2 = keys(kp_ref, kc_ref, b, sl)
            v2 = keys(vp_ref, vc_ref, b, sl)
            vts[b, j] = jnp.concatenate(
                [jnp.concatenate([jnp.where(keep, v2, zero) for keep in halves], axis=0).T,
                 count_rows], axis=0)
            for t in range(KV_REP // 2):
                r0 = KV_REP * j + 2 * t
                qq = q_ref[rows, r0 * LANES:(r0 + 2) * LANES]
                qq = jnp.concatenate([qq[:, :LANES], qq[:, LANES:]], axis=0)
                for half, keep in enumerate(halves):
                    scores[b, j, t, half] = lax.dot_general(
                        jnp.where(keep, k2, zero), qq, (((1,), (1,)), ((), ())),
                        preferred_element_type=_F32)
    return scores, vts


def _attn_softmax_pv(sinks_ref, scores, vts, acc_ref, seq_start, blocks):
    two = 2 * BLOCK
    ik = lax.broadcasted_iota(jnp.int32, (two, two), 0)
    iq = lax.broadcasted_iota(jnp.int32, (two, two), 1) % BLOCK
    rel = ik - iq
    band = (rel >= 1) & (rel <= BLOCK)
    band_first = band & ((ik >= BLOCK) | jnp.logical_not(seq_start))
    first_slab = lax.broadcasted_iota(jnp.int32, (1, two), 1) < BLOCK
    row_o = lax.broadcasted_iota(jnp.int32, (LANES, two), 0)
    for b in blocks:
        rows = slice(b * BLOCK, (b + 1) * BLOCK)
        allowed = band_first if b == 0 else band
        for j in range(N_KV_HEADS // 2):
            for t in range(KV_REP // 2):
                r0 = KV_REP * j + 2 * t
                probs, sink_terms = [], []
                for half in range(2):
                    h0 = 2 * KV_REP * j + KV_REP * half + 2 * t
                    sink = jnp.where(first_slab, sinks_ref[h0], sinks_ref[h0 + 1])
                    s = jnp.where(allowed, scores[b, j, t, half], -1e30)
                    m = jnp.maximum(jnp.max(s, axis=0, keepdims=True), sink)
                    probs.append(jnp.exp2(s - m).astype(_BF16))
                    sink_terms.append(jnp.exp2(sink - m))
                o_t = jnp.dot(vts[b, j], jnp.concatenate(probs, axis=0), preferred_element_type=_F32)
                rdens = [1.0 / (o_t[LANES + half:LANES + half + 1] + sink_terms[half]) for half in range(2)]
                o_t = o_t[:LANES] * jnp.where(row_o < HEAD_DIM, rdens[0], rdens[1])
                for half in range(2):
                    heads = slice(half * HEAD_DIM, (half + 1) * HEAD_DIM)
                    slab = KV_REP * j + (KV_REP // 2) * half + t
                    acc_ref[rows, slab * LANES:(slab + 1) * LANES] = jnp.concatenate(
                        [o_t[heads, :BLOCK], o_t[heads, BLOCK:]], axis=0).T


def _ssm_weights(a_re, a_im, log_step, b_re, b_im, c_re, c_im, d_skip, n_scan):
    f32 = _F32
    exps, keep = [], []
    for k in range(SCAN_STEPS):
        exps += [SSM_Q << k] * SUBLANES
        keep += [0.0] * (1 << k) + [1.0] * (SUBLANES - (1 << k))
    exps += [SSM_Q * j for j in range(SUBLANES)] + [SSM_Q << k for k in range(SCAN_STEPS, n_scan)]
    keep += [1.0] * (SUBLANES + n_scan - SCAN_STEPS)
    n_rows = len(exps)
    row_consts = jnp.asarray(np.stack([np.repeat(np.asarray(v, np.float32)[:, None], HALF_STATE, axis=1)
                                       for v in (exps, keep)]))

    lam = jnp.stack([a_re.astype(f32), a_im.astype(f32),
                     jnp.broadcast_to(log_step.astype(f32)[:, None], (SSM_G, SSM_P))])
    lam_lanes = jnp.broadcast_to(lam.reshape(3, N_HALVES, HALF_GROUPS // 2, 1, 2 * SSM_P),
                                 (3, N_HALVES, HALF_GROUPS // 2, 2, 2 * SSM_P))
    lam_lanes = lam_lanes.reshape(3, N_HALVES, HALF_STATE).transpose(1, 0, 2)
    lam_rows = jnp.concatenate([lam, lam], axis=2).reshape(3, N_HALVES, HALF_GROUPS, LANES)
    bt = jnp.stack([b_re, b_im]).astype(f32).transpose(0, 1, 3, 2)
    bt = jnp.concatenate([bt, bt], axis=3).reshape(2, N_HALVES, HALF, LANES)
    ct = jnp.stack([c_re, c_im]).astype(f32)
    ct = jnp.concatenate([ct, ct], axis=3).reshape(2, N_HALVES, HALF, LANES)
    skip = d_skip.astype(f32).reshape(N_HALVES, 1, HALF)

    step = 4
    assert N_HALVES % step == 0

    def per_half(shape):
        return pl.BlockSpec((shape[0], step) + shape[2:], lambda e: (0, e) + (0,) * (len(shape) - 2))

    cs, cl, decay = pl.pallas_call(
        _ssm_tables_kernel,
        grid=(N_HALVES // step,),
        in_specs=[per_half(lam_rows.shape), pl.BlockSpec((step, 3, HALF_STATE), lambda e: (e, 0, 0)),
                  per_half(bt.shape), per_half(ct.shape), _resident(row_consts.shape)],
        out_specs=[pl.BlockSpec((step, SSM_Q, HALF, LANES), lambda e: (e, 0, 0, 0)),
                   pl.BlockSpec((step, SSM_Q + 1, HALF, LANES), lambda e: (e, 0, 0, 0)),
                   pl.BlockSpec((step, n_rows, HALF_STATE), lambda e: (e, 0, 0))],
        out_shape=[jax.ShapeDtypeStruct((N_HALVES, SSM_Q, HALF, LANES), f32),
                   jax.ShapeDtypeStruct((N_HALVES, SSM_Q + 1, HALF, LANES), f32),
                   jax.ShapeDtypeStruct((N_HALVES, n_rows, HALF_STATE), f32)],
        compiler_params=pltpu.CompilerParams(dimension_semantics=("arbitrary",)),
        name="s5_tables",
    )(lam_rows, lam_lanes, bt, ct, row_consts)
    return cs, cl, skip, decay


def _ssm_tables_kernel(lam_ref, lanes_ref, b_ref, c_ref, rows_ref, cs_ref, cl_ref, dec_ref):
    for j in range(cs_ref.shape[0]):
        _ssm_tables_half(j, lam_ref, lanes_ref, b_ref, c_ref, rows_ref, cs_ref, cl_ref, dec_ref)


def _ssm_tables_half(j, lam_ref, lanes_ref, b_ref, c_ref, rows_ref, cs_ref, cl_ref, dec_ref):
    a_re, a_im = lam_ref[0, j], lam_ref[1, j]
    delta = jnp.exp(lam_ref[2, j])
    zr, zi = a_re * delta, a_im * delta

    def group_rows(a):
        return jnp.concatenate([jnp.broadcast_to(a[g:g + 1], (SSM_H, LANES)) for g in range(HALF_GROUPS)], axis=0)

    def lam_pow(d):
        mag = jnp.exp(float(d) * zr)
        return mag * jnp.cos(float(d) * zi), mag * jnp.sin(float(d) * zi)

    lr, li = lam_pow(1)
    den = a_re * a_re + a_im * a_im
    fr = group_rows(((lr - 1.0) * a_re + li * a_im) / den)
    fi = group_rows((li * a_re - (lr - 1.0) * a_im) / den)
    b_re, b_im = b_ref[0, j], b_ref[1, j]
    bbr = fr * b_re - fi * b_im
    bbi = fr * b_im + fi * b_re
    c_re, c_im = c_ref[0, j], c_ref[1, j]
    re_lanes = lax.broadcasted_iota(jnp.int32, (HALF, LANES), 1) < SSM_P

    for d in range(SSM_Q + 1):
        pr, pi = lam_pow(d)
        pr, pi = group_rows(pr), group_rows(pi)
        cl_ref[j, d] = jnp.where(re_lanes, c_re * pr - c_im * pi, -(c_re * pi + c_im * pr))
        if d < SSM_Q:
            cs_ref[j, SSM_Q - 1 - d] = jnp.where(re_lanes, pr * bbr - pi * bbi, pr * bbi + pi * bbr)

    a_re, a_im = lanes_ref[j, 0:1, :], lanes_ref[j, 1:2, :]
    delta = jnp.exp(lanes_ref[j, 2:3, :])
    m, keep = rows_ref[0], rows_ref[1]
    mag = jnp.exp(m * (a_re * delta))
    ang = m * (a_im * delta)
    im_lanes = (lax.broadcasted_iota(jnp.int32, ang.shape, 1) // LANES) % 2 == 1
    dec_ref[j] = mag * jnp.where(im_lanes, jnp.sin(ang), jnp.cos(ang)) * keep


def _expand_half(cs_ref, cl_ref, skip_ref, w1_ref, w2_ref, e):
    r = lax.broadcasted_iota(jnp.int32, (LANES, LANES), 0)
    c = lax.broadcasted_iota(jnp.int32, (LANES, LANES), 1)
    same_group = ((r % HALF) // SSM_H) == ((c % HALF) // SSM_H)
    rs = lax.broadcasted_iota(jnp.int32, (LANES, HALF_STATE), 0)
    cst = lax.broadcasted_iota(jnp.int32, (LANES, HALF_STATE), 1)
    rep_s = ((rs // SSM_P == (cst // LANES) % 2) & (rs % SSM_P == cst % SSM_P)).astype(_BF16)
    diag_s = ((rs % HALF) // SSM_H) == (2 * (cst // (2 * LANES)) + (cst % LANES) // SSM_P)

    none = jnp.zeros((HALF, LANES), _F32)
    b_bar = cs_ref[e, SSM_Q - 1]
    lhs = jnp.concatenate([jnp.concatenate([b_bar, none], axis=1),
                           jnp.concatenate([none, b_bar], axis=1)], axis=0)
    c_lam = lambda d: cl_ref[e, d] if d >= 0 else none
    skip = jnp.concatenate([skip_ref[e], skip_ref[e]], axis=1)
    tiles = []
    for delta in range(SSM_Q // 2):
        rhs = jnp.concatenate(
            [jnp.concatenate([c_lam(2 * delta), c_lam(2 * delta - 1)], axis=1),
             jnp.concatenate([c_lam(2 * delta + 1), c_lam(2 * delta)], axis=1)], axis=0)
        k = lax.dot_general(lhs, rhs, (((1,), (1,)), ((), ())),
                            precision=lax.Precision.HIGHEST, preferred_element_type=_F32)
        if delta == 0:
            k = k + jnp.where(r == c, skip, 0.0)
        tiles.append(jnp.where(same_group, k, 0.0).astype(_BF16))
    zero = jnp.zeros((LANES, LANES), _BF16)
    two_offsets = lambda ref, i: jnp.concatenate([ref[e, i], ref[e, i + 1]], axis=0).astype(_BF16)
    for b in range(SSM_Q // 2):
        rows = slice(b * LANES, (b + 1) * LANES)
        for b2 in range(SSM_Q // 2):
            w1_ref[e, rows, b2 * LANES:(b2 + 1) * LANES] = tiles[b2 - b] if b2 >= b else zero
        w_in = jnp.where(diag_s, jnp.dot(two_offsets(cs_ref, 2 * b), rep_s, preferred_element_type=_F32), 0.0)
        w1_ref[e, rows, HALF_IN:] = w_in.astype(_BF16)
        w_out_t = jnp.where(diag_s, jnp.dot(two_offsets(cl_ref, 2 * b + 1), rep_s,
                                            preferred_element_type=_F32), 0.0)
        w2_ref[e, rows, :] = w_out_t.astype(_BF16)


def _ssm_kernel(x_ref, cs_ref, cl_ref, skip_ref, dec_ref, y_ref, w1_ref, w2_ref, h_ref, t_scr):
    nc = x_ref.shape[0]
    nb = nc // SUBLANES
    piece = 2 * LANES

    def cmul_add(acc_r, acc_i, pr, pi, tr, ti):
        return acc_r + pr * tr - pi * ti, acc_i + pr * ti + pi * tr

    def roll_in_block(a, shift):
        return pltpu.roll(a.reshape(nb, SUBLANES, LANES), shift, axis=1).reshape(nc, LANES)

    @pl.when((pl.program_id(1) == 0) & (pl.program_id(2) == 0))
    def _():
        for e in range(2):
            _expand_half(cs_ref, cl_ref, skip_ref, w1_ref, w2_ref, e)

    @pl.when(pl.program_id(2) == 0)
    def _():
        h_ref[...] = jnp.zeros_like(h_ref)

    xs = [x_ref[:, e * HALF_IN:(e + 1) * HALF_IN] for e in range(2)]
    s_all = [jnp.dot(xs[e], w1_ref[e, :, HALF_IN:], preferred_element_type=_F32) for e in range(2)]
    sub = lax.broadcasted_iota(jnp.int32, (nc, LANES), 0) % SUBLANES
    brow = lax.broadcasted_iota(jnp.int32, (nb, LANES), 0)
    ys = [None, None]
    for e in range(2):
        for p in range(HALF_STATE // piece):
            re, im = slice(p * piece, p * piece + LANES), slice(p * piece + LANES, (p + 1) * piece)
            table = lambda row0, n: (dec_ref[e, row0:row0 + n, re], dec_ref[e, row0:row0 + n, im])
            tall = lambda a: jnp.concatenate([a] * nb, axis=0)
            h_re = slice(e * HALF_STATE + p * piece, e * HALF_STATE + p * piece + LANES)
            h_im = slice(h_re.stop, h_re.stop + LANES)
            slot = 2 * (e * (HALF_STATE // piece) + p)

            sr, si = s_all[e][:, re], s_all[e][:, im]
            hin_r, hin_i = h_ref[0:1, h_re], h_ref[0:1, h_im]

            for k in range(SCAN_STEPS):
                mr, mi = table(k * SUBLANES, SUBLANES)
                sr, si = cmul_add(sr, si, tall(mr), tall(mi),
                                  roll_in_block(sr, 1 << k), roll_in_block(si, 1 << k))

            t_scr[slot] = sr
            t_scr[slot + 1] = si
            br = t_scr[slot, pl.ds(SUBLANES - 1, nb, stride=SUBLANES), :]
            bi = t_scr[slot + 1, pl.ds(SUBLANES - 1, nb, stride=SUBLANES), :]
            pr, pi = table(TABLE_BLOCKS, 1)
            br = br + jnp.where(brow == 0, pr * hin_r - pi * hin_i, 0.0)
            bi = bi + jnp.where(brow == 0, pr * hin_i + pi * hin_r, 0.0)
            for k in range(dec_ref.shape[1] - TABLE_BLOCKS):
                shift = 1 << k
                tr = jnp.where(brow >= shift, pltpu.roll(br, shift, axis=0), 0.0)
                ti = jnp.where(brow >= shift, pltpu.roll(bi, shift, axis=0), 0.0)
                br, bi = cmul_add(br, bi, *table(TABLE_BLOCKS + k, 1), tr, ti)
            h_ref[0:1, h_re] = br[nb - 1:nb]
            h_ref[0:1, h_im] = bi[nb - 1:nb]
            bpr = jnp.where(brow == 0, hin_r, pltpu.roll(br, 1, axis=0))
            bpi = jnp.where(brow == 0, hin_i, pltpu.roll(bi, 1, axis=0))

            rep = lambda a: jnp.concatenate(
                [jnp.broadcast_to(a[k:k + 1], (SUBLANES, LANES)) for k in range(nb)], axis=0)
            fr, fi = table(TABLE_POWERS, SUBLANES)
            tr = jnp.where(sub >= 1, roll_in_block(sr, 1), 0.0)
            ti = jnp.where(sub >= 1, roll_in_block(si, 1), 0.0)
            prev_r, prev_i = cmul_add(tr, ti, tall(fr), tall(fi), rep(bpr), rep(bpi))
            hprev = jnp.concatenate([prev_r, prev_i], axis=1).astype(_BF16)
            part = lax.dot_general(hprev, w2_ref[e, :, p * piece:(p + 1) * piece], (((1,), (1,)), ((), ())),
                                   preferred_element_type=_F32)
            if ys[e] is None:
                mid = HALF_IN // 2
                ys[e] = jnp.concatenate(
                    [jnp.dot(xs[e][:, :mid], w1_ref[e, :mid, :mid], preferred_element_type=_F32),
                     jnp.dot(xs[e], w1_ref[e, :, mid:HALF_IN], preferred_element_type=_F32)], axis=1) + part
            else:
                ys[e] = ys[e] + part

    low = lax.broadcasted_iota(jnp.int32, (nc, LANES), 1) < HALF
    for b in range(SSM_Q // 2):
        y0, y1 = (ys[e][:, b * LANES:(b + 1) * LANES] for e in range(2))
        y_ref[pl.ds(2 * b, nc, stride=SSM_Q), :] = jnp.where(low, y0, _swap_halves(y1))
        y_ref[pl.ds(2 * b + 1, nc, stride=SSM_Q), :] = jnp.where(low, _swap_halves(y0), y1)


def _ssm(ucat, cs, cl, skip, decay, batch, seq, nc):
    tiles = seq // (nc * SSM_Q)
    pair = lambda a: pl.BlockSpec((2,) + a.shape[1:], lambda s, b, t: (s,) + (0,) * (a.ndim - 1))
    return pl.pallas_call(
        _ssm_kernel,
        grid=(N_SLABS, batch, tiles),
        in_specs=[pl.BlockSpec((nc, SLAB_IN), lambda s, b, t: (b * tiles + t, s)),
                  pair(cs), pair(cl), pair(skip), pair(decay)],
        out_specs=pl.BlockSpec((nc * SSM_Q, LANES), lambda s, b, t: (b * tiles + t, s)),
        out_shape=jax.ShapeDtypeStruct((batch * seq, SSM_W), _F32),
        scratch_shapes=[pltpu.VMEM((2, HALF_IN, HALF_IN + HALF_STATE), _BF16),
                        pltpu.VMEM((2, HALF_IN, HALF_STATE), _BF16),
                        pltpu.VMEM((SUBLANES, SLAB_STATE), _F32),
                        pltpu.VMEM((SLAB_STATE // LANES, nc, LANES), _F32)],
        compiler_params=pltpu.CompilerParams(
            dimension_semantics=("arbitrary", "arbitrary", "arbitrary"),
            vmem_limit_bytes=VMEM_LIMIT),
        name="s5_mixer",
    )(ucat, cs, cl, skip, decay)


def _gelu_tanh(y):
    c = math.sqrt(2.0 / math.pi)
    return (0.5 * y) * (1.0 + jnp.tanh(y * (c + (c * 0.044715) * (y * y))))


def _mix_out_kernel(sinks_ref, x_ref, q_ref, kp_ref, kc_ref, vp_ref, vc_ref, za_ref, y_ref, zs_ref,
                    wg_ref, bg_ref, wo_ref, out_ref, acc_ref, *, tiles_per_seq):
    scores, vts = _attn_scores(q_ref, kp_ref, kc_ref, vp_ref, vc_ref)

    y = _gelu_tanh(y_ref[...].astype(_F32))
    gate = jnp.dot(y.astype(_BF16), wg_ref[...], preferred_element_type=_F32) + bg_ref[...]
    o = y * jax.nn.sigmoid(gate) * zs_ref[...].astype(_F32)
    rs = lax.rsqrt(jnp.mean(o * o, axis=-1, keepdims=True) + NORM_EPS)
    out_ref[...] = x_ref[...] + rs * jnp.dot(o.astype(_BF16), wo_ref[ATTN_W:, :], preferred_element_type=_F32)

    seq_start = pl.program_id(0) % tiles_per_seq == 0
    _attn_softmax_pv(sinks_ref, scores, vts, acc_ref, seq_start, range(acc_ref.shape[0] // BLOCK))
    g = acc_ref[...] * za_ref[...].astype(_F32)
    rs = lax.rsqrt(jnp.mean(g * g, axis=-1, keepdims=True) + NORM_EPS)
    out_ref[...] += rs * jnp.dot(g.astype(_BF16), wo_ref[:ATTN_W, :], preferred_element_type=_F32)


def _mix_out(sinks, x2, q, k, v, za, y, zs, wg, bg, wo, seq, tm):
    t = x2.shape[0]
    blocks = tm // BLOCK
    row = lambda w: pl.BlockSpec((tm, w), lambda i: (i, 0))
    prev = lambda w: pl.BlockSpec((BLOCK, w), lambda i: (jnp.maximum(i * blocks - 1, 0), 0))
    return pl.pallas_call(
        functools.partial(_mix_out_kernel, tiles_per_seq=seq // tm),
        grid=(t // tm,),
        in_specs=[pl.BlockSpec(memory_space=pltpu.SMEM), row(D_MODEL), row(ATTN_W), prev(KV_W), row(KV_W),
                  prev(KV_W), row(KV_W), row(ATTN_W), row(SSM_W), row(SSM_W),
                  _resident(wg.shape), _resident(bg.shape), _resident(wo.shape)],
        out_specs=row(D_MODEL),
        out_shape=jax.ShapeDtypeStruct((t, D_MODEL), x2.dtype),
        scratch_shapes=[pltpu.VMEM((tm, ATTN_W), _F32)],
        compiler_params=pltpu.CompilerParams(dimension_semantics=("arbitrary",),
                                             vmem_limit_bytes=VMEM_LIMIT),
        name="mix_out",
    )(sinks, x2, q, k, k, v, v, za, y, zs, wg, bg, wo)


def _pick_tile(n, target, multiple):
    best = multiple
    for cand in range(multiple, min(n, target) + 1, multiple):
        if n % cand == 0:
            best = cand
    return best


def kernel(x, positions, norm_w, w_in, q_norm_w, k_norm_w, sinks, a_re, a_im, log_step,
           b_re, b_im, c_re, c_im, d_skip, w_glu, b_glu, attn_out_norm_w, ssm_out_norm_w, w_out):
    batch, seq, _ = x.shape
    assert seq % BLOCK == 0 and x.shape[2] == D_MODEL and positions.shape == (batch, seq)
    assert w_in.shape == (D_MODEL, IN_W) and w_out.shape == (ATTN_W + SSM_W, D_MODEL)
    assert a_re.shape == (SSM_G, SSM_P) and b_re.shape == (SSM_G, SSM_P, SSM_H) and c_re.shape == (SSM_G, SSM_H, SSM_P)
    assert sinks.shape == (N_HEADS,) and w_glu.shape == (SSM_W, SSM_W)
    t = batch * seq
    f32 = _F32

    w_in_k = (norm_w.astype(f32)[:, None] * w_in.astype(f32)).astype(_BF16)
    out_gain = jnp.concatenate([attn_out_norm_w.astype(f32), ssm_out_norm_w.astype(f32)])
    w_out_k = (out_gain[:, None] * w_out.astype(f32)).astype(_BF16)
    log2e = math.log2(math.e)
    qw = (jnp.tile(q_norm_w.astype(f32), N_HEADS) * (log2e / math.sqrt(HEAD_DIM)))[None, :]
    kw = jnp.tile(k_norm_w.astype(f32), N_KV_HEADS)[None, :]
    inv_freq = ROPE_THETA ** (-jnp.arange(0, HEAD_DIM, 2, dtype=f32) / HEAD_DIM)
    invf = jnp.tile(inv_freq, LANES // (HEAD_DIM // 2))[None, :]
    sgn = jnp.tile(jnp.concatenate([-jnp.ones(HEAD_DIM // 2, f32), jnp.ones(HEAD_DIM // 2, f32)]),
                   LANES // HEAD_DIM)[None, :]

    x2 = x.reshape(t, D_MODEL)
    pos2 = positions.reshape(t // LANES, LANES).astype(jnp.int32)

    tm = _pick_tile(seq, 512, BLOCK)
    q, k, v, za, ucat, zs = _in_proj(x2, pos2, w_in_k, qw, kw, invf, sgn, tm)


    nc = _pick_tile(seq // SSM_Q, 512, 16)
    assert nc & (nc - 1) == 0, "chunk rows per tile must be a power of two for the scan"
    cs, cl, skip, decay = _ssm_weights(a_re, a_im, log_step, b_re, b_im, c_re, c_im, d_skip,
                                     n_scan=int(math.log2(nc)))
    y = _ssm(ucat, cs, cl, skip, decay, batch, seq, nc)

    out = _mix_out(sinks.astype(f32) * log2e, x2, q, k, v, za, y, zs, w_glu.astype(_BF16),
                   b_glu.astype(f32)[None, :], w_out_k, seq, tm)
    return out.reshape(x.shape)
```

```python
import functools
import math

import jax
import jax.numpy as jnp
import numpy as np
from jax import lax
from jax.experimental import pallas as pl
from jax.experimental.pallas import tpu as pltpu

D_MODEL = 2048
ATTN_W = 1024
KV_W = 256
HEAD_DIM = 64
N_HEADS = 16
N_KV_HEADS = 4
KV_REP = N_HEADS // N_KV_HEADS
SSM_W = 1024
SSM_H = 16
SSM_G = 64
SSM_P = 64
BLOCK = 128
ROPE_THETA = 10000.0
NORM_EPS = 1e-6
IN_W = 2 * ATTN_W + 2 * KV_W + 2 * SSM_W

LANES = 128
SUBLANES = 8
MXU_DIM = 256
SSM_Q = 8
N_SLABS = SSM_W // LANES
HALF = LANES // 2
HALF_GROUPS = HALF // SSM_H
N_HALVES = SSM_W // HALF
HALF_IN = SSM_Q * HALF
HALF_STATE = 2 * HALF_GROUPS * SSM_P
SLAB_IN = 2 * HALF_IN
SLAB_STATE = 2 * HALF_STATE
SCAN_STEPS = 3
TABLE_POWERS = SCAN_STEPS * SUBLANES
TABLE_BLOCKS = TABLE_POWERS + SUBLANES
VMEM_LIMIT = 56 * 1024 * 1024

_BF16 = jnp.bfloat16
_F32 = jnp.float32


def _resident(shape):
    nd = len(shape)
    return pl.BlockSpec(shape, lambda *_: (0,) * nd, pipeline_mode=pl.Buffered(1))


def _swap_halves(a):
    return pltpu.roll(a, HEAD_DIM, axis=1)


def _pair_slabs(n0, n1, n2, n3):
    low = lax.broadcasted_iota(jnp.int32, n0.shape, 1) < HEAD_DIM
    return (jnp.where(low, n0, _swap_halves(n2)), jnp.where(low, _swap_halves(n0), n2),
            jnp.where(low, n1, _swap_halves(n3)), jnp.where(low, _swap_halves(n1), n3))


def _rope(t, cos, sin_signed):
    lane = lax.broadcasted_iota(jnp.int32, t.shape, 1)
    first_half = (lane % HEAD_DIM) < (HEAD_DIM // 2)
    swapped = jnp.where(first_half,
                        pltpu.roll(t, LANES - HEAD_DIM // 2, axis=1),
                        pltpu.roll(t, HEAD_DIM // 2, axis=1))
    return t * cos + swapped * sin_signed


def _in_proj_kernel(x_ref, pos_ref, w_ref, qw_ref, kw_ref, invf_ref, sgn_ref,
                    q_ref, k_ref, v_ref, za_ref, ucat_ref, zs_ref, hn_ref, u_ref):
    tm = x_ref.shape[0]
    x = x_ref[...]
    hn_ref[...] = x.astype(_BF16)
    rs = jnp.broadcast_to(lax.rsqrt(jnp.mean(x * x, axis=-1, keepdims=True) + NORM_EPS), (tm, LANES))
    wide = 2 * MXU_DIM
    rs_wide = jnp.concatenate([rs] * (wide // LANES), axis=1)

    nsub = tm // LANES
    ang = jnp.concatenate(
        [jnp.broadcast_to(pos_ref[pl.ds(pl.program_id(0) * nsub + j, 1), :].astype(_F32),
                          (LANES, LANES)).T for j in range(nsub)], axis=0) * invf_ref[...]
    cos = jnp.cos(ang)
    sin_signed = jnp.sin(ang) * sgn_ref[...]

    def proj(c0):
        return jnp.dot(hn_ref[...], w_ref[:, c0:c0 + wide], preferred_element_type=_F32) * rs_wide

    def normed_rope(acc, nw):
        low = lax.broadcasted_iota(jnp.int32, (tm, LANES), 1) < HEAD_DIM
        out = []
        for s in range(MXU_DIM // LANES):
            a = acc[:, s * LANES:(s + 1) * LANES]
            sq = a * a
            ssq = jnp.where(low, jnp.sum(jnp.where(low, sq, 0.0), axis=-1, keepdims=True),
                            jnp.sum(jnp.where(low, 0.0, sq), axis=-1, keepdims=True))
            t = a * lax.rsqrt(ssq * (1.0 / HEAD_DIM) + NORM_EPS) * nw[:, s * LANES:(s + 1) * LANES]
            out.append(_rope(t, cos, sin_signed))
        return out

    base = 2 * ATTN_W + 2 * KV_W
    nc = tm // SSM_Q
    for c in range(SSM_W // wide):
        acc = proj(base + c * wide)
        for s in range(wide // LANES):
            u_ref[c * (wide // LANES) + s] = acc[:, s * LANES:(s + 1) * LANES]
    low_half = lax.broadcasted_iota(jnp.int32, (nc, LANES), 1) < HALF
    for s in range(N_SLABS):
        for b in range(SSM_Q // 2):
            even = u_ref[s, pl.ds(2 * b, nc, stride=SSM_Q), :]
            odd = u_ref[s, pl.ds(2 * b + 1, nc, stride=SSM_Q), :]
            c0 = s * SLAB_IN + b * LANES
            ucat_ref[:, c0:c0 + LANES] = jnp.where(low_half, even, _swap_halves(odd)).astype(_BF16)
            c0 += HALF_IN
            ucat_ref[:, c0:c0 + LANES] = jnp.where(low_half, _swap_halves(even), odd).astype(_BF16)

    for c in range(ATTN_W // wide):
        acc = proj(c * wide)
        slabs = []
        for h in range(wide // MXU_DIM):
            n0 = c * wide + h * MXU_DIM
            slabs += normed_rope(acc[:, h * MXU_DIM:(h + 1) * MXU_DIM], qw_ref[:, n0:n0 + MXU_DIM])
        for r, slab in enumerate(_pair_slabs(*slabs)):
            q_ref[:, c * wide + r * LANES:c * wide + (r + 1) * LANES] = slab.astype(_BF16)
    acc = proj(ATTN_W)
    for s, slab in enumerate(normed_rope(acc[:, :KV_W], kw_ref[...])):
        k_ref[:, s * LANES:(s + 1) * LANES] = slab.astype(_BF16)
    v_ref[...] = acc[:, KV_W:].astype(_BF16)
    for out_ref, base in ((za_ref, ATTN_W + 2 * KV_W), (zs_ref, 2 * ATTN_W + 2 * KV_W + SSM_W)):
        for c in range(ATTN_W // wide):
            z = proj(base + c * wide)
            out_ref[:, c * wide:(c + 1) * wide] = (z * jax.nn.sigmoid(z)).astype(_BF16)


def _in_proj(x2, pos2, w_in, qw, kw, invf, sgn, tm):
    t = x2.shape[0]
    row = lambda w: pl.BlockSpec((tm, w), lambda i: (i, 0))
    return pl.pallas_call(
        _in_proj_kernel,
        grid=(t // tm,),
        in_specs=[row(D_MODEL), _resident(pos2.shape), _resident(w_in.shape), _resident(qw.shape),
                  _resident(kw.shape), _resident(invf.shape), _resident(sgn.shape)],
        out_specs=[row(ATTN_W), row(KV_W), row(KV_W), row(ATTN_W),
                   pl.BlockSpec((tm // SSM_Q, N_SLABS * SLAB_IN), lambda i: (i, 0)), row(SSM_W)],
        out_shape=[jax.ShapeDtypeStruct((t, ATTN_W), _BF16),
                   jax.ShapeDtypeStruct((t, KV_W), _BF16),
                   jax.ShapeDtypeStruct((t, KV_W), _BF16),
                   jax.ShapeDtypeStruct((t, ATTN_W), _BF16),
                   jax.ShapeDtypeStruct((t // SSM_Q, N_SLABS * SLAB_IN), _BF16),
                   jax.ShapeDtypeStruct((t, SSM_W), _BF16)],
        scratch_shapes=[pltpu.VMEM((tm, D_MODEL), _BF16), pltpu.VMEM((N_SLABS, tm, LANES), _F32)],
        compiler_params=pltpu.CompilerParams(dimension_semantics=("arbitrary",),
                                             vmem_limit_bytes=VMEM_LIMIT),
        name="in_proj",
    )(x2, pos2, w_in, qw, kw, invf, sgn)


def _attn_scores(q_ref, kp_ref, kc_ref, vp_ref, vc_ref):
    blocks = q_ref.shape[0] // BLOCK
    two = 2 * BLOCK
    lane_kv = lax.broadcasted_iota(jnp.int32, (two, LANES), 1)
    halves = [lane_kv < HEAD_DIM, lane_kv >= HEAD_DIM]
    zero = jnp.zeros((two, LANES), _BF16)
    key_head = lax.broadcasted_iota(jnp.int32, (2 * SUBLANES, 2 * two), 1) // two
    count_rows = (lax.broadcasted_iota(jnp.int32, (2 * SUBLANES, 2 * two), 0) == key_head).astype(_BF16)

    def keys(prev_ref, cur_ref, b, sl):
        if b == 0:
            return jnp.concatenate([prev_ref[:, sl], cur_ref[0:BLOCK, sl]], axis=0)
        return cur_ref[(b - 1) * BLOCK:(b + 1) * BLOCK, sl]

    scores, vts = {}, {}
    for b in range(blocks):
        rows = slice(b * BLOCK, (b + 1) * BLOCK)
        for j in range(N_KV_HEADS // 2):
            sl = slice(j * LANES, (j + 1) * LANES)
            k2 = keys(kp_ref, kc_ref, b, sl)
            v2 = keys(vp_ref, vc_ref, b, sl)
            vts[b, j] = jnp.concatenate(
                [jnp.concatenate([jnp.where(keep, v2, zero) for keep in halves], axis=0).T,
                 count_rows], axis=0)
            for t in range(KV_REP // 2):
                r0 = KV_REP * j + 2 * t
                qq = q_ref[rows, r0 * LANES:(r0 + 2) * LANES]
                qq = jnp.concatenate([qq[:, :LANES], qq[:, LANES:]], axis=0)
                for half, keep in enumerate(halves):
                    scores[b, j, t, half] = lax.dot_general(
                        jnp.where(keep, k2, zero), qq, (((1,), (1,)), ((), ())),
                        preferred_element_type=_F32)
    return scores, vts


def _attn_softmax_pv(sinks_ref, scores, vts, acc_ref, seq_start, blocks):
    two = 2 * BLOCK
    ik = lax.broadcasted_iota(jnp.int32, (two, two), 0)
    iq = lax.broadcasted_iota(jnp.int32, (two, two), 1) % BLOCK
    rel = ik - iq
    band = (rel >= 1) & (rel <= BLOCK)
    band_first = band & ((ik >= BLOCK) | jnp.logical_not(seq_start))
    first_slab = lax.broadcasted_iota(jnp.int32, (1, two), 1) < BLOCK
    row_o = lax.broadcasted_iota(jnp.int32, (LANES, two), 0)
    for b in blocks:
        rows = slice(b * BLOCK, (b + 1) * BLOCK)
        allowed = band_first if b == 0 else band
        for j in range(N_KV_HEADS // 2):
            for t in range(KV_REP // 2):
                r0 = KV_REP * j + 2 * t
                probs, sink_terms = [], []
                for half in range(2):
                    h0 = 2 * KV_REP * j + KV_REP * half + 2 * t
                    sink = jnp.where(first_slab, sinks_ref[h0], sinks_ref[h0 + 1])
                    s = jnp.where(allowed, scores[b, j, t, half], -1e30)
                    m = jnp.maximum(jnp.max(s, axis=0, keepdims=True), sink)
                    probs.append(jnp.exp2(s - m).astype(_BF16))
                    sink_terms.append(jnp.exp2(sink - m))
                o_t = jnp.dot(vts[b, j], jnp.concatenate(probs, axis=0), preferred_element_type=_F32)
                rdens = [1.0 / (o_t[LANES + half:LANES + half + 1] + sink_terms[half]) for half in range(2)]
                o_t = o_t[:LANES] * jnp.where(row_o < HEAD_DIM, rdens[0], rdens[1])
                for half in range(2):
                    heads = slice(half * HEAD_DIM, (half + 1) * HEAD_DIM)
                    slab = KV_REP * j + (KV_REP // 2) * half + t
                    acc_ref[rows, slab * LANES:(slab + 1) * LANES] = jnp.concatenate(
                        [o_t[heads, :BLOCK], o_t[heads, BLOCK:]], axis=0).T


def _ssm_weights(a_re, a_im, log_step, b_re, b_im, c_re, c_im, d_skip, n_scan):
    f32 = _F32
    exps = [SSM_Q * j for j in range(SUBLANES)] + [SSM_Q << k for k in range(SCAN_STEPS, n_scan)]
    keep = [float(j >= (1 << k)) for k in range(SCAN_STEPS) for j in range(SUBLANES)]
    n_rows = TABLE_POWERS + len(exps)

    def lane_wide(v):
        return jnp.asarray(np.repeat(np.asarray(v, np.float32)[:, None], HALF_STATE, axis=1))

    exp_rows, keep_rows = lane_wide(exps), lane_wide(keep)

    lam = jnp.stack([a_re.astype(f32), a_im.astype(f32),
                     jnp.broadcast_to(log_step.astype(f32)[:, None], (SSM_G, SSM_P))])
    lam_lanes = jnp.broadcast_to(lam.reshape(3, N_HALVES, HALF_GROUPS // 2, 1, 2 * SSM_P),
                                 (3, N_HALVES, HALF_GROUPS // 2, 2, 2 * SSM_P))
    lam_lanes = lam_lanes.reshape(3, N_HALVES, HALF_STATE).transpose(1, 0, 2)
    lam_rows = jnp.concatenate([lam, lam], axis=2).reshape(3, N_HALVES, HALF_GROUPS, LANES)
    bt = jnp.stack([b_re, b_im]).astype(f32).transpose(0, 1, 3, 2)
    bt = jnp.concatenate([bt, bt], axis=3).reshape(2, N_HALVES, HALF, LANES)
    ct = jnp.stack([c_re, c_im]).astype(f32)
    ct = jnp.concatenate([ct, ct], axis=3).reshape(2, N_HALVES, HALF, LANES)
    skip = d_skip.astype(f32).reshape(N_HALVES, 1, HALF)

    step = 4
    assert N_HALVES % step == 0

    def per_half(shape):
        return pl.BlockSpec((shape[0], step) + shape[2:], lambda e: (0, e) + (0,) * (len(shape) - 2))

    cs, cl, decay = pl.pallas_call(
        _ssm_tables_kernel,
        grid=(N_HALVES // step,),
        in_specs=[per_half(lam_rows.shape), pl.BlockSpec((step, 3, HALF_STATE), lambda e: (e, 0, 0)),
                  per_half(bt.shape), per_half(ct.shape), _resident(exp_rows.shape), _resident(keep_rows.shape)],
        out_specs=[pl.BlockSpec((step, SSM_Q, HALF, LANES), lambda e: (e, 0, 0, 0)),
                   pl.BlockSpec((step, SSM_Q + 1, HALF, LANES), lambda e: (e, 0, 0, 0)),
                   pl.BlockSpec((step, n_rows, HALF_STATE), lambda e: (e, 0, 0))],
        out_shape=[jax.ShapeDtypeStruct((N_HALVES, SSM_Q, HALF, LANES), f32),
                   jax.ShapeDtypeStruct((N_HALVES, SSM_Q + 1, HALF, LANES), f32),
                   jax.ShapeDtypeStruct((N_HALVES, n_rows, HALF_STATE), f32)],
        compiler_params=pltpu.CompilerParams(dimension_semantics=("arbitrary",)),
        name="s5_tables",
    )(lam_rows, lam_lanes, bt, ct, exp_rows, keep_rows)
    return cs, cl, skip, decay


def _ssm_tables_kernel(*refs):
    for j in range(refs[-1].shape[0]):
        _ssm_tables_half(j, *refs)


def _ssm_tables_half(j, lam_ref, lanes_ref, b_ref, c_ref, exp_ref, keep_ref, cs_ref, cl_ref, dec_ref):
    a_re, a_im = lam_ref[0, j], lam_ref[1, j]
    delta = jnp.exp(lam_ref[2, j])
    zr, zi = a_re * delta, a_im * delta

    def group_rows(a):
        return jnp.concatenate([jnp.broadcast_to(a[g:g + 1], (SSM_H, LANES)) for g in range(HALF_GROUPS)], axis=0)

    def lam_pow(d):
        mag = jnp.exp(float(d) * zr)
        return mag * jnp.cos(float(d) * zi), mag * jnp.sin(float(d) * zi)

    lr, li = lam_pow(1)
    den = a_re * a_re + a_im * a_im
    fr = group_rows(((lr - 1.0) * a_re + li * a_im) / den)
    fi = group_rows((li * a_re - (lr - 1.0) * a_im) / den)
    b_re, b_im = b_ref[0, j], b_ref[1, j]
    bbr = fr * b_re - fi * b_im
    bbi = fr * b_im + fi * b_re
    c_re, c_im = c_ref[0, j], c_ref[1, j]
    re_lanes = lax.broadcasted_iota(jnp.int32, (HALF, LANES), 1) < SSM_P

    for d in range(SSM_Q + 1):
        pr, pi = lam_pow(d)
        pr, pi = group_rows(pr), group_rows(pi)
        cl_ref[j, d] = jnp.where(re_lanes, c_re * pr - c_im * pi, -(c_re * pi + c_im * pr))
        if d < SSM_Q:
            cs_ref[j, SSM_Q - 1 - d] = jnp.where(re_lanes, pr * bbr - pi * bbi, pr * bbi + pi * bbr)

    a_re, a_im = lanes_ref[j, 0:1, :], lanes_ref[j, 1:2, :]
    delta = jnp.exp(lanes_ref[j, 2:3, :])
    m = exp_ref[...]
    mag = jnp.exp(m * (a_re * delta))
    ang = m * (a_im * delta)
    im_lanes = (lax.broadcasted_iota(jnp.int32, ang.shape, 1) // LANES) % 2 == 1
    powers = mag * jnp.where(im_lanes, jnp.sin(ang), jnp.cos(ang))
    dec_ref[j, TABLE_POWERS:, :] = powers
    for k in range(SCAN_STEPS):
        rows = slice(k * SUBLANES, (k + 1) * SUBLANES)
        dec_ref[j, rows, :] = jnp.broadcast_to(powers[1 << k:(1 << k) + 1], (SUBLANES, HALF_STATE)) * keep_ref[rows, :]


def _expand_half(cs_ref, cl_ref, skip_ref, w1_ref, w2_ref, e):
    r = lax.broadcasted_iota(jnp.int32, (LANES, LANES), 0)
    c = lax.broadcasted_iota(jnp.int32, (LANES, LANES), 1)
    same_group = ((r % HALF) // SSM_H) == ((c % HALF) // SSM_H)
    rs = lax.broadcasted_iota(jnp.int32, (LANES, HALF_STATE), 0)
    cst = lax.broadcasted_iota(jnp.int32, (LANES, HALF_STATE), 1)
    rep_s = ((rs // SSM_P == (cst // LANES) % 2) & (rs % SSM_P == cst % SSM_P)).astype(_BF16)
    diag_s = ((rs % HALF) // SSM_H) == (2 * (cst // (2 * LANES)) + (cst % LANES) // SSM_P)

    none = jnp.zeros((HALF, LANES), _F32)
    b_bar = cs_ref[e, SSM_Q - 1]
    lhs = jnp.concatenate([jnp.concatenate([b_bar, none], axis=1),
                           jnp.concatenate([none, b_bar], axis=1)], axis=0)
    c_lam = lambda d: cl_ref[e, d] if d >= 0 else none
    skip = jnp.concatenate([skip_ref[e], skip_ref[e]], axis=1)
    tiles = []
    for delta in range(SSM_Q // 2):
        rhs = jnp.concatenate(
            [jnp.concatenate([c_lam(2 * delta), c_lam(2 * delta - 1)], axis=1),
             jnp.concatenate([c_lam(2 * delta + 1), c_lam(2 * delta)], axis=1)], axis=0)
        k = lax.dot_general(lhs, rhs, (((1,), (1,)), ((), ())),
                            precision=lax.Precision.HIGHEST, preferred_element_type=_F32)
        if delta == 0:
            k = k + jnp.where(r == c, skip, 0.0)
        tiles.append(jnp.where(same_group, k, 0.0).astype(_BF16))
    zero = jnp.zeros((LANES, LANES), _BF16)
    two_offsets = lambda ref, i: jnp.concatenate([ref[e, i], ref[e, i + 1]], axis=0).astype(_BF16)
    for b in range(SSM_Q // 2):
        rows = slice(b * LANES, (b + 1) * LANES)
        for b2 in range(SSM_Q // 2):
            w1_ref[e, rows, b2 * LANES:(b2 + 1) * LANES] = tiles[b2 - b] if b2 >= b else zero
        w_in = jnp.where(diag_s, jnp.dot(two_offsets(cs_ref, 2 * b), rep_s, preferred_element_type=_F32), 0.0)
        w1_ref[e, rows, HALF_IN:] = w_in.astype(_BF16)
        w_out_t = jnp.where(diag_s, jnp.dot(two_offsets(cl_ref, 2 * b + 1), rep_s,
                                            preferred_element_type=_F32), 0.0)
        w2_ref[e, rows, :] = w_out_t.astype(_BF16)


def _ssm_kernel(x_ref, cs_ref, cl_ref, skip_ref, dec_ref, y_ref, w1_ref, w2_ref, h_ref, t_scr):
    nc = x_ref.shape[0]
    nb = nc // SUBLANES
    piece = 2 * LANES

    def cmul_add(acc_r, acc_i, pr, pi, tr, ti):
        return acc_r + pr * tr - pi * ti, acc_i + pr * ti + pi * tr

    def roll_in_block(a, shift):
        return pltpu.roll(a.reshape(nb, SUBLANES, LANES), shift, axis=1).reshape(nc, LANES)

    @pl.when((pl.program_id(1) == 0) & (pl.program_id(2) == 0))
    def _():
        for e in range(2):
            _expand_half(cs_ref, cl_ref, skip_ref, w1_ref, w2_ref, e)

    @pl.when(pl.program_id(2) == 0)
    def _():
        h_ref[...] = jnp.zeros_like(h_ref)

    xs = [x_ref[:, e * HALF_IN:(e + 1) * HALF_IN] for e in range(2)]
    s_all = [jnp.dot(xs[e], w1_ref[e, :, HALF_IN:], preferred_element_type=_F32) for e in range(2)]
    sub = lax.broadcasted_iota(jnp.int32, (nc, LANES), 0) % SUBLANES
    brow = lax.broadcasted_iota(jnp.int32, (nb, LANES), 0)
    ys = [None, None]
    for e in range(2):
        for p in range(HALF_STATE // piece):
            re, im = slice(p * piece, p * piece + LANES), slice(p * piece + LANES, (p + 1) * piece)
            table = lambda row0, n: (dec_ref[e, row0:row0 + n, re], dec_ref[e, row0:row0 + n, im])
            tall = lambda a: jnp.concatenate([a] * nb, axis=0)
            h_re = slice(e * HALF_STATE + p * piece, e * HALF_STATE + p * piece + LANES)
            h_im = slice(h_re.stop, h_re.stop + LANES)
            slot = 2 * (e * (HALF_STATE // piece) + p)

            sr, si = s_all[e][:, re], s_all[e][:, im]
            hin_r, hin_i = h_ref[0:1, h_re], h_ref[0:1, h_im]

            for k in range(SCAN_STEPS):
                mr, mi = table(k * SUBLANES, SUBLANES)
                sr, si = cmul_add(sr, si, tall(mr), tall(mi),
                                  roll_in_block(sr, 1 << k), roll_in_block(si, 1 << k))

            t_scr[slot] = sr
            t_scr[slot + 1] = si
            br = t_scr[slot, pl.ds(SUBLANES - 1, nb, stride=SUBLANES), :]
            bi = t_scr[slot + 1, pl.ds(SUBLANES - 1, nb, stride=SUBLANES), :]
            pr, pi = table(TABLE_BLOCKS, 1)
            br = br + jnp.where(brow == 0, pr * hin_r - pi * hin_i, 0.0)
            bi = bi + jnp.where(brow == 0, pr * hin_i + pi * hin_r, 0.0)
            for k in range(dec_ref.shape[1] - TABLE_BLOCKS):
                shift = 1 << k
                tr = jnp.where(brow >= shift, pltpu.roll(br, shift, axis=0), 0.0)
                ti = jnp.where(brow >= shift, pltpu.roll(bi, shift, axis=0), 0.0)
                br, bi = cmul_add(br, bi, *table(TABLE_BLOCKS + k, 1), tr, ti)
            h_ref[0:1, h_re] = br[nb - 1:nb]
            h_ref[0:1, h_im] = bi[nb - 1:nb]
            bpr = jnp.where(brow == 0, hin_r, pltpu.roll(br, 1, axis=0))
            bpi = jnp.where(brow == 0, hin_i, pltpu.roll(bi, 1, axis=0))

            rep = lambda a: jnp.concatenate(
                [jnp.broadcast_to(a[k:k + 1], (SUBLANES, LANES)) for k in range(nb)], axis=0)
            fr, fi = table(TABLE_POWERS, SUBLANES)
            tr = jnp.where(sub >= 1, roll_in_block(sr, 1), 0.0)
            ti = jnp.where(sub >= 1, roll_in_block(si, 1), 0.0)
            prev_r, prev_i = cmul_add(tr, ti, tall(fr), tall(fi), rep(bpr), rep(bpi))
            hprev = jnp.concatenate([prev_r, prev_i], axis=1).astype(_BF16)
            part = lax.dot_general(hprev, w2_ref[e, :, p * piece:(p + 1) * piece], (((1,), (1,)), ((), ())),
                                   preferred_element_type=_F32)
            if ys[e] is None:
                mid = HALF_IN // 2
                ys[e] = jnp.concatenate(
                    [jnp.dot(xs[e][:, :mid], w1_ref[e, :mid, :mid], preferred_element_type=_F32),
                     jnp.dot(xs[e], w1_ref[e, :, mid:HALF_IN], preferred_element_type=_F32)], axis=1) + part
            else:
                ys[e] = ys[e] + part

    low = lax.broadcasted_iota(jnp.int32, (nc, LANES), 1) < HALF
    for b in range(SSM_Q // 2):
        y0, y1 = (ys[e][:, b * LANES:(b + 1) * LANES] for e in range(2))
        y_ref[pl.ds(2 * b, nc, stride=SSM_Q), :] = jnp.where(low, y0, _swap_halves(y1))
        y_ref[pl.ds(2 * b + 1, nc, stride=SSM_Q), :] = jnp.where(low, _swap_halves(y0), y1)


def _ssm(ucat, cs, cl, skip, decay, batch, seq, nc):
    tiles = seq // (nc * SSM_Q)
    pair = lambda a: pl.BlockSpec((2,) + a.shape[1:], lambda s, b, t: (s,) + (0,) * (a.ndim - 1))
    return pl.pallas_call(
        _ssm_kernel,
        grid=(N_SLABS, batch, tiles),
        in_specs=[pl.BlockSpec((nc, SLAB_IN), lambda s, b, t: (b * tiles + t, s)),
                  pair(cs), pair(cl), pair(skip), pair(decay)],
        out_specs=pl.BlockSpec((nc * SSM_Q, LANES), lambda s, b, t: (b * tiles + t, s)),
        out_shape=jax.ShapeDtypeStruct((batch * seq, SSM_W), _F32),
        scratch_shapes=[pltpu.VMEM((2, HALF_IN, HALF_IN + HALF_STATE), _BF16),
                        pltpu.VMEM((2, HALF_IN, HALF_STATE), _BF16),
                        pltpu.VMEM((SUBLANES, SLAB_STATE), _F32),
                        pltpu.VMEM((SLAB_STATE // LANES, nc, LANES), _F32)],
        compiler_params=pltpu.CompilerParams(
            dimension_semantics=("arbitrary", "arbitrary", "arbitrary"),
            vmem_limit_bytes=VMEM_LIMIT),
        name="s5_mixer",
    )(ucat, cs, cl, skip, decay)


def _gelu_tanh(y):
    c = math.sqrt(2.0 / math.pi)
    return (0.5 * y) * (1.0 + jnp.tanh(y * (c + (c * 0.044715) * (y * y))))


def _mix_out_kernel(sinks_ref, x_ref, q_ref, kp_ref, kc_ref, vp_ref, vc_ref, za_ref, y_ref, zs_ref,
                    wg_ref, bg_ref, wo_ref, out_ref, acc_ref, *, tiles_per_seq):
    scores, vts = _attn_scores(q_ref, kp_ref, kc_ref, vp_ref, vc_ref)

    y = _gelu_tanh(y_ref[...].astype(_F32))
    gate = jnp.dot(y.astype(_BF16), wg_ref[...], preferred_element_type=_F32) + bg_ref[...]
    o = y * jax.nn.sigmoid(gate) * zs_ref[...].astype(_F32)
    rs = lax.rsqrt(jnp.mean(o * o, axis=-1, keepdims=True) + NORM_EPS)
    out_ref[...] = x_ref[...] + rs * jnp.dot(o.astype(_BF16), wo_ref[ATTN_W:, :], preferred_element_type=_F32)

    seq_start = pl.program_id(0) % tiles_per_seq == 0
    _attn_softmax_pv(sinks_ref, scores, vts, acc_ref, seq_start, range(acc_ref.shape[0] // BLOCK))
    g = acc_ref[...] * za_ref[...].astype(_F32)
    rs = lax.rsqrt(jnp.mean(g * g, axis=-1, keepdims=True) + NORM_EPS)
    out_ref[...] += rs * jnp.dot(g.astype(_BF16), wo_ref[:ATTN_W, :], preferred_element_type=_F32)


def _mix_out(sinks, x2, q, k, v, za, y, zs, wg, bg, wo, seq, tm):
    t = x2.shape[0]
    blocks = tm // BLOCK
    row = lambda w: pl.BlockSpec((tm, w), lambda i: (i, 0))
    prev = lambda w: pl.BlockSpec((BLOCK, w), lambda i: (jnp.maximum(i * blocks - 1, 0), 0))
    return pl.pallas_call(
        functools.partial(_mix_out_kernel, tiles_per_seq=seq // tm),
        grid=(t // tm,),
        in_specs=[pl.BlockSpec(memory_space=pltpu.SMEM), row(D_MODEL), row(ATTN_W), prev(KV_W), row(KV_W),
                  prev(KV_W), row(KV_W), row(ATTN_W), row(SSM_W), row(SSM_W),
                  _resident(wg.shape), _resident(bg.shape), _resident(wo.shape)],
        out_specs=row(D_MODEL),
        out_shape=jax.ShapeDtypeStruct((t, D_MODEL), x2.dtype),
        scratch_shapes=[pltpu.VMEM((tm, ATTN_W), _F32)],
        compiler_params=pltpu.CompilerParams(dimension_semantics=("arbitrary",),
                                             vmem_limit_bytes=VMEM_LIMIT),
        name="mix_out",
    )(sinks, x2, q, k, k, v, v, za, y, zs, wg, bg, wo)


def _pick_tile(n, target, multiple):
    best = multiple
    for cand in range(multiple, min(n, target) + 1, multiple):
        if n % cand == 0:
            best = cand
    return best


def kernel(x, positions, norm_w, w_in, q_norm_w, k_norm_w, sinks, a_re, a_im, log_step,
           b_re, b_im, c_re, c_im, d_skip, w_glu, b_glu, attn_out_norm_w, ssm_out_norm_w, w_out):
    batch, seq, _ = x.shape
    assert seq % BLOCK == 0 and x.shape[2] == D_MODEL and positions.shape == (batch, seq)
    assert w_in.shape == (D_MODEL, IN_W) and w_out.shape == (ATTN_W + SSM_W, D_MODEL)
    assert a_re.shape == (SSM_G, SSM_P) and b_re.shape == (SSM_G, SSM_P, SSM_H) and c_re.shape == (SSM_G, SSM_H, SSM_P)
    assert sinks.shape == (N_HEADS,) and w_glu.shape == (SSM_W, SSM_W)
    t = batch * seq
    f32 = _F32

    w_in_k = (norm_w.astype(f32)[:, None] * w_in.astype(f32)).astype(_BF16)
    out_gain = jnp.concatenate([attn_out_norm_w.astype(f32), ssm_out_norm_w.astype(f32)])
    w_out_k = (out_gain[:, None] * w_out.astype(f32)).astype(_BF16)
    log2e = math.log2(math.e)
    qw = (jnp.tile(q_norm_w.astype(f32), N_HEADS) * (log2e / math.sqrt(HEAD_DIM)))[None, :]
    kw = jnp.tile(k_norm_w.astype(f32), N_KV_HEADS)[None, :]
    inv_freq = ROPE_THETA ** (-jnp.arange(0, HEAD_DIM, 2, dtype=f32) / HEAD_DIM)
    invf = jnp.tile(inv_freq, LANES // (HEAD_DIM // 2))[None, :]
    sgn = jnp.tile(jnp.concatenate([-jnp.ones(HEAD_DIM // 2, f32), jnp.ones(HEAD_DIM // 2, f32)]),
                   LANES // HEAD_DIM)[None, :]

    x2 = x.reshape(t, D_MODEL)
    pos2 = positions.reshape(t // LANES, LANES).astype(jnp.int32)

    tm = _pick_tile(seq, 512, BLOCK)
    q, k, v, za, ucat, zs = _in_proj(x2, pos2, w_in_k, qw, kw, invf, sgn, tm)


    nc = _pick_tile(seq // SSM_Q, 512, 16)
    assert nc & (nc - 1) == 0, "chunk rows per tile must be a power of two for the scan"
    cs, cl, skip, decay = _ssm_weights(a_re, a_im, log_step, b_re, b_im, c_re, c_im, d_skip,
                                     n_scan=int(math.log2(nc)))
    y = _ssm(ucat, cs, cl, skip, decay, batch, seq, nc)

    out = _mix_out(sinks.astype(f32) * log2e, x2, q, k, v, za, y, zs, w_glu.astype(_BF16),
                   b_glu.astype(f32)[None, :], w_out_k, seq, tm)
    return out.reshape(x.shape)
```
